```python
import math
import jax, jax.numpy as jnp
from jax import lax
import numpy as np

D_MODEL = 1024
BATCH = 2
SEQ = 8192
DEPTH = 2

HY_WIDTH = 768
HY_EMB = 33
HY_FFN = 64
HY_FAST_DECAY_PCT = 0.3
HY_SLOW_DECAY_PCT = 1.5
HY_TARGET = 1e-2

HEAD_DIM = 64
ATT_GROUPS = ((128, 1), (512, 4), (2048, 16))
HEADS_PER_GROUP = 4
N_HEADS = HEADS_PER_GROUP * len(ATT_GROUPS)
ATT_WIDTH = N_HEADS * HEAD_DIM
ATT_OUT = HEADS_PER_GROUP * HEAD_DIM
ROPE_THETA = 10000.0

N_BRANCHES = 2
IN_SPLITS = (3 * HY_WIDTH, ATT_WIDTH, ATT_WIDTH, ATT_WIDTH, D_MODEL, D_MODEL)
IN_COLS = sum(IN_SPLITS)

N_EXPERTS = 16
N_GROUPS = 4
EXPERTS_PER_GROUP = N_EXPERTS // N_GROUPS
TOP_K = 2
D_EXPERT = 512

NORM_EPS = 1e-6
MASK_VALUE = -1e30

kernel_name = "hybrid_hyena_dilated_attn_grouped_moe_encoder"


def rms_norm(x, g):
    xf = x.astype(jnp.float32)
    y = xf * lax.rsqrt(jnp.mean(xf * xf, axis=-1, keepdims=True) + NORM_EPS)
    return (y * g.astype(jnp.float32)).astype(x.dtype)


def modulate(h, shift, scale):
    return h * (1 + scale[:, None, :]) + shift[:, None, :]


def short_conv(u, w, b):
    L = u.shape[1]
    up = jnp.pad(u, ((0, 0), (1, 1), (0, 0)))
    return up[:, :L] * w[0] + up[:, 1:L + 1] * w[1] + up[:, 2:] * w[2] + b


def hyena_filters(L, w1, b1, w2, b2, w3, b3, w4, freq):
    f32 = jnp.float32
    t = jnp.linspace(0.0, 1.0, L, dtype=f32)[:, None]
    bands = (HY_EMB - 1) // 2
    w = 2.0 * math.pi * jnp.arange(L, dtype=f32)[:, None] / L
    f = jnp.linspace(1e-4, bands - 1, bands, dtype=f32)[None, :]
    z = jnp.concatenate([t, jnp.cos(f * w), -jnp.sin(f * w)], axis=-1)
    fr = freq.astype(f32)
    h = jnp.sin(fr * (z @ w1.astype(f32) + b1.astype(f32)))
    h = jnp.sin(fr * (h @ w2.astype(f32) + b2.astype(f32)))
    h = jnp.sin(fr * (h @ w3.astype(f32) + b3.astype(f32)))
    k = (h @ w4.astype(f32)).reshape(L, 2, HY_WIDTH)
    max_decay = math.log(HY_TARGET) / HY_FAST_DECAY_PCT
    min_decay = math.log(HY_TARGET) / HY_SLOW_DECAY_PCT
    deltas = jnp.linspace(min_decay, max_decay, HY_WIDTH, dtype=f32)
    decay = jnp.exp(-t * jnp.abs(deltas)[None, :])
    k = k * decay[:, None, :]
    return k[:, 0], k[:, 1]


def bidir_long_conv(z, k_fwd, k_bwd):
    L, C = k_fwd.shape
    kern = jnp.concatenate([k_fwd, jnp.zeros((1, C), jnp.float32), k_bwd[:0:-1]], axis=0)
    kf = jnp.fft.rfft(kern, axis=0)
    zf = jnp.fft.rfft(z.astype(jnp.float32), n=2 * L, axis=1)
    y = jnp.fft.irfft(zf * kf[None], n=2 * L, axis=1)[:, :L]
    return y.astype(z.dtype)


def hyena_mixer(u, w_sc, b_sc, w1, b1, w2, b2, w3, b3, w4, freq, hy_bias):
    L = u.shape[1]
    u = short_conv(u, w_sc, b_sc)
    x0, x1, v = jnp.split(u, 3, axis=-1)
    zv = v * x1
    k_fwd, k_bwd = hyena_filters(L, w1, b1, w2, b2, w3, b3, w4, freq)
    y = bidir_long_conv(zv, k_fwd, k_bwd) + zv * hy_bias
    return y * x0


def rope(x, pos):
    half = HEAD_DIM // 2
    inv = ROPE_THETA ** (-jnp.arange(half, dtype=jnp.float32) / half)
    ang = pos.astype(jnp.float32)[:, None] * inv[None, :]
    cos = jnp.cos(ang)[None, :, None, :]
    sin = jnp.sin(ang)[None, :, None, :]
    xf = x.astype(jnp.float32)
    a, b = xf[..., :half], xf[..., half:]
    return jnp.concatenate([a * cos - b * sin, b * cos + a * sin], axis=-1).astype(x.dtype)


def dilated_band_attention(q, k, v, dil, radius):
    B, L, H, Dh = q.shape
    Ls = L // dil
    nb = -(-Ls // radius)
    Lp = nb * radius

    def strided(t):
        return t.reshape(B, Ls, dil, H, Dh).transpose(0, 2, 3, 1, 4)

    pad4 = ((0, 0), (0, 0), (0, 0))
    qs = jnp.pad(strided(q), pad4 + ((0, Lp - Ls), (0, 0))).reshape(B, dil, H, nb, radius, Dh)
    kp = jnp.pad(strided(k), pad4 + ((radius, Lp - Ls + radius), (0, 0))).reshape(B, dil, H, nb + 2, radius, Dh)
    vp = jnp.pad(strided(v), pad4 + ((radius, Lp - Ls + radius), (0, 0))).reshape(B, dil, H, nb + 2, radius, Dh)

    def band(t):
        return jnp.concatenate([t[:, :, :, 0:nb], t[:, :, :, 1:nb + 1], t[:, :, :, 2:nb + 2]], axis=4)

    kb, vb = band(kp), band(vp)
    s = jnp.einsum('bghnqd,bghnkd->bghnqk', qs, kb, preferred_element_type=jnp.float32) * (Dh ** -0.5)
    blk = jnp.arange(nb)[:, None] * radius
    qpos = blk + jnp.arange(radius)[None, :]
    kpos = blk - radius + jnp.arange(3 * radius)[None, :]
    mask = ((jnp.abs(qpos[:, :, None] - kpos[:, None, :]) <= radius)
            & (kpos[:, None, :] >= 0) & (kpos[:, None, :] < Ls))
    s = jnp.where(mask, s, MASK_VALUE)
    m = jnp.max(s, axis=-1, keepdims=True)
    p = jnp.where(mask, jnp.exp(s - m), 0.0)
    den = jnp.maximum(jnp.sum(p, axis=-1, keepdims=True), 1e-30)
    o = jnp.einsum('bghnqk,bghnkd->bghnqd', (p / den).astype(v.dtype), vb)
    lse = (m + jnp.log(den))[..., 0]
    o = o.reshape(B, dil, H, Lp, Dh)[:, :, :, :Ls].transpose(0, 3, 1, 2, 4).reshape(B, L, H, Dh)
    lse = lse.reshape(B, dil, H, Lp)[:, :, :, :Ls].transpose(0, 3, 1, 2).reshape(B, L, H)
    return o, lse


def dilated_attention_mixer(q, k, v):
    B, L = q.shape[:2]
    outs, lses = [], []
    for g, (win, dil) in enumerate(ATT_GROUPS):
        sl = slice(g * HEADS_PER_GROUP, (g + 1) * HEADS_PER_GROUP)
        o, l = dilated_band_attention(q[:, :, sl], k[:, :, sl], v[:, :, sl], dil, win // (2 * dil))
        outs.append(o)
        lses.append(l)
    alpha = jax.nn.softmax(jnp.stack(lses, axis=0), axis=0)
    o = jnp.sum(alpha[..., None].astype(q.dtype) * jnp.stack(outs, axis=0), axis=0)
    return o.reshape(B, L, ATT_OUT)


def grouped_moe(h, router_w, router_bias, w1, w3, w2):
    B, L, D = h.shape
    t = h.reshape(B * L, D)
    scores = jax.nn.sigmoid(jnp.matmul(t, router_w, preferred_element_type=jnp.float32))
    biased = scores + router_bias.astype(jnp.float32)
    grp = biased.reshape(-1, N_GROUPS, EXPERTS_PER_GROUP)
    grp_score = jnp.sum(lax.top_k(grp, TOP_K)[0], axis=-1)
    sel = jnp.argmax(grp_score, axis=-1)
    in_grp = jnp.arange(N_GROUPS)[None, :] == sel[:, None]
    masked = jnp.where(in_grp[:, :, None], grp, MASK_VALUE).reshape(-1, N_EXPERTS)
    _, idx = lax.top_k(masked, TOP_K)
    wts = jnp.take_along_axis(scores, idx, axis=-1)
    wts = wts / jnp.sum(wts, axis=-1, keepdims=True)
    gates = jnp.sum(jax.nn.one_hot(idx, N_EXPERTS, dtype=jnp.float32) * wts[..., None], axis=1)
    gates = gates.astype(t.dtype)
    y = jnp.zeros_like(t)
    for e in range(N_EXPERTS):
        a = jax.nn.silu(t @ w1[e]) * (t @ w3[e])
        y = y + gates[:, e:e + 1] * (a @ w2[e])
    return y.reshape(B, L, D)


def setup_inputs(seed: int = 0) -> dict:
    key = jax.random.key(seed)
    ks = iter(jax.random.split(key, 32))

    def nrm(shape, scale):
        return jax.random.normal(next(ks), shape, jnp.float32) * scale

    D, E, F = D_MODEL, N_EXPERTS, D_EXPERT
    C3 = 3 * HY_WIDTH
    return {
        "x": nrm((BATCH, SEQ, D), 1.0),
        "c": nrm((BATCH, D), 1.0),
        "norm1_g": 1.0 + nrm((DEPTH, D), 0.02),
        "norm2_g": 1.0 + nrm((DEPTH, D), 0.02),
        "w_ada": nrm((DEPTH, D, 6 * D), D ** -0.5),
        "b_ada": nrm((DEPTH, 6 * D), 0.02),
        "w_in": nrm((DEPTH, D, IN_COLS), D ** -0.5),
        "w_sc": nrm((DEPTH, 3, C3), 3 ** -0.5),
        "b_sc": nrm((DEPTH, C3), 0.02),
        "hf_w1": nrm((DEPTH, HY_EMB, HY_FFN), HY_EMB ** -0.5),
        "hf_b1": nrm((DEPTH, HY_FFN), 0.1),
        "hf_w2": nrm((DEPTH, HY_FFN, HY_FFN), HY_FFN ** -0.5),
        "hf_b2": nrm((DEPTH, HY_FFN), 0.1),
        "hf_w3": nrm((DEPTH, HY_FFN, HY_FFN), HY_FFN ** -0.5),
        "hf_b3": nrm((DEPTH, HY_FFN), 0.1),
        "hf_w4": nrm((DEPTH, HY_FFN, 2 * HY_WIDTH), 0.01),
        "hf_freq": 1.0 + nrm((DEPTH, HY_FFN), 0.01),
        "hy_bias": nrm((DEPTH, HY_WIDTH), 0.5),
        "w_br_h": nrm((DEPTH, HY_WIDTH, D), HY_WIDTH ** -0.5),
        "w_br_a": nrm((DEPTH, ATT_OUT, D), ATT_OUT ** -0.5),
        "w_out": nrm((DEPTH, D, D), D ** -0.5),
        "router_w": nrm((D, E), D ** -0.5),
        "router_bias": nrm((E,), 0.01),
        "moe_w1": nrm((DEPTH, E, D, F), D ** -0.5),
        "moe_w3": nrm((DEPTH, E, D, F), D ** -0.5),
        "moe_w2": nrm((DEPTH, E, F, D), F ** -0.5),
        "final_g": 1.0 + nrm((D,), 0.02),
    }


def reference(x, c, norm1_g, norm2_g, w_ada, b_ada, w_in, w_sc, b_sc,
              hf_w1, hf_b1, hf_w2, hf_b2, hf_w3, hf_b3, hf_w4, hf_freq, hy_bias,
              w_br_h, w_br_a, w_out, router_w, router_bias, moe_w1, moe_w3, moe_w2, final_g):
    B, L, D = x.shape
    pos = jnp.arange(L)
    split_idx = [int(v) for v in np.cumsum(IN_SPLITS)[:-1]]
    c_act = jax.nn.silu(c)
    for i in range(DEPTH):
        mod = c_act @ w_ada[i] + b_ada[i]
        sh1, sc1, g1, sh2, sc2, g2 = jnp.split(mod, 6, axis=-1)

        h = modulate(rms_norm(x, norm1_g[i]), sh1, sc1)
        proj = h @ w_in[i]
        u_hy, q, k, v, gate_hy, gate_at = jnp.split(proj, split_idx, axis=-1)
        y_hy = hyena_mixer(u_hy, w_sc[i], b_sc[i], hf_w1[i], hf_b1[i], hf_w2[i], hf_b2[i],
                           hf_w3[i], hf_b3[i], hf_w4[i], hf_freq[i], hy_bias[i])
        q = rope(q.reshape(B, L, N_HEADS, HEAD_DIM), pos)
        k = rope(k.reshape(B, L, N_HEADS, HEAD_DIM), pos)
        v = v.reshape(B, L, N_HEADS, HEAD_DIM)
        y_at = dilated_attention_mixer(q, k, v)
        merged = (jax.nn.sigmoid(gate_hy) * (y_hy @ w_br_h[i])
                  + jax.nn.sigmoid(gate_at) * (y_at @ w_br_a[i]))
        x = x + g1[:, None, :] * (merged @ w_out[i])

        h = modulate(rms_norm(x, norm2_g[i]), sh2, sc2)
        x = x + g2[:, None, :] * grouped_moe(h, router_w, router_bias, moe_w1[i], moe_w3[i], moe_w2[i])
    return rms_norm(x, final_g)
```

```python
import functools
import math

import jax
import jax.numpy as jnp
import numpy as np
from jax import lax
from jax.experimental import pallas as pl
from jax.experimental.pallas import tpu as pltpu

D_MODEL = 1024
DEPTH = 2
HY_WIDTH = 768
HY_EMB = 33
HY_FAST_DECAY_PCT = 0.3
HY_SLOW_DECAY_PCT = 1.5
HY_TARGET = 1e-2
HEAD_DIM = 64
ATT_GROUPS = ((128, 1), (512, 4), (2048, 16))
HEADS_PER_GROUP = 4
N_HEADS = HEADS_PER_GROUP * len(ATT_GROUPS)
ATT_WIDTH = N_HEADS * HEAD_DIM
ATT_OUT = HEADS_PER_GROUP * HEAD_DIM
ROPE_THETA = 10000.0
IN_SPLITS = (3 * HY_WIDTH, ATT_WIDTH, ATT_WIDTH, ATT_WIDTH, D_MODEL, D_MODEL)
N_EXPERTS = 16
N_GROUPS = 4
EXPERTS_PER_GROUP = N_EXPERTS // N_GROUPS
D_EXPERT = 512
NORM_EPS = 1e-6
MASK_VALUE = -1e30

LANES = 128
VMEM_LIMIT_BYTES = 56 * 1024 * 1024

PAIRS_PER_GROUP = 6
N_BUCKETS = N_GROUPS * PAIRS_PER_GROUP
BUCKET_ROWS = 32
ROUTER_TM = 512
EXPERT_TM = 256
ROW_BLOCK = 1024
GATE_COLS = LANES
F32 = jnp.float32
BF16 = jnp.bfloat16


def _params(*sem):
    return pltpu.CompilerParams(dimension_semantics=sem, vmem_limit_bytes=VMEM_LIMIT_BYTES)


def _router_kernel(x_ref, g_ref, sc_ref, sh_ref, rwt_ref, rb_ref, hg_ref, stats_ref, cnt_ref, base_ref):
    i = pl.program_id(0)
    tm = x_ref.shape[0]

    @pl.when(i == 0)
    def _():
        base_ref[...] = jnp.zeros_like(base_ref)

    x = x_ref[...]
    r = lax.rsqrt(jnp.mean(x * x, axis=-1, keepdims=True) + NORM_EPS)
    h = (x * r) * g_ref[...]
    h = h * (1.0 + sc_ref[0]) + sh_ref[0]

    logits = lax.dot_general(rwt_ref[...], h, (((1,), (1,)), ((), ())),
                             precision=lax.Precision.HIGHEST, preferred_element_type=F32)
    scores = 1.0 / (1.0 + jnp.exp(-logits))
    biased = scores + rb_ref[...]

    def row(a, k):
        return a[k:k + 1, :]

    sel = jnp.zeros((1, tm), jnp.int32)
    best = None
    for g in range(N_GROUPS):
        a, b, c, d = (row(biased, 4 * g + k) for k in range(4))
        m_ab, n_ab = jnp.maximum(a, b), jnp.minimum(a, b)
        m_cd, n_cd = jnp.maximum(c, d), jnp.minimum(c, d)
        gs = jnp.maximum(m_ab, m_cd) + jnp.maximum(jnp.minimum(m_ab, m_cd), jnp.maximum(n_ab, n_cd))
        if g == 0:
            best = gs
        else:
            better = gs > best
            sel = jnp.where(better, g, sel)
            best = jnp.where(better, gs, best)

    v, u = [], []
    for k in range(EXPERTS_PER_GROUP):
        vk = jnp.zeros((1, tm), F32)
        uk = jnp.zeros((1, tm), F32)
        for g in range(N_GROUPS):
            vk = jnp.where(sel == g, row(biased, 4 * g + k), vk)
            uk = jnp.where(sel == g, row(scores, 4 * g + k), uk)
        v.append(vk)
        u.append(uk)

    i1 = jnp.zeros((1, tm), jnp.int32)
    b1 = v[0]
    for k in range(1, EXPERTS_PER_GROUP):
        gt = v[k] > b1
        i1 = jnp.where(gt, k, i1)
        b1 = jnp.where(gt, v[k], b1)
    i2 = jnp.zeros((1, tm), jnp.int32)
    b2 = jnp.full((1, tm), -jnp.inf, F32)
    for k in range(EXPERTS_PER_GROUP):
        cand = (i1 != k) & (v[k] > b2)
        i2 = jnp.where(cand, k, i2)
        b2 = jnp.where(cand, v[k], b2)

    lo = jnp.minimum(i1, i2)
    hi = jnp.maximum(i1, i2)
    pair = jnp.where(lo == 0, hi - 1, jnp.where(lo == 1, hi + 1, 5))
    bucket = sel * PAIRS_PER_GROUP + pair

    u_lo = jnp.zeros((1, tm), F32)
    u_hi = jnp.zeros((1, tm), F32)
    for k in range(EXPERTS_PER_GROUP):
        u_lo = jnp.where(lo == k, u[k], u_lo)
        u_hi = jnp.where(hi == k, u[k], u_hi)
    tot = u_lo + u_hi
    w_lo = u_lo / tot
    w_hi = u_hi / tot

    rows = lax.broadcasted_iota(jnp.int32, (BUCKET_ROWS, tm), 0)
    onehot = (rows == bucket).astype(F32)
    t_src = lax.broadcasted_iota(jnp.int32, (tm, tm), 0)
    t_dst = lax.broadcasted_iota(jnp.int32, (tm, tm), 1)
    before = (t_src < t_dst).astype(BF16)
    cum = jnp.dot(onehot.astype(BF16), before, preferred_element_type=F32)
    base = base_ref[...]
    rank = jnp.sum(onehot * (cum + base), axis=0, keepdims=True)
    base = base + jnp.sum(onehot, axis=1, keepdims=True)
    base_ref[...] = base
    cnt_ref[...] = jnp.broadcast_to(base, cnt_ref.shape)

    srow = lax.broadcasted_iota(jnp.int32, (8, tm), 0)
    stats_ref[...] = jnp.where(srow == 0, bucket.astype(F32), jnp.where(srow == 1, rank, 0.0))

    grow = lax.broadcasted_iota(jnp.int32, (GATE_COLS, tm), 0)
    gates_t = jnp.where(grow == 0, w_lo, jnp.where(grow == 1, w_hi, 0.0))
    hg_ref[:, :D_MODEL] = h
    hg_ref[:, D_MODEL:] = gates_t.T


def _router(x, norm_g, sc, sh, router_wt, router_b, seq_len):
    T, D = x.shape
    tm = ROUTER_TM
    per_batch = seq_len // tm
    return pl.pallas_call(
        _router_kernel,
        out_shape=(jax.ShapeDtypeStruct((T, D + GATE_COLS), F32),
                   jax.ShapeDtypeStruct((8, T), F32),
                   jax.ShapeDtypeStruct((BUCKET_ROWS, LANES), F32)),
        grid=(T // tm,),
        in_specs=[pl.BlockSpec((tm, D), lambda i: (i, 0)),
                  pl.BlockSpec((1, D), lambda i: (0, 0)),
                  pl.BlockSpec((1, 1, D), lambda i: (i // per_batch, 0, 0)),
                  pl.BlockSpec((1, 1, D), lambda i: (i // per_batch, 0, 0)),
                  pl.BlockSpec((N_EXPERTS, D), lambda i: (0, 0)),
                  pl.BlockSpec((N_EXPERTS, 1), lambda i: (0, 0))],
        out_specs=(pl.BlockSpec((tm, D + GATE_COLS), lambda i: (i, 0)),
                   pl.BlockSpec((8, tm), lambda i: (0, i)),
                   pl.BlockSpec((BUCKET_ROWS, LANES), lambda i: (0, 0))),
        scratch_shapes=[pltpu.VMEM((BUCKET_ROWS, 1), F32)],
        compiler_params=_params("arbitrary"),
        name="moe_router",
    )(x, norm_g, sc, sh, router_wt, router_b)


def _dispatch_kernel(pos_ref, hg_hbm, xs_init_hbm, xs_hbm, sem):
    del xs_init_hbm
    base = pl.program_id(0) * ROW_BLOCK

    def issue(r, carry):
        pltpu.make_async_copy(hg_hbm.at[pl.ds(base + r, 1), :],
                              xs_hbm.at[pl.ds(pos_ref[r], 1), :], sem).start()
        return carry

    lax.fori_loop(0, ROW_BLOCK, issue, 0)
    pltpu.make_async_copy(hg_hbm.at[pl.ds(0, ROW_BLOCK), :],
                          xs_hbm.at[pl.ds(0, ROW_BLOCK), :], sem).wait()


def _dispatch(pos, hg, n_rows):
    T, W = hg.shape
    zeros = jnp.zeros((n_rows, W), F32)
    return pl.pallas_call(
        _dispatch_kernel,
        out_shape=jax.ShapeDtypeStruct((n_rows, W), F32),
        grid=(T // ROW_BLOCK,),
        in_specs=[pl.BlockSpec((ROW_BLOCK,), lambda i: (i,), memory_space=pltpu.SMEM),
                  pl.BlockSpec(memory_space=pl.ANY),
                  pl.BlockSpec(memory_space=pl.ANY)],
        out_specs=pl.BlockSpec(memory_space=pl.ANY),
        scratch_shapes=[pltpu.SemaphoreType.DMA],
        input_output_aliases={2: 0},
        compiler_params=_params("arbitrary"),
        name="moe_dispatch",
    )(pos, hg, zeros)


def _expert_kernel(e_lo_ref, e_hi_ref, n_used_ref, xs_ref, w1a, w3a, w2a, w1b, w3b, w2b, y_ref):
    del e_lo_ref, e_hi_ref
    used = pl.program_id(0) < n_used_ref[0]

    @pl.when(jnp.logical_not(used))
    def _():
        y_ref[...] = jnp.zeros_like(y_ref)

    @pl.when(used)
    def _():
        xb = xs_ref[:, :D_MODEL].astype(BF16)
        g_lo = xs_ref[:, D_MODEL:D_MODEL + 1]
        g_hi = xs_ref[:, D_MODEL + 1:D_MODEL + 2]

        def ffn(w1, w3, w2):
            a = jnp.dot(xb, w1[0], preferred_element_type=F32)
            b = jnp.dot(xb, w3[0], preferred_element_type=F32)
            act = (a / (1.0 + jnp.exp(-a))) * b
            return jnp.dot(act.astype(BF16), w2[0], preferred_element_type=F32)

        y_ref[...] = g_lo * ffn(w1a, w3a, w2a) + g_hi * ffn(w1b, w3b, w2b)


def _experts(tile_lo, tile_hi, n_used, xs, w1, w3, w2):
    n_rows, W = xs.shape
    D, F = w1.shape[1], w1.shape[2]
    n_tiles = n_rows // EXPERT_TM

    def x_map(j, lo, hi, nu):
        return (jnp.minimum(j, nu[0] - 1), 0)

    def w_lo_map(j, lo, hi, nu):
        return (lo[j], 0, 0)

    def w_hi_map(j, lo, hi, nu):
        return (hi[j], 0, 0)

    grid_spec = pltpu.PrefetchScalarGridSpec(
        num_scalar_prefetch=3,
        grid=(n_tiles,),
        in_specs=[pl.BlockSpec((EXPERT_TM, W), x_map),
                  pl.BlockSpec((1, D, F), w_lo_map), pl.BlockSpec((1, D, F), w_lo_map),
                  pl.BlockSpec((1, F, D), w_lo_map),
                  pl.BlockSpec((1, D, F), w_hi_map), pl.BlockSpec((1, D, F), w_hi_map),
                  pl.BlockSpec((1, F, D), w_hi_map)],
        out_specs=pl.BlockSpec((EXPERT_TM, D), lambda j, lo, hi, nu: (j, 0)),
    )
    return pl.pallas_call(
        _expert_kernel,
        out_shape=jax.ShapeDtypeStruct((n_rows, D), F32),
        grid_spec=grid_spec,
        compiler_params=_params("arbitrary"),
        name="moe_experts",
    )(tile_lo, tile_hi, n_used, xs, w1, w3, w2, w1, w3, w2)


def _combine_kernel(pos_ref, x_ref, g2_ref, ys_hbm, o_ref, buf, sem):
    def issue(r, carry):
        pltpu.make_async_copy(ys_hbm.at[pl.ds(pos_ref[r], 1), :], buf.at[pl.ds(r, 1), :], sem).start()
        return carry

    lax.fori_loop(0, ROW_BLOCK, issue, 0)
    pltpu.make_async_copy(ys_hbm.at[pl.ds(0, ROW_BLOCK), :], buf, sem).wait()
    o_ref[...] = x_ref[...] + g2_ref[0] * buf[...]


def _combine(pos, x, g2, ys, seq_len):
    T, D = x.shape
    per_batch = seq_len // ROW_BLOCK
    return pl.pallas_call(
        _combine_kernel,
        out_shape=jax.ShapeDtypeStruct((T, D), F32),
        grid=(T // ROW_BLOCK,),
        in_specs=[pl.BlockSpec((ROW_BLOCK,), lambda i: (i,), memory_space=pltpu.SMEM),
                  pl.BlockSpec((ROW_BLOCK, D), lambda i: (i, 0)),
                  pl.BlockSpec((1, 1, D), lambda i: (i // per_batch, 0, 0)),
                  pl.BlockSpec(memory_space=pl.ANY)],
        out_specs=pl.BlockSpec((ROW_BLOCK, D), lambda i: (i, 0)),
        scratch_shapes=[pltpu.VMEM((ROW_BLOCK, D), F32), pltpu.SemaphoreType.DMA],
        compiler_params=_params("arbitrary"),
        name="moe_combine",
    )(pos, x, g2, ys)


def _moe_layer(x, norm_g, sc2, sh2, g2, router_wt, router_b, w1, w3, w2, seq_len):
    T, D = x.shape
    hg, stats, counts = _router(x, norm_g, sc2, sh2, router_wt, router_b, seq_len)

    cnt = counts[:N_BUCKETS, 0].astype(jnp.int32)
    padded = ((cnt + EXPERT_TM - 1) // EXPERT_TM) * EXPERT_TM
    ends = jnp.cumsum(padded)
    starts = ends - padded
    bucket = stats[0].astype(jnp.int32)
    pos = starts[bucket] + stats[1].astype(jnp.int32)
    n_tiles = T // EXPERT_TM + N_BUCKETS
    tile_bucket = jnp.searchsorted(ends, jnp.arange(n_tiles, dtype=jnp.int32) * EXPERT_TM, side="right")
    tile_bucket = jnp.minimum(tile_bucket, N_BUCKETS - 1).astype(jnp.int32)
    grp, pair = tile_bucket // PAIRS_PER_GROUP, tile_bucket % PAIRS_PER_GROUP
    pair_lo = jnp.array([0, 0, 0, 1, 1, 2], jnp.int32)[pair]
    pair_hi = jnp.array([1, 2, 3, 2, 3, 3], jnp.int32)[pair]
    tile_lo = grp * EXPERTS_PER_GROUP + pair_lo
    tile_hi = grp * EXPERTS_PER_GROUP + pair_hi
    n_used = (ends[-1] // EXPERT_TM).astype(jnp.int32).reshape(1)

    xs = _dispatch(pos, hg, n_tiles * EXPERT_TM)
    ys = _experts(tile_lo, tile_hi, n_used, xs, w1, w3, w2)
    return _combine(pos, x, g2, ys, seq_len)


def _final_norm_kernel(x_ref, g_ref, o_ref):
    x = x_ref[...]
    r = lax.rsqrt(jnp.mean(x * x, axis=-1, keepdims=True) + NORM_EPS)
    o_ref[...] = (x * r) * g_ref[...]


def _final_norm(x, g):
    T, D = x.shape
    tm = 1024
    return pl.pallas_call(
        _final_norm_kernel,
        out_shape=jax.ShapeDtypeStruct((T, D), F32),
        grid=(T // tm,),
        in_specs=[pl.BlockSpec((tm, D), lambda i: (i, 0)), pl.BlockSpec((1, D), lambda i: (0, 0))],
        out_specs=pl.BlockSpec((tm, D), lambda i: (i, 0)),
        compiler_params=_params("arbitrary"),
        name="final_norm",
    )(x, g)


def _rms_norm(x, g):
    y = x * lax.rsqrt(jnp.mean(x * x, axis=-1, keepdims=True) + NORM_EPS)
    return y * g


def _short_conv(u, w, b):
    L = u.shape[1]
    up = jnp.pad(u, ((0, 0), (1, 1), (0, 0)))
    return up[:, :L] * w[0] + up[:, 1:L + 1] * w[1] + up[:, 2:] * w[2] + b


def _hyena_filters(L, w1, b1, w2, b2, w3, b3, w4, freq):
    t = jnp.linspace(0.0, 1.0, L, dtype=F32)[:, None]
    bands = (HY_EMB - 1) // 2
    w = 2.0 * math.pi * jnp.arange(L, dtype=F32)[:, None] / L
    f = jnp.linspace(1e-4, bands - 1, bands, dtype=F32)[None, :]
    z = jnp.concatenate([t, jnp.cos(f * w), -jnp.sin(f * w)], axis=-1)
    h = jnp.sin(freq * (z @ w1 + b1))
    h = jnp.sin(freq * (h @ w2 + b2))
    h = jnp.sin(freq * (h @ w3 + b3))
    k = (h @ w4).reshape(L, 2, HY_WIDTH)
    max_decay = math.log(HY_TARGET) / HY_FAST_DECAY_PCT
    min_decay = math.log(HY_TARGET) / HY_SLOW_DECAY_PCT
    deltas = jnp.linspace(min_decay, max_decay, HY_WIDTH, dtype=F32)
    decay = jnp.exp(-t * jnp.abs(deltas)[None, :])
    k = k * decay[:, None, :]
    return k[:, 0], k[:, 1]


def _bidir_long_conv(z, k_fwd, k_bwd):
    L, C = k_fwd.shape
    kern = jnp.concatenate([k_fwd, jnp.zeros((1, C), F32), k_bwd[:0:-1]], axis=0)
    kf = jnp.fft.rfft(kern, axis=0)
    zf = jnp.fft.rfft(z, n=2 * L, axis=1)
    return jnp.fft.irfft(zf * kf[None], n=2 * L, axis=1)[:, :L]


def _hyena_mixer(u, w_sc, b_sc, w1, b1, w2, b2, w3, b3, w4, freq, hy_bias):
    L = u.shape[1]
    u = _short_conv(u, w_sc, b_sc)
    x0, x1, v = jnp.split(u, 3, axis=-1)
    zv = v * x1
    k_fwd, k_bwd = _hyena_filters(L, w1, b1, w2, b2, w3, b3, w4, freq)
    y = _bidir_long_conv(zv, k_fwd, k_bwd) + zv * hy_bias
    return y * x0


def _rope(x, pos):
    half = HEAD_DIM // 2
    inv = ROPE_THETA ** (-jnp.arange(half, dtype=F32) / half)
    ang = pos.astype(F32)[:, None] * inv[None, :]
    cos = jnp.cos(ang)[None, :, None, :]
    sin = jnp.sin(ang)[None, :, None, :]
    a, b = x[..., :half], x[..., half:]
    return jnp.concatenate([a * cos - b * sin, b * cos + a * sin], axis=-1)


def _dilated_band_attention(q, k, v, dil, radius):
    B, L, H, Dh = q.shape
    Ls = L // dil
    nb = -(-Ls // radius)
    Lp = nb * radius

    def strided(t):
        return t.reshape(B, Ls, dil, H, Dh).transpose(0, 2, 3, 1, 4)

    pad4 = ((0, 0), (0, 0), (0, 0))
    qs = jnp.pad(strided(q), pad4 + ((0, Lp - Ls), (0, 0))).reshape(B, dil, H, nb, radius, Dh)
    kp = jnp.pad(strided(k), pad4 + ((radius, Lp - Ls + radius), (0, 0))).reshape(B, dil, H, nb + 2, radius, Dh)
    vp = jnp.pad(strided(v), pad4 + ((radius, Lp - Ls + radius), (0, 0))).reshape(B, dil, H, nb + 2, radius, Dh)

    def band(t):
        return jnp.concatenate([t[:, :, :, 0:nb], t[:, :, :, 1:nb + 1], t[:, :, :, 2:nb + 2]], axis=4)

    kb, vb = band(kp), band(vp)
    s = jnp.einsum('bghnqd,bghnkd->bghnqk', qs, kb, preferred_element_type=F32) * (Dh ** -0.5)
    blk = jnp.arange(nb)[:, None] * radius
    qpos = blk + jnp.arange(radius)[None, :]
    kpos = blk - radius + jnp.arange(3 * radius)[None, :]
    mask = ((jnp.abs(qpos[:, :, None] - kpos[:, None, :]) <= radius)
            & (kpos[:, None, :] >= 0) & (kpos[:, None, :] < Ls))
    s = jnp.where(mask, s, MASK_VALUE)
    m = jnp.max(s, axis=-1, keepdims=True)
    p = jnp.where(mask, jnp.exp(s - m), 0.0)
    den = jnp.maximum(jnp.sum(p, axis=-1, keepdims=True), 1e-30)
    o = jnp.einsum('bghnqk,bghnkd->bghnqd', p / den, vb)
    lse = (m + jnp.log(den))[..., 0]
    o = o.reshape(B, dil, H, Lp, Dh)[:, :, :, :Ls].transpose(0, 3, 1, 2, 4).reshape(B, L, H, Dh)
    lse = lse.reshape(B, dil, H, Lp)[:, :, :, :Ls].transpose(0, 3, 1, 2).reshape(B, L, H)
    return o, lse


def _dilated_attention_mixer(q, k, v):
    B, L = q.shape[:2]
    outs, lses = [], []
    for g, (win, dil) in enumerate(ATT_GROUPS):
        sl = slice(g * HEADS_PER_GROUP, (g + 1) * HEADS_PER_GROUP)
        o, l = _dilated_band_attention(q[:, :, sl], k[:, :, sl], v[:, :, sl], dil, win // (2 * dil))
        outs.append(o)
        lses.append(l)
    alpha = jax.nn.softmax(jnp.stack(lses, axis=0), axis=0)
    o = jnp.sum(alpha[..., None] * jnp.stack(outs, axis=0), axis=0)
    return o.reshape(B, L, ATT_OUT)


def kernel(x, c, norm1_g, norm2_g, w_ada, b_ada, w_in, w_sc, b_sc, hf_w1, hf_b1, hf_w2, hf_b2, hf_w3, hf_b3,
           hf_w4, hf_freq, hy_bias, w_br_h, w_br_a, w_out, router_w, router_bias, moe_w1, moe_w3, moe_w2, final_g):
    B, L, D = x.shape
    T = B * L
    pos = jnp.arange(L)
    split_idx = [int(v) for v in np.cumsum(IN_SPLITS)[:-1]]
    c_act = jax.nn.silu(c)
    router_wt = router_w.T
    router_b = router_bias.reshape(N_EXPERTS, 1)
    for i in range(DEPTH):
        mod = c_act @ w_ada[i] + b_ada[i]
        sh1, sc1, g1, sh2, sc2, g2 = jnp.split(mod, 6, axis=-1)

        h = _rms_norm(x, norm1_g[i]) * (1 + sc1[:, None, :]) + sh1[:, None, :]
        proj = h @ w_in[i]
        u_hy, q, k, v, gate_hy, gate_at = jnp.split(proj, split_idx, axis=-1)
        y_hy = _hyena_mixer(u_hy, w_sc[i], b_sc[i], hf_w1[i], hf_b1[i], hf_w2[i], hf_b2[i],
                            hf_w3[i], hf_b3[i], hf_w4[i], hf_freq[i], hy_bias[i])
        q = _rope(q.reshape(B, L, N_HEADS, HEAD_DIM), pos)
        k = _rope(k.reshape(B, L, N_HEADS, HEAD_DIM), pos)
        v = v.reshape(B, L, N_HEADS, HEAD_DIM)
        y_at = _dilated_attention_mixer(q, k, v)
        merged = (jax.nn.sigmoid(gate_hy) * (y_hy @ w_br_h[i]) + jax.nn.sigmoid(gate_at) * (y_at @ w_br_a[i]))
        x = x + g1[:, None, :] * (merged @ w_out[i])

        xt = _moe_layer(x.reshape(T, D), norm2_g[i].reshape(1, D), sc2.reshape(B, 1, D), sh2.reshape(B, 1, D),
                        g2.reshape(B, 1, D), router_wt, router_b,
                        moe_w1[i].astype(BF16), moe_w3[i].astype(BF16), moe_w2[i].astype(BF16), L)
        x = xt.reshape(B, L, D)
    return _final_norm(x.reshape(T, D), final_g.reshape(1, D)).reshape(B, L, D)
```

```python
import math

import jax
import jax.numpy as jnp
import numpy as np
from jax import lax
from jax.experimental import pallas as pl
from jax.experimental.pallas import tpu as pltpu

D_MODEL = 1024
DEPTH = 2
HY_WIDTH = 768
HY_EMB = 33
HY_FFN = 64
HY_FAST_DECAY_PCT = 0.3
HY_SLOW_DECAY_PCT = 1.5
HY_TARGET = 1e-2
HEAD_DIM = 64
ATT_GROUPS = ((128, 1), (512, 4), (2048, 16))
HEADS_PER_GROUP = 4
N_HEADS = HEADS_PER_GROUP * len(ATT_GROUPS)
ATT_WIDTH = N_HEADS * HEAD_DIM
ATT_OUT = HEADS_PER_GROUP * HEAD_DIM
ROPE_THETA = 10000.0
IN_SPLITS = (3 * HY_WIDTH, ATT_WIDTH, ATT_WIDTH, ATT_WIDTH, D_MODEL, D_MODEL)
N_EXPERTS = 16
N_GROUPS = 4
EXPERTS_PER_GROUP = N_EXPERTS // N_GROUPS
D_EXPERT = 512
NORM_EPS = 1e-6
MASK_VALUE = -1e30

LANES = 128
MXU_DIM = 256
VMEM_LIMIT_BYTES = 56 * 1024 * 1024

F32 = jnp.float32
BF16 = jnp.bfloat16
HIGHEST = lax.Precision.HIGHEST

PROJ_TM = 2048
PROJ_TN = MXU_DIM
HY_COLS = 3 * HY_WIDTH
QKV_COLS = 3 * ATT_WIDTH
GATE_COLS2 = 2 * D_MODEL

FFT_R = 128
FFT_KH = FFT_R // 2 + 1
FFT_KP = 72
FFT_SLAB = 2 * FFT_KP
CONV_CB = 128

PAIRS_PER_GROUP = 6
N_BUCKETS = N_GROUPS * PAIRS_PER_GROUP
BUCKET_ROWS = 32
ROUTER_TM = 512
EXPERT_TM = 256
ROW_BLOCK = 1024
GATE_COLS = LANES


def _params(*sem):
    return pltpu.CompilerParams(dimension_semantics=sem, vmem_limit_bytes=VMEM_LIMIT_BYTES)


def _sigmoid(x):
    return 1.0 / (1.0 + jnp.exp(-x))


def _ada_kernel(c_ref, w_ref, b_ref, o_ref):
    c = c_ref[...]
    c_act = c * _sigmoid(c)
    o_ref[0] = jnp.dot(c_act, w_ref[0], precision=HIGHEST, preferred_element_type=F32) + b_ref[0]


def _ada(c_pad, w_ada, b_ada):
    depth, D, N = w_ada.shape
    rows = c_pad.shape[0]
    tn = N // 4
    return pl.pallas_call(
        _ada_kernel,
        out_shape=jax.ShapeDtypeStruct((depth, rows, N), F32),
        grid=(depth, N // tn),
        in_specs=[pl.BlockSpec((rows, D), lambda l, j: (0, 0)),
                  pl.BlockSpec((1, D, tn), lambda l, j: (l, 0, j)),
                  pl.BlockSpec((1, 1, tn), lambda l, j: (l, 0, j))],
        out_specs=pl.BlockSpec((1, rows, tn), lambda l, j: (l, 0, j)),
        compiler_params=_params("arbitrary", "arbitrary"),
        name="ada_mod",
    )(c_pad, w_ada, b_ada.reshape(depth, 1, N))


def _norm_mod_kernel(x_ref, g_ref, sc_ref, sh_ref, o_ref):
    x = x_ref[...]
    r = lax.rsqrt(jnp.mean(x * x, axis=-1, keepdims=True) + NORM_EPS)
    h = (x * r) * g_ref[...]
    o_ref[...] = (h * (1.0 + sc_ref[0]) + sh_ref[0]).astype(o_ref.dtype)


def _norm_mod(x, g, sc, sh, seq_len):
    T, D = x.shape
    tm = 1024
    per_batch = seq_len // tm
    return pl.pallas_call(
        _norm_mod_kernel,
        out_shape=jax.ShapeDtypeStruct((T, D), BF16),
        grid=(T // tm,),
        in_specs=[pl.BlockSpec((tm, D), lambda i: (i, 0)),
                  pl.BlockSpec((1, D), lambda i: (0, 0)),
                  pl.BlockSpec((1, 1, D), lambda i: (i // per_batch, 0, 0)),
                  pl.BlockSpec((1, 1, D), lambda i: (i // per_batch, 0, 0))],
        out_specs=pl.BlockSpec((tm, D), lambda i: (i, 0)),
        compiler_params=_params("arbitrary"),
        name="norm_mod",
    )(x, g, sc, sh)


def _proj_kernel(h_ref, w_ref, o_ref):
    o_ref[...] = jnp.dot(h_ref[...], w_ref[...], preferred_element_type=F32).astype(o_ref.dtype)


def _proj_rope_kernel(h_ref, w_ref, cos_ref, sin_ref, rot_ref, o_ref, *, n_q_tiles):
    j = pl.program_id(1)
    acc = jnp.dot(h_ref[...], w_ref[...], preferred_element_type=F32)

    @pl.when(j < 2 * n_q_tiles)
    def _():
        cos = jnp.concatenate([cos_ref[...], cos_ref[...]], axis=1)
        sin = jnp.concatenate([sin_ref[...], sin_ref[...]], axis=1)
        swapped = jnp.dot(acc.astype(BF16), rot_ref[...], preferred_element_type=F32)
        out = acc * cos + swapped * sin
        scale = jnp.where(j < n_q_tiles, HEAD_DIM ** -0.5, 1.0)
        o_ref[...] = (out * scale).astype(o_ref.dtype)

    @pl.when(j >= 2 * n_q_tiles)
    def _():
        o_ref[...] = acc.astype(o_ref.dtype)


def _proj(h, w, col0, n_cols, rope=None, seq_len=None):
    T, D = h.shape
    tm, tn = PROJ_TM, PROJ_TN
    off = col0 // tn
    grid = (T // tm, n_cols // tn)
    h_spec = pl.BlockSpec((tm, D), lambda i, j: (i, 0))
    w_spec = pl.BlockSpec((D, tn), lambda i, j: (0, off + j))
    o_spec = pl.BlockSpec((tm, tn), lambda i, j: (i, j))
    out_shape = jax.ShapeDtypeStruct((T, n_cols), BF16)
    if rope is None:
        return pl.pallas_call(_proj_kernel, out_shape=out_shape, grid=grid, in_specs=[h_spec, w_spec],
                              out_specs=o_spec, compiler_params=_params("arbitrary", "arbitrary"),
                              name="proj")(h, w)
    cos_t, sin_t, rot = rope
    per_batch = seq_len // tm
    tab_spec = pl.BlockSpec((tm, LANES), lambda i, j: (i % per_batch, 0))
    kern = lambda *refs: _proj_rope_kernel(*refs, n_q_tiles=ATT_WIDTH // tn)
    return pl.pallas_call(kern, out_shape=out_shape, grid=grid,
                          in_specs=[h_spec, w_spec, tab_spec, tab_spec,
                                    pl.BlockSpec((tn, tn), lambda i, j: (0, 0))],
                          out_specs=o_spec, compiler_params=_params("arbitrary", "arbitrary"),
                          name="proj_rope")(h, w, cos_t, sin_t, rot)


def _rope_tables(seq_len):
    half = HEAD_DIM // 2
    inv = ROPE_THETA ** (-jnp.arange(half, dtype=F32) / half)
    ang = jnp.arange(seq_len, dtype=F32)[:, None] * inv[None, :]
    reps = LANES // half
    cos_t = jnp.tile(jnp.cos(ang), (1, reps))
    sin_t = jnp.tile(jnp.sin(ang), (1, reps))
    rot = np.zeros((PROJ_TN, PROJ_TN), np.float32)
    for j in range(PROJ_TN):
        if j % HEAD_DIM < half:
            rot[j + half, j] = -1.0
        else:
            rot[j - half, j] = 1.0
    return cos_t, sin_t, jnp.asarray(rot, BF16)


def _hyena_filter_kernel(feat_ref, w1_ref, b1_ref, w2_ref, b2_ref, w3_ref, b3_ref, w4_ref, fr_ref,
                         dl_ref, o_ref, *, seq_len):
    i = pl.program_id(0)
    tm = feat_ref.shape[0]
    z = feat_ref[...]
    fr = fr_ref[...]
    h = jnp.sin(fr * (jnp.dot(z, w1_ref[...], precision=HIGHEST, preferred_element_type=F32) + b1_ref[...]))
    h = jnp.sin(fr * (jnp.dot(h, w2_ref[...], precision=HIGHEST, preferred_element_type=F32) + b2_ref[...]))
    h = jnp.sin(fr * (jnp.dot(h, w3_ref[...], precision=HIGHEST, preferred_element_type=F32) + b3_ref[...]))
    k = jnp.dot(h, w4_ref[...], precision=HIGHEST, preferred_element_type=F32)
    decay = jnp.exp(-z[:, 0:1] * jnp.abs(dl_ref[...]))
    n = i * tm + lax.broadcasted_iota(jnp.int32, (tm, 1), 0)
    o_ref[...] = jnp.where(n == seq_len, 0.0, k * decay)


def _hyena_filter(feat, w1, b1, w2, b2, w3, b3, w4, freq, deltas, seq_len):
    n_rows = feat.shape[0]
    C = HY_WIDTH
    tm = 1024
    fwd_tiles = seq_len // tm
    full = lambda i: (0, 0)
    kern = lambda *refs: _hyena_filter_kernel(*refs, seq_len=seq_len)
    return pl.pallas_call(
        kern,
        out_shape=jax.ShapeDtypeStruct((n_rows, C), F32),
        grid=(n_rows // tm,),
        in_specs=[pl.BlockSpec((tm, LANES), lambda i: (i, 0)),
                  pl.BlockSpec((LANES, LANES), full), pl.BlockSpec((1, LANES), full),
                  pl.BlockSpec((LANES, LANES), full), pl.BlockSpec((1, LANES), full),
                  pl.BlockSpec((LANES, LANES), full), pl.BlockSpec((1, LANES), full),
                  pl.BlockSpec((LANES, C), lambda i: (0, i // fwd_tiles)),
                  pl.BlockSpec((1, LANES), full),
                  pl.BlockSpec((1, C), full)],
        out_specs=pl.BlockSpec((tm, C), lambda i: (i, 0)),
        compiler_params=_params("arbitrary"),
        name="hyena_filter",
    )(feat, w1, b1, w2, b2, w3, b3, w4, freq, deltas)


def _hyena_features(seq_len):
    L = seq_len
    n = np.arange(2 * L)
    m = np.where(n <= L, n, 2 * L - n)
    m = np.where(n == L, 0, m)
    t = jnp.linspace(0.0, 1.0, L, dtype=F32)[:, None]
    bands = (HY_EMB - 1) // 2
    w = 2.0 * math.pi * jnp.arange(L, dtype=F32)[:, None] / L
    f = jnp.linspace(1e-4, bands - 1, bands, dtype=F32)[None, :]
    z = jnp.concatenate([t, jnp.cos(f * w), -jnp.sin(f * w)], axis=-1)
    z = jnp.pad(z, ((0, 0), (0, LANES - HY_EMB)))
    max_decay = math.log(HY_TARGET) / HY_FAST_DECAY_PCT
    min_decay = math.log(HY_TARGET) / HY_SLOW_DECAY_PCT
    deltas = jnp.linspace(min_decay, max_decay, HY_WIDTH, dtype=F32)[None, :]
    return z[jnp.asarray(m)], deltas


def _pad2(a, rows, cols):
    return jnp.pad(a, ((0, rows - a.shape[0]), (0, cols - a.shape[1])))


def _dft_matrices():
    R, KH, KP = FFT_R, FFT_KH, FFT_KP
    N = R * R
    n1 = np.arange(R)[:, None, None]
    k2 = np.arange(KP)[None, :, None]
    n2 = np.arange(R)[None, None, :]
    phase = 2.0 * np.pi * ((n2 * k2 % R) / R + (n1 * k2) / N)
    live = (k2 < KH)
    a1 = np.concatenate([np.cos(phase) * live, -np.sin(phase) * live], axis=1)
    wgt = np.where((k2 == 0) | (k2 == R // 2), 1.0, 2.0) * live / N
    b1 = np.concatenate([np.cos(phase) * wgt, -np.sin(phase) * wgt], axis=1)
    b1 = np.transpose(b1, (0, 2, 1))[:, :R // 2, :]
    th = 2.0 * np.pi * (np.arange(R)[:, None] * np.arange(R)[None, :] % R) / R
    c, s = np.cos(th), np.sin(th)
    w2f = np.block([[c, s], [-s, c]])
    w2i = np.block([[c, -s], [s, c]])
    as_bf = lambda a: jnp.asarray(a.astype(np.float32), BF16)
    return as_bf(a1), as_bf(a1[:, :, :R // 2]), as_bf(b1), as_bf(w2f), as_bf(w2i)


def _fft_stage1(src_ref, a1_ref, g_ref, n_rows):
    def body(n1, carry):
        xs = src_ref[pl.ds(n1, n_rows, stride=FFT_R), :].astype(BF16)
        slab = jnp.dot(a1_ref[n1], xs, preferred_element_type=F32)
        g_ref[pl.ds(pl.multiple_of(n1 * FFT_SLAB, 8), FFT_SLAB), :] = slab
        return carry
    lax.fori_loop(0, FFT_R, body, 0)


def _load_freq_rows(g_ref, k2):
    re = g_ref[pl.ds(k2, FFT_R, stride=FFT_SLAB), :]
    im = g_ref[pl.ds(FFT_KP + k2, FFT_R, stride=FFT_SLAB), :]
    return jnp.concatenate([re, im], axis=0)


def _filter_fft_kernel(kern_ref, a1_ref, w2_ref, o_ref, g_ref):
    _fft_stage1(kern_ref, a1_ref, g_ref, FFT_R)

    def body(k2, carry):
        gk = _load_freq_rows(g_ref, k2).astype(BF16)
        o_ref[k2] = jnp.dot(w2_ref[...], gk, preferred_element_type=F32).astype(o_ref.dtype)
        return carry
    lax.fori_loop(0, FFT_KH, body, 0)


def _filter_fft(kern, a1_full, w2f):
    n_rows, C = kern.shape
    cb = CONV_CB
    return pl.pallas_call(
        _filter_fft_kernel,
        out_shape=jax.ShapeDtypeStruct((FFT_KH, 2 * FFT_R, C), BF16),
        grid=(C // cb,),
        in_specs=[pl.BlockSpec((n_rows, cb), lambda j: (0, j)),
                  pl.BlockSpec((FFT_R, FFT_SLAB, FFT_R), lambda j: (0, 0, 0)),
                  pl.BlockSpec((2 * FFT_R, 2 * FFT_R), lambda j: (0, 0))],
        out_specs=pl.BlockSpec((FFT_KH, 2 * FFT_R, cb), lambda j: (0, 0, j)),
        scratch_shapes=[pltpu.VMEM((FFT_R * FFT_SLAB, cb), F32)],
        compiler_params=_params("arbitrary"),
        name="hyena_filter_fft",
    )(kern, a1_full, w2f)


def _short_conv_kernel(x0_ref, x1_ref, v_ref, p0_ref, p1_ref, pv_ref, n0_ref, n1_ref, nv_ref,
                       w0_ref, w1_ref, wv_ref, b0_ref, b1_ref, bv_ref, x0c_ref, zv_ref, scr, *, tiles_per_seq):
    i = pl.program_id(0)
    tm = x0_ref.shape[0]
    first = (i % tiles_per_seq) == 0
    last = (i % tiles_per_seq) == tiles_per_seq - 1
    halo = p0_ref.shape[0]

    def conv(u_ref, prev_ref, next_ref, w_ref, b_ref):
        prev_row = jnp.where(first, 0.0, prev_ref[halo - 1:halo, :].astype(F32))
        next_row = jnp.where(last, 0.0, next_ref[0:1, :].astype(F32))
        scr[7:8, :] = prev_row
        scr[8:8 + tm, :] = u_ref[...].astype(F32)
        scr[8 + tm:9 + tm, :] = next_row
        w = w_ref[...]
        return (scr[pl.ds(7, tm), :] * w[0:1] + scr[pl.ds(8, tm), :] * w[1:2]
                + scr[pl.ds(9, tm), :] * w[2:3] + b_ref[...])

    x0c_ref[...] = conv(x0_ref, p0_ref, n0_ref, w0_ref, b0_ref).astype(x0c_ref.dtype)
    x1c = conv(x1_ref, p1_ref, n1_ref, w1_ref, b1_ref)
    vc = conv(v_ref, pv_ref, nv_ref, wv_ref, bv_ref)
    zv_ref[...] = vc * x1c


def _short_conv(u, w_sc, b_sc, seq_len):
    T = u.shape[0]
    C = HY_WIDTH
    tm, cb, halo = 1024, 256, 16
    ncb = C // cb
    tiles_per_seq = seq_len // tm
    hb = tm // halo
    n_halo = T // halo

    def part(p):
        return pl.BlockSpec((tm, cb), lambda i, j: (i, p * ncb + j))

    def prev(p):
        return pl.BlockSpec((halo, cb), lambda i, j: (jnp.maximum(i * hb - 1, 0), p * ncb + j))

    def nxt(p):
        return pl.BlockSpec((halo, cb), lambda i, j: (jnp.minimum((i + 1) * hb, n_halo - 1), p * ncb + j))

    def wpart(p):
        return pl.BlockSpec((3, cb), lambda i, j: (0, p * ncb + j))

    def bpart(p):
        return pl.BlockSpec((1, cb), lambda i, j: (0, p * ncb + j))

    kern = lambda *refs: _short_conv_kernel(*refs, tiles_per_seq=tiles_per_seq)
    b2 = b_sc.reshape(1, 3 * C)
    return pl.pallas_call(
        kern,
        out_shape=(jax.ShapeDtypeStruct((T, C), BF16), jax.ShapeDtypeStruct((T, C), F32)),
        grid=(T // tm, ncb),
        in_specs=[part(0), part(1), part(2), prev(0), prev(1), prev(2), nxt(0), nxt(1), nxt(2),
                  wpart(0), wpart(1), wpart(2), bpart(0), bpart(1), bpart(2)],
        out_specs=(pl.BlockSpec((tm, cb), lambda i, j: (i, j)), pl.BlockSpec((tm, cb), lambda i, j: (i, j))),
        scratch_shapes=[pltpu.VMEM((tm + 16, cb), F32)],
        compiler_params=_params("arbitrary", "arbitrary"),
        name="hyena_short_conv",
    )(u, u, u, u, u, u, u, u, u, w_sc, w_sc, w_sc, b2, b2, b2)


def _long_conv_kernel(zv_ref, x0_ref, kf_ref, bias_ref, a1_ref, b1_ref, w2f_ref, w2i_ref, o_ref, g_ref):
    seq_len = zv_ref.shape[1]
    n2_rows = seq_len // FFT_R
    zv2 = zv_ref.at[0]
    o2 = o_ref.at[0]
    _fft_stage1(zv2, a1_ref, g_ref, n2_rows)

    def freq_body(k2, carry):
        gk = _load_freq_rows(g_ref, k2).astype(BF16)
        x = jnp.dot(w2f_ref[...], gk, preferred_element_type=F32)
        kf = kf_ref[k2].astype(F32)
        xr, xi = x[:FFT_R], x[FFT_R:]
        kr, ki = kf[:FFT_R], kf[FFT_R:]
        p = jnp.concatenate([xr * kr - xi * ki, xr * ki + xi * kr], axis=0).astype(BF16)
        hk = jnp.dot(w2i_ref[...], p, preferred_element_type=F32)
        g_ref[pl.ds(k2, FFT_R, stride=FFT_SLAB), :] = hk[:FFT_R]
        g_ref[pl.ds(FFT_KP + k2, FFT_R, stride=FFT_SLAB), :] = hk[FFT_R:]
        return carry
    lax.fori_loop(0, FFT_KH, freq_body, 0)

    def time_body(n1, carry):
        slab = g_ref[pl.ds(pl.multiple_of(n1 * FFT_SLAB, 8), FFT_SLAB), :].astype(BF16)
        o2[pl.ds(n1, n2_rows, stride=FFT_R), :] = jnp.dot(b1_ref[n1], slab, preferred_element_type=F32)
        return carry
    lax.fori_loop(0, FFT_R, time_body, 0)

    chunk = 512
    bias = bias_ref[...]

    def out_body(c, carry):
        rows = pl.ds(pl.multiple_of(c * chunk, chunk), chunk)
        o2[rows, :] = (o2[rows, :] + zv2[rows, :] * bias) * x0_ref[0, rows, :].astype(F32)
        return carry
    lax.fori_loop(0, seq_len // chunk, out_body, 0)


def _long_conv(zv, x0c, kf, hy_bias, mats):
    B, L, C = zv.shape
    cb = CONV_CB
    _, a1_half, b1, w2f, w2i = mats
    blk = lambda j, b: (b, 0, j)
    return pl.pallas_call(
        _long_conv_kernel,
        out_shape=jax.ShapeDtypeStruct((B, L, C), F32),
        grid=(C // cb, B),
        in_specs=[pl.BlockSpec((1, L, cb), blk),
                  pl.BlockSpec((1, L, cb), blk),
                  pl.BlockSpec((FFT_KH, 2 * FFT_R, cb), lambda j, b: (0, 0, j)),
                  pl.BlockSpec((1, cb), lambda j, b: (0, j)),
                  pl.BlockSpec((FFT_R, FFT_SLAB, FFT_R // 2), lambda j, b: (0, 0, 0)),
                  pl.BlockSpec((FFT_R, FFT_R // 2, FFT_SLAB), lambda j, b: (0, 0, 0)),
                  pl.BlockSpec((2 * FFT_R, 2 * FFT_R), lambda j, b: (0, 0)),
                  pl.BlockSpec((2 * FFT_R, 2 * FFT_R), lambda j, b: (0, 0))],
        out_specs=pl.BlockSpec((1, L, cb), blk),
        scratch_shapes=[pltpu.VMEM((FFT_R * FFT_SLAB, cb), F32)],
        compiler_params=_params("arbitrary", "arbitrary"),
        name="hyena_long_conv",
    )(zv, x0c, kf, hy_bias.reshape(1, C), a1_half, b1, w2f, w2i)


ATT_TQ = 128
ATT_RADIUS = 64
ATT_WINDOW = ATT_TQ + 2 * ATT_RADIUS


def _attn_kernel(q_ref, k_ref, v_ref, o_ref, lse_ref):
    i = pl.program_id(2)
    ls = k_ref.shape[1]
    q = q_ref[0]
    start = jnp.clip(i * ATT_TQ - ATT_RADIUS, 0, ls - ATT_WINDOW)
    start = pl.multiple_of(start, ATT_RADIUS)
    kw = k_ref[0, pl.ds(start, ATT_WINDOW), :]
    vw = v_ref[0, pl.ds(start, ATT_WINDOW), :]
    qpos = i * ATT_TQ + lax.broadcasted_iota(jnp.int32, (ATT_TQ, ATT_WINDOW), 0)
    kpos = start + lax.broadcasted_iota(jnp.int32, (ATT_TQ, ATT_WINDOW), 1)
    band = jnp.abs(qpos - kpos) <= ATT_RADIUS
    head_of_col = lax.broadcasted_iota(jnp.int32, (1, ATT_OUT), 1) // HEAD_DIM
    out = jnp.zeros((ATT_TQ, ATT_OUT), F32)
    lse = jnp.zeros((ATT_TQ, ATT_OUT), F32)
    for h in range(HEADS_PER_GROUP):
        mine = head_of_col == h
        qh = jnp.where(mine, q, jnp.zeros_like(q))
        s = lax.dot_general(qh, kw, (((1,), (1,)), ((), ())), preferred_element_type=F32)
        s = jnp.where(band, s, MASK_VALUE)
        m = jnp.max(s, axis=-1, keepdims=True)
        p = jnp.exp(s - m)
        den = jnp.sum(p, axis=-1, keepdims=True)
        pv = jnp.dot(p.astype(BF16), vw, preferred_element_type=F32)
        out = jnp.where(mine, pv / den, out)
        lse = jnp.where(mine, m + jnp.log(den), lse)
    o_ref[0] = out.astype(o_ref.dtype)
    lse_ref[0] = lse


def _attention_group(qkv, g, dil, batch, seq_len):
    ls = seq_len // dil
    n_groups = len(ATT_GROUPS)
    blocks_per_row = QKV_COLS // ATT_OUT
    view = qkv.reshape(batch, ls, dil * QKV_COLS)

    def col(which):
        return lambda b, r, i: (b, 0, r * blocks_per_row + which * n_groups + g)

    q_map = lambda b, r, i: (b, i, r * blocks_per_row + g)
    o_map = lambda b, r, i: (b, i, r)
    o, lse = pl.pallas_call(
        _attn_kernel,
        out_shape=(jax.ShapeDtypeStruct((batch, ls, dil * ATT_OUT), BF16),
                   jax.ShapeDtypeStruct((batch, ls, dil * ATT_OUT), F32)),
        grid=(batch, dil, ls // ATT_TQ),
        in_specs=[pl.BlockSpec((1, ATT_TQ, ATT_OUT), q_map),
                  pl.BlockSpec((1, ls, ATT_OUT), col(1)),
                  pl.BlockSpec((1, ls, ATT_OUT), col(2))],
        out_specs=(pl.BlockSpec((1, ATT_TQ, ATT_OUT), o_map), pl.BlockSpec((1, ATT_TQ, ATT_OUT), o_map)),
        compiler_params=_params("arbitrary", "arbitrary", "arbitrary"),
        name=f"dilated_attn_d{dil}",
    )(view, view, view)
    T = batch * seq_len
    return o.reshape(T, ATT_OUT), lse.reshape(T, ATT_OUT)


def _merge_kernel(x_ref, yhy_ref, o1_ref, o2_ref, o3_ref, l1_ref, l2_ref, l3_ref, gh_ref, ga_ref,
                  wh_ref, wa_ref, wo_ref, g1_ref, out_ref):
    l1, l2, l3 = l1_ref[...], l2_ref[...], l3_ref[...]
    m = jnp.maximum(jnp.maximum(l1, l2), l3)
    e1, e2, e3 = jnp.exp(l1 - m), jnp.exp(l2 - m), jnp.exp(l3 - m)
    tot = e1 + e2 + e3
    y_at = (e1 * o1_ref[...].astype(F32) + e2 * o2_ref[...].astype(F32) + e3 * o3_ref[...].astype(F32)) / tot
    a = jnp.dot(yhy_ref[...].astype(BF16), wh_ref[...], preferred_element_type=F32)
    b = jnp.dot(y_at.astype(BF16), wa_ref[...], preferred_element_type=F32)
    merged = _sigmoid(gh_ref[...].astype(F32)) * a + _sigmoid(ga_ref[...].astype(F32)) * b
    upd = jnp.dot(merged.astype(BF16), wo_ref[...], preferred_element_type=F32)
    out_ref[...] = x_ref[...] + g1_ref[0] * upd


def _merge(x, y_hy, attn, gates, w_br_h, w_br_a, w_out, g1, seq_len):
    T, D = x.shape
    tm = 512
    per_batch = seq_len // tm
    (o1, l1), (o2, l2), (o3, l3) = attn
    row = lambda w: pl.BlockSpec((tm, w), lambda i: (i, 0))
    full = lambda a: pl.BlockSpec(a.shape, lambda i: (0, 0))
    return pl.pallas_call(
        _merge_kernel,
        out_shape=jax.ShapeDtypeStruct((T, D), F32),
        grid=(T // tm,),
        in_specs=[row(D), row(HY_WIDTH), row(ATT_OUT), row(ATT_OUT), row(ATT_OUT),
                  row(ATT_OUT), row(ATT_OUT), row(ATT_OUT),
                  pl.BlockSpec((tm, D), lambda i: (i, 0)), pl.BlockSpec((tm, D), lambda i: (i, 1)),
                  full(w_br_h), full(w_br_a), full(w_out),
                  pl.BlockSpec((1, 1, D), lambda i: (i // per_batch, 0, 0))],
        out_specs=row(D),
        compiler_params=_params("arbitrary"),
        name="mixer_merge",
    )(x, y_hy, o1, o2, o3, l1, l2, l3, gates, gates, w_br_h, w_br_a, w_out, g1)


def _router_kernel(x_ref, g_ref, sc_ref, sh_ref, rwt_ref, rb_ref, hg_ref, stats_ref, cnt_ref, base_ref):
    i = pl.program_id(0)
    tm = x_ref.shape[0]

    @pl.when(i == 0)
    def _():
        base_ref[...] = jnp.zeros_like(base_ref)

    x = x_ref[...]
    r = lax.rsqrt(jnp.mean(x * x, axis=-1, keepdims=True) + NORM_EPS)
    h = (x * r) * g_ref[...]
    h = h * (1.0 + sc_ref[0]) + sh_ref[0]

    logits = lax.dot_general(rwt_ref[...], h, (((1,), (1,)), ((), ())),
                             precision=HIGHEST, preferred_element_type=F32)
    scores = _sigmoid(logits)
    biased = scores + rb_ref[...]

    def row(a, k):
        return a[k:k + 1, :]

    sel = jnp.zeros((1, tm), jnp.int32)
    best = None
    for g in range(N_GROUPS):
        a, b, c, d = (row(biased, 4 * g + k) for k in range(4))
        m_ab, n_ab = jnp.maximum(a, b), jnp.minimum(a, b)
        m_cd, n_cd = jnp.maximum(c, d), jnp.minimum(c, d)
        gs = jnp.maximum(m_ab, m_cd) + jnp.maximum(jnp.minimum(m_ab, m_cd), jnp.maximum(n_ab, n_cd))
        if g == 0:
            best = gs
        else:
            better = gs > best
            sel = jnp.where(better, g, sel)
            best = jnp.where(better, gs, best)

    v, u = [], []
    for k in range(EXPERTS_PER_GROUP):
        vk = jnp.zeros((1, tm), F32)
        uk = jnp.zeros((1, tm), F32)
        for g in range(N_GROUPS):
            vk = jnp.where(sel == g, row(biased, 4 * g + k), vk)
            uk = jnp.where(sel == g, row(scores, 4 * g + k), uk)
        v.append(vk)
        u.append(uk)

    i1 = jnp.zeros((1, tm), jnp.int32)
    b1 = v[0]
    for k in range(1, EXPERTS_PER_GROUP):
        gt = v[k] > b1
        i1 = jnp.where(gt, k, i1)
        b1 = jnp.where(gt, v[k], b1)
    i2 = jnp.zeros((1, tm), jnp.int32)
    b2 = jnp.full((1, tm), -jnp.inf, F32)
    for k in range(EXPERTS_PER_GROUP):
        cand = (i1 != k) & (v[k] > b2)
        i2 = jnp.where(cand, k, i2)
        b2 = jnp.where(cand, v[k], b2)

    lo = jnp.minimum(i1, i2)
    hi = jnp.maximum(i1, i2)
    pair = jnp.where(lo == 0, hi - 1, jnp.where(lo == 1, hi + 1, 5))
    bucket = sel * PAIRS_PER_GROUP + pair

    u_lo = jnp.zeros((1, tm), F32)
    u_hi = jnp.zeros((1, tm), F32)
    for k in range(EXPERTS_PER_GROUP):
        u_lo = jnp.where(lo == k, u[k], u_lo)
        u_hi = jnp.where(hi == k, u[k], u_hi)
    tot = u_lo + u_hi
    w_lo = u_lo / tot
    w_hi = u_hi / tot

    rows = lax.broadcasted_iota(jnp.int32, (BUCKET_ROWS, tm), 0)
    onehot = (rows == bucket).astype(F32)
    t_src = lax.broadcasted_iota(jnp.int32, (tm, tm), 0)
    t_dst = lax.broadcasted_iota(jnp.int32, (tm, tm), 1)
    before = (t_src < t_dst).astype(BF16)
    cum = jnp.dot(onehot.astype(BF16), before, preferred_element_type=F32)
    base = base_ref[...]
    rank = jnp.sum(onehot * (cum + base), axis=0, keepdims=True)
    base = base + jnp.sum(onehot, axis=1, keepdims=True)
    base_ref[...] = base
    cnt_ref[...] = jnp.broadcast_to(base, cnt_ref.shape)

    srow = lax.broadcasted_iota(jnp.int32, (8, tm), 0)
    stats_ref[...] = jnp.where(srow == 0, bucket.astype(F32), jnp.where(srow == 1, rank, 0.0))

    grow = lax.broadcasted_iota(jnp.int32, (GATE_COLS, tm), 0)
    gates_t = jnp.where(grow == 0, w_lo, jnp.where(grow == 1, w_hi, 0.0))
    hg_ref[:, :D_MODEL] = h
    hg_ref[:, D_MODEL:] = gates_t.T


def _router(x, norm_g, sc, sh, router_wt, router_b, seq_len):
    T, D = x.shape
    tm = ROUTER_TM
    per_batch = seq_len // tm
    return pl.pallas_call(
        _router_kernel,
        out_shape=(jax.ShapeDtypeStruct((T, D + GATE_COLS), F32),
                   jax.ShapeDtypeStruct((8, T), F32),
                   jax.ShapeDtypeStruct((BUCKET_ROWS, LANES), F32)),
        grid=(T // tm,),
        in_specs=[pl.BlockSpec((tm, D), lambda i: (i, 0)),
                  pl.BlockSpec((1, D), lambda i: (0, 0)),
                  pl.BlockSpec((1, 1, D), lambda i: (i // per_batch, 0, 0)),
                  pl.BlockSpec((1, 1, D), lambda i: (i // per_batch, 0, 0)),
                  pl.BlockSpec((N_EXPERTS, D), lambda i: (0, 0)),
                  pl.BlockSpec((N_EXPERTS, 1), lambda i: (0, 0))],
        out_specs=(pl.BlockSpec((tm, D + GATE_COLS), lambda i: (i, 0)),
                   pl.BlockSpec((8, tm), lambda i: (0, i)),
                   pl.BlockSpec((BUCKET_ROWS, LANES), lambda i: (0, 0))),
        scratch_shapes=[pltpu.VMEM((BUCKET_ROWS, 1), F32)],
        compiler_params=_params("arbitrary"),
        name="moe_router",
    )(x, norm_g, sc, sh, router_wt, router_b)


def _dispatch_kernel(pos_ref, hg_ref, xs_init_hbm, xs_hbm, sem):
    del xs_init_hbm

    def issue(r, carry):
        pltpu.make_async_copy(hg_ref.at[pl.ds(r, 1), :], xs_hbm.at[pl.ds(pos_ref[r], 1), :], sem).start()
        return carry

    lax.fori_loop(0, ROW_BLOCK, issue, 0)
    pltpu.make_async_copy(hg_ref, xs_hbm.at[pl.ds(0, ROW_BLOCK), :], sem).wait()


def _dispatch(pos, hg, n_rows):
    T, W = hg.shape
    zeros = jnp.zeros((n_rows, W), F32)
    return pl.pallas_call(
        _dispatch_kernel,
        out_shape=jax.ShapeDtypeStruct((n_rows, W), F32),
        grid=(T // ROW_BLOCK,),
        in_specs=[pl.BlockSpec((ROW_BLOCK,), lambda i: (i,), memory_space=pltpu.SMEM),
                  pl.BlockSpec((ROW_BLOCK, W), lambda i: (i, 0)),
                  pl.BlockSpec(memory_space=pl.ANY)],
        out_specs=pl.BlockSpec(memory_space=pl.ANY),
        scratch_shapes=[pltpu.SemaphoreType.DMA],
        input_output_aliases={2: 0},
        compiler_params=_params("arbitrary"),
        name="moe_dispatch",
    )(pos, hg, zeros)


def _expert_kernel(e_lo_ref, e_hi_ref, n_used_ref, xs_ref, w1a, w3a, w2a, w1b, w3b, w2b, y_ref):
    del e_lo_ref, e_hi_ref
    used = pl.program_id(0) < n_used_ref[0]

    @pl.when(jnp.logical_not(used))
    def _():
        y_ref[...] = jnp.zeros_like(y_ref)

    @pl.when(used)
    def _():
        xb = xs_ref[:, :D_MODEL].astype(BF16)
        g_lo = xs_ref[:, D_MODEL:D_MODEL + 1]
        g_hi = xs_ref[:, D_MODEL + 1:D_MODEL + 2]

        def ffn(w1, w3, w2):
            a = jnp.dot(xb, w1[0], preferred_element_type=F32)
            b = jnp.dot(xb, w3[0], preferred_element_type=F32)
            act = (a * _sigmoid(a)) * b
            return jnp.dot(act.astype(BF16), w2[0], preferred_element_type=F32)

        y_ref[...] = g_lo * ffn(w1a, w3a, w2a) + g_hi * ffn(w1b, w3b, w2b)


def _experts(tile_lo, tile_hi, n_used, xs, w1, w3, w2):
    n_rows, W = xs.shape
    D, F = w1.shape[1], w1.shape[2]
    n_tiles = n_rows // EXPERT_TM

    def x_map(j, lo, hi, nu):
        return (jnp.minimum(j, nu[0] - 1), 0)

    def w_lo_map(j, lo, hi, nu):
        return (lo[j], 0, 0)

    def w_hi_map(j, lo, hi, nu):
        return (hi[j], 0, 0)

    grid_spec = pltpu.PrefetchScalarGridSpec(
        num_scalar_prefetch=3,
        grid=(n_tiles,),
        in_specs=[pl.BlockSpec((EXPERT_TM, W), x_map),
                  pl.BlockSpec((1, D, F), w_lo_map), pl.BlockSpec((1, D, F), w_lo_map),
                  pl.BlockSpec((1, F, D), w_lo_map),
                  pl.BlockSpec((1, D, F), w_hi_map), pl.BlockSpec((1, D, F), w_hi_map),
                  pl.BlockSpec((1, F, D), w_hi_map)],
        out_specs=pl.BlockSpec((EXPERT_TM, D), lambda j, lo, hi, nu: (j, 0)),
    )
    return pl.pallas_call(
        _expert_kernel,
        out_shape=jax.ShapeDtypeStruct((n_rows, D), F32),
        grid_spec=grid_spec,
        compiler_params=_params("arbitrary"),
        name="moe_experts",
    )(tile_lo, tile_hi, n_used, xs, w1, w3, w2, w1, w3, w2)


def _combine_kernel(pos_ref, x_ref, g2_ref, ys_hbm, o_ref, buf, sem):
    def issue(r, carry):
        pltpu.make_async_copy(ys_hbm.at[pl.ds(pos_ref[r], 1), :], buf.at[pl.ds(r, 1), :], sem).start()
        return carry

    lax.fori_loop(0, ROW_BLOCK, issue, 0)
    pltpu.make_async_copy(ys_hbm.at[pl.ds(0, ROW_BLOCK), :], buf, sem).wait()
    o_ref[...] = x_ref[...] + g2_ref[0] * buf[...]


def _combine(pos, x, g2, ys, seq_len):
    T, D = x.shape
    per_batch = seq_len // ROW_BLOCK
    return pl.pallas_call(
        _combine_kernel,
        out_shape=jax.ShapeDtypeStruct((T, D), F32),
        grid=(T // ROW_BLOCK,),
        in_specs=[pl.BlockSpec((ROW_BLOCK,), lambda i: (i,), memory_space=pltpu.SMEM),
                  pl.BlockSpec((ROW_BLOCK, D), lambda i: (i, 0)),
                  pl.BlockSpec((1, 1, D), lambda i: (i // per_batch, 0, 0)),
                  pl.BlockSpec(memory_space=pl.ANY)],
        out_specs=pl.BlockSpec((ROW_BLOCK, D), lambda i: (i, 0)),
        scratch_shapes=[pltpu.VMEM((ROW_BLOCK, D), F32), pltpu.SemaphoreType.DMA],
        compiler_params=_params("arbitrary"),
        name="moe_combine",
    )(pos, x, g2, ys)


def _moe_layer(x, norm_g, sc2, sh2, g2, router_wt, router_b, w1, w3, w2, seq_len):
    T, D = x.shape
    hg, stats, counts = _router(x, norm_g, sc2, sh2, router_wt, router_b, seq_len)

    cnt = counts[:N_BUCKETS, 0].astype(jnp.int32)
    padded = ((cnt + EXPERT_TM - 1) // EXPERT_TM) * EXPERT_TM
    ends = jnp.cumsum(padded)
    starts = ends - padded
    bucket = stats[0].astype(jnp.int32)
    pos = starts[bucket] + stats[1].astype(jnp.int32)
    n_tiles = T // EXPERT_TM + N_BUCKETS
    tile_bucket = jnp.searchsorted(ends, jnp.arange(n_tiles, dtype=jnp.int32) * EXPERT_TM, side="right")
    tile_bucket = jnp.minimum(tile_bucket, N_BUCKETS - 1).astype(jnp.int32)
    grp, pair = tile_bucket // PAIRS_PER_GROUP, tile_bucket % PAIRS_PER_GROUP
    pair_lo = jnp.array([0, 0, 0, 1, 1, 2], jnp.int32)[pair]
    pair_hi = jnp.array([1, 2, 3, 2, 3, 3], jnp.int32)[pair]
    tile_lo = grp * EXPERTS_PER_GROUP + pair_lo
    tile_hi = grp * EXPERTS_PER_GROUP + pair_hi
    n_used = (ends[-1] // EXPERT_TM).astype(jnp.int32).reshape(1)

    xs = _dispatch(pos, hg, n_tiles * EXPERT_TM)
    ys = _experts(tile_lo, tile_hi, n_used, xs, w1, w3, w2)
    return _combine(pos, x, g2, ys, seq_len)


def _final_norm_kernel(x_ref, g_ref, o_ref):
    x = x_ref[...]
    r = lax.rsqrt(jnp.mean(x * x, axis=-1, keepdims=True) + NORM_EPS)
    o_ref[...] = (x * r) * g_ref[...]


def _final_norm(x, g):
    T, D = x.shape
    tm = 1024
    return pl.pallas_call(
        _final_norm_kernel,
        out_shape=jax.ShapeDtypeStruct((T, D), F32),
        grid=(T // tm,),
        in_specs=[pl.BlockSpec((tm, D), lambda i: (i, 0)), pl.BlockSpec((1, D), lambda i: (0, 0))],
        out_specs=pl.BlockSpec((tm, D), lambda i: (i, 0)),
        compiler_params=_params("arbitrary"),
        name="final_norm",
    )(x, g)


def kernel(x, c, norm1_g, norm2_g, w_ada, b_ada, w_in, w_sc, b_sc, hf_w1, hf_b1, hf_w2, hf_b2, hf_w3, hf_b3,
           hf_w4, hf_freq, hy_bias, w_br_h, w_br_a, w_out, router_w, router_bias, moe_w1, moe_w3, moe_w2, final_g):
    B, L, D = x.shape
    T = B * L
    C = HY_WIDTH
    xt = x.reshape(T, D)

    rope = _rope_tables(L)
    feat, deltas = _hyena_features(L)
    mats = _dft_matrices()

    c_pad = jnp.pad(c, ((0, 8 - B), (0, 0)))
    mod = _ada(c_pad, w_ada, b_ada)[:, :B]
    router_wt = router_w.T
    router_b = router_bias.reshape(N_EXPERTS, 1)

    for i in range(DEPTH):
        sh1, sc1, g1, sh2, sc2, g2 = (mod[i, :, k * D:(k + 1) * D].reshape(B, 1, D) for k in range(6))

        h = _norm_mod(xt, norm1_g[i].reshape(1, D), sc1, sh1, L)
        w_in_b = w_in[i].astype(BF16)
        u = _proj(h, w_in_b, 0, HY_COLS)
        qkv = _proj(h, w_in_b, HY_COLS, QKV_COLS, rope=rope, seq_len=L)
        gates = _proj(h, w_in_b, HY_COLS + QKV_COLS, GATE_COLS2)

        kern = _hyena_filter(feat, _pad2(hf_w1[i], LANES, LANES), _pad2(hf_b1[i][None], 1, LANES),
                             _pad2(hf_w2[i], LANES, LANES), _pad2(hf_b2[i][None], 1, LANES),
                             _pad2(hf_w3[i], LANES, LANES), _pad2(hf_b3[i][None], 1, LANES),
                             _pad2(hf_w4[i], LANES, 2 * C), _pad2(hf_freq[i][None], 1, LANES), deltas, L)
        kf = _filter_fft(kern, mats[0], mats[3])
        x0c, zv = _short_conv(u, w_sc[i], b_sc[i], L)
        y_hy = _long_conv(zv.reshape(B, L, C), x0c.reshape(B, L, C), kf, hy_bias[i], mats).reshape(T, C)

        attn = [_attention_group(qkv, g, dil, B, L) for g, (_, dil) in enumerate(ATT_GROUPS)]
        xt = _merge(xt, y_hy, attn, gates, w_br_h[i].astype(BF16), w_br_a[i].astype(BF16),
                    w_out[i].astype(BF16), g1, L)

        xt = _moe_layer(xt, norm2_g[i].reshape(1, D), sc2, sh2, g2, router_wt, router_b,
                        moe_w1[i].astype(BF16), moe_w3[i].astype(BF16), moe_w2[i].astype(BF16), L)
    return _final_norm(xt, final_g.reshape(1, D)).reshape(B, L, D)
```

```python
import math

import jax
import jax.numpy as jnp
import numpy as np
from jax import lax
from jax.experimental import pallas as pl
from jax.experimental.pallas import tpu as pltpu

D_MODEL = 1024
DEPTH = 2
HY_WIDTH = 768
HY_EMB = 33
HY_FFN = 64
HY_FAST_DECAY_PCT = 0.3
HY_SLOW_DECAY_PCT = 1.5
HY_TARGET = 1e-2
HEAD_DIM = 64
ATT_GROUPS = ((128, 1), (512, 4), (2048, 16))
HEADS_PER_GROUP = 4
N_HEADS = HEADS_PER_GROUP * len(ATT_GROUPS)
ATT_WIDTH = N_HEADS * HEAD_DIM
ATT_OUT = HEADS_PER_GROUP * HEAD_DIM
ROPE_THETA = 10000.0
IN_SPLITS = (3 * HY_WIDTH, ATT_WIDTH, ATT_WIDTH, ATT_WIDTH, D_MODEL, D_MODEL)
N_EXPERTS = 16
N_GROUPS = 4
EXPERTS_PER_GROUP = N_EXPERTS // N_GROUPS
D_EXPERT = 512
NORM_EPS = 1e-6
MASK_VALUE = -1e30

LANES = 128
MXU_DIM = 256
VMEM_LIMIT_BYTES = 56 * 1024 * 1024

F32 = jnp.float32
BF16 = jnp.bfloat16
HIGHEST = lax.Precision.HIGHEST

PROJ_TM = 2048
PROJ_TN = MXU_DIM
HY_COLS = 3 * HY_WIDTH
QKV_COLS = 3 * ATT_WIDTH
GATE_COLS2 = 2 * D_MODEL

FFT_R = 128
FFT_KH = FFT_R // 2 + 1
FFT_KP = 72
FFT_SLAB = 2 * FFT_KP
CONV_CB = 128
FFT_UNROLL_TIME = 8
FFT_UNROLL_FREQ = 5

PAIRS_PER_GROUP = 6
N_BUCKETS = N_GROUPS * PAIRS_PER_GROUP
BUCKET_ROWS = 32
ROUTER_TM = 512
EXPERT_TM = 256
ROW_BLOCK = 1024
GATE_COLS = LANES


def _params(*sem):
    return pltpu.CompilerParams(dimension_semantics=sem, vmem_limit_bytes=VMEM_LIMIT_BYTES)


def _sigmoid(x):
    return 1.0 / (1.0 + jnp.exp(-x))


def _ada_kernel(c_ref, w_ref, b_ref, o_ref):
    c = c_ref[...]
    c_act = c * _sigmoid(c)
    o_ref[0] = jnp.dot(c_act, w_ref[0], precision=HIGHEST, preferred_element_type=F32) + b_ref[0]


def _ada(c_pad, w_ada, b_ada):
    depth, D, N = w_ada.shape
    rows = c_pad.shape[0]
    tn = N // 4
    return pl.pallas_call(
        _ada_kernel,
        out_shape=jax.ShapeDtypeStruct((depth, rows, N), F32),
        grid=(depth, N // tn),
        in_specs=[pl.BlockSpec((rows, D), lambda l, j: (0, 0)),
                  pl.BlockSpec((1, D, tn), lambda l, j: (l, 0, j)),
                  pl.BlockSpec((1, 1, tn), lambda l, j: (l, 0, j))],
        out_specs=pl.BlockSpec((1, rows, tn), lambda l, j: (l, 0, j)),
        compiler_params=_params("arbitrary", "arbitrary"),
        name="ada_mod",
    )(c_pad, w_ada, b_ada.reshape(depth, 1, N))


def _norm_mod_kernel(x_ref, g_ref, sc_ref, sh_ref, o_ref):
    x = x_ref[...]
    r = lax.rsqrt(jnp.mean(x * x, axis=-1, keepdims=True) + NORM_EPS)
    h = (x * r) * g_ref[...]
    o_ref[...] = (h * (1.0 + sc_ref[0]) + sh_ref[0]).astype(o_ref.dtype)


def _norm_mod(x, g, sc, sh, seq_len):
    T, D = x.shape
    tm = 1024
    per_batch = seq_len // tm
    return pl.pallas_call(
        _norm_mod_kernel,
        out_shape=jax.ShapeDtypeStruct((T, D), BF16),
        grid=(T // tm,),
        in_specs=[pl.BlockSpec((tm, D), lambda i: (i, 0)),
                  pl.BlockSpec((1, D), lambda i: (0, 0)),
                  pl.BlockSpec((1, 1, D), lambda i: (i // per_batch, 0, 0)),
                  pl.BlockSpec((1, 1, D), lambda i: (i // per_batch, 0, 0))],
        out_specs=pl.BlockSpec((tm, D), lambda i: (i, 0)),
        compiler_params=_params("arbitrary"),
        name="norm_mod",
    )(x, g, sc, sh)


def _proj_kernel(h_ref, w_ref, o_ref):
    o_ref[...] = jnp.dot(h_ref[...], w_ref[...], preferred_element_type=F32).astype(o_ref.dtype)


def _proj_rope_kernel(h_ref, w_ref, cos_ref, sin_ref, rot_ref, o_ref, *, n_q_tiles):
    j = pl.program_id(1)
    acc = jnp.dot(h_ref[...], w_ref[...], preferred_element_type=F32)

    @pl.when(j < 2 * n_q_tiles)
    def _():
        cos = jnp.concatenate([cos_ref[...], cos_ref[...]], axis=1)
        sin = jnp.concatenate([sin_ref[...], sin_ref[...]], axis=1)
        swapped = jnp.dot(acc.astype(BF16), rot_ref[...], preferred_element_type=F32)
        out = acc * cos + swapped * sin
        scale = jnp.where(j < n_q_tiles, HEAD_DIM ** -0.5, 1.0)
        o_ref[...] = (out * scale).astype(o_ref.dtype)

    @pl.when(j >= 2 * n_q_tiles)
    def _():
        o_ref[...] = acc.astype(o_ref.dtype)


def _proj(h, w, col0, n_cols, rope=None, seq_len=None, col_step=1):
    T, D = h.shape
    tm, tn = PROJ_TM, PROJ_TN
    off = col0 // tn
    grid = (T // tm, n_cols // tn)
    h_spec = pl.BlockSpec((tm, D), lambda i, j: (i, 0))
    w_spec = pl.BlockSpec((D, tn), lambda i, j: (0, off + j * col_step))
    o_spec = pl.BlockSpec((tm, tn), lambda i, j: (i, j))
    out_shape = jax.ShapeDtypeStruct((T, n_cols), BF16)
    if rope is None:
        return pl.pallas_call(_proj_kernel, out_shape=out_shape, grid=grid, in_specs=[h_spec, w_spec],
                              out_specs=o_spec, compiler_params=_params("arbitrary", "arbitrary"),
                              name="proj")(h, w)
    cos_t, sin_t, rot = rope
    per_batch = seq_len // tm
    tab_spec = pl.BlockSpec((tm, LANES), lambda i, j: (i % per_batch, 0))
    kern = lambda *refs: _proj_rope_kernel(*refs, n_q_tiles=1)
    return pl.pallas_call(kern, out_shape=out_shape, grid=grid,
                          in_specs=[h_spec, w_spec, tab_spec, tab_spec,
                                    pl.BlockSpec((tn, tn), lambda i, j: (0, 0))],
                          out_specs=o_spec, compiler_params=_params("arbitrary", "arbitrary"),
                          name="proj_rope")(h, w, cos_t, sin_t, rot)


def _rope_tables(seq_len):
    half = HEAD_DIM // 2
    inv = ROPE_THETA ** (-jnp.arange(half, dtype=F32) / half)
    ang = jnp.arange(seq_len, dtype=F32)[:, None] * inv[None, :]
    reps = LANES // half
    cos_t = jnp.tile(jnp.cos(ang), (1, reps))
    sin_t = jnp.tile(jnp.sin(ang), (1, reps))
    rot = np.zeros((PROJ_TN, PROJ_TN), np.float32)
    for j in range(PROJ_TN):
        if j % HEAD_DIM < half:
            rot[j + half, j] = -1.0
        else:
            rot[j - half, j] = 1.0
    return cos_t, sin_t, jnp.asarray(rot, BF16)


def _hyena_filter_kernel(feat_ref, w1_ref, b1_ref, w2_ref, b2_ref, w3_ref, b3_ref, w4_ref, fr_ref,
                         dl_ref, o_ref, *, seq_len):
    i = pl.program_id(0)
    tm = feat_ref.shape[0]
    z = feat_ref[...]
    fr = fr_ref[...]
    h = jnp.sin(fr * (jnp.dot(z, w1_ref[...], precision=HIGHEST, preferred_element_type=F32) + b1_ref[...]))
    h = jnp.sin(fr * (jnp.dot(h, w2_ref[...], precision=HIGHEST, preferred_element_type=F32) + b2_ref[...]))
    h = jnp.sin(fr * (jnp.dot(h, w3_ref[...], precision=HIGHEST, preferred_element_type=F32) + b3_ref[...]))
    k = jnp.dot(h, w4_ref[...], precision=HIGHEST, preferred_element_type=F32)
    decay = jnp.exp(-z[:, 0:1] * jnp.abs(dl_ref[...]))
    n = i * tm + lax.broadcasted_iota(jnp.int32, (tm, 1), 0)
    o_ref[...] = jnp.where(n == seq_len, 0.0, k * decay)


def _hyena_filter(feat, w1, b1, w2, b2, w3, b3, w4, freq, deltas, seq_len):
    n_rows = feat.shape[0]
    C = HY_WIDTH
    tm = 1024
    fwd_tiles = seq_len // tm
    full = lambda i: (0, 0)
    kern = lambda *refs: _hyena_filter_kernel(*refs, seq_len=seq_len)
    return pl.pallas_call(
        kern,
        out_shape=jax.ShapeDtypeStruct((n_rows, C), F32),
        grid=(n_rows // tm,),
        in_specs=[pl.BlockSpec((tm, LANES), lambda i: (i, 0)),
                  pl.BlockSpec((LANES, LANES), full), pl.BlockSpec((1, LANES), full),
                  pl.BlockSpec((LANES, LANES), full), pl.BlockSpec((1, LANES), full),
                  pl.BlockSpec((LANES, LANES), full), pl.BlockSpec((1, LANES), full),
                  pl.BlockSpec((LANES, C), lambda i: (0, i // fwd_tiles)),
                  pl.BlockSpec((1, LANES), full),
                  pl.BlockSpec((1, C), full)],
        out_specs=pl.BlockSpec((tm, C), lambda i: (i, 0)),
        compiler_params=_params("arbitrary"),
        name="hyena_filter",
    )(feat, w1, b1, w2, b2, w3, b3, w4, freq, deltas)


def _hyena_features(seq_len):
    L = seq_len
    n = np.arange(2 * L)
    m = np.where(n <= L, n, 2 * L - n)
    m = np.where(n == L, 0, m)
    t = jnp.linspace(0.0, 1.0, L, dtype=F32)[:, None]
    bands = (HY_EMB - 1) // 2
    w = 2.0 * math.pi * jnp.arange(L, dtype=F32)[:, None] / L
    f = jnp.linspace(1e-4, bands - 1, bands, dtype=F32)[None, :]
    z = jnp.concatenate([t, jnp.cos(f * w), -jnp.sin(f * w)], axis=-1)
    z = jnp.pad(z, ((0, 0), (0, LANES - HY_EMB)))
    max_decay = math.log(HY_TARGET) / HY_FAST_DECAY_PCT
    min_decay = math.log(HY_TARGET) / HY_SLOW_DECAY_PCT
    deltas = jnp.linspace(min_decay, max_decay, HY_WIDTH, dtype=F32)[None, :]
    return z[jnp.asarray(m)], deltas


def _pad2(a, rows, cols):
    return jnp.pad(a, ((0, rows - a.shape[0]), (0, cols - a.shape[1])))


def _dft_matrices():
    R, KH, KP = FFT_R, FFT_KH, FFT_KP
    N = R * R
    n1 = np.arange(R)[:, None, None]
    k2 = np.arange(KP)[None, :, None]
    n2 = np.arange(R)[None, None, :]
    phase = 2.0 * np.pi * ((n2 * k2 % R) / R + (n1 * k2) / N)
    live = (k2 < KH)
    a1 = np.concatenate([np.cos(phase) * live, -np.sin(phase) * live], axis=1)
    wgt = np.where((k2 == 0) | (k2 == R // 2), 1.0, 2.0) * live / N
    b1 = np.concatenate([np.cos(phase) * wgt, -np.sin(phase) * wgt], axis=1)
    b1 = np.transpose(b1, (0, 2, 1))[:, :R // 2, :]
    th = 2.0 * np.pi * (np.arange(R)[:, None] * np.arange(R)[None, :] % R) / R
    c, s = np.cos(th), np.sin(th)
    w2f = np.block([[c, s], [-s, c]])
    w2i = np.block([[c, -s], [s, c]])
    as_bf = lambda a: jnp.asarray(a.astype(np.float32)).astype(BF16)
    return as_bf(a1), as_bf(a1[:, :, :R // 2]), as_bf(b1), as_bf(w2f), as_bf(w2i)


def _fft_stage1(src_ref, a1_ref, g_ref, n_rows):
    def body(n1, carry):
        xs = src_ref[pl.ds(n1, n_rows, stride=FFT_R), :].astype(BF16)
        slab = jnp.dot(a1_ref[n1], xs, preferred_element_type=F32)
        g_ref[pl.ds(pl.multiple_of(n1 * FFT_SLAB, 8), FFT_SLAB), :] = slab
        return carry
    lax.fori_loop(0, FFT_R, body, 0, unroll=FFT_UNROLL_TIME)


def _load_freq_rows(g_ref, k2):
    re = g_ref[pl.ds(k2, FFT_R, stride=FFT_SLAB), :]
    im = g_ref[pl.ds(FFT_KP + k2, FFT_R, stride=FFT_SLAB), :]
    return jnp.concatenate([re, im], axis=0)


def _filter_fft_kernel(kern_ref, a1_ref, w2_ref, o_ref, g_ref):
    _fft_stage1(kern_ref, a1_ref, g_ref, FFT_R)

    def body(k2, carry):
        gk = _load_freq_rows(g_ref, k2).astype(BF16)
        o_ref[k2] = jnp.dot(w2_ref[...], gk, preferred_element_type=F32).astype(o_ref.dtype)
        return carry
    lax.fori_loop(0, FFT_KH, body, 0, unroll=FFT_UNROLL_FREQ)


def _filter_fft(kern, a1_full, w2f):
    n_rows, C = kern.shape
    cb = CONV_CB
    return pl.pallas_call(
        _filter_fft_kernel,
        out_shape=jax.ShapeDtypeStruct((FFT_KH, 2 * FFT_R, C), BF16),
        grid=(C // cb,),
        in_specs=[pl.BlockSpec((n_rows, cb), lambda j: (0, j)),
                  pl.BlockSpec((FFT_R, FFT_SLAB, FFT_R), lambda j: (0, 0, 0)),
                  pl.BlockSpec((2 * FFT_R, 2 * FFT_R), lambda j: (0, 0))],
        out_specs=pl.BlockSpec((FFT_KH, 2 * FFT_R, cb), lambda j: (0, 0, j)),
        scratch_shapes=[pltpu.VMEM((FFT_R * FFT_SLAB, cb), F32)],
        compiler_params=_params("arbitrary"),
        name="hyena_filter_fft",
    )(kern, a1_full, w2f)


def _short_conv_kernel(x0_ref, x1_ref, v_ref, p0_ref, p1_ref, pv_ref, n0_ref, n1_ref, nv_ref,
                       w0_ref, w1_ref, wv_ref, b0_ref, b1_ref, bv_ref, x0c_ref, zv_ref, scr, *, tiles_per_seq):
    i = pl.program_id(0)
    tm = x0_ref.shape[0]
    first = (i % tiles_per_seq) == 0
    last = (i % tiles_per_seq) == tiles_per_seq - 1
    halo = p0_ref.shape[0]

    def conv(u_ref, prev_ref, next_ref, w_ref, b_ref):
        prev_row = jnp.where(first, 0.0, prev_ref[halo - 1:halo, :].astype(F32))
        next_row = jnp.where(last, 0.0, next_ref[0:1, :].astype(F32))
        scr[7:8, :] = prev_row
        scr[8:8 + tm, :] = u_ref[...].astype(F32)
        scr[8 + tm:9 + tm, :] = next_row
        w = w_ref[...]
        return (scr[pl.ds(7, tm), :] * w[0:1] + scr[pl.ds(8, tm), :] * w[1:2]
                + scr[pl.ds(9, tm), :] * w[2:3] + b_ref[...])

    x0c_ref[...] = conv(x0_ref, p0_ref, n0_ref, w0_ref, b0_ref).astype(x0c_ref.dtype)
    x1c = conv(x1_ref, p1_ref, n1_ref, w1_ref, b1_ref)
    vc = conv(v_ref, pv_ref, nv_ref, wv_ref, bv_ref)
    zv_ref[...] = vc * x1c


def _short_conv(u, w_sc, b_sc, seq_len):
    T = u.shape[0]
    C = HY_WIDTH
    tm, cb, halo = 1024, 256, 16
    ncb = C // cb
    tiles_per_seq = seq_len // tm
    hb = tm // halo
    n_halo = T // halo

    def part(p):
        return pl.BlockSpec((tm, cb), lambda i, j: (i, p * ncb + j))

    def prev(p):
        return pl.BlockSpec((halo, cb), lambda i, j: (jnp.maximum(i * hb - 1, 0), p * ncb + j))

    def nxt(p):
        return pl.BlockSpec((halo, cb), lambda i, j: (jnp.minimum((i + 1) * hb, n_halo - 1), p * ncb + j))

    def wpart(p):
        return pl.BlockSpec((3, cb), lambda i, j: (0, p * ncb + j))

    def bpart(p):
        return pl.BlockSpec((1, cb), lambda i, j: (0, p * ncb + j))

    kern = lambda *refs: _short_conv_kernel(*refs, tiles_per_seq=tiles_per_seq)
    b2 = b_sc.reshape(1, 3 * C)
    return pl.pallas_call(
        kern,
        out_shape=(jax.ShapeDtypeStruct((T, C), BF16), jax.ShapeDtypeStruct((T, C), F32)),
        grid=(T // tm, ncb),
        in_specs=[part(0), part(1), part(2), prev(0), prev(1), prev(2), nxt(0), nxt(1), nxt(2),
                  wpart(0), wpart(1), wpart(2), bpart(0), bpart(1), bpart(2)],
        out_specs=(pl.BlockSpec((tm, cb), lambda i, j: (i, j)), pl.BlockSpec((tm, cb), lambda i, j: (i, j))),
        scratch_shapes=[pltpu.VMEM((tm + 16, cb), F32)],
        compiler_params=_params("arbitrary", "arbitrary"),
        name="hyena_short_conv",
    )(u, u, u, u, u, u, u, u, u, w_sc, w_sc, w_sc, b2, b2, b2)


def _long_conv_kernel(zv_ref, x0_ref, kf_ref, bias_ref, a1_ref, b1_ref, w2f_ref, w2i_ref, o_ref, g_ref):
    seq_len = zv_ref.shape[1]
    n2_rows = seq_len // FFT_R
    zv2 = zv_ref.at[0]
    o2 = o_ref.at[0]
    _fft_stage1(zv2, a1_ref, g_ref, n2_rows)

    def freq_body(k2, carry):
        gk = _load_freq_rows(g_ref, k2).astype(BF16)
        x = jnp.dot(w2f_ref[...], gk, preferred_element_type=F32)
        kf = kf_ref[k2].astype(F32)
        xr, xi = x[:FFT_R], x[FFT_R:]
        kr, ki = kf[:FFT_R], kf[FFT_R:]
        p = jnp.concatenate([xr * kr - xi * ki, xr * ki + xi * kr], axis=0).astype(BF16)
        hk = jnp.dot(w2i_ref[...], p, preferred_element_type=F32)
        g_ref[pl.ds(k2, FFT_R, stride=FFT_SLAB), :] = hk[:FFT_R]
        g_ref[pl.ds(FFT_KP + k2, FFT_R, stride=FFT_SLAB), :] = hk[FFT_R:]
        return carry
    lax.fori_loop(0, FFT_KH, freq_body, 0, unroll=FFT_UNROLL_FREQ)

    def time_body(n1, carry):
        slab = g_ref[pl.ds(pl.multiple_of(n1 * FFT_SLAB, 8), FFT_SLAB), :].astype(BF16)
        o2[pl.ds(n1, n2_rows, stride=FFT_R), :] = jnp.dot(b1_ref[n1], slab, preferred_element_type=F32)
        return carry
    lax.fori_loop(0, FFT_R, time_body, 0, unroll=FFT_UNROLL_TIME)

    chunk = 512
    bias = bias_ref[...]

    def out_body(c, carry):
        rows = pl.ds(pl.multiple_of(c * chunk, chunk), chunk)
        o2[rows, :] = (o2[rows, :] + zv2[rows, :] * bias) * x0_ref[0, rows, :].astype(F32)
        return carry
    lax.fori_loop(0, seq_len // chunk, out_body, 0)


def _long_conv(zv, x0c, kf, hy_bias, mats):
    B, L, C = zv.shape
    cb = CONV_CB
    _, a1_half, b1, w2f, w2i = mats
    blk = lambda j, b: (b, 0, j)
    return pl.pallas_call(
        _long_conv_kernel,
        out_shape=jax.ShapeDtypeStruct((B, L, C), F32),
        grid=(C // cb, B),
        in_specs=[pl.BlockSpec((1, L, cb), blk),
                  pl.BlockSpec((1, L, cb), blk),
                  pl.BlockSpec((FFT_KH, 2 * FFT_R, cb), lambda j, b: (0, 0, j)),
                  pl.BlockSpec((1, cb), lambda j, b: (0, j)),
                  pl.BlockSpec((FFT_R, FFT_SLAB, FFT_R // 2), lambda j, b: (0, 0, 0)),
                  pl.BlockSpec((FFT_R, FFT_R // 2, FFT_SLAB), lambda j, b: (0, 0, 0)),
                  pl.BlockSpec((2 * FFT_R, 2 * FFT_R), lambda j, b: (0, 0)),
                  pl.BlockSpec((2 * FFT_R, 2 * FFT_R), lambda j, b: (0, 0))],
        out_specs=pl.BlockSpec((1, L, cb), blk),
        scratch_shapes=[pltpu.VMEM((FFT_R * FFT_SLAB, cb), F32)],
        compiler_params=_params("arbitrary", "arbitrary"),
        name="hyena_long_conv",
    )(zv, x0c, kf, hy_bias.reshape(1, C), a1_half, b1, w2f, w2i)


ATT_TQ = 128
ATT_RADIUS = 64
ATT_WINDOW = ATT_TQ + 2 * ATT_RADIUS


ATT_SUB = 4


def _attn_kernel(q_ref, k_ref, v_ref, o_ref, lse_ref):
    i = pl.program_id(2)
    ls = k_ref.shape[1]
    nh = HEADS_PER_GROUP
    head_of_col = lax.broadcasted_iota(jnp.int32, (1, ATT_OUT), 1) // HEAD_DIM
    row_iota = lax.broadcasted_iota(jnp.int32, (nh * ATT_TQ, ATT_WINDOW), 0) % ATT_TQ
    col_iota = lax.broadcasted_iota(jnp.int32, (nh * ATT_TQ, ATT_WINDOW), 1)
    rel = row_iota - col_iota
    for sub in range(ATT_SUB):
        q0 = (i * ATT_SUB + sub) * ATT_TQ
        rows = slice(sub * ATT_TQ, (sub + 1) * ATT_TQ)
        q = q_ref[0, rows, :]
        start = jnp.clip(q0 - ATT_RADIUS, 0, ls - ATT_WINDOW)
        start = pl.multiple_of(start, ATT_RADIUS)
        kw = k_ref[0, pl.ds(start, ATT_WINDOW), :]
        vw = v_ref[0, pl.ds(start, ATT_WINDOW), :]
        band = jnp.abs((q0 - start) + rel) <= ATT_RADIUS
        zero = jnp.zeros_like(q)
        q4 = jnp.concatenate([jnp.where(head_of_col == h, q, zero) for h in range(nh)], axis=0)
        s = lax.dot_general(q4, kw, (((1,), (1,)), ((), ())), preferred_element_type=F32)
        s = jnp.where(band, s, MASK_VALUE)
        m = jnp.max(s, axis=-1, keepdims=True)
        p = jnp.exp(s - m)
        den = jnp.sum(p, axis=-1, keepdims=True)
        pv = jnp.dot(p.astype(BF16), vw, preferred_element_type=F32) / den
        lse4 = m + jnp.log(den)
        out = jnp.zeros((ATT_TQ, ATT_OUT), F32)
        lse = jnp.zeros((ATT_TQ, ATT_OUT), F32)
        for h in range(nh):
            mine = head_of_col == h
            hrows = slice(h * ATT_TQ, (h + 1) * ATT_TQ)
            out = jnp.where(mine, pv[hrows], out)
            lse = jnp.where(mine, lse4[hrows], lse)
        o_ref[0, rows, :] = out.astype(o_ref.dtype)
        lse_ref[0, rows, :] = lse


def _attention_group(qkv_g, dil, batch, seq_len):
    ls = seq_len // dil
    tq = ATT_SUB * ATT_TQ
    view = qkv_g.reshape(batch, ls, dil * 3 * ATT_OUT)

    def col(which):
        return lambda b, r, i: (b, 0, r * 3 + which)

    q_map = lambda b, r, i: (b, i, r * 3)
    o_map = lambda b, r, i: (b, i, r)
    o, lse = pl.pallas_call(
        _attn_kernel,
        out_shape=(jax.ShapeDtypeStruct((batch, ls, dil * ATT_OUT), BF16),
                   jax.ShapeDtypeStruct((batch, ls, dil * ATT_OUT), F32)),
        grid=(batch, dil, ls // tq),
        in_specs=[pl.BlockSpec((1, tq, ATT_OUT), q_map),
                  pl.BlockSpec((1, ls, ATT_OUT), col(1)),
                  pl.BlockSpec((1, ls, ATT_OUT), col(2))],
        out_specs=(pl.BlockSpec((1, tq, ATT_OUT), o_map), pl.BlockSpec((1, tq, ATT_OUT), o_map)),
        compiler_params=_params("arbitrary", "arbitrary", "arbitrary"),
        name=f"dilated_attn_d{dil}",
    )(view, view, view)
    T = batch * seq_len
    return o.reshape(T, ATT_OUT), lse.reshape(T, ATT_OUT)


def _merge_kernel(x_ref, yhy_ref, o1_ref, o2_ref, o3_ref, l1_ref, l2_ref, l3_ref, gh_ref, ga_ref,
                  wh_ref, wa_ref, wo_ref, g1_ref, out_ref):
    l1, l2, l3 = l1_ref[...], l2_ref[...], l3_ref[...]
    m = jnp.maximum(jnp.maximum(l1, l2), l3)
    e1, e2, e3 = jnp.exp(l1 - m), jnp.exp(l2 - m), jnp.exp(l3 - m)
    tot = e1 + e2 + e3
    y_at = (e1 * o1_ref[...].astype(F32) + e2 * o2_ref[...].astype(F32) + e3 * o3_ref[...].astype(F32)) / tot
    a = jnp.dot(yhy_ref[...].astype(BF16), wh_ref[...], preferred_element_type=F32)
    b = jnp.dot(y_at.astype(BF16), wa_ref[...], preferred_element_type=F32)
    merged = _sigmoid(gh_ref[...].astype(F32)) * a + _sigmoid(ga_ref[...].astype(F32)) * b
    upd = jnp.dot(merged.astype(BF16), wo_ref[...], preferred_element_type=F32)
    out_ref[...] = x_ref[...] + g1_ref[0] * upd


def _merge(x, y_hy, attn, gates, w_br_h, w_br_a, w_out, g1, seq_len):
    T, D = x.shape
    tm = 512
    per_batch = seq_len // tm
    (o1, l1), (o2, l2), (o3, l3) = attn
    row = lambda w: pl.BlockSpec((tm, w), lambda i: (i, 0))
    full = lambda a: pl.BlockSpec(a.shape, lambda i: (0, 0))
    return pl.pallas_call(
        _merge_kernel,
        out_shape=jax.ShapeDtypeStruct((T, D), F32),
        grid=(T // tm,),
        in_specs=[row(D), row(HY_WIDTH), row(ATT_OUT), row(ATT_OUT), row(ATT_OUT),
                  row(ATT_OUT), row(ATT_OUT), row(ATT_OUT),
                  pl.BlockSpec((tm, D), lambda i: (i, 0)), pl.BlockSpec((tm, D), lambda i: (i, 1)),
                  full(w_br_h), full(w_br_a), full(w_out),
                  pl.BlockSpec((1, 1, D), lambda i: (i // per_batch, 0, 0))],
        out_specs=row(D),
        compiler_params=_params("arbitrary"),
        name="mixer_merge",
    )(x, y_hy, o1, o2, o3, l1, l2, l3, gates, gates, w_br_h, w_br_a, w_out, g1)


def _router_kernel(x_ref, g_ref, sc_ref, sh_ref, rwt_ref, rb_ref, hg_ref, stats_ref, cnt_ref, base_ref):
    i = pl.program_id(0)
    tm = x_ref.shape[0]

    @pl.when(i == 0)
    def _():
        base_ref[...] = jnp.zeros_like(base_ref)

    x = x_ref[...]
    r = lax.rsqrt(jnp.mean(x * x, axis=-1, keepdims=True) + NORM_EPS)
    h = (x * r) * g_ref[...]
    h = h * (1.0 + sc_ref[0]) + sh_ref[0]

    logits = lax.dot_general(rwt_ref[...], h, (((1,), (1,)), ((), ())),
                             precision=HIGHEST, preferred_element_type=F32)
    scores = _sigmoid(logits)
    biased = scores + rb_ref[...]

    def row(a, k):
        return a[k:k + 1, :]

    sel = jnp.zeros((1, tm), jnp.int32)
    best = None
    for g in range(N_GROUPS):
        a, b, c, d = (row(biased, 4 * g + k) for k in range(4))
        m_ab, n_ab = jnp.maximum(a, b), jnp.minimum(a, b)
        m_cd, n_cd = jnp.maximum(c, d), jnp.minimum(c, d)
        gs = jnp.maximum(m_ab, m_cd) + jnp.maximum(jnp.minimum(m_ab, m_cd), jnp.maximum(n_ab, n_cd))
        if g == 0:
            best = gs
        else:
            better = gs > best
            sel = jnp.where(better, g, sel)
            best = jnp.where(better, gs, best)

    v, u = [], []
    for k in range(EXPERTS_PER_GROUP):
        vk = jnp.zeros((1, tm), F32)
        uk = jnp.zeros((1, tm), F32)
        for g in range(N_GROUPS):
            vk = jnp.where(sel == g, row(biased, 4 * g + k), vk)
            uk = jnp.where(sel == g, row(scores, 4 * g + k), uk)
        v.append(vk)
        u.append(uk)

    i1 = jnp.zeros((1, tm), jnp.int32)
    b1 = v[0]
    for k in range(1, EXPERTS_PER_GROUP):
        gt = v[k] > b1
        i1 = jnp.where(gt, k, i1)
        b1 = jnp.where(gt, v[k], b1)
    i2 = jnp.zeros((1, tm), jnp.int32)
    b2 = jnp.full((1, tm), -jnp.inf, F32)
    for k in range(EXPERTS_PER_GROUP):
        cand = (i1 != k) & (v[k] > b2)
        i2 = jnp.where(cand, k, i2)
        b2 = jnp.where(cand, v[k], b2)

    lo = jnp.minimum(i1, i2)
    hi = jnp.maximum(i1, i2)
    pair = jnp.where(lo == 0, hi - 1, jnp.where(lo == 1, hi + 1, 5))
    bucket = sel * PAIRS_PER_GROUP + pair

    u_lo = jnp.zeros((1, tm), F32)
    u_hi = jnp.zeros((1, tm), F32)
    for k in range(EXPERTS_PER_GROUP):
        u_lo = jnp.where(lo == k, u[k], u_lo)
        u_hi = jnp.where(hi == k, u[k], u_hi)
    tot = u_lo + u_hi
    w_lo = u_lo / tot
    w_hi = u_hi / tot

    rows = lax.broadcasted_iota(jnp.int32, (BUCKET_ROWS, tm), 0)
    onehot = (rows == bucket).astype(F32)
    t_src = lax.broadcasted_iota(jnp.int32, (tm, tm), 0)
    t_dst = lax.broadcasted_iota(jnp.int32, (tm, tm), 1)
    before = (t_src < t_dst).astype(BF16)
    cum = jnp.dot(onehot.astype(BF16), before, preferred_element_type=F32)
    base = base_ref[...]
    rank = jnp.sum(onehot * (cum + base), axis=0, keepdims=True)
    base = base + jnp.sum(onehot, axis=1, keepdims=True)
    base_ref[...] = base
    cnt_ref[...] = jnp.broadcast_to(base, cnt_ref.shape)

    srow = lax.broadcasted_iota(jnp.int32, (8, tm), 0)
    stats_ref[...] = jnp.where(srow == 0, bucket.astype(F32), jnp.where(srow == 1, rank, 0.0))

    grow = lax.broadcasted_iota(jnp.int32, (GATE_COLS, tm), 0)
    gates_t = jnp.where(grow == 0, w_lo, jnp.where(grow == 1, w_hi, 0.0))
    hg_ref[:, :D_MODEL] = h
    hg_ref[:, D_MODEL:] = gates_t.T


def _router(x, norm_g, sc, sh, router_wt, router_b, seq_len):
    T, D = x.shape
    tm = ROUTER_TM
    per_batch = seq_len // tm
    return pl.pallas_call(
        _router_kernel,
        out_shape=(jax.ShapeDtypeStruct((T, D + GATE_COLS), F32),
                   jax.ShapeDtypeStruct((8, T), F32),
                   jax.ShapeDtypeStruct((BUCKET_ROWS, LANES), F32)),
        grid=(T // tm,),
        in_specs=[pl.BlockSpec((tm, D), lambda i: (i, 0)),
                  pl.BlockSpec((1, D), lambda i: (0, 0)),
                  pl.BlockSpec((1, 1, D), lambda i: (i // per_batch, 0, 0)),
                  pl.BlockSpec((1, 1, D), lambda i: (i // per_batch, 0, 0)),
                  pl.BlockSpec((N_EXPERTS, D), lambda i: (0, 0)),
                  pl.BlockSpec((N_EXPERTS, 1), lambda i: (0, 0))],
        out_specs=(pl.BlockSpec((tm, D + GATE_COLS), lambda i: (i, 0)),
                   pl.BlockSpec((8, tm), lambda i: (0, i)),
                   pl.BlockSpec((BUCKET_ROWS, LANES), lambda i: (0, 0))),
        scratch_shapes=[pltpu.VMEM((BUCKET_ROWS, 1), F32)],
        compiler_params=_params("arbitrary"),
        name="moe_router",
    )(x, norm_g, sc, sh, router_wt, router_b)


def _dispatch_kernel(pos_ref, hg_ref, xs_init_hbm, xs_hbm, sem):
    del xs_init_hbm

    def issue(r, carry):
        pltpu.make_async_copy(hg_ref.at[pl.ds(r, 1), :], xs_hbm.at[pl.ds(pos_ref[r], 1), :], sem).start()
        return carry

    lax.fori_loop(0, ROW_BLOCK, issue, 0, unroll=8)
    pltpu.make_async_copy(hg_ref, xs_hbm.at[pl.ds(0, ROW_BLOCK), :], sem).wait()


def _dispatch(pos, hg, n_rows):
    T, W = hg.shape
    zeros = jnp.zeros((n_rows, W), F32)
    return pl.pallas_call(
        _dispatch_kernel,
        out_shape=jax.ShapeDtypeStruct((n_rows, W), F32),
        grid=(T // ROW_BLOCK,),
        in_specs=[pl.BlockSpec((ROW_BLOCK,), lambda i: (i,), memory_space=pltpu.SMEM),
                  pl.BlockSpec((ROW_BLOCK, W), lambda i: (i, 0)),
                  pl.BlockSpec(memory_space=pl.ANY)],
        out_specs=pl.BlockSpec(memory_space=pl.ANY),
        scratch_shapes=[pltpu.SemaphoreType.DMA],
        input_output_aliases={2: 0},
        compiler_params=_params("arbitrary"),
        name="moe_dispatch",
    )(pos, hg, zeros)


def _expert_kernel(e_lo_ref, e_hi_ref, n_used_ref, xs_ref, w1a, w3a, w2a, w1b, w3b, w2b, y_ref):
    del e_lo_ref, e_hi_ref
    used = pl.program_id(0) < n_used_ref[0]

    @pl.when(jnp.logical_not(used))
    def _():
        y_ref[...] = jnp.zeros_like(y_ref)

    @pl.when(used)
    def _():
        xb = xs_ref[:, :D_MODEL].astype(BF16)
        g_lo = xs_ref[:, D_MODEL:D_MODEL + 1]
        g_hi = xs_ref[:, D_MODEL + 1:D_MODEL + 2]

        def ffn(w1, w3, w2):
            a = jnp.dot(xb, w1[0], preferred_element_type=F32)
            b = jnp.dot(xb, w3[0], preferred_element_type=F32)
            act = (a * _sigmoid(a)) * b
            return jnp.dot(act.astype(BF16), w2[0], preferred_element_type=F32)

        y_ref[...] = g_lo * ffn(w1a, w3a, w2a) + g_hi * ffn(w1b, w3b, w2b)


def _experts(tile_lo, tile_hi, n_used, xs, w1, w3, w2):
    n_rows, W = xs.shape
    D, F = w1.shape[1], w1.shape[2]
    n_tiles = n_rows // EXPERT_TM

    def x_map(j, lo, hi, nu):
        return (jnp.minimum(j, nu[0] - 1), 0)

    def w_lo_map(j, lo, hi, nu):
        return (lo[j], 0, 0)

    def w_hi_map(j, lo, hi, nu):
        return (hi[j], 0, 0)

    grid_spec = pltpu.PrefetchScalarGridSpec(
        num_scalar_prefetch=3,
        grid=(n_tiles,),
        in_specs=[pl.BlockSpec((EXPERT_TM, W), x_map),
                  pl.BlockSpec((1, D, F), w_lo_map), pl.BlockSpec((1, D, F), w_lo_map),
                  pl.BlockSpec((1, F, D), w_lo_map),
                  pl.BlockSpec((1, D, F), w_hi_map), pl.BlockSpec((1, D, F), w_hi_map),
                  pl.BlockSpec((1, F, D), w_hi_map)],
        out_specs=pl.BlockSpec((EXPERT_TM, D), lambda j, lo, hi, nu: (j, 0)),
    )
    return pl.pallas_call(
        _expert_kernel,
        out_shape=jax.ShapeDtypeStruct((n_rows, D), F32),
        grid_spec=grid_spec,
        compiler_params=_params("arbitrary"),
        name="moe_experts",
    )(tile_lo, tile_hi, n_used, xs, w1, w3, w2, w1, w3, w2)


def _combine_kernel(pos_ref, x_ref, g2_ref, ys_hbm, o_ref, buf, sem):
    def issue(r, carry):
        pltpu.make_async_copy(ys_hbm.at[pl.ds(pos_ref[r], 1), :], buf.at[pl.ds(r, 1), :], sem).start()
        return carry

    lax.fori_loop(0, ROW_BLOCK, issue, 0, unroll=8)
    pltpu.make_async_copy(ys_hbm.at[pl.ds(0, ROW_BLOCK), :], buf, sem).wait()
    o_ref[...] = x_ref[...] + g2_ref[0] * buf[...]


def _combine(pos, x, g2, ys, seq_len):
    T, D = x.shape
    per_batch = seq_len // ROW_BLOCK
    return pl.pallas_call(
        _combine_kernel,
        out_shape=jax.ShapeDtypeStruct((T, D), F32),
        grid=(T // ROW_BLOCK,),
        in_specs=[pl.BlockSpec((ROW_BLOCK,), lambda i: (i,), memory_space=pltpu.SMEM),
                  pl.BlockSpec((ROW_BLOCK, D), lambda i: (i, 0)),
                  pl.BlockSpec((1, 1, D), lambda i: (i // per_batch, 0, 0)),
                  pl.BlockSpec(memory_space=pl.ANY)],
        out_specs=pl.BlockSpec((ROW_BLOCK, D), lambda i: (i, 0)),
        scratch_shapes=[pltpu.VMEM((ROW_BLOCK, D), F32), pltpu.SemaphoreType.DMA],
        compiler_params=_params("arbitrary"),
        name="moe_combine",
    )(pos, x, g2, ys)


def _moe_layer(x, norm_g, sc2, sh2, g2, router_wt, router_b, w1, w3, w2, seq_len):
    T, D = x.shape
    hg, stats, counts = _router(x, norm_g, sc2, sh2, router_wt, router_b, seq_len)

    cnt = counts[:N_BUCKETS, 0].astype(jnp.int32)
    padded = ((cnt + EXPERT_TM - 1) // EXPERT_TM) * EXPERT_TM
    ends = jnp.cumsum(padded)
    starts = ends - padded
    bucket = stats[0].astype(jnp.int32)
    pos = starts[bucket] + stats[1].astype(jnp.int32)
    n_tiles = T // EXPERT_TM + N_BUCKETS
    tile_row0 = jnp.arange(n_tiles, dtype=jnp.int32) * EXPERT_TM
    tile_bucket = jnp.sum((ends[None, :] <= tile_row0[:, None]).astype(jnp.int32), axis=1)
    tile_bucket = jnp.minimum(tile_bucket, N_BUCKETS - 1)
    grp, pair = tile_bucket // PAIRS_PER_GROUP, tile_bucket % PAIRS_PER_GROUP
    pair_lo = jnp.array([0, 0, 0, 1, 1, 2], jnp.int32)[pair]
    pair_hi = jnp.array([1, 2, 3, 2, 3, 3], jnp.int32)[pair]
    tile_lo = grp * EXPERTS_PER_GROUP + pair_lo
    tile_hi = grp * EXPERTS_PER_GROUP + pair_hi
    n_used = (ends[-1] // EXPERT_TM).astype(jnp.int32).reshape(1)

    xs = _dispatch(pos, hg, n_tiles * EXPERT_TM)
    ys = _experts(tile_lo, tile_hi, n_used, xs, w1, w3, w2)
    return _combine(pos, x, g2, ys, seq_len)


def _final_norm_kernel(x_ref, g_ref, o_ref):
    x = x_ref[...]
    r = lax.rsqrt(jnp.mean(x * x, axis=-1, keepdims=True) + NORM_EPS)
    o_ref[...] = (x * r) * g_ref[...]


def _final_norm(x, g):
    T, D = x.shape
    tm = 1024
    return pl.pallas_call(
        _final_norm_kernel,
        out_shape=jax.ShapeDtypeStruct((T, D), F32),
        grid=(T // tm,),
        in_specs=[pl.BlockSpec((tm, D), lambda i: (i, 0)), pl.BlockSpec((1, D), lambda i: (0, 0))],
        out_specs=pl.BlockSpec((tm, D), lambda i: (i, 0)),
        compiler_params=_params("arbitrary"),
        name="final_norm",
    )(x, g)


def kernel(x, c, norm1_g, norm2_g, w_ada, b_ada, w_in, w_sc, b_sc, hf_w1, hf_b1, hf_w2, hf_b2, hf_w3, hf_b3,
           hf_w4, hf_freq, hy_bias, w_br_h, w_br_a, w_out, router_w, router_bias, moe_w1, moe_w3, moe_w2, final_g):
    B, L, D = x.shape
    T = B * L
    C = HY_WIDTH
    xt = x.reshape(T, D)

    rope = _rope_tables(L)
    feat, deltas = _hyena_features(L)
    mats = _dft_matrices()

    c_pad = jnp.pad(c, ((0, 8 - B), (0, 0)))
    mod = _ada(c_pad, w_ada, b_ada)[:, :B]
    router_wt = router_w.T
    router_b = router_bias.reshape(N_EXPERTS, 1)

    for i in range(DEPTH):
        sh1, sc1, g1, sh2, sc2, g2 = (mod[i, :, k * D:(k + 1) * D].reshape(B, 1, D) for k in range(6))

        h = _norm_mod(xt, norm1_g[i].reshape(1, D), sc1, sh1, L)
        w_in_b = w_in[i].astype(BF16)
        u = _proj(h, w_in_b, 0, HY_COLS)
        n_groups = len(ATT_GROUPS)
        qkv = [_proj(h, w_in_b, HY_COLS + g * ATT_OUT, 3 * ATT_OUT, rope=rope, seq_len=L, col_step=n_groups)
               for g in range(n_groups)]
        gates = _proj(h, w_in_b, HY_COLS + QKV_COLS, GATE_COLS2)

        kern = _hyena_filter(feat, _pad2(hf_w1[i], LANES, LANES), _pad2(hf_b1[i][None], 1, LANES),
                             _pad2(hf_w2[i], LANES, LANES), _pad2(hf_b2[i][None], 1, LANES),
                             _pad2(hf_w3[i], LANES, LANES), _pad2(hf_b3[i][None], 1, LANES),
                             _pad2(hf_w4[i], LANES, 2 * C), _pad2(hf_freq[i][None], 1, LANES), deltas, L)
        kf = _filter_fft(kern, mats[0], mats[3])
        x0c, zv = _short_conv(u, w_sc[i], b_sc[i], L)
        y_hy = _long_conv(zv.reshape(B, L, C), x0c.reshape(B, L, C), kf, hy_bias[i], mats).reshape(T, C)

        attn = [_attention_group(qkv[g], dil, B, L) for g, (_, dil) in enumerate(ATT_GROUPS)]
        xt = _merge(xt, y_hy, attn, gates, w_br_h[i].astype(BF16), w_br_a[i].astype(BF16),
                    w_out[i].astype(BF16), g1, L)

        xt = _moe_layer(xt, norm2_g[i].reshape(1, D), sc2, sh2, g2, router_wt, router_b,
                        moe_w1[i].astype(BF16), moe_w3[i].astype(BF16), moe_w2[i].astype(BF16), L)
    return _final_norm(xt, final_g.reshape(1, D)).reshape(B, L, D)
```

```python
import math

import jax
import jax.numpy as jnp
import numpy as np
from jax import lax
from jax.experimental import pallas as pl
from jax.experimental.pallas import tpu as pltpu

D_MODEL = 1024
DEPTH = 2
HY_WIDTH = 768
HY_EMB = 33
HY_FFN = 64
HY_FAST_DECAY_PCT = 0.3
HY_SLOW_DECAY_PCT = 1.5
HY_TARGET = 1e-2
HEAD_DIM = 64
ATT_GROUPS = ((128, 1), (512, 4), (2048, 16))
HEADS_PER_GROUP = 4
N_HEADS = HEADS_PER_GROUP * len(ATT_GROUPS)
ATT_WIDTH = N_HEADS * HEAD_DIM
ATT_OUT = HEADS_PER_GROUP * HEAD_DIM
ROPE_THETA = 10000.0
IN_SPLITS = (3 * HY_WIDTH, ATT_WIDTH, ATT_WIDTH, ATT_WIDTH, D_MODEL, D_MODEL)
N_EXPERTS = 16
N_GROUPS = 4
EXPERTS_PER_GROUP = N_EXPERTS // N_GROUPS
D_EXPERT = 512
NORM_EPS = 1e-6
MASK_VALUE = -1e30

LANES = 128
MXU_DIM = 256
VMEM_LIMIT_BYTES = 56 * 1024 * 1024

F32 = jnp.float32
BF16 = jnp.bfloat16
HIGHEST = lax.Precision.HIGHEST

PROJ_TM = 2048
PROJ_TN = MXU_DIM
HY_COLS = 3 * HY_WIDTH
QKV_COLS = 3 * ATT_WIDTH
GATE_COLS2 = 2 * D_MODEL

FFT_R = 128
FFT_KH = FFT_R // 2 + 1
FFT_KP = 72
FFT_SLAB = 2 * FFT_KP
CONV_CB = 128
FFT_UNROLL_TIME = 8
FFT_UNROLL_FREQ = 5

PAIRS_PER_GROUP = 6
N_BUCKETS = N_GROUPS * PAIRS_PER_GROUP
BUCKET_ROWS = 32
ROUTER_TM = 512
EXPERT_TM = 256
ROW_BLOCK = 1024
GATE_COLS = LANES


def _params(*sem):
    return pltpu.CompilerParams(dimension_semantics=sem, vmem_limit_bytes=VMEM_LIMIT_BYTES)


def _sigmoid(x):
    return 1.0 / (1.0 + jnp.exp(-x))


def _ada_kernel(c_ref, w_ref, b_ref, o_ref):
    c = c_ref[...]
    c_act = c * _sigmoid(c)
    o_ref[0] = jnp.dot(c_act, w_ref[0], precision=HIGHEST, preferred_element_type=F32) + b_ref[0]


def _ada(c_pad, w_ada, b_ada):
    depth, D, N = w_ada.shape
    rows = c_pad.shape[0]
    tn = N // 4
    return pl.pallas_call(
        _ada_kernel,
        out_shape=jax.ShapeDtypeStruct((depth, rows, N), F32),
        grid=(depth, N // tn),
        in_specs=[pl.BlockSpec((rows, D), lambda l, j: (0, 0)),
                  pl.BlockSpec((1, D, tn), lambda l, j: (l, 0, j)),
                  pl.BlockSpec((1, 1, tn), lambda l, j: (l, 0, j))],
        out_specs=pl.BlockSpec((1, rows, tn), lambda l, j: (l, 0, j)),
        compiler_params=_params("arbitrary", "arbitrary"),
        name="ada_mod",
    )(c_pad, w_ada, b_ada.reshape(depth, 1, N))


def _norm_mod_kernel(x_ref, g_ref, sc_ref, sh_ref, o_ref):
    x = x_ref[...]
    r = lax.rsqrt(jnp.mean(x * x, axis=-1, keepdims=True) + NORM_EPS)
    h = (x * r) * g_ref[...]
    o_ref[...] = (h * (1.0 + sc_ref[0]) + sh_ref[0]).astype(o_ref.dtype)


def _norm_mod(x, g, sc, sh, seq_len):
    T, D = x.shape
    tm = 1024
    per_batch = seq_len // tm
    return pl.pallas_call(
        _norm_mod_kernel,
        out_shape=jax.ShapeDtypeStruct((T, D), BF16),
        grid=(T // tm,),
        in_specs=[pl.BlockSpec((tm, D), lambda i: (i, 0)),
                  pl.BlockSpec((1, D), lambda i: (0, 0)),
                  pl.BlockSpec((1, 1, D), lambda i: (i // per_batch, 0, 0)),
                  pl.BlockSpec((1, 1, D), lambda i: (i // per_batch, 0, 0))],
        out_specs=pl.BlockSpec((tm, D), lambda i: (i, 0)),
        compiler_params=_params("arbitrary"),
        name="norm_mod",
    )(x, g, sc, sh)


def _proj_kernel(h_ref, w_ref, o_ref):
    o_ref[...] = jnp.dot(h_ref[...], w_ref[...], preferred_element_type=F32).astype(o_ref.dtype)


def _proj_rope_kernel(h_ref, w_ref, cos_ref, sin_ref, rot_ref, o_ref, *, n_q_tiles):
    j = pl.program_id(1)
    acc = jnp.dot(h_ref[...], w_ref[...], preferred_element_type=F32)

    @pl.when(j < 2 * n_q_tiles)
    def _():
        cos = jnp.concatenate([cos_ref[...], cos_ref[...]], axis=1)
        sin = jnp.concatenate([sin_ref[...], sin_ref[...]], axis=1)
        swapped = jnp.dot(acc.astype(BF16), rot_ref[...], preferred_element_type=F32)
        out = acc * cos + swapped * sin
        scale = jnp.where(j < n_q_tiles, HEAD_DIM ** -0.5, 1.0)
        o_ref[...] = (out * scale).astype(o_ref.dtype)

    @pl.when(j >= 2 * n_q_tiles)
    def _():
        o_ref[...] = acc.astype(o_ref.dtype)


def _proj(h, w, col0, n_cols, rope=None, seq_len=None, col_step=1, tn=PROJ_TN, out_dtype=BF16):
    T, D = h.shape
    tm = PROJ_TM
    assert col0 % tn == 0 and n_cols % tn == 0 and (rope is None or tn == PROJ_TN)
    off = col0 // tn
    grid = (T // tm, n_cols // tn)
    h_spec = pl.BlockSpec((tm, D), lambda i, j: (i, 0))
    w_spec = pl.BlockSpec((D, tn), lambda i, j: (0, off + j * col_step))
    o_spec = pl.BlockSpec((tm, tn), lambda i, j: (i, j))
    out_shape = jax.ShapeDtypeStruct((T, n_cols), out_dtype)
    if rope is None:
        return pl.pallas_call(_proj_kernel, out_shape=out_shape, grid=grid, in_specs=[h_spec, w_spec],
                              out_specs=o_spec, compiler_params=_params("arbitrary", "arbitrary"),
                              name="proj")(h, w)
    cos_t, sin_t, rot = rope
    per_batch = seq_len // tm
    tab_spec = pl.BlockSpec((tm, LANES), lambda i, j: (i % per_batch, 0))
    kern = lambda *refs: _proj_rope_kernel(*refs, n_q_tiles=1)
    return pl.pallas_call(kern, out_shape=out_shape, grid=grid,
                          in_specs=[h_spec, w_spec, tab_spec, tab_spec,
                                    pl.BlockSpec((tn, tn), lambda i, j: (0, 0))],
                          out_specs=o_spec, compiler_params=_params("arbitrary", "arbitrary"),
                          name="proj_rope")(h, w, cos_t, sin_t, rot)


def _rope_tables(seq_len):
    half = HEAD_DIM // 2
    inv = ROPE_THETA ** (-jnp.arange(half, dtype=F32) / half)
    ang = jnp.arange(seq_len, dtype=F32)[:, None] * inv[None, :]
    reps = LANES // half
    cos_t = jnp.tile(jnp.cos(ang), (1, reps))
    sin_t = jnp.tile(jnp.sin(ang), (1, reps))
    rot = np.zeros((PROJ_TN, PROJ_TN), np.float32)
    for j in range(PROJ_TN):
        if j % HEAD_DIM < half:
            rot[j + half, j] = -1.0
        else:
            rot[j - half, j] = 1.0
    return cos_t, sin_t, jnp.asarray(rot, BF16)


def _hyena_filter_kernel(feat_ref, w1_ref, b1_ref, w2_ref, b2_ref, w3_ref, b3_ref, w4_ref, fr_ref,
                         dl_ref, o_ref, *, seq_len):
    i = pl.program_id(0)
    tm = feat_ref.shape[0]
    z = feat_ref[...]
    fr = fr_ref[...]
    h = jnp.sin(fr * (jnp.dot(z, w1_ref[...], precision=HIGHEST, preferred_element_type=F32) + b1_ref[...]))
    h = jnp.sin(fr * (jnp.dot(h, w2_ref[...], precision=HIGHEST, preferred_element_type=F32) + b2_ref[...]))
    h = jnp.sin(fr * (jnp.dot(h, w3_ref[...], precision=HIGHEST, preferred_element_type=F32) + b3_ref[...]))
    k = jnp.dot(h, w4_ref[...], precision=HIGHEST, preferred_element_type=F32)
    decay = jnp.exp(-z[:, 0:1] * jnp.abs(dl_ref[...]))
    n = i * tm + lax.broadcasted_iota(jnp.int32, (tm, 1), 0)
    o_ref[...] = jnp.where(n == seq_len, 0.0, k * decay)


def _hyena_filter(feat, w1, b1, w2, b2, w3, b3, w4, freq, deltas, seq_len):
    n_rows = feat.shape[0]
    C = HY_WIDTH
    tm = 1024
    fwd_tiles = seq_len // tm
    full = lambda i: (0, 0)
    kern = lambda *refs: _hyena_filter_kernel(*refs, seq_len=seq_len)
    return pl.pallas_call(
        kern,
        out_shape=jax.ShapeDtypeStruct((n_rows, C), F32),
        grid=(n_rows // tm,),
        in_specs=[pl.BlockSpec((tm, LANES), lambda i: (i, 0)),
                  pl.BlockSpec((LANES, LANES), full), pl.BlockSpec((1, LANES), full),
                  pl.BlockSpec((LANES, LANES), full), pl.BlockSpec((1, LANES), full),
                  pl.BlockSpec((LANES, LANES), full), pl.BlockSpec((1, LANES), full),
                  pl.BlockSpec((LANES, C), lambda i: (0, i // fwd_tiles)),
                  pl.BlockSpec((1, LANES), full),
                  pl.BlockSpec((1, C), full)],
        out_specs=pl.BlockSpec((tm, C), lambda i: (i, 0)),
        compiler_params=_params("arbitrary"),
        name="hyena_filter",
    )(feat, w1, b1, w2, b2, w3, b3, w4, freq, deltas)


def _hyena_features(seq_len):
    L = seq_len
    n = np.arange(2 * L)
    m = np.where(n <= L, n, 2 * L - n)
    m = np.where(n == L, 0, m)
    t = jnp.linspace(0.0, 1.0, L, dtype=F32)[:, None]
    bands = (HY_EMB - 1) // 2
    w = 2.0 * math.pi * jnp.arange(L, dtype=F32)[:, None] / L
    f = jnp.linspace(1e-4, bands - 1, bands, dtype=F32)[None, :]
    z = jnp.concatenate([t, jnp.cos(f * w), -jnp.sin(f * w)], axis=-1)
    z = jnp.pad(z, ((0, 0), (0, LANES - HY_EMB)))
    max_decay = math.log(HY_TARGET) / HY_FAST_DECAY_PCT
    min_decay = math.log(HY_TARGET) / HY_SLOW_DECAY_PCT
    deltas = jnp.linspace(min_decay, max_decay, HY_WIDTH, dtype=F32)[None, :]
    return z[jnp.asarray(m)], deltas


def _pad2(a, rows, cols):
    return jnp.pad(a, ((0, rows - a.shape[0]), (0, cols - a.shape[1])))


def _dft_matrices():
    R, KH, KP = FFT_R, FFT_KH, FFT_KP
    N = R * R
    n1 = np.arange(R)[:, None, None]
    k2 = np.arange(KP)[None, :, None]
    n2 = np.arange(R)[None, None, :]
    phase = 2.0 * np.pi * ((n2 * k2 % R) / R + (n1 * k2) / N)
    live = (k2 < KH)
    a1 = np.concatenate([np.cos(phase) * live, -np.sin(phase) * live], axis=1)
    wgt = np.where((k2 == 0) | (k2 == R // 2), 1.0, 2.0) * live / N
    b1 = np.concatenate([np.cos(phase) * wgt, -np.sin(phase) * wgt], axis=1)
    b1 = np.transpose(b1, (0, 2, 1))[:, :R // 2, :]
    th = 2.0 * np.pi * (np.arange(R)[:, None] * np.arange(R)[None, :] % R) / R
    c, s = np.cos(th), np.sin(th)
    w2f = np.block([[c, s], [-s, c]])
    w2i = np.block([[c, -s], [s, c]])
    as_bf = lambda a: jnp.asarray(a.astype(np.float32)).astype(BF16)
    return as_bf(a1), as_bf(a1[:, :, :R // 2]), as_bf(b1), as_bf(w2f), as_bf(w2i)


def _fft_stage1(src_ref, a1_ref, g_ref, n_rows):
    def body(n1, carry):
        xs = src_ref[pl.ds(n1, n_rows, stride=FFT_R), :].astype(BF16)
        slab = jnp.dot(a1_ref[n1], xs, preferred_element_type=F32)
        g_ref[pl.ds(pl.multiple_of(n1 * FFT_SLAB, 8), FFT_SLAB), :] = slab
        return carry
    lax.fori_loop(0, FFT_R, body, 0, unroll=FFT_UNROLL_TIME)


def _load_freq_rows(g_ref, k2):
    re = g_ref[pl.ds(k2, FFT_R, stride=FFT_SLAB), :]
    im = g_ref[pl.ds(FFT_KP + k2, FFT_R, stride=FFT_SLAB), :]
    return jnp.concatenate([re, im], axis=0)


def _filter_fft_kernel(kern_ref, a1_ref, w2_ref, o_ref, g_ref):
    _fft_stage1(kern_ref, a1_ref, g_ref, FFT_R)

    def body(k2, carry):
        gk = _load_freq_rows(g_ref, k2).astype(BF16)
        o_ref[k2] = jnp.dot(w2_ref[...], gk, preferred_element_type=F32).astype(o_ref.dtype)
        return carry
    lax.fori_loop(0, FFT_KH, body, 0, unroll=FFT_UNROLL_FREQ)


def _filter_fft(kern, a1_full, w2f):
    n_rows, C = kern.shape
    cb = CONV_CB
    return pl.pallas_call(
        _filter_fft_kernel,
        out_shape=jax.ShapeDtypeStruct((FFT_KH, 2 * FFT_R, C), BF16),
        grid=(C // cb,),
        in_specs=[pl.BlockSpec((n_rows, cb), lambda j: (0, j)),
                  pl.BlockSpec((FFT_R, FFT_SLAB, FFT_R), lambda j: (0, 0, 0)),
                  pl.BlockSpec((2 * FFT_R, 2 * FFT_R), lambda j: (0, 0))],
        out_specs=pl.BlockSpec((FFT_KH, 2 * FFT_R, cb), lambda j: (0, 0, j)),
        scratch_shapes=[pltpu.VMEM((FFT_R * FFT_SLAB, cb), F32)],
        compiler_params=_params("arbitrary"),
        name="hyena_filter_fft",
    )(kern, a1_full, w2f)


def _short_conv_kernel(x0_ref, x1_ref, v_ref, p0_ref, p1_ref, pv_ref, n0_ref, n1_ref, nv_ref,
                       w0_ref, w1_ref, wv_ref, b0_ref, b1_ref, bv_ref, x0c_ref, zv_ref, scr, *, tiles_per_seq):
    i = pl.program_id(0)
    tm = x0_ref.shape[0]
    first = (i % tiles_per_seq) == 0
    last = (i % tiles_per_seq) == tiles_per_seq - 1
    halo = p0_ref.shape[0]

    def conv(u_ref, prev_ref, next_ref, w_ref, b_ref):
        prev_row = jnp.where(first, 0.0, prev_ref[halo - 1:halo, :].astype(F32))
        next_row = jnp.where(last, 0.0, next_ref[0:1, :].astype(F32))
        scr[7:8, :] = prev_row
        scr[8:8 + tm, :] = u_ref[...].astype(F32)
        scr[8 + tm:9 + tm, :] = next_row
        w = w_ref[...]
        return (scr[pl.ds(7, tm), :] * w[0:1] + scr[pl.ds(8, tm), :] * w[1:2]
                + scr[pl.ds(9, tm), :] * w[2:3] + b_ref[...])

    x0c_ref[...] = conv(x0_ref, p0_ref, n0_ref, w0_ref, b0_ref).astype(x0c_ref.dtype)
    x1c = conv(x1_ref, p1_ref, n1_ref, w1_ref, b1_ref)
    vc = conv(v_ref, pv_ref, nv_ref, wv_ref, bv_ref)
    zv_ref[...] = vc * x1c


def _short_conv(u, w_sc, b_sc, seq_len):
    T = u.shape[0]
    C = HY_WIDTH
    tm, cb, halo = 1024, 256, 16
    ncb = C // cb
    tiles_per_seq = seq_len // tm
    hb = tm // halo
    n_halo = T // halo

    def part(p):
        return pl.BlockSpec((tm, cb), lambda i, j: (i, p * ncb + j))

    def prev(p):
        return pl.BlockSpec((halo, cb), lambda i, j: (jnp.maximum(i * hb - 1, 0), p * ncb + j))

    def nxt(p):
        return pl.BlockSpec((halo, cb), lambda i, j: (jnp.minimum((i + 1) * hb, n_halo - 1), p * ncb + j))

    def wpart(p):
        return pl.BlockSpec((3, cb), lambda i, j: (0, p * ncb + j))

    def bpart(p):
        return pl.BlockSpec((1, cb), lambda i, j: (0, p * ncb + j))

    kern = lambda *refs: _short_conv_kernel(*refs, tiles_per_seq=tiles_per_seq)
    b2 = b_sc.reshape(1, 3 * C)
    return pl.pallas_call(
        kern,
        out_shape=(jax.ShapeDtypeStruct((T, C), BF16), jax.ShapeDtypeStruct((T, C), F32)),
        grid=(T // tm, ncb),
        in_specs=[part(0), part(1), part(2), prev(0), prev(1), prev(2), nxt(0), nxt(1), nxt(2),
                  wpart(0), wpart(1), wpart(2), bpart(0), bpart(1), bpart(2)],
        out_specs=(pl.BlockSpec((tm, cb), lambda i, j: (i, j)), pl.BlockSpec((tm, cb), lambda i, j: (i, j))),
        scratch_shapes=[pltpu.VMEM((tm + 16, cb), F32)],
        compiler_params=_params("arbitrary", "arbitrary"),
        name="hyena_short_conv",
    )(u, u, u, u, u, u, u, u, u, w_sc, w_sc, w_sc, b2, b2, b2)


def _long_conv_kernel(zv_ref, x0_ref, kf_ref, bias_ref, a1_ref, b1_ref, w2f_ref, w2i_ref, o_ref, g_ref):
    seq_len = zv_ref.shape[1]
    n2_rows = seq_len // FFT_R
    zv2 = zv_ref.at[0]
    o2 = o_ref.at[0]
    _fft_stage1(zv2, a1_ref, g_ref, n2_rows)

    def freq_body(k2, carry):
        gk = _load_freq_rows(g_ref, k2).astype(BF16)
        x = jnp.dot(w2f_ref[...], gk, preferred_element_type=F32)
        kf = kf_ref[k2].astype(F32)
        xr, xi = x[:FFT_R], x[FFT_R:]
        kr, ki = kf[:FFT_R], kf[FFT_R:]
        p = jnp.concatenate([xr * kr - xi * ki, xr * ki + xi * kr], axis=0).astype(BF16)
        hk = jnp.dot(w2i_ref[...], p, preferred_element_type=F32)
        g_ref[pl.ds(k2, FFT_R, stride=FFT_SLAB), :] = hk[:FFT_R]
        g_ref[pl.ds(FFT_KP + k2, FFT_R, stride=FFT_SLAB), :] = hk[FFT_R:]
        return carry
    lax.fori_loop(0, FFT_KH, freq_body, 0, unroll=FFT_UNROLL_FREQ)

    def time_body(n1, carry):
        slab = g_ref[pl.ds(pl.multiple_of(n1 * FFT_SLAB, 8), FFT_SLAB), :].astype(BF16)
        o2[pl.ds(n1, n2_rows, stride=FFT_R), :] = jnp.dot(b1_ref[n1], slab, preferred_element_type=F32)
        return carry
    lax.fori_loop(0, FFT_R, time_body, 0, unroll=FFT_UNROLL_TIME)

    chunk = 512
    bias = bias_ref[...]

    def out_body(c, carry):
        rows = pl.ds(pl.multiple_of(c * chunk, chunk), chunk)
        o2[rows, :] = (o2[rows, :] + zv2[rows, :] * bias) * x0_ref[0, rows, :].astype(F32)
        return carry
    lax.fori_loop(0, seq_len // chunk, out_body, 0)


def _long_conv(zv, x0c, kf, hy_bias, mats):
    B, L, C = zv.shape
    cb = CONV_CB
    _, a1_half, b1, w2f, w2i = mats
    blk = lambda j, b: (b, 0, j)
    return pl.pallas_call(
        _long_conv_kernel,
        out_shape=jax.ShapeDtypeStruct((B, L, C), F32),
        grid=(C // cb, B),
        in_specs=[pl.BlockSpec((1, L, cb), blk),
                  pl.BlockSpec((1, L, cb), blk),
                  pl.BlockSpec((FFT_KH, 2 * FFT_R, cb), lambda j, b: (0, 0, j)),
                  pl.BlockSpec((1, cb), lambda j, b: (0, j)),
                  pl.BlockSpec((FFT_R, FFT_SLAB, FFT_R // 2), lambda j, b: (0, 0, 0)),
                  pl.BlockSpec((FFT_R, FFT_R // 2, FFT_SLAB), lambda j, b: (0, 0, 0)),
                  pl.BlockSpec((2 * FFT_R, 2 * FFT_R), lambda j, b: (0, 0)),
                  pl.BlockSpec((2 * FFT_R, 2 * FFT_R), lambda j, b: (0, 0))],
        out_specs=pl.BlockSpec((1, L, cb), blk),
        scratch_shapes=[pltpu.VMEM((FFT_R * FFT_SLAB, cb), F32)],
        compiler_params=_params("arbitrary", "arbitrary"),
        name="hyena_long_conv",
    )(zv, x0c, kf, hy_bias.reshape(1, C), a1_half, b1, w2f, w2i)


ATT_TQ = 128
ATT_RADIUS = 64
ATT_WINDOW = ATT_TQ + 2 * ATT_RADIUS


ATT_SUB = 4


def _attn_kernel(q_ref, k_ref, v_ref, o_ref, lse_ref):
    i = pl.program_id(2)
    ls = k_ref.shape[1]
    nh = HEADS_PER_GROUP
    head_of_col = lax.broadcasted_iota(jnp.int32, (1, ATT_OUT), 1) // HEAD_DIM
    row_iota = lax.broadcasted_iota(jnp.int32, (nh * ATT_TQ, ATT_WINDOW), 0) % ATT_TQ
    col_iota = lax.broadcasted_iota(jnp.int32, (nh * ATT_TQ, ATT_WINDOW), 1)
    rel = row_iota - col_iota
    for sub in range(ATT_SUB):
        q0 = (i * ATT_SUB + sub) * ATT_TQ
        rows = slice(sub * ATT_TQ, (sub + 1) * ATT_TQ)
        q = q_ref[0, rows, :]
        start = jnp.clip(q0 - ATT_RADIUS, 0, ls - ATT_WINDOW)
        start = pl.multiple_of(start, ATT_RADIUS)
        kw = k_ref[0, pl.ds(start, ATT_WINDOW), :]
        vw = v_ref[0, pl.ds(start, ATT_WINDOW), :]
        out, lse = _attn_unit(q, kw, vw, rel, q0 - start, head_of_col)
        o_ref[0, rows, :] = out.astype(o_ref.dtype)
        lse_ref[0, rows, :] = lse


def _attn_unit(q, kw, vw, rel, q_minus_start, head_of_col):
    nh = HEADS_PER_GROUP
    band = jnp.abs(q_minus_start + rel) <= ATT_RADIUS
    zero = jnp.zeros_like(q)
    q4 = jnp.concatenate([jnp.where(head_of_col == h, q, zero) for h in range(nh)], axis=0)
    s = lax.dot_general(q4, kw, (((1,), (1,)), ((), ())), preferred_element_type=F32)
    s = jnp.where(band, s, MASK_VALUE)
    m = jnp.max(s, axis=-1, keepdims=True)
    p = jnp.exp(s - m)
    den = jnp.sum(p, axis=-1, keepdims=True)
    pv = jnp.dot(p.astype(BF16), vw, preferred_element_type=F32) / den
    lse4 = m + jnp.log(den)
    out = jnp.zeros((ATT_TQ, ATT_OUT), F32)
    lse = jnp.zeros((ATT_TQ, ATT_OUT), F32)
    for h in range(nh):
        mine = head_of_col == h
        hrows = slice(h * ATT_TQ, (h + 1) * ATT_TQ)
        out = jnp.where(mine, pv[hrows], out)
        lse = jnp.where(mine, lse4[hrows], lse)
    return out, lse


ATT_CHUNK = 2048


def _attn_strided_kernel(q0_ref, q1_ref, k0_ref, k1_ref, v0_ref, v1_ref, o_ref, lse_ref, o_scr, lse_scr, *, dil):
    i = pl.program_id(1)
    ls = q0_ref.shape[1] // dil

    def gather(lo_ref, hi_ref, first, n):
        rows = pl.ds(first, n, stride=dil)
        return jnp.concatenate([lo_ref[0, rows, :], hi_ref[0, rows, :]], axis=1).astype(BF16)

    units = ATT_CHUNK // ATT_TQ
    shift = dil.bit_length() - 1
    head_of_col = lax.broadcasted_iota(jnp.int32, (1, ATT_OUT), 1) // HEAD_DIM
    row_iota = lax.broadcasted_iota(jnp.int32, (HEADS_PER_GROUP * ATT_TQ, ATT_WINDOW), 0) % ATT_TQ
    rel = row_iota - lax.broadcasted_iota(jnp.int32, (HEADS_PER_GROUP * ATT_TQ, ATT_WINDOW), 1)

    def unit(u, carry):
        r = u & (dil - 1)
        sb = u >> shift
        m0 = i * (ATT_CHUNK // dil) + sb * ATT_TQ
        start = jnp.clip(m0 - ATT_RADIUS, 0, ls - ATT_WINDOW)
        q = gather(q0_ref, q1_ref, m0 * dil + r, ATT_TQ)
        kw = gather(k0_ref, k1_ref, start * dil + r, ATT_WINDOW)
        vw = gather(v0_ref, v1_ref, start * dil + r, ATT_WINDOW)
        out, lse = _attn_unit(q, kw, vw, rel, m0 - start, head_of_col)
        dst = pl.ds(sb * ATT_TQ * dil + r, ATT_TQ, stride=dil)
        for half in range(2):
            lanes = slice(half * LANES, (half + 1) * LANES)
            o_scr[half, dst, :] = out[:, lanes]
            lse_scr[half, dst, :] = lse[:, lanes]
        return carry
    lax.fori_loop(0, units, unit, 0, unroll=2)
    for half in range(2):
        lanes = slice(half * LANES, (half + 1) * LANES)
        o_ref[0, :, lanes] = o_scr[half]
        lse_ref[0, :, lanes] = lse_scr[half]


def _attention_strided(qkv_g, dil, batch, seq_len):
    view = qkv_g.reshape(batch, seq_len, 3 * ATT_OUT)
    halves = [pl.BlockSpec((1, seq_len, LANES), lambda b, i, c=c: (b, 0, c), pipeline_mode=pl.Buffered(1))
              for c in range(3 * ATT_OUT // LANES)]
    o_spec = pl.BlockSpec((1, ATT_CHUNK, ATT_OUT), lambda b, i: (b, i, 0))
    kern = lambda *refs: _attn_strided_kernel(*refs, dil=dil)
    o, lse = pl.pallas_call(
        kern,
        out_shape=(jax.ShapeDtypeStruct((batch, seq_len, ATT_OUT), F32),
                   jax.ShapeDtypeStruct((batch, seq_len, ATT_OUT), F32)),
        grid=(batch, seq_len // ATT_CHUNK),
        in_specs=halves,
        out_specs=(o_spec, o_spec),
        scratch_shapes=[pltpu.VMEM((2, ATT_CHUNK, LANES), F32), pltpu.VMEM((2, ATT_CHUNK, LANES), F32)],
        compiler_params=_params("arbitrary", "arbitrary"),
        name=f"dilated_attn_d{dil}",
    )(*([view] * len(halves)))
    T = batch * seq_len
    return o.reshape(T, ATT_OUT), lse.reshape(T, ATT_OUT)


def _attention_group(qkv_g, dil, batch, seq_len):
    ls = seq_len // dil
    tq = ATT_SUB * ATT_TQ
    view = qkv_g.reshape(batch, ls, dil * 3 * ATT_OUT)

    def col(which):
        return lambda b, r, i: (b, 0, r * 3 + which)

    q_map = lambda b, r, i: (b, i, r * 3)
    o_map = lambda b, r, i: (b, i, r)
    o, lse = pl.pallas_call(
        _attn_kernel,
        out_shape=(jax.ShapeDtypeStruct((batch, ls, dil * ATT_OUT), BF16),
                   jax.ShapeDtypeStruct((batch, ls, dil * ATT_OUT), F32)),
        grid=(batch, dil, ls // tq),
        in_specs=[pl.BlockSpec((1, tq, ATT_OUT), q_map),
                  pl.BlockSpec((1, ls, ATT_OUT), col(1)),
                  pl.BlockSpec((1, ls, ATT_OUT), col(2))],
        out_specs=(pl.BlockSpec((1, tq, ATT_OUT), o_map), pl.BlockSpec((1, tq, ATT_OUT), o_map)),
        compiler_params=_params("arbitrary", "arbitrary", "arbitrary"),
        name=f"dilated_attn_d{dil}",
    )(view, view, view)
    T = batch * seq_len
    return o.reshape(T, ATT_OUT), lse.reshape(T, ATT_OUT)


def _merge_kernel(x_ref, yhy_ref, o1_ref, o2_ref, o3_ref, l1_ref, l2_ref, l3_ref, gh_ref, ga_ref,
                  wh_ref, wa_ref, wo_ref, g1_ref, out_ref):
    l1, l2, l3 = l1_ref[...], l2_ref[...], l3_ref[...]
    m = jnp.maximum(jnp.maximum(l1, l2), l3)
    e1, e2, e3 = jnp.exp(l1 - m), jnp.exp(l2 - m), jnp.exp(l3 - m)
    tot = e1 + e2 + e3
    y_at = (e1 * o1_ref[...].astype(F32) + e2 * o2_ref[...].astype(F32) + e3 * o3_ref[...].astype(F32)) / tot
    a = jnp.dot(yhy_ref[...].astype(BF16), wh_ref[...], preferred_element_type=F32)
    b = jnp.dot(y_at.astype(BF16), wa_ref[...], preferred_element_type=F32)
    merged = _sigmoid(gh_ref[...].astype(F32)) * a + _sigmoid(ga_ref[...].astype(F32)) * b
    upd = jnp.dot(merged.astype(BF16), wo_ref[...], preferred_element_type=F32)
    out_ref[...] = x_ref[...] + g1_ref[0] * upd


def _merge(x, y_hy, attn, gates, w_br_h, w_br_a, w_out, g1, seq_len):
    T, D = x.shape
    tm = 512
    per_batch = seq_len // tm
    (o1, l1), (o2, l2), (o3, l3) = attn
    row = lambda w: pl.BlockSpec((tm, w), lambda i: (i, 0))
    full = lambda a: pl.BlockSpec(a.shape, lambda i: (0, 0))
    return pl.pallas_call(
        _merge_kernel,
        out_shape=jax.ShapeDtypeStruct((T, D), F32),
        grid=(T // tm,),
        in_specs=[row(D), row(HY_WIDTH), row(ATT_OUT), row(ATT_OUT), row(ATT_OUT),
                  row(ATT_OUT), row(ATT_OUT), row(ATT_OUT),
                  pl.BlockSpec((tm, D), lambda i: (i, 0)), pl.BlockSpec((tm, D), lambda i: (i, 1)),
                  full(w_br_h), full(w_br_a), full(w_out),
                  pl.BlockSpec((1, 1, D), lambda i: (i // per_batch, 0, 0))],
        out_specs=row(D),
        compiler_params=_params("arbitrary"),
        name="mixer_merge",
    )(x, y_hy, o1, o2, o3, l1, l2, l3, gates, gates, w_br_h, w_br_a, w_out, g1)


def _router_kernel(x_ref, g_ref, sc_ref, sh_ref, rwt_ref, rb_ref, hg_ref, stats_ref, cnt_ref, base_ref):
    i = pl.program_id(0)
    tm = x_ref.shape[0]

    @pl.when(i == 0)
    def _():
        base_ref[...] = jnp.zeros_like(base_ref)

    x = x_ref[...]
    r = lax.rsqrt(jnp.mean(x * x, axis=-1, keepdims=True) + NORM_EPS)
    h = (x * r) * g_ref[...]
    h = h * (1.0 + sc_ref[0]) + sh_ref[0]

    logits = lax.dot_general(rwt_ref[...], h, (((1,), (1,)), ((), ())),
                             precision=HIGHEST, preferred_element_type=F32)
    scores = _sigmoid(logits)
    biased = scores + rb_ref[...]

    def row(a, k):
        return a[k:k + 1, :]

    sel = jnp.zeros((1, tm), jnp.int32)
    best = None
    for g in range(N_GROUPS):
        a, b, c, d = (row(biased, 4 * g + k) for k in range(4))
        m_ab, n_ab = jnp.maximum(a, b), jnp.minimum(a, b)
        m_cd, n_cd = jnp.maximum(c, d), jnp.minimum(c, d)
        gs = jnp.maximum(m_ab, m_cd) + jnp.maximum(jnp.minimum(m_ab, m_cd), jnp.maximum(n_ab, n_cd))
        if g == 0:
            best = gs
        else:
            better = gs > best
            sel = jnp.where(better, g, sel)
            best = jnp.where(better, gs, best)

    v, u = [], []
    for k in range(EXPERTS_PER_GROUP):
        vk = jnp.zeros((1, tm), F32)
        uk = jnp.zeros((1, tm), F32)
        for g in range(N_GROUPS):
            vk = jnp.where(sel == g, row(biased, 4 * g + k), vk)
            uk = jnp.where(sel == g, row(scores, 4 * g + k), uk)
        v.append(vk)
        u.append(uk)

    i1 = jnp.zeros((1, tm), jnp.int32)
    b1 = v[0]
    for k in range(1, EXPERTS_PER_GROUP):
        gt = v[k] > b1
        i1 = jnp.where(gt, k, i1)
        b1 = jnp.where(gt, v[k], b1)
    i2 = jnp.zeros((1, tm), jnp.int32)
    b2 = jnp.full((1, tm), -jnp.inf, F32)
    for k in range(EXPERTS_PER_GROUP):
        cand = (i1 != k) & (v[k] > b2)
        i2 = jnp.where(cand, k, i2)
        b2 = jnp.where(cand, v[k], b2)

    lo = jnp.minimum(i1, i2)
    hi = jnp.maximum(i1, i2)
    pair = jnp.where(lo == 0, hi - 1, jnp.where(lo == 1, hi + 1, 5))
    bucket = sel * PAIRS_PER_GROUP + pair

    u_lo = jnp.zeros((1, tm), F32)
    u_hi = jnp.zeros((1, tm), F32)
    for k in range(EXPERTS_PER_GROUP):
        u_lo = jnp.where(lo == k, u[k], u_lo)
        u_hi = jnp.where(hi == k, u[k], u_hi)
    tot = u_lo + u_hi
    w_lo = u_lo / tot
    w_hi = u_hi / tot

    rows = lax.broadcasted_iota(jnp.int32, (BUCKET_ROWS, tm), 0)
    onehot = (rows == bucket).astype(F32)
    t_src = lax.broadcasted_iota(jnp.int32, (tm, tm), 0)
    t_dst = lax.broadcasted_iota(jnp.int32, (tm, tm), 1)
    before = (t_src < t_dst).astype(BF16)
    cum = jnp.dot(onehot.astype(BF16), before, preferred_element_type=F32)
    base = base_ref[...]
    rank = jnp.sum(onehot * (cum + base), axis=0, keepdims=True)
    base = base + jnp.sum(onehot, axis=1, keepdims=True)
    base_ref[...] = base
    cnt_ref[...] = jnp.broadcast_to(base, cnt_ref.shape)

    srow = lax.broadcasted_iota(jnp.int32, (8, tm), 0)
    stats_ref[...] = jnp.where(srow == 0, bucket.astype(F32), jnp.where(srow == 1, rank, 0.0))

    grow = lax.broadcasted_iota(jnp.int32, (GATE_COLS, tm), 0)
    gates_t = jnp.where(grow == 0, w_lo, jnp.where(grow == 1, w_hi, 0.0))
    hg_ref[:, :D_MODEL] = h
    hg_ref[:, D_MODEL:] = gates_t.T


def _router(x, norm_g, sc, sh, router_wt, router_b, seq_len):
    T, D = x.shape
    tm = ROUTER_TM
    per_batch = seq_len // tm
    return pl.pallas_call(
        _router_kernel,
        out_shape=(jax.ShapeDtypeStruct((T, D + GATE_COLS), F32),
                   jax.ShapeDtypeStruct((8, T), F32),
                   jax.ShapeDtypeStruct((BUCKET_ROWS, LANES), F32)),
        grid=(T // tm,),
        in_specs=[pl.BlockSpec((tm, D), lambda i: (i, 0)),
                  pl.BlockSpec((1, D), lambda i: (0, 0)),
                  pl.BlockSpec((1, 1, D), lambda i: (i // per_batch, 0, 0)),
                  pl.BlockSpec((1, 1, D), lambda i: (i // per_batch, 0, 0)),
                  pl.BlockSpec((N_EXPERTS, D), lambda i: (0, 0)),
                  pl.BlockSpec((N_EXPERTS, 1), lambda i: (0, 0))],
        out_specs=(pl.BlockSpec((tm, D + GATE_COLS), lambda i: (i, 0)),
                   pl.BlockSpec((8, tm), lambda i: (0, i)),
                   pl.BlockSpec((BUCKET_ROWS, LANES), lambda i: (0, 0))),
        scratch_shapes=[pltpu.VMEM((BUCKET_ROWS, 1), F32)],
        compiler_params=_params("arbitrary"),
        name="moe_router",
    )(x, norm_g, sc, sh, router_wt, router_b)


def _dispatch_kernel(pos_ref, hg_ref, xs_init_hbm, xs_hbm, sem):
    del xs_init_hbm

    def issue(r, carry):
        pltpu.make_async_copy(hg_ref.at[pl.ds(r, 1), :], xs_hbm.at[pl.ds(pos_ref[r], 1), :], sem).start()
        return carry

    lax.fori_loop(0, ROW_BLOCK, issue, 0, unroll=8)
    pltpu.make_async_copy(hg_ref, xs_hbm.at[pl.ds(0, ROW_BLOCK), :], sem).wait()


def _dispatch(pos, hg, n_rows):
    T, W = hg.shape
    zeros = jnp.zeros((n_rows, W), F32)
    return pl.pallas_call(
        _dispatch_kernel,
        out_shape=jax.ShapeDtypeStruct((n_rows, W), F32),
        grid=(T // ROW_BLOCK,),
        in_specs=[pl.BlockSpec((ROW_BLOCK,), lambda i: (i,), memory_space=pltpu.SMEM),
                  pl.BlockSpec((ROW_BLOCK, W), lambda i: (i, 0)),
                  pl.BlockSpec(memory_space=pl.ANY)],
        out_specs=pl.BlockSpec(memory_space=pl.ANY),
        scratch_shapes=[pltpu.SemaphoreType.DMA],
        input_output_aliases={2: 0},
        compiler_params=_params("arbitrary"),
        name="moe_dispatch",
    )(pos, hg, zeros)


def _expert_kernel(e_lo_ref, e_hi_ref, n_used_ref, xs_ref, w1a, w3a, w2a, w1b, w3b, w2b, y_ref):
    del e_lo_ref, e_hi_ref
    used = pl.program_id(0) < n_used_ref[0]

    @pl.when(jnp.logical_not(used))
    def _():
        y_ref[...] = jnp.zeros_like(y_ref)

    @pl.when(used)
    def _():
        xb = xs_ref[:, :D_MODEL].astype(BF16)
        g_lo = xs_ref[:, D_MODEL:D_MODEL + 1]
        g_hi = xs_ref[:, D_MODEL + 1:D_MODEL + 2]

        def ffn(w1, w3, w2):
            a = jnp.dot(xb, w1[0], preferred_element_type=F32)
            b = jnp.dot(xb, w3[0], preferred_element_type=F32)
            act = (a * _sigmoid(a)) * b
            return jnp.dot(act.astype(BF16), w2[0], preferred_element_type=F32)

        y_ref[...] = g_lo * ffn(w1a, w3a, w2a) + g_hi * ffn(w1b, w3b, w2b)


def _experts(tile_lo, tile_hi, n_used, xs, w1, w3, w2):
    n_rows, W = xs.shape
    D, F = w1.shape[1], w1.shape[2]
    n_tiles = n_rows // EXPERT_TM

    def x_map(j, lo, hi, nu):
        return (jnp.minimum(j, nu[0] - 1), 0)

    def w_lo_map(j, lo, hi, nu):
        return (lo[j], 0, 0)

    def w_hi_map(j, lo, hi, nu):
        return (hi[j], 0, 0)

    grid_spec = pltpu.PrefetchScalarGridSpec(
        num_scalar_prefetch=3,
        grid=(n_tiles,),
        in_specs=[pl.BlockSpec((EXPERT_TM, W), x_map),
                  pl.BlockSpec((1, D, F), w_lo_map), pl.BlockSpec((1, D, F), w_lo_map),
                  pl.BlockSpec((1, F, D), w_lo_map),
                  pl.BlockSpec((1, D, F), w_hi_map), pl.BlockSpec((1, D, F), w_hi_map),
                  pl.BlockSpec((1, F, D), w_hi_map)],
        out_specs=pl.BlockSpec((EXPERT_TM, D), lambda j, lo, hi, nu: (j, 0)),
    )
    return pl.pallas_call(
        _expert_kernel,
        out_shape=jax.ShapeDtypeStruct((n_rows, D), F32),
        grid_spec=grid_spec,
        compiler_params=_params("arbitrary"),
        name="moe_experts",
    )(tile_lo, tile_hi, n_used, xs, w1, w3, w2, w1, w3, w2)


def _combine_kernel(pos_ref, x_ref, g2_ref, ys_hbm, o_ref, buf, sem):
    def issue(r, carry):
        pltpu.make_async_copy(ys_hbm.at[pl.ds(pos_ref[r], 1), :], buf.at[pl.ds(r, 1), :], sem).start()
        return carry

    lax.fori_loop(0, ROW_BLOCK, issue, 0, unroll=8)
    pltpu.make_async_copy(ys_hbm.at[pl.ds(0, ROW_BLOCK), :], buf, sem).wait()
    o_ref[...] = x_ref[...] + g2_ref[0] * buf[...]


def _combine(pos, x, g2, ys, seq_len):
    T, D = x.shape
    per_batch = seq_len // ROW_BLOCK
    return pl.pallas_call(
        _combine_kernel,
        out_shape=jax.ShapeDtypeStruct((T, D), F32),
        grid=(T // ROW_BLOCK,),
        in_specs=[pl.BlockSpec((ROW_BLOCK,), lambda i: (i,), memory_space=pltpu.SMEM),
                  pl.BlockSpec((ROW_BLOCK, D), lambda i: (i, 0)),
                  pl.BlockSpec((1, 1, D), lambda i: (i // per_batch, 0, 0)),
                  pl.BlockSpec(memory_space=pl.ANY)],
        out_specs=pl.BlockSpec((ROW_BLOCK, D), lambda i: (i, 0)),
        scratch_shapes=[pltpu.VMEM((ROW_BLOCK, D), F32), pltpu.SemaphoreType.DMA],
        compiler_params=_params("arbitrary"),
        name="moe_combine",
    )(pos, x, g2, ys)


def _moe_layer(x, norm_g, sc2, sh2, g2, router_wt, router_b, w1, w3, w2, seq_len):
    T, D = x.shape
    hg, stats, counts = _router(x, norm_g, sc2, sh2, router_wt, router_b, seq_len)

    cnt = counts[:N_BUCKETS, 0].astype(jnp.int32)
    padded = ((cnt + EXPERT_TM - 1) // EXPERT_TM) * EXPERT_TM
    ends = jnp.cumsum(padded)
    starts = ends - padded
    bucket = stats[0].astype(jnp.int32)
    pos = starts[bucket] + stats[1].astype(jnp.int32)
    n_tiles = T // EXPERT_TM + N_BUCKETS
    tile_row0 = jnp.arange(n_tiles, dtype=jnp.int32) * EXPERT_TM
    tile_bucket = jnp.sum((ends[None, :] <= tile_row0[:, None]).astype(jnp.int32), axis=1)
    tile_bucket = jnp.minimum(tile_bucket, N_BUCKETS - 1)
    grp, pair = tile_bucket // PAIRS_PER_GROUP, tile_bucket % PAIRS_PER_GROUP
    pair_lo = jnp.array([0, 0, 0, 1, 1, 2], jnp.int32)[pair]
    pair_hi = jnp.array([1, 2, 3, 2, 3, 3], jnp.int32)[pair]
    tile_lo = grp * EXPERTS_PER_GROUP + pair_lo
    tile_hi = grp * EXPERTS_PER_GROUP + pair_hi
    n_used = (ends[-1] // EXPERT_TM).astype(jnp.int32).reshape(1)

    xs = _dispatch(pos, hg, n_tiles * EXPERT_TM)
    ys = _experts(tile_lo, tile_hi, n_used, xs, w1, w3, w2)
    return _combine(pos, x, g2, ys, seq_len)


def _final_norm_kernel(x_ref, g_ref, o_ref):
    x = x_ref[...]
    r = lax.rsqrt(jnp.mean(x * x, axis=-1, keepdims=True) + NORM_EPS)
    o_ref[...] = (x * r) * g_ref[...]


def _final_norm(x, g):
    T, D = x.shape
    tm = 1024
    return pl.pallas_call(
        _final_norm_kernel,
        out_shape=jax.ShapeDtypeStruct((T, D), F32),
        grid=(T // tm,),
        in_specs=[pl.BlockSpec((tm, D), lambda i: (i, 0)), pl.BlockSpec((1, D), lambda i: (0, 0))],
        out_specs=pl.BlockSpec((tm, D), lambda i: (i, 0)),
        compiler_params=_params("arbitrary"),
        name="final_norm",
    )(x, g)


def kernel(x, c, norm1_g, norm2_g, w_ada, b_ada, w_in, w_sc, b_sc, hf_w1, hf_b1, hf_w2, hf_b2, hf_w3, hf_b3,
           hf_w4, hf_freq, hy_bias, w_br_h, w_br_a, w_out, router_w, router_bias, moe_w1, moe_w3, moe_w2, final_g):
    B, L, D = x.shape
    T = B * L
    C = HY_WIDTH
    xt = x.reshape(T, D)

    rope = _rope_tables(L)
    feat, deltas = _hyena_features(L)
    mats = _dft_matrices()

    c_pad = jnp.pad(c, ((0, 8 - B), (0, 0)))
    mod = _ada(c_pad, w_ada, b_ada)[:, :B]
    router_wt = router_w.T
    router_b = router_bias.reshape(N_EXPERTS, 1)

    for i in range(DEPTH):
        sh1, sc1, g1, sh2, sc2, g2 = (mod[i, :, k * D:(k + 1) * D].reshape(B, 1, D) for k in range(6))

        h = _norm_mod(xt, norm1_g[i].reshape(1, D), sc1, sh1, L)
        w_in_b = w_in[i].astype(BF16)
        u = _proj(h, w_in_b, 0, HY_COLS, tn=HY_WIDTH)
        n_groups = len(ATT_GROUPS)
        qkv = [_proj(h, w_in_b, HY_COLS + g * ATT_OUT, 3 * ATT_OUT, rope=rope, seq_len=L, col_step=n_groups,
                     out_dtype=BF16 if dil == 1 else F32) for g, (_, dil) in enumerate(ATT_GROUPS)]
        gates = _proj(h, w_in_b, HY_COLS + QKV_COLS, GATE_COLS2, tn=D_MODEL // 2)

        kern = _hyena_filter(feat, _pad2(hf_w1[i], LANES, LANES), _pad2(hf_b1[i][None], 1, LANES),
                             _pad2(hf_w2[i], LANES, LANES), _pad2(hf_b2[i][None], 1, LANES),
                             _pad2(hf_w3[i], LANES, LANES), _pad2(hf_b3[i][None], 1, LANES),
                             _pad2(hf_w4[i], LANES, 2 * C), _pad2(hf_freq[i][None], 1, LANES), deltas, L)
        kf = _filter_fft(kern, mats[0], mats[3])
        x0c, zv = _short_conv(u, w_sc[i], b_sc[i], L)
        y_hy = _long_conv(zv.reshape(B, L, C), x0c.reshape(B, L, C), kf, hy_bias[i], mats).reshape(T, C)

        attn = [(_attention_group if dil == 1 else _attention_strided)(qkv[g], dil, B, L)
                for g, (_, dil) in enumerate(ATT_GROUPS)]
        xt = _merge(xt, y_hy, attn, gates, w_br_h[i].astype(BF16), w_br_a[i].astype(BF16),
                    w_out[i].astype(BF16), g1, L)

        xt = _moe_layer(xt, norm2_g[i].reshape(1, D), sc2, sh2, g2, router_wt, router_b,
                        moe_w1[i].astype(BF16), moe_w3[i].astype(BF16), moe_w2[i].astype(BF16), L)
    return _final_norm(xt, final_g.reshape(1, D)).reshape(B, L, D)
```

```python
import math

import jax
import jax.numpy as jnp
import numpy as np
from jax import lax
from jax.experimental import pallas as pl
from jax.experimental.pallas import tpu as pltpu

D_MODEL = 1024
DEPTH = 2
HY_WIDTH = 768
HY_EMB = 33
HY_FFN = 64
HY_FAST_DECAY_PCT = 0.3
HY_SLOW_DECAY_PCT = 1.5
HY_TARGET = 1e-2
HEAD_DIM = 64
ATT_GROUPS = ((128, 1), (512, 4), (2048, 16))
HEADS_PER_GROUP = 4
N_HEADS = HEADS_PER_GROUP * len(ATT_GROUPS)
ATT_WIDTH = N_HEADS * HEAD_DIM
ATT_OUT = HEADS_PER_GROUP * HEAD_DIM
ROPE_THETA = 10000.0
IN_SPLITS = (3 * HY_WIDTH, ATT_WIDTH, ATT_WIDTH, ATT_WIDTH, D_MODEL, D_MODEL)
N_EXPERTS = 16
N_GROUPS = 4
EXPERTS_PER_GROUP = N_EXPERTS // N_GROUPS
D_EXPERT = 512
NORM_EPS = 1e-6
MASK_VALUE = -1e30

LANES = 128
MXU_DIM = 256
VMEM_LIMIT_BYTES = 56 * 1024 * 1024

F32 = jnp.float32
BF16 = jnp.bfloat16
HIGHEST = lax.Precision.HIGHEST

PROJ_TM = 2048
PROJ_TN = MXU_DIM
HY_COLS = 3 * HY_WIDTH
QKV_COLS = 3 * ATT_WIDTH
GATE_COLS2 = 2 * D_MODEL

FFT_R = 128
FFT_KH = FFT_R // 2 + 1
FFT_KP = 72
FFT_SLAB = 2 * FFT_KP
FFT_G_PITCH = FFT_SLAB + 8
FFT_T_PITCH = FFT_R + 8
CONV_CB = 128
FFT_UNROLL_TIME = 8
FFT_UNROLL_FREQ = 5

PAIRS_PER_GROUP = 6
N_BUCKETS = N_GROUPS * PAIRS_PER_GROUP
BUCKET_ROWS = 32
ROUTER_TM = 512
EXPERT_TM = 256
ROW_BLOCK = 1024
GATE_COLS = LANES


def _params(*sem):
    return pltpu.CompilerParams(dimension_semantics=sem, vmem_limit_bytes=VMEM_LIMIT_BYTES)


def _sigmoid(x):
    return 1.0 / (1.0 + jnp.exp(-x))


def _store_time_padded(o_ref, val):
    groups = val.shape[0] // FFT_R
    o_ref[:, :FFT_R, :] = val.reshape(groups, FFT_R, val.shape[1])
    o_ref[:, FFT_R:, :] = jnp.zeros((groups, FFT_T_PITCH - FFT_R, val.shape[1]), o_ref.dtype)


def _time_padded_shape(rows, cols):
    return jax.ShapeDtypeStruct((rows // FFT_R, FFT_T_PITCH, cols), F32)


def _ada_kernel(c_ref, w_ref, b_ref, o_ref):
    c = c_ref[...]
    c_act = c * _sigmoid(c)
    o_ref[0] = jnp.dot(c_act, w_ref[0], precision=HIGHEST, preferred_element_type=F32) + b_ref[0]


def _ada(c_pad, w_ada, b_ada):
    depth, D, N = w_ada.shape
    rows = c_pad.shape[0]
    tn = N // 4
    return pl.pallas_call(
        _ada_kernel,
        out_shape=jax.ShapeDtypeStruct((depth, rows, N), F32),
        grid=(depth, N // tn),
        in_specs=[pl.BlockSpec((rows, D), lambda l, j: (0, 0)),
                  pl.BlockSpec((1, D, tn), lambda l, j: (l, 0, j)),
                  pl.BlockSpec((1, 1, tn), lambda l, j: (l, 0, j))],
        out_specs=pl.BlockSpec((1, rows, tn), lambda l, j: (l, 0, j)),
        compiler_params=_params("arbitrary", "arbitrary"),
        name="ada_mod",
    )(c_pad, w_ada, b_ada.reshape(depth, 1, N))


def _norm_mod_kernel(x_ref, g_ref, sc_ref, sh_ref, o_ref):
    x = x_ref[...]
    r = lax.rsqrt(jnp.mean(x * x, axis=-1, keepdims=True) + NORM_EPS)
    h = (x * r) * g_ref[...]
    o_ref[...] = (h * (1.0 + sc_ref[0]) + sh_ref[0]).astype(o_ref.dtype)


def _norm_mod(x, g, sc, sh, seq_len):
    T, D = x.shape
    tm = 1024
    per_batch = seq_len // tm
    return pl.pallas_call(
        _norm_mod_kernel,
        out_shape=jax.ShapeDtypeStruct((T, D), BF16),
        grid=(T // tm,),
        in_specs=[pl.BlockSpec((tm, D), lambda i: (i, 0)),
                  pl.BlockSpec((1, D), lambda i: (0, 0)),
                  pl.BlockSpec((1, 1, D), lambda i: (i // per_batch, 0, 0)),
                  pl.BlockSpec((1, 1, D), lambda i: (i // per_batch, 0, 0))],
        out_specs=pl.BlockSpec((tm, D), lambda i: (i, 0)),
        compiler_params=_params("arbitrary"),
        name="norm_mod",
    )(x, g, sc, sh)


def _proj_kernel(h_ref, w_ref, o_ref):
    o_ref[...] = jnp.dot(h_ref[...], w_ref[...], preferred_element_type=F32).astype(o_ref.dtype)


def _proj_rope_kernel(h_ref, w_ref, cos_ref, sin_ref, rot_ref, o_ref, *, n_q_tiles):
    j = pl.program_id(1)
    acc = jnp.dot(h_ref[...], w_ref[...], preferred_element_type=F32)

    @pl.when(j < 2 * n_q_tiles)
    def _():
        cos = jnp.concatenate([cos_ref[...], cos_ref[...]], axis=1)
        sin = jnp.concatenate([sin_ref[...], sin_ref[...]], axis=1)
        swapped = jnp.dot(acc.astype(BF16), rot_ref[...], preferred_element_type=F32)
        out = acc * cos + swapped * sin
        scale = jnp.where(j < n_q_tiles, HEAD_DIM ** -0.5, 1.0)
        o_ref[...] = (out * scale).astype(o_ref.dtype)

    @pl.when(j >= 2 * n_q_tiles)
    def _():
        o_ref[...] = acc.astype(o_ref.dtype)


def _proj(h, w, col0, n_cols, rope=None, seq_len=None, col_step=1, tn=PROJ_TN, out_dtype=BF16):
    T, D = h.shape
    tm = PROJ_TM
    assert col0 % tn == 0 and n_cols % tn == 0 and (rope is None or tn == PROJ_TN)
    off = col0 // tn
    grid = (T // tm, n_cols // tn)
    h_spec = pl.BlockSpec((tm, D), lambda i, j: (i, 0))
    w_spec = pl.BlockSpec((D, tn), lambda i, j: (0, off + j * col_step))
    o_spec = pl.BlockSpec((tm, tn), lambda i, j: (i, j))
    out_shape = jax.ShapeDtypeStruct((T, n_cols), out_dtype)
    if rope is None:
        return pl.pallas_call(_proj_kernel, out_shape=out_shape, grid=grid, in_specs=[h_spec, w_spec],
                              out_specs=o_spec, compiler_params=_params("arbitrary", "arbitrary"),
                              name="proj")(h, w)
    cos_t, sin_t, rot = rope
    per_batch = seq_len // tm
    tab_spec = pl.BlockSpec((tm, LANES), lambda i, j: (i % per_batch, 0))
    kern = lambda *refs: _proj_rope_kernel(*refs, n_q_tiles=1)
    return pl.pallas_call(kern, out_shape=out_shape, grid=grid,
                          in_specs=[h_spec, w_spec, tab_spec, tab_spec,
                                    pl.BlockSpec((tn, tn), lambda i, j: (0, 0))],
                          out_specs=o_spec, compiler_params=_params("arbitrary", "arbitrary"),
                          name="proj_rope")(h, w, cos_t, sin_t, rot)


def _rope_tables(seq_len):
    half = HEAD_DIM // 2
    inv = ROPE_THETA ** (-jnp.arange(half, dtype=F32) / half)
    ang = jnp.arange(seq_len, dtype=F32)[:, None] * inv[None, :]
    reps = LANES // half
    cos_t = jnp.tile(jnp.cos(ang), (1, reps))
    sin_t = jnp.tile(jnp.sin(ang), (1, reps))
    rot = np.zeros((PROJ_TN, PROJ_TN), np.float32)
    for j in range(PROJ_TN):
        if j % HEAD_DIM < half:
            rot[j + half, j] = -1.0
        else:
            rot[j - half, j] = 1.0
    return cos_t, sin_t, jnp.asarray(rot, BF16)


def _hyena_filter_kernel(feat_ref, w1_ref, b1_ref, w2_ref, b2_ref, w3_ref, b3_ref, w4_ref, fr_ref,
                         dl_ref, o_ref, *, seq_len):
    i = pl.program_id(0)
    tm = feat_ref.shape[0]
    z = feat_ref[...]
    fr = fr_ref[...]
    h = jnp.sin(fr * (jnp.dot(z, w1_ref[...], precision=HIGHEST, preferred_element_type=F32) + b1_ref[...]))
    h = jnp.sin(fr * (jnp.dot(h, w2_ref[...], precision=HIGHEST, preferred_element_type=F32) + b2_ref[...]))
    h = jnp.sin(fr * (jnp.dot(h, w3_ref[...], precision=HIGHEST, preferred_element_type=F32) + b3_ref[...]))
    k = jnp.dot(h, w4_ref[...], precision=HIGHEST, preferred_element_type=F32)
    decay = jnp.exp(-z[:, 0:1] * jnp.abs(dl_ref[...]))
    n = i * tm + lax.broadcasted_iota(jnp.int32, (tm, 1), 0)
    _store_time_padded(o_ref, jnp.where(n == seq_len, 0.0, k * decay))


def _hyena_filter(feat, w1, b1, w2, b2, w3, b3, w4, freq, deltas, seq_len):
    n_rows = feat.shape[0]
    C = HY_WIDTH
    tm = 1024
    fwd_tiles = seq_len // tm
    full = lambda i: (0, 0)
    kern = lambda *refs: _hyena_filter_kernel(*refs, seq_len=seq_len)
    return pl.pallas_call(
        kern,
        out_shape=_time_padded_shape(n_rows, C),
        grid=(n_rows // tm,),
        in_specs=[pl.BlockSpec((tm, LANES), lambda i: (i, 0)),
                  pl.BlockSpec((LANES, LANES), full), pl.BlockSpec((1, LANES), full),
                  pl.BlockSpec((LANES, LANES), full), pl.BlockSpec((1, LANES), full),
                  pl.BlockSpec((LANES, LANES), full), pl.BlockSpec((1, LANES), full),
                  pl.BlockSpec((LANES, C), lambda i: (0, i // fwd_tiles)),
                  pl.BlockSpec((1, LANES), full),
                  pl.BlockSpec((1, C), full)],
        out_specs=pl.BlockSpec((tm // FFT_R, FFT_T_PITCH, C), lambda i: (i, 0, 0)),
        compiler_params=_params("arbitrary"),
        name="hyena_filter",
    )(feat, w1, b1, w2, b2, w3, b3, w4, freq, deltas)


def _hyena_features(seq_len):
    L = seq_len
    n = np.arange(2 * L)
    m = np.where(n <= L, n, 2 * L - n)
    m = np.where(n == L, 0, m)
    t = jnp.linspace(0.0, 1.0, L, dtype=F32)[:, None]
    bands = (HY_EMB - 1) // 2
    w = 2.0 * math.pi * jnp.arange(L, dtype=F32)[:, None] / L
    f = jnp.linspace(1e-4, bands - 1, bands, dtype=F32)[None, :]
    z = jnp.concatenate([t, jnp.cos(f * w), -jnp.sin(f * w)], axis=-1)
    z = jnp.pad(z, ((0, 0), (0, LANES - HY_EMB)))
    max_decay = math.log(HY_TARGET) / HY_FAST_DECAY_PCT
    min_decay = math.log(HY_TARGET) / HY_SLOW_DECAY_PCT
    deltas = jnp.linspace(min_decay, max_decay, HY_WIDTH, dtype=F32)[None, :]
    return z[jnp.asarray(m)], deltas


def _pad2(a, rows, cols):
    return jnp.pad(a, ((0, rows - a.shape[0]), (0, cols - a.shape[1])))


def _dft_matrices():
    R, KH, KP = FFT_R, FFT_KH, FFT_KP
    N = R * R
    n1 = np.arange(R)[:, None, None]
    k2 = np.arange(KP)[None, :, None]
    n2 = np.arange(R)[None, None, :]
    phase = 2.0 * np.pi * ((n2 * k2 % R) / R + (n1 * k2) / N)
    live = (k2 < KH)
    a1 = np.concatenate([np.cos(phase) * live, -np.sin(phase) * live], axis=1)
    wgt = np.where((k2 == 0) | (k2 == R // 2), 1.0, 2.0) * live / N
    b1 = np.concatenate([np.cos(phase) * wgt, -np.sin(phase) * wgt], axis=1)
    b1 = np.transpose(b1, (0, 2, 1))[:, :R // 2, :]
    th = 2.0 * np.pi * (np.arange(R)[:, None] * np.arange(R)[None, :] % R) / R
    c, s = np.cos(th), np.sin(th)
    w2f = np.block([[c, s], [-s, c]])
    w2i = np.block([[c, -s], [s, c]])
    as_bf = lambda a: jnp.asarray(a.astype(np.float32)).astype(BF16)
    return as_bf(a1), as_bf(a1[:, :, :R // 2]), as_bf(b1), as_bf(w2f), as_bf(w2i)


def _fft_stage1(src_ref, a1_ref, g_ref, n_rows):
    def body(n1, carry):
        xs = src_ref[pl.ds(n1, n_rows, stride=FFT_T_PITCH), :].astype(BF16)
        slab = jnp.dot(a1_ref[n1], xs, preferred_element_type=F32)
        g_ref[pl.ds(pl.multiple_of(n1 * FFT_G_PITCH, 8), FFT_SLAB), :] = slab
        return carry
    lax.fori_loop(0, FFT_R, body, 0, unroll=FFT_UNROLL_TIME)


def _load_freq_rows(g_ref, k2):
    re = g_ref[pl.ds(k2, FFT_R, stride=FFT_G_PITCH), :]
    im = g_ref[pl.ds(FFT_KP + k2, FFT_R, stride=FFT_G_PITCH), :]
    return jnp.concatenate([re, im], axis=0)


def _filter_fft_kernel(kern_ref, a1_ref, w2_ref, o_ref, g_ref):
    _fft_stage1(kern_ref, a1_ref, g_ref, FFT_R)

    def body(k2, carry):
        gk = _load_freq_rows(g_ref, k2).astype(BF16)
        o_ref[k2] = jnp.dot(w2_ref[...], gk, preferred_element_type=F32).astype(o_ref.dtype)
        return carry
    lax.fori_loop(0, FFT_KH, body, 0, unroll=FFT_UNROLL_FREQ)


def _filter_fft(kern, a1_full, w2f):
    n_rows, C = kern.shape
    cb = CONV_CB
    return pl.pallas_call(
        _filter_fft_kernel,
        out_shape=jax.ShapeDtypeStruct((FFT_KH, 2 * FFT_R, C), BF16),
        grid=(C // cb,),
        in_specs=[pl.BlockSpec((n_rows, cb), lambda j: (0, j)),
                  pl.BlockSpec((FFT_R, FFT_SLAB, FFT_R), lambda j: (0, 0, 0)),
                  pl.BlockSpec((2 * FFT_R, 2 * FFT_R), lambda j: (0, 0))],
        out_specs=pl.BlockSpec((FFT_KH, 2 * FFT_R, cb), lambda j: (0, 0, j)),
        scratch_shapes=[pltpu.VMEM((FFT_R * FFT_G_PITCH, cb), F32)],
        compiler_params=_params("arbitrary"),
        name="hyena_filter_fft",
    )(kern, a1_full, w2f)


def _short_conv_kernel(x0_ref, x1_ref, v_ref, p0_ref, p1_ref, pv_ref, n0_ref, n1_ref, nv_ref,
                       w0_ref, w1_ref, wv_ref, b0_ref, b1_ref, bv_ref, x0c_ref, zv_ref, scr, *, tiles_per_seq):
    i = pl.program_id(0)
    tm = x0_ref.shape[0]
    first = (i % tiles_per_seq) == 0
    last = (i % tiles_per_seq) == tiles_per_seq - 1
    halo = p0_ref.shape[0]

    def conv(u_ref, prev_ref, next_ref, w_ref, b_ref):
        prev_row = jnp.where(first, 0.0, prev_ref[halo - 1:halo, :].astype(F32))
        next_row = jnp.where(last, 0.0, next_ref[0:1, :].astype(F32))
        scr[7:8, :] = prev_row
        scr[8:8 + tm, :] = u_ref[...].astype(F32)
        scr[8 + tm:9 + tm, :] = next_row
        w = w_ref[...]
        return (scr[pl.ds(7, tm), :] * w[0:1] + scr[pl.ds(8, tm), :] * w[1:2]
                + scr[pl.ds(9, tm), :] * w[2:3] + b_ref[...])

    x0c_ref[...] = conv(x0_ref, p0_ref, n0_ref, w0_ref, b0_ref).astype(x0c_ref.dtype)
    x1c = conv(x1_ref, p1_ref, n1_ref, w1_ref, b1_ref)
    vc = conv(v_ref, pv_ref, nv_ref, wv_ref, bv_ref)
    _store_time_padded(zv_ref, vc * x1c)


def _short_conv(u, w_sc, b_sc, seq_len):
    T = u.shape[0]
    C = HY_WIDTH
    tm, cb, halo = 1024, 256, 16
    ncb = C // cb
    tiles_per_seq = seq_len // tm
    hb = tm // halo
    n_halo = T // halo

    def part(p):
        return pl.BlockSpec((tm, cb), lambda i, j: (i, p * ncb + j))

    def prev(p):
        return pl.BlockSpec((halo, cb), lambda i, j: (jnp.maximum(i * hb - 1, 0), p * ncb + j))

    def nxt(p):
        return pl.BlockSpec((halo, cb), lambda i, j: (jnp.minimum((i + 1) * hb, n_halo - 1), p * ncb + j))

    def wpart(p):
        return pl.BlockSpec((3, cb), lambda i, j: (0, p * ncb + j))

    def bpart(p):
        return pl.BlockSpec((1, cb), lambda i, j: (0, p * ncb + j))

    kern = lambda *refs: _short_conv_kernel(*refs, tiles_per_seq=tiles_per_seq)
    b2 = b_sc.reshape(1, 3 * C)
    return pl.pallas_call(
        kern,
        out_shape=(jax.ShapeDtypeStruct((T, C), BF16), _time_padded_shape(T, C)),
        grid=(T // tm, ncb),
        in_specs=[part(0), part(1), part(2), prev(0), prev(1), prev(2), nxt(0), nxt(1), nxt(2),
                  wpart(0), wpart(1), wpart(2), bpart(0), bpart(1), bpart(2)],
        out_specs=(pl.BlockSpec((tm, cb), lambda i, j: (i, j)),
                   pl.BlockSpec((tm // FFT_R, FFT_T_PITCH, cb), lambda i, j: (i, 0, j))),
        scratch_shapes=[pltpu.VMEM((tm + 16, cb), F32)],
        compiler_params=_params("arbitrary", "arbitrary"),
        name="hyena_short_conv",
    )(u, u, u, u, u, u, u, u, u, w_sc, w_sc, w_sc, b2, b2, b2)


def _long_conv_kernel(zv_ref, x0_ref, kf_ref, bias_ref, a1_ref, b1_ref, w2f_ref, w2i_ref, o_ref, g_ref):
    n2_rows = zv_ref.shape[1] // FFT_T_PITCH
    zv2 = zv_ref.at[0]
    o2 = o_ref.at[0]
    _fft_stage1(zv2, a1_ref, g_ref, n2_rows)

    def freq_body(k2, carry):
        gk = _load_freq_rows(g_ref, k2).astype(BF16)
        x = jnp.dot(w2f_ref[...], gk, preferred_element_type=F32)
        kf = kf_ref[k2].astype(F32)
        xr, xi = x[:FFT_R], x[FFT_R:]
        kr, ki = kf[:FFT_R], kf[FFT_R:]
        p = jnp.concatenate([xr * kr - xi * ki, xr * ki + xi * kr], axis=0).astype(BF16)
        hk = jnp.dot(w2i_ref[...], p, preferred_element_type=F32)
        g_ref[pl.ds(k2, FFT_R, stride=FFT_G_PITCH), :] = hk[:FFT_R]
        g_ref[pl.ds(FFT_KP + k2, FFT_R, stride=FFT_G_PITCH), :] = hk[FFT_R:]
        return carry
    lax.fori_loop(0, FFT_KH, freq_body, 0, unroll=FFT_UNROLL_FREQ)

    def time_body(n1, carry):
        slab = g_ref[pl.ds(pl.multiple_of(n1 * FFT_G_PITCH, 8), FFT_SLAB), :].astype(BF16)
        o2[pl.ds(n1, n2_rows, stride=FFT_T_PITCH), :] = jnp.dot(b1_ref[n1], slab, preferred_element_type=F32)
        return carry
    lax.fori_loop(0, FFT_R, time_body, 0, unroll=FFT_UNROLL_TIME)

    bias = bias_ref[...]
    pad_zeros = jnp.zeros((FFT_T_PITCH - FFT_R, o_ref.shape[2]), F32)

    def out_body(g, carry):
        prow = pl.multiple_of(g * FFT_T_PITCH, 8)
        rows = pl.ds(prow, FFT_R)
        x0 = x0_ref[0, pl.ds(pl.multiple_of(g * FFT_R, FFT_R), FFT_R), :].astype(F32)
        o2[rows, :] = (o2[rows, :] + zv2[rows, :] * bias) * x0
        o2[pl.ds(prow + FFT_R, FFT_T_PITCH - FFT_R), :] = pad_zeros
        return carry
    lax.fori_loop(0, n2_rows, out_body, 0, unroll=4)


def _long_conv(zv, x0c, kf, hy_bias, mats):
    B, Lp, C = zv.shape
    L = x0c.shape[1]
    cb = CONV_CB
    _, a1_half, b1, w2f, w2i = mats
    blk = lambda j, b: (b, 0, j)
    return pl.pallas_call(
        _long_conv_kernel,
        out_shape=jax.ShapeDtypeStruct((B, Lp, C), F32),
        grid=(C // cb, B),
        in_specs=[pl.BlockSpec((1, Lp, cb), blk),
                  pl.BlockSpec((1, L, cb), blk),
                  pl.BlockSpec((FFT_KH, 2 * FFT_R, cb), lambda j, b: (0, 0, j)),
                  pl.BlockSpec((1, cb), lambda j, b: (0, j)),
                  pl.BlockSpec((FFT_R, FFT_SLAB, FFT_R // 2), lambda j, b: (0, 0, 0)),
                  pl.BlockSpec((FFT_R, FFT_R // 2, FFT_SLAB), lambda j, b: (0, 0, 0)),
                  pl.BlockSpec((2 * FFT_R, 2 * FFT_R), lambda j, b: (0, 0)),
                  pl.BlockSpec((2 * FFT_R, 2 * FFT_R), lambda j, b: (0, 0))],
        out_specs=pl.BlockSpec((1, Lp, cb), blk),
        scratch_shapes=[pltpu.VMEM((FFT_R * FFT_G_PITCH, cb), F32)],
        compiler_params=_params("arbitrary", "arbitrary"),
        name="hyena_long_conv",
    )(zv, x0c, kf, hy_bias.reshape(1, C), a1_half, b1, w2f, w2i)


ATT_TQ = 128
ATT_RADIUS = 64
ATT_WINDOW = ATT_TQ + 2 * ATT_RADIUS


ATT_SUB = 4


def _attn_kernel(q_ref, k_ref, v_ref, o_ref, lse_ref):
    i = pl.program_id(2)
    ls = k_ref.shape[1]
    nh = HEADS_PER_GROUP
    head_of_col = lax.broadcasted_iota(jnp.int32, (1, ATT_OUT), 1) // HEAD_DIM
    row_iota = lax.broadcasted_iota(jnp.int32, (nh * ATT_TQ, ATT_WINDOW), 0) % ATT_TQ
    col_iota = lax.broadcasted_iota(jnp.int32, (nh * ATT_TQ, ATT_WINDOW), 1)
    rel = row_iota - col_iota
    for sub in range(ATT_SUB):
        q0 = (i * ATT_SUB + sub) * ATT_TQ
        rows = slice(sub * ATT_TQ, (sub + 1) * ATT_TQ)
        q = q_ref[0, rows, :]
        start = jnp.clip(q0 - ATT_RADIUS, 0, ls - ATT_WINDOW)
        start = pl.multiple_of(start, ATT_RADIUS)
        kw = k_ref[0, pl.ds(start, ATT_WINDOW), :]
        vw = v_ref[0, pl.ds(start, ATT_WINDOW), :]
        out, lse = _attn_unit(q, kw, vw, rel, q0 - start, head_of_col)
        o_ref[0, rows, :] = out.astype(o_ref.dtype)
        lse_ref[0, rows, :] = lse


def _attn_unit(q, kw, vw, rel, q_minus_start, head_of_col):
    nh = HEADS_PER_GROUP
    band = jnp.abs(q_minus_start + rel) <= ATT_RADIUS
    zero = jnp.zeros_like(q)
    q4 = jnp.concatenate([jnp.where(head_of_col == h, q, zero) for h in range(nh)], axis=0)
    s = lax.dot_general(q4, kw, (((1,), (1,)), ((), ())), preferred_element_type=F32)
    s = jnp.where(band, s, MASK_VALUE)
    m = jnp.max(s, axis=-1, keepdims=True)
    p = jnp.exp(s - m)
    den = jnp.sum(p, axis=-1, keepdims=True)
    pv = jnp.dot(p.astype(BF16), vw, preferred_element_type=F32) / den
    lse4 = m + jnp.log(den)
    out = jnp.zeros((ATT_TQ, ATT_OUT), F32)
    lse = jnp.zeros((ATT_TQ, ATT_OUT), F32)
    for h in range(nh):
        mine = head_of_col == h
        hrows = slice(h * ATT_TQ, (h + 1) * ATT_TQ)
        out = jnp.where(mine, pv[hrows], out)
        lse = jnp.where(mine, lse4[hrows], lse)
    return out, lse


ATT_CHUNK = 2048


def _attn_strided_kernel(q0_ref, q1_ref, k0_ref, k1_ref, v0_ref, v1_ref, o_ref, lse_ref, o_scr, lse_scr, *, dil):
    i = pl.program_id(1)
    ls = q0_ref.shape[1] // dil

    def gather(lo_ref, hi_ref, first, n):
        rows = pl.ds(first, n, stride=dil)
        return jnp.concatenate([lo_ref[0, rows, :], hi_ref[0, rows, :]], axis=1).astype(BF16)

    units = ATT_CHUNK // ATT_TQ
    shift = dil.bit_length() - 1
    head_of_col = lax.broadcasted_iota(jnp.int32, (1, ATT_OUT), 1) // HEAD_DIM
    row_iota = lax.broadcasted_iota(jnp.int32, (HEADS_PER_GROUP * ATT_TQ, ATT_WINDOW), 0) % ATT_TQ
    rel = row_iota - lax.broadcasted_iota(jnp.int32, (HEADS_PER_GROUP * ATT_TQ, ATT_WINDOW), 1)

    def unit(u, carry):
        r = u & (dil - 1)
        sb = u >> shift
        m0 = i * (ATT_CHUNK // dil) + sb * ATT_TQ
        start = jnp.clip(m0 - ATT_RADIUS, 0, ls - ATT_WINDOW)
        q = gather(q0_ref, q1_ref, m0 * dil + r, ATT_TQ)
        kw = gather(k0_ref, k1_ref, start * dil + r, ATT_WINDOW)
        vw = gather(v0_ref, v1_ref, start * dil + r, ATT_WINDOW)
        out, lse = _attn_unit(q, kw, vw, rel, m0 - start, head_of_col)
        dst = pl.ds(sb * ATT_TQ * dil + r, ATT_TQ, stride=dil)
        for half in range(2):
            lanes = slice(half * LANES, (half + 1) * LANES)
            o_scr[half, dst, :] = out[:, lanes]
            lse_scr[half, dst, :] = lse[:, lanes]
        return carry
    lax.fori_loop(0, units, unit, 0, unroll=2)
    for half in range(2):
        lanes = slice(half * LANES, (half + 1) * LANES)
        o_ref[0, :, lanes] = o_scr[half]
        lse_ref[0, :, lanes] = lse_scr[half]


def _attention_strided(qkv_g, dil, batch, seq_len):
    view = qkv_g.reshape(batch, seq_len, 3 * ATT_OUT)
    halves = [pl.BlockSpec((1, seq_len, LANES), lambda b, i, c=c: (b, 0, c), pipeline_mode=pl.Buffered(1))
              for c in range(3 * ATT_OUT // LANES)]
    o_spec = pl.BlockSpec((1, ATT_CHUNK, ATT_OUT), lambda b, i: (b, i, 0))
    kern = lambda *refs: _attn_strided_kernel(*refs, dil=dil)
    o, lse = pl.pallas_call(
        kern,
        out_shape=(jax.ShapeDtypeStruct((batch, seq_len, ATT_OUT), F32),
                   jax.ShapeDtypeStruct((batch, seq_len, ATT_OUT), F32)),
        grid=(batch, seq_len // ATT_CHUNK),
        in_specs=halves,
        out_specs=(o_spec, o_spec),
        scratch_shapes=[pltpu.VMEM((2, ATT_CHUNK, LANES), F32), pltpu.VMEM((2, ATT_CHUNK, LANES), F32)],
        compiler_params=_params("arbitrary", "arbitrary"),
        name=f"dilated_attn_d{dil}",
    )(*([view] * len(halves)))
    T = batch * seq_len
    return o.reshape(T, ATT_OUT), lse.reshape(T, ATT_OUT)


def _attention_group(qkv_g, dil, batch, seq_len):
    ls = seq_len // dil
    tq = ATT_SUB * ATT_TQ
    view = qkv_g.reshape(batch, ls, dil * 3 * ATT_OUT)

    def col(which):
        return lambda b, r, i: (b, 0, r * 3 + which)

    q_map = lambda b, r, i: (b, i, r * 3)
    o_map = lambda b, r, i: (b, i, r)
    o, lse = pl.pallas_call(
        _attn_kernel,
        out_shape=(jax.ShapeDtypeStruct((batch, ls, dil * ATT_OUT), BF16),
                   jax.ShapeDtypeStruct((batch, ls, dil * ATT_OUT), F32)),
        grid=(batch, dil, ls // tq),
        in_specs=[pl.BlockSpec((1, tq, ATT_OUT), q_map),
                  pl.BlockSpec((1, ls, ATT_OUT), col(1)),
                  pl.BlockSpec((1, ls, ATT_OUT), col(2))],
        out_specs=(pl.BlockSpec((1, tq, ATT_OUT), o_map), pl.BlockSpec((1, tq, ATT_OUT), o_map)),
        compiler_params=_params("arbitrary", "arbitrary", "arbitrary"),
        name=f"dilated_attn_d{dil}",
    )(view, view, view)
    T = batch * seq_len
    return o.reshape(T, ATT_OUT), lse.reshape(T, ATT_OUT)


def _merge_kernel(x_ref, yhy_ref, o1_ref, o2_ref, o3_ref, l1_ref, l2_ref, l3_ref, gh_ref, ga_ref,
                  wh_ref, wa_ref, wo_ref, g1_ref, out_ref):
    l1, l2, l3 = l1_ref[...], l2_ref[...], l3_ref[...]
    m = jnp.maximum(jnp.maximum(l1, l2), l3)
    e1, e2, e3 = jnp.exp(l1 - m), jnp.exp(l2 - m), jnp.exp(l3 - m)
    tot = e1 + e2 + e3
    y_at = (e1 * o1_ref[...].astype(F32) + e2 * o2_ref[...].astype(F32) + e3 * o3_ref[...].astype(F32)) / tot
    y_hy = yhy_ref[:, :FFT_R, :].reshape(x_ref.shape[0], yhy_ref.shape[2])
    a = jnp.dot(y_hy.astype(BF16), wh_ref[...], preferred_element_type=F32)
    b = jnp.dot(y_at.astype(BF16), wa_ref[...], preferred_element_type=F32)
    merged = _sigmoid(gh_ref[...].astype(F32)) * a + _sigmoid(ga_ref[...].astype(F32)) * b
    upd = jnp.dot(merged.astype(BF16), wo_ref[...], preferred_element_type=F32)
    out_ref[...] = x_ref[...] + g1_ref[0] * upd


def _merge(x, y_hy, attn, gates, w_br_h, w_br_a, w_out, g1, seq_len):
    T, D = x.shape
    tm = 512
    per_batch = seq_len // tm
    (o1, l1), (o2, l2), (o3, l3) = attn
    row = lambda w: pl.BlockSpec((tm, w), lambda i: (i, 0))
    full = lambda a: pl.BlockSpec(a.shape, lambda i: (0, 0))
    return pl.pallas_call(
        _merge_kernel,
        out_shape=jax.ShapeDtypeStruct((T, D), F32),
        grid=(T // tm,),
        in_specs=[row(D), pl.BlockSpec((tm // FFT_R, FFT_T_PITCH, HY_WIDTH), lambda i: (i, 0, 0)),
                  row(ATT_OUT), row(ATT_OUT), row(ATT_OUT),
                  row(ATT_OUT), row(ATT_OUT), row(ATT_OUT),
                  pl.BlockSpec((tm, D), lambda i: (i, 0)), pl.BlockSpec((tm, D), lambda i: (i, 1)),
                  full(w_br_h), full(w_br_a), full(w_out),
                  pl.BlockSpec((1, 1, D), lambda i: (i // per_batch, 0, 0))],
        out_specs=row(D),
        compiler_params=_params("arbitrary"),
        name="mixer_merge",
    )(x, y_hy, o1, o2, o3, l1, l2, l3, gates, gates, w_br_h, w_br_a, w_out, g1)


def _router_kernel(x_ref, g_ref, sc_ref, sh_ref, rwt_ref, rb_ref, hg_ref, stats_ref, cnt_ref, base_ref):
    i = pl.program_id(0)
    tm = x_ref.shape[0]

    @pl.when(i == 0)
    def _():
        base_ref[...] = jnp.zeros_like(base_ref)

    x = x_ref[...]
    r = lax.rsqrt(jnp.mean(x * x, axis=-1, keepdims=True) + NORM_EPS)
    h = (x * r) * g_ref[...]
    h = h * (1.0 + sc_ref[0]) + sh_ref[0]

    logits = lax.dot_general(rwt_ref[...], h, (((1,), (1,)), ((), ())),
                             precision=HIGHEST, preferred_element_type=F32)
    scores = _sigmoid(logits)
    biased = scores + rb_ref[...]

    def row(a, k):
        return a[k:k + 1, :]

    sel = jnp.zeros((1, tm), jnp.int32)
    best = None
    for g in range(N_GROUPS):
        a, b, c, d = (row(biased, 4 * g + k) for k in range(4))
        m_ab, n_ab = jnp.maximum(a, b), jnp.minimum(a, b)
        m_cd, n_cd = jnp.maximum(c, d), jnp.minimum(c, d)
        gs = jnp.maximum(m_ab, m_cd) + jnp.maximum(jnp.minimum(m_ab, m_cd), jnp.maximum(n_ab, n_cd))
        if g == 0:
            best = gs
        else:
            better = gs > best
            sel = jnp.where(better, g, sel)
            best = jnp.where(better, gs, best)

    v, u = [], []
    for k in range(EXPERTS_PER_GROUP):
        vk = jnp.zeros((1, tm), F32)
        uk = jnp.zeros((1, tm), F32)
        for g in range(N_GROUPS):
            vk = jnp.where(sel == g, row(biased, 4 * g + k), vk)
            uk = jnp.where(sel == g, row(scores, 4 * g + k), uk)
        v.append(vk)
        u.append(uk)

    i1 = jnp.zeros((1, tm), jnp.int32)
    b1 = v[0]
    for k in range(1, EXPERTS_PER_GROUP):
        gt = v[k] > b1
        i1 = jnp.where(gt, k, i1)
        b1 = jnp.where(gt, v[k], b1)
    i2 = jnp.zeros((1, tm), jnp.int32)
    b2 = jnp.full((1, tm), -jnp.inf, F32)
    for k in range(EXPERTS_PER_GROUP):
        cand = (i1 != k) & (v[k] > b2)
        i2 = jnp.where(cand, k, i2)
        b2 = jnp.where(cand, v[k], b2)

    lo = jnp.minimum(i1, i2)
    hi = jnp.maximum(i1, i2)
    pair = jnp.where(lo == 0, hi - 1, jnp.where(lo == 1, hi + 1, 5))
    bucket = sel * PAIRS_PER_GROUP + pair

    u_lo = jnp.zeros((1, tm), F32)
    u_hi = jnp.zeros((1, tm), F32)
    for k in range(EXPERTS_PER_GROUP):
        u_lo = jnp.where(lo == k, u[k], u_lo)
        u_hi = jnp.where(hi == k, u[k], u_hi)
    tot = u_lo + u_hi
    w_lo = u_lo / tot
    w_hi = u_hi / tot

    rows = lax.broadcasted_iota(jnp.int32, (BUCKET_ROWS, tm), 0)
    onehot = (rows == bucket).astype(F32)
    t_src = lax.broadcasted_iota(jnp.int32, (tm, tm), 0)
    t_dst = lax.broadcasted_iota(jnp.int32, (tm, tm), 1)
    before = (t_src < t_dst).astype(BF16)
    cum = jnp.dot(onehot.astype(BF16), before, preferred_element_type=F32)
    base = base_ref[...]
    rank = jnp.sum(onehot * (cum + base), axis=0, keepdims=True)
    base = base + jnp.sum(onehot, axis=1, keepdims=True)
    base_ref[...] = base
    cnt_ref[...] = jnp.broadcast_to(base, cnt_ref.shape)

    srow = lax.broadcasted_iota(jnp.int32, (8, tm), 0)
    stats_ref[...] = jnp.where(srow == 0, bucket.astype(F32), jnp.where(srow == 1, rank, 0.0))

    grow = lax.broadcasted_iota(jnp.int32, (GATE_COLS, tm), 0)
    gates_t = jnp.where(grow == 0, w_lo, jnp.where(grow == 1, w_hi, 0.0))
    hg_ref[:, :D_MODEL] = h
    hg_ref[:, D_MODEL:] = gates_t.T


def _router(x, norm_g, sc, sh, router_wt, router_b, seq_len):
    T, D = x.shape
    tm = ROUTER_TM
    per_batch = seq_len // tm
    return pl.pallas_call(
        _router_kernel,
        out_shape=(jax.ShapeDtypeStruct((T, D + GATE_COLS), F32),
                   jax.ShapeDtypeStruct((8, T), F32),
                   jax.ShapeDtypeStruct((BUCKET_ROWS, LANES), F32)),
        grid=(T // tm,),
        in_specs=[pl.BlockSpec((tm, D), lambda i: (i, 0)),
                  pl.BlockSpec((1, D), lambda i: (0, 0)),
                  pl.BlockSpec((1, 1, D), lambda i: (i // per_batch, 0, 0)),
                  pl.BlockSpec((1, 1, D), lambda i: (i // per_batch, 0, 0)),
                  pl.BlockSpec((N_EXPERTS, D), lambda i: (0, 0)),
                  pl.BlockSpec((N_EXPERTS, 1), lambda i: (0, 0))],
        out_specs=(pl.BlockSpec((tm, D + GATE_COLS), lambda i: (i, 0)),
                   pl.BlockSpec((8, tm), lambda i: (0, i)),
                   pl.BlockSpec((BUCKET_ROWS, LANES), lambda i: (0, 0))),
        scratch_shapes=[pltpu.VMEM((BUCKET_ROWS, 1), F32)],
        compiler_params=_params("arbitrary"),
        name="moe_router",
    )(x, norm_g, sc, sh, router_wt, router_b)


def _dispatch_kernel(pos_ref, hg_ref, xs_init_hbm, xs_hbm, sem):
    del xs_init_hbm

    def issue(r, carry):
        pltpu.make_async_copy(hg_ref.at[pl.ds(r, 1), :], xs_hbm.at[pl.ds(pos_ref[r], 1), :], sem).start()
        return carry

    lax.fori_loop(0, ROW_BLOCK, issue, 0, unroll=8)
    pltpu.make_async_copy(hg_ref, xs_hbm.at[pl.ds(0, ROW_BLOCK), :], sem).wait()


def _dispatch(pos, hg, n_rows):
    T, W = hg.shape
    zeros = jnp.zeros((n_rows, W), F32)
    return pl.pallas_call(
        _dispatch_kernel,
        out_shape=jax.ShapeDtypeStruct((n_rows, W), F32),
        grid=(T // ROW_BLOCK,),
        in_specs=[pl.BlockSpec((ROW_BLOCK,), lambda i: (i,), memory_space=pltpu.SMEM),
                  pl.BlockSpec((ROW_BLOCK, W), lambda i: (i, 0)),
                  pl.BlockSpec(memory_space=pl.ANY)],
        out_specs=pl.BlockSpec(memory_space=pl.ANY),
        scratch_shapes=[pltpu.SemaphoreType.DMA],
        input_output_aliases={2: 0},
        compiler_params=_params("arbitrary"),
        name="moe_dispatch",
    )(pos, hg, zeros)


def _expert_kernel(e_lo_ref, e_hi_ref, n_used_ref, xs_ref, w1a, w3a, w2a, w1b, w3b, w2b, y_ref):
    del e_lo_ref, e_hi_ref
    used = pl.program_id(0) < n_used_ref[0]

    @pl.when(jnp.logical_not(used))
    def _():
        y_ref[...] = jnp.zeros_like(y_ref)

    @pl.when(used)
    def _():
        xb = xs_ref[:, :D_MODEL].astype(BF16)
        g_lo = xs_ref[:, D_MODEL:D_MODEL + 1]
        g_hi = xs_ref[:, D_MODEL + 1:D_MODEL + 2]

        def ffn(w1, w3, w2):
            a = jnp.dot(xb, w1[0], preferred_element_type=F32)
            b = jnp.dot(xb, w3[0], preferred_element_type=F32)
            act = (a * _sigmoid(a)) * b
            return jnp.dot(act.astype(BF16), w2[0], preferred_element_type=F32)

        y_ref[...] = g_lo * ffn(w1a, w3a, w2a) + g_hi * ffn(w1b, w3b, w2b)


def _experts(tile_lo, tile_hi, n_used, xs, w1, w3, w2):
    n_rows, W = xs.shape
    D, F = w1.shape[1], w1.shape[2]
    n_tiles = n_rows // EXPERT_TM

    def x_map(j, lo, hi, nu):
        return (jnp.minimum(j, nu[0] - 1), 0)

    def w_lo_map(j, lo, hi, nu):
        return (lo[j], 0, 0)

    def w_hi_map(j, lo, hi, nu):
        return (hi[j], 0, 0)

    grid_spec = pltpu.PrefetchScalarGridSpec(
        num_scalar_prefetch=3,
        grid=(n_tiles,),
        in_specs=[pl.BlockSpec((EXPERT_TM, W), x_map),
                  pl.BlockSpec((1, D, F), w_lo_map), pl.BlockSpec((1, D, F), w_lo_map),
                  pl.BlockSpec((1, F, D), w_lo_map),
                  pl.BlockSpec((1, D, F), w_hi_map), pl.BlockSpec((1, D, F), w_hi_map),
                  pl.BlockSpec((1, F, D), w_hi_map)],
        out_specs=pl.BlockSpec((EXPERT_TM, D), lambda j, lo, hi, nu: (j, 0)),
    )
    return pl.pallas_call(
        _expert_kernel,
        out_shape=jax.ShapeDtypeStruct((n_rows, D), F32),
        grid_spec=grid_spec,
        compiler_params=_params("arbitrary"),
        name="moe_experts",
    )(tile_lo, tile_hi, n_used, xs, w1, w3, w2, w1, w3, w2)


def _combine_kernel(pos_ref, x_ref, g2_ref, ys_hbm, o_ref, buf, sem):
    def issue(r, carry):
        pltpu.make_async_copy(ys_hbm.at[pl.ds(pos_ref[r], 1), :], buf.at[pl.ds(r, 1), :], sem).start()
        return carry

    lax.fori_loop(0, ROW_BLOCK, issue, 0, unroll=8)
    pltpu.make_async_copy(ys_hbm.at[pl.ds(0, ROW_BLOCK), :], buf, sem).wait()
    o_ref[...] = x_ref[...] + g2_ref[0] * buf[...]


def _combine(pos, x, g2, ys, seq_len):
    T, D = x.shape
    per_batch = seq_len // ROW_BLOCK
    return pl.pallas_call(
        _combine_kernel,
        out_shape=jax.ShapeDtypeStruct((T, D), F32),
        grid=(T // ROW_BLOCK,),
        in_specs=[pl.BlockSpec((ROW_BLOCK,), lambda i: (i,), memory_space=pltpu.SMEM),
                  pl.BlockSpec((ROW_BLOCK, D), lambda i: (i, 0)),
                  pl.BlockSpec((1, 1, D), lambda i: (i // per_batch, 0, 0)),
                  pl.BlockSpec(memory_space=pl.ANY)],
        out_specs=pl.BlockSpec((ROW_BLOCK, D), lambda i: (i, 0)),
        scratch_shapes=[pltpu.VMEM((ROW_BLOCK, D), F32), pltpu.SemaphoreType.DMA],
        compiler_params=_params("arbitrary"),
        name="moe_combine",
    )(pos, x, g2, ys)


def _moe_layer(x, norm_g, sc2, sh2, g2, router_wt, router_b, w1, w3, w2, seq_len):
    T, D = x.shape
    hg, stats, counts = _router(x, norm_g, sc2, sh2, router_wt, router_b, seq_len)

    cnt = counts[:N_BUCKETS, 0].astype(jnp.int32)
    padded = ((cnt + EXPERT_TM - 1) // EXPERT_TM) * EXPERT_TM
    ends = jnp.cumsum(padded)
    starts = ends - padded
    bucket = stats[0].astype(jnp.int32)
    pos = starts[bucket] + stats[1].astype(jnp.int32)
    n_tiles = T // EXPERT_TM + N_BUCKETS
    tile_row0 = jnp.arange(n_tiles, dtype=jnp.int32) * EXPERT_TM
    tile_bucket = jnp.sum((ends[None, :] <= tile_row0[:, None]).astype(jnp.int32), axis=1)
    tile_bucket = jnp.minimum(tile_bucket, N_BUCKETS - 1)
    grp, pair = tile_bucket // PAIRS_PER_GROUP, tile_bucket % PAIRS_PER_GROUP
    pair_lo = jnp.array([0, 0, 0, 1, 1, 2], jnp.int32)[pair]
    pair_hi = jnp.array([1, 2, 3, 2, 3, 3], jnp.int32)[pair]
    tile_lo = grp * EXPERTS_PER_GROUP + pair_lo
    tile_hi = grp * EXPERTS_PER_GROUP + pair_hi
    n_used = (ends[-1] // EXPERT_TM).astype(jnp.int32).reshape(1)

    xs = _dispatch(pos, hg, n_tiles * EXPERT_TM)
    ys = _experts(tile_lo, tile_hi, n_used, xs, w1, w3, w2)
    return _combine(pos, x, g2, ys, seq_len)


def _final_norm_kernel(x_ref, g_ref, o_ref):
    x = x_ref[...]
    r = lax.rsqrt(jnp.mean(x * x, axis=-1, keepdims=True) + NORM_EPS)
    o_ref[...] = (x * r) * g_ref[...]


def _final_norm(x, g):
    T, D = x.shape
    tm = 1024
    return pl.pallas_call(
        _final_norm_kernel,
        out_shape=jax.ShapeDtypeStruct((T, D), F32),
        grid=(T // tm,),
        in_specs=[pl.BlockSpec((tm, D), lambda i: (i, 0)), pl.BlockSpec((1, D), lambda i: (0, 0))],
        out_specs=pl.BlockSpec((tm, D), lambda i: (i, 0)),
        compiler_params=_params("arbitrary"),
        name="final_norm",
    )(x, g)


def kernel(x, c, norm1_g, norm2_g, w_ada, b_ada, w_in, w_sc, b_sc, hf_w1, hf_b1, hf_w2, hf_b2, hf_w3, hf_b3,
           hf_w4, hf_freq, hy_bias, w_br_h, w_br_a, w_out, router_w, router_bias, moe_w1, moe_w3, moe_w2, final_g):
    B, L, D = x.shape
    T = B * L
    C = HY_WIDTH
    xt = x.reshape(T, D)

    rope = _rope_tables(L)
    feat, deltas = _hyena_features(L)
    mats = _dft_matrices()

    c_pad = jnp.pad(c, ((0, 8 - B), (0, 0)))
    mod = _ada(c_pad, w_ada, b_ada)[:, :B]
    router_wt = router_w.T
    router_b = router_bias.reshape(N_EXPERTS, 1)

    for i in range(DEPTH):
        sh1, sc1, g1, sh2, sc2, g2 = (mod[i, :, k * D:(k + 1) * D].reshape(B, 1, D) for k in range(6))

        h = _norm_mod(xt, norm1_g[i].reshape(1, D), sc1, sh1, L)
        w_in_b = w_in[i].astype(BF16)
        u = _proj(h, w_in_b, 0, HY_COLS, tn=HY_WIDTH)
        n_groups = len(ATT_GROUPS)
        qkv = [_proj(h, w_in_b, HY_COLS + g * ATT_OUT, 3 * ATT_OUT, rope=rope, seq_len=L, col_step=n_groups,
                     out_dtype=BF16 if dil == 1 else F32) for g, (_, dil) in enumerate(ATT_GROUPS)]
        gates = _proj(h, w_in_b, HY_COLS + QKV_COLS, GATE_COLS2, tn=D_MODEL // 2)

        kern = _hyena_filter(feat, _pad2(hf_w1[i], LANES, LANES), _pad2(hf_b1[i][None], 1, LANES),
                             _pad2(hf_w2[i], LANES, LANES), _pad2(hf_b2[i][None], 1, LANES),
                             _pad2(hf_w3[i], LANES, LANES), _pad2(hf_b3[i][None], 1, LANES),
                             _pad2(hf_w4[i], LANES, 2 * C), _pad2(hf_freq[i][None], 1, LANES), deltas, L)
        kf = _filter_fft(kern.reshape(-1, C), mats[0], mats[3])
        x0c, zv = _short_conv(u, w_sc[i], b_sc[i], L)
        lp = L // FFT_R * FFT_T_PITCH
        y_hy = _long_conv(zv.reshape(B, lp, C), x0c.reshape(B, L, C), kf, hy_bias[i], mats)
        y_hy = y_hy.reshape(T // FFT_R, FFT_T_PITCH, C)

        attn = [(_attention_group if dil == 1 else _attention_strided)(qkv[g], dil, B, L)
                for g, (_, dil) in enumerate(ATT_GROUPS)]
        xt = _merge(xt, y_hy, attn, gates, w_br_h[i].astype(BF16), w_br_a[i].astype(BF16),
                    w_out[i].astype(BF16), g1, L)

        xt = _moe_layer(xt, norm2_g[i].reshape(1, D), sc2, sh2, g2, router_wt, router_b,
                        moe_w1[i].astype(BF16), moe_w3[i].astype(BF16), moe_w2[i].astype(BF16), L)
    return _final_norm(xt, final_g.reshape(1, D)).reshape(B, L, D)
```

```python
import math

import jax
import jax.numpy as jnp
import numpy as np
from jax import lax
from jax.experimental import pallas as pl
from jax.experimental.pallas import tpu as pltpu

D_MODEL = 1024
DEPTH = 2
HY_WIDTH = 768
HY_EMB = 33
HY_FFN = 64
HY_FAST_DECAY_PCT = 0.3
HY_SLOW_DECAY_PCT = 1.5
HY_TARGET = 1e-2
HEAD_DIM = 64
ATT_GROUPS = ((128, 1), (512, 4), (2048, 16))
HEADS_PER_GROUP = 4
N_HEADS = HEADS_PER_GROUP * len(ATT_GROUPS)
ATT_WIDTH = N_HEADS * HEAD_DIM
ATT_OUT = HEADS_PER_GROUP * HEAD_DIM
ROPE_THETA = 10000.0
IN_SPLITS = (3 * HY_WIDTH, ATT_WIDTH, ATT_WIDTH, ATT_WIDTH, D_MODEL, D_MODEL)
N_EXPERTS = 16
N_GROUPS = 4
EXPERTS_PER_GROUP = N_EXPERTS // N_GROUPS
D_EXPERT = 512
NORM_EPS = 1e-6
MASK_VALUE = -1e30

LANES = 128
MXU_DIM = 256
VMEM_LIMIT_BYTES = 56 * 1024 * 1024

F32 = jnp.float32
BF16 = jnp.bfloat16
HIGHEST = lax.Precision.HIGHEST

PROJ_TM = 2048
PROJ_TN = MXU_DIM
HY_COLS = 3 * HY_WIDTH
QKV_COLS = 3 * ATT_WIDTH
GATE_COLS2 = 2 * D_MODEL

FFT_R = 128
FFT_KH = FFT_R // 2 + 1
FFT_KP = 72
FFT_SLAB = 2 * FFT_KP
FFT_G_PITCH = FFT_SLAB + 8
FFT_T_PITCH = FFT_R + 8
CONV_CB = 128
FFT_UNROLL_TIME = 8
FFT_UNROLL_FREQ = 5

PAIRS_PER_GROUP = 6
N_BUCKETS = N_GROUPS * PAIRS_PER_GROUP
BUCKET_ROWS = 32
ROUTER_TM = 512
EXPERT_TM = 256
ROW_BLOCK = 1024
GATE_COLS = LANES


def _params(*sem):
    return pltpu.CompilerParams(dimension_semantics=sem, vmem_limit_bytes=VMEM_LIMIT_BYTES)


def _sigmoid(x):
    return 1.0 / (1.0 + jnp.exp(-x))


def _store_time_padded(o_ref, val):
    groups = val.shape[0] // FFT_R
    o_ref[:, :FFT_R, :] = val.reshape(groups, FFT_R, val.shape[1])
    o_ref[:, FFT_R:, :] = jnp.zeros((groups, FFT_T_PITCH - FFT_R, val.shape[1]), o_ref.dtype)


def _time_padded_shape(rows, cols):
    return jax.ShapeDtypeStruct((rows // FFT_R, FFT_T_PITCH, cols), F32)


def _ada_kernel(c_ref, w_ref, b_ref, o_ref):
    c = c_ref[...]
    c_act = c * _sigmoid(c)
    o_ref[0] = jnp.dot(c_act, w_ref[0], precision=HIGHEST, preferred_element_type=F32) + b_ref[0]


def _ada(c_pad, w_ada, b_ada):
    depth, D, N = w_ada.shape
    rows = c_pad.shape[0]
    tn = N // 4
    return pl.pallas_call(
        _ada_kernel,
        out_shape=jax.ShapeDtypeStruct((depth, rows, N), F32),
        grid=(depth, N // tn),
        in_specs=[pl.BlockSpec((rows, D), lambda l, j: (0, 0)),
                  pl.BlockSpec((1, D, tn), lambda l, j: (l, 0, j)),
                  pl.BlockSpec((1, 1, tn), lambda l, j: (l, 0, j))],
        out_specs=pl.BlockSpec((1, rows, tn), lambda l, j: (l, 0, j)),
        compiler_params=_params("arbitrary", "arbitrary"),
        name="ada_mod",
    )(c_pad, w_ada, b_ada.reshape(depth, 1, N))


def _norm_mod_kernel(x_ref, g_ref, sc_ref, sh_ref, o_ref):
    x = x_ref[...]
    r = lax.rsqrt(jnp.mean(x * x, axis=-1, keepdims=True) + NORM_EPS)
    h = (x * r) * g_ref[...]
    o_ref[...] = (h * (1.0 + sc_ref[0]) + sh_ref[0]).astype(o_ref.dtype)


def _norm_mod(x, g, sc, sh, seq_len):
    T, D = x.shape
    tm = 1024
    per_batch = seq_len // tm
    return pl.pallas_call(
        _norm_mod_kernel,
        out_shape=jax.ShapeDtypeStruct((T, D), BF16),
        grid=(T // tm,),
        in_specs=[pl.BlockSpec((tm, D), lambda i: (i, 0)),
                  pl.BlockSpec((1, D), lambda i: (0, 0)),
                  pl.BlockSpec((1, 1, D), lambda i: (i // per_batch, 0, 0)),
                  pl.BlockSpec((1, 1, D), lambda i: (i // per_batch, 0, 0))],
        out_specs=pl.BlockSpec((tm, D), lambda i: (i, 0)),
        compiler_params=_params("arbitrary"),
        name="norm_mod",
    )(x, g, sc, sh)


def _proj_kernel(h_ref, w_ref, o_ref):
    o_ref[...] = jnp.dot(h_ref[...], w_ref[...], preferred_element_type=F32).astype(o_ref.dtype)


def _proj_rope_kernel(h_ref, w_ref, cos_ref, sin_ref, rot_ref, o_ref, *, n_q_tiles):
    j = pl.program_id(1)
    acc = jnp.dot(h_ref[...], w_ref[...], preferred_element_type=F32)

    @pl.when(j < 2 * n_q_tiles)
    def _():
        cos = jnp.concatenate([cos_ref[...], cos_ref[...]], axis=1)
        sin = jnp.concatenate([sin_ref[...], sin_ref[...]], axis=1)
        swapped = jnp.dot(acc.astype(BF16), rot_ref[...], preferred_element_type=F32)
        out = acc * cos + swapped * sin
        scale = jnp.where(j < n_q_tiles, HEAD_DIM ** -0.5, 1.0)
        o_ref[...] = (out * scale).astype(o_ref.dtype)

    @pl.when(j >= 2 * n_q_tiles)
    def _():
        o_ref[...] = acc.astype(o_ref.dtype)


def _proj(h, w, col0, n_cols, rope=None, seq_len=None, col_step=1, tn=PROJ_TN, out_dtype=BF16):
    T, D = h.shape
    tm = PROJ_TM
    assert col0 % tn == 0 and n_cols % tn == 0 and (rope is None or tn == PROJ_TN)
    off = col0 // tn
    grid = (T // tm, n_cols // tn)
    h_spec = pl.BlockSpec((tm, D), lambda i, j: (i, 0))
    w_spec = pl.BlockSpec((D, tn), lambda i, j: (0, off + j * col_step))
    o_spec = pl.BlockSpec((tm, tn), lambda i, j: (i, j))
    out_shape = jax.ShapeDtypeStruct((T, n_cols), out_dtype)
    if rope is None:
        return pl.pallas_call(_proj_kernel, out_shape=out_shape, grid=grid, in_specs=[h_spec, w_spec],
                              out_specs=o_spec, compiler_params=_params("arbitrary", "arbitrary"),
                              name="proj")(h, w)
    cos_t, sin_t, rot = rope
    per_batch = seq_len // tm
    tab_spec = pl.BlockSpec((tm, LANES), lambda i, j: (i % per_batch, 0))
    kern = lambda *refs: _proj_rope_kernel(*refs, n_q_tiles=1)
    return pl.pallas_call(kern, out_shape=out_shape, grid=grid,
                          in_specs=[h_spec, w_spec, tab_spec, tab_spec,
                                    pl.BlockSpec((tn, tn), lambda i, j: (0, 0))],
                          out_specs=o_spec, compiler_params=_params("arbitrary", "arbitrary"),
                          name="proj_rope")(h, w, cos_t, sin_t, rot)


def _rope_tables(seq_len):
    half = HEAD_DIM // 2
    inv = ROPE_THETA ** (-jnp.arange(half, dtype=F32) / half)
    ang = jnp.arange(seq_len, dtype=F32)[:, None] * inv[None, :]
    reps = LANES // half
    cos_t = jnp.tile(jnp.cos(ang), (1, reps))
    sin_t = jnp.tile(jnp.sin(ang), (1, reps))
    rot = np.zeros((PROJ_TN, PROJ_TN), np.float32)
    for j in range(PROJ_TN):
        if j % HEAD_DIM < half:
            rot[j + half, j] = -1.0
        else:
            rot[j - half, j] = 1.0
    return cos_t, sin_t, jnp.asarray(rot, BF16)


def _hyena_filter_kernel(feat_ref, w1_ref, b1_ref, w2_ref, b2_ref, w3_ref, b3_ref, w4_ref, fr_ref,
                         dl_ref, o_ref):
    i = pl.program_id(0)
    tm = feat_ref.shape[0]
    z = feat_ref[...]
    fr = fr_ref[...]
    h = jnp.sin(fr * (jnp.dot(z, w1_ref[...], precision=HIGHEST, preferred_element_type=F32) + b1_ref[...]))
    h = jnp.sin(fr * (jnp.dot(h, w2_ref[...], precision=HIGHEST, preferred_element_type=F32) + b2_ref[...]))
    h = jnp.sin(fr * (jnp.dot(h, w3_ref[...], precision=HIGHEST, preferred_element_type=F32) + b3_ref[...]))
    k = jnp.dot(h, w4_ref[...], precision=HIGHEST, preferred_element_type=F32)
    decay = jnp.exp(-z[:, 0:1] * jnp.abs(dl_ref[...]))
    pos = i * tm + lax.broadcasted_iota(jnp.int32, (tm, 1), 0)
    is_bwd = lax.broadcasted_iota(jnp.int32, (1, k.shape[1]), 1) >= HY_WIDTH
    _store_time_padded(o_ref, jnp.where((pos == 0) & is_bwd, 0.0, k * decay))


def _hyena_filter(feat, w1, b1, w2, b2, w3, b3, w4, freq, deltas2):
    n_rows = feat.shape[0]
    C2 = 2 * HY_WIDTH
    tm = 1024
    full = lambda i: (0, 0)
    return pl.pallas_call(
        _hyena_filter_kernel,
        out_shape=_time_padded_shape(n_rows, C2),
        grid=(n_rows // tm,),
        in_specs=[pl.BlockSpec((tm, LANES), lambda i: (i, 0)),
                  pl.BlockSpec((LANES, LANES), full), pl.BlockSpec((1, LANES), full),
                  pl.BlockSpec((LANES, LANES), full), pl.BlockSpec((1, LANES), full),
                  pl.BlockSpec((LANES, LANES), full), pl.BlockSpec((1, LANES), full),
                  pl.BlockSpec((LANES, C2), full),
                  pl.BlockSpec((1, LANES), full),
                  pl.BlockSpec((1, C2), full)],
        out_specs=pl.BlockSpec((tm // FFT_R, FFT_T_PITCH, C2), lambda i: (i, 0, 0)),
        compiler_params=_params("arbitrary"),
        name="hyena_filter",
    )(feat, w1, b1, w2, b2, w3, b3, w4, freq, deltas2)


def _hyena_features(seq_len):
    L = seq_len
    t = jnp.linspace(0.0, 1.0, L, dtype=F32)[:, None]
    bands = (HY_EMB - 1) // 2
    w = 2.0 * math.pi * jnp.arange(L, dtype=F32)[:, None] / L
    f = jnp.linspace(1e-4, bands - 1, bands, dtype=F32)[None, :]
    z = jnp.concatenate([t, jnp.cos(f * w), -jnp.sin(f * w)], axis=-1)
    z = jnp.pad(z, ((0, 0), (0, LANES - HY_EMB)))
    max_decay = math.log(HY_TARGET) / HY_FAST_DECAY_PCT
    min_decay = math.log(HY_TARGET) / HY_SLOW_DECAY_PCT
    deltas = jnp.linspace(min_decay, max_decay, HY_WIDTH, dtype=F32)[None, :]
    return z, jnp.concatenate([deltas, deltas], axis=1)


def _pad2(a, rows, cols):
    return jnp.pad(a, ((0, rows - a.shape[0]), (0, cols - a.shape[1])))


def _dft_matrices():
    R, KH, KP = FFT_R, FFT_KH, FFT_KP
    N = R * R
    n1 = np.arange(R)[:, None, None]
    k2 = np.arange(KP)[None, :, None]
    n2 = np.arange(R)[None, None, :]
    phase = 2.0 * np.pi * ((n2 * k2 % R) / R + (n1 * k2) / N)
    live = (k2 < KH)
    a1 = np.concatenate([np.cos(phase) * live, -np.sin(phase) * live], axis=1)
    wgt = np.where((k2 == 0) | (k2 == R // 2), 1.0, 2.0) * live / N
    b1 = np.concatenate([np.cos(phase) * wgt, -np.sin(phase) * wgt], axis=1)
    b1 = np.transpose(b1, (0, 2, 1))[:, :R // 2, :]
    th = 2.0 * np.pi * (np.arange(R)[:, None] * np.arange(R)[None, :] % R) / R
    c, s = np.cos(th), np.sin(th)
    w2f = np.block([[c, s], [-s, c]])
    w2i = np.block([[c, -s], [s, c]])
    w2c = np.block([[c, s], [s, -c]])
    as_bf = lambda a: jnp.asarray(a.astype(np.float32)).astype(BF16)
    return dict(a1=as_bf(a1[:, :, :R // 2]), b1=as_bf(b1), w2f=as_bf(w2f), w2i=as_bf(w2i),
                w2fc=as_bf(np.concatenate([w2f, w2c], axis=1)))


def _fft_stage1(src_ref, a1_ref, g_ref, n_rows):
    def body(n1, carry):
        xs = src_ref[pl.ds(n1, n_rows, stride=FFT_T_PITCH), :].astype(BF16)
        slab = jnp.dot(a1_ref[n1], xs, preferred_element_type=F32)
        g_ref[pl.ds(pl.multiple_of(n1 * FFT_G_PITCH, 8), FFT_SLAB), :] = slab
        return carry
    lax.fori_loop(0, FFT_R, body, 0, unroll=FFT_UNROLL_TIME)


def _load_freq_rows(g_ref, k2):
    re = g_ref[pl.ds(k2, FFT_R, stride=FFT_G_PITCH), :]
    im = g_ref[pl.ds(FFT_KP + k2, FFT_R, stride=FFT_G_PITCH), :]
    return jnp.concatenate([re, im], axis=0)


def _filter_fft_kernel(fwd_ref, bwd_ref, a1_ref, w2_ref, o_ref, gf_ref, gb_ref):
    n2_rows = fwd_ref.shape[0] // FFT_T_PITCH
    _fft_stage1(fwd_ref, a1_ref, gf_ref, n2_rows)
    _fft_stage1(bwd_ref, a1_ref, gb_ref, n2_rows)

    def body(k2, carry):
        gk = jnp.concatenate([_load_freq_rows(gf_ref, k2), _load_freq_rows(gb_ref, k2)], axis=0).astype(BF16)
        o_ref[k2] = jnp.dot(w2_ref[...], gk, preferred_element_type=F32).astype(o_ref.dtype)
        return carry
    lax.fori_loop(0, FFT_KH, body, 0, unroll=FFT_UNROLL_FREQ)


def _filter_fft(k2, mats):
    n_rows = k2.shape[0]
    C = HY_WIDTH
    cb = CONV_CB
    ncb = C // cb
    return pl.pallas_call(
        _filter_fft_kernel,
        out_shape=jax.ShapeDtypeStruct((FFT_KH, 2 * FFT_R, C), BF16),
        grid=(ncb,),
        in_specs=[pl.BlockSpec((n_rows, cb), lambda j: (0, j)),
                  pl.BlockSpec((n_rows, cb), lambda j: (0, ncb + j)),
                  pl.BlockSpec((FFT_R, FFT_SLAB, FFT_R // 2), lambda j: (0, 0, 0)),
                  pl.BlockSpec((2 * FFT_R, 4 * FFT_R), lambda j: (0, 0))],
        out_specs=pl.BlockSpec((FFT_KH, 2 * FFT_R, cb), lambda j: (0, 0, j)),
        scratch_shapes=[pltpu.VMEM((FFT_R * FFT_G_PITCH, cb), F32), pltpu.VMEM((FFT_R * FFT_G_PITCH, cb), F32)],
        compiler_params=_params("arbitrary"),
        name="hyena_filter_fft",
    )(k2, k2, mats["a1"], mats["w2fc"])


def _short_conv_kernel(x0_ref, x1_ref, v_ref, p0_ref, p1_ref, pv_ref, n0_ref, n1_ref, nv_ref,
                       w0_ref, w1_ref, wv_ref, b0_ref, b1_ref, bv_ref, x0c_ref, zv_ref, scr, *, tiles_per_seq):
    i = pl.program_id(0)
    tm = x0_ref.shape[0]
    first = (i % tiles_per_seq) == 0
    last = (i % tiles_per_seq) == tiles_per_seq - 1
    halo = p0_ref.shape[0]

    def conv(u_ref, prev_ref, next_ref, w_ref, b_ref):
        prev_row = jnp.where(first, 0.0, prev_ref[halo - 1:halo, :].astype(F32))
        next_row = jnp.where(last, 0.0, next_ref[0:1, :].astype(F32))
        scr[7:8, :] = prev_row
        scr[8:8 + tm, :] = u_ref[...].astype(F32)
        scr[8 + tm:9 + tm, :] = next_row
        w = w_ref[...]
        return (scr[pl.ds(7, tm), :] * w[0:1] + scr[pl.ds(8, tm), :] * w[1:2]
                + scr[pl.ds(9, tm), :] * w[2:3] + b_ref[...])

    x0c_ref[...] = conv(x0_ref, p0_ref, n0_ref, w0_ref, b0_ref).astype(x0c_ref.dtype)
    x1c = conv(x1_ref, p1_ref, n1_ref, w1_ref, b1_ref)
    vc = conv(v_ref, pv_ref, nv_ref, wv_ref, bv_ref)
    _store_time_padded(zv_ref, vc * x1c)


def _short_conv(u, w_sc, b_sc, seq_len):
    T = u.shape[0]
    C = HY_WIDTH
    tm, cb, halo = 1024, 256, 16
    ncb = C // cb
    tiles_per_seq = seq_len // tm
    hb = tm // halo
    n_halo = T // halo

    def part(p):
        return pl.BlockSpec((tm, cb), lambda i, j: (i, p * ncb + j))

    def prev(p):
        return pl.BlockSpec((halo, cb), lambda i, j: (jnp.maximum(i * hb - 1, 0), p * ncb + j))

    def nxt(p):
        return pl.BlockSpec((halo, cb), lambda i, j: (jnp.minimum((i + 1) * hb, n_halo - 1), p * ncb + j))

    def wpart(p):
        return pl.BlockSpec((3, cb), lambda i, j: (0, p * ncb + j))

    def bpart(p):
        return pl.BlockSpec((1, cb), lambda i, j: (0, p * ncb + j))

    kern = lambda *refs: _short_conv_kernel(*refs, tiles_per_seq=tiles_per_seq)
    b2 = b_sc.reshape(1, 3 * C)
    return pl.pallas_call(
        kern,
        out_shape=(jax.ShapeDtypeStruct((T, C), BF16), _time_padded_shape(T, C)),
        grid=(T // tm, ncb),
        in_specs=[part(0), part(1), part(2), prev(0), prev(1), prev(2), nxt(0), nxt(1), nxt(2),
                  wpart(0), wpart(1), wpart(2), bpart(0), bpart(1), bpart(2)],
        out_specs=(pl.BlockSpec((tm, cb), lambda i, j: (i, j)),
                   pl.BlockSpec((tm // FFT_R, FFT_T_PITCH, cb), lambda i, j: (i, 0, j))),
        scratch_shapes=[pltpu.VMEM((tm + 16, cb), F32)],
        compiler_params=_params("arbitrary", "arbitrary"),
        name="hyena_short_conv",
    )(u, u, u, u, u, u, u, u, u, w_sc, w_sc, w_sc, b2, b2, b2)


def _long_conv_kernel(zv_ref, x0_ref, kf_ref, bias_ref, a1_ref, b1_ref, w2f_ref, w2i_ref, o_ref, g_ref):
    n2_rows = zv_ref.shape[1] // FFT_T_PITCH
    zv2 = zv_ref.at[0]
    o2 = o_ref.at[0]
    _fft_stage1(zv2, a1_ref, g_ref, n2_rows)

    def freq_body(k2, carry):
        gk = _load_freq_rows(g_ref, k2).astype(BF16)
        x = jnp.dot(w2f_ref[...], gk, preferred_element_type=F32)
        kf = kf_ref[k2].astype(F32)
        xr, xi = x[:FFT_R], x[FFT_R:]
        kr, ki = kf[:FFT_R], kf[FFT_R:]
        p = jnp.concatenate([xr * kr - xi * ki, xr * ki + xi * kr], axis=0).astype(BF16)
        hk = jnp.dot(w2i_ref[...], p, preferred_element_type=F32)
        g_ref[pl.ds(k2, FFT_R, stride=FFT_G_PITCH), :] = hk[:FFT_R]
        g_ref[pl.ds(FFT_KP + k2, FFT_R, stride=FFT_G_PITCH), :] = hk[FFT_R:]
        return carry
    lax.fori_loop(0, FFT_KH, freq_body, 0, unroll=FFT_UNROLL_FREQ)

    def time_body(n1, carry):
        slab = g_ref[pl.ds(pl.multiple_of(n1 * FFT_G_PITCH, 8), FFT_SLAB), :].astype(BF16)
        o2[pl.ds(n1, n2_rows, stride=FFT_T_PITCH), :] = jnp.dot(b1_ref[n1], slab, preferred_element_type=F32)
        return carry
    lax.fori_loop(0, FFT_R, time_body, 0, unroll=FFT_UNROLL_TIME)

    bias = bias_ref[...]
    pad_zeros = jnp.zeros((FFT_T_PITCH - FFT_R, o_ref.shape[2]), F32)

    def out_body(g, carry):
        prow = pl.multiple_of(g * FFT_T_PITCH, 8)
        rows = pl.ds(prow, FFT_R)
        x0 = x0_ref[0, pl.ds(pl.multiple_of(g * FFT_R, FFT_R), FFT_R), :].astype(F32)
        o2[rows, :] = (o2[rows, :] + zv2[rows, :] * bias) * x0
        o2[pl.ds(prow + FFT_R, FFT_T_PITCH - FFT_R), :] = pad_zeros
        return carry
    lax.fori_loop(0, n2_rows, out_body, 0, unroll=4)


def _long_conv(zv, x0c, kf, hy_bias, mats):
    B, Lp, C = zv.shape
    L = x0c.shape[1]
    cb = CONV_CB
    blk = lambda j, b: (b, 0, j)
    return pl.pallas_call(
        _long_conv_kernel,
        out_shape=jax.ShapeDtypeStruct((B, Lp, C), F32),
        grid=(C // cb, B),
        in_specs=[pl.BlockSpec((1, Lp, cb), blk),
                  pl.BlockSpec((1, L, cb), blk),
                  pl.BlockSpec((FFT_KH, 2 * FFT_R, cb), lambda j, b: (0, 0, j)),
                  pl.BlockSpec((1, cb), lambda j, b: (0, j)),
                  pl.BlockSpec((FFT_R, FFT_SLAB, FFT_R // 2), lambda j, b: (0, 0, 0)),
                  pl.BlockSpec((FFT_R, FFT_R // 2, FFT_SLAB), lambda j, b: (0, 0, 0)),
                  pl.BlockSpec((2 * FFT_R, 2 * FFT_R), lambda j, b: (0, 0)),
                  pl.BlockSpec((2 * FFT_R, 2 * FFT_R), lambda j, b: (0, 0))],
        out_specs=pl.BlockSpec((1, Lp, cb), blk),
        scratch_shapes=[pltpu.VMEM((FFT_R * FFT_G_PITCH, cb), F32)],
        compiler_params=_params("arbitrary", "arbitrary"),
        name="hyena_long_conv",
    )(zv, x0c, kf, hy_bias.reshape(1, C), mats["a1"], mats["b1"], mats["w2f"], mats["w2i"])


ATT_TQ = 128
ATT_RADIUS = 64
ATT_WINDOW = ATT_TQ + 2 * ATT_RADIUS


ATT_SUB = 4


def _attn_kernel(q_ref, k_ref, v_ref, o_ref, lse_ref):
    i = pl.program_id(2)
    ls = k_ref.shape[1]
    nh = HEADS_PER_GROUP
    head_of_col = lax.broadcasted_iota(jnp.int32, (1, ATT_OUT), 1) // HEAD_DIM
    row_iota = lax.broadcasted_iota(jnp.int32, (nh * ATT_TQ, ATT_WINDOW), 0) % ATT_TQ
    col_iota = lax.broadcasted_iota(jnp.int32, (nh * ATT_TQ, ATT_WINDOW), 1)
    rel = row_iota - col_iota
    for sub in range(ATT_SUB):
        q0 = (i * ATT_SUB + sub) * ATT_TQ
        rows = slice(sub * ATT_TQ, (sub + 1) * ATT_TQ)
        q = q_ref[0, rows, :]
        start = jnp.clip(q0 - ATT_RADIUS, 0, ls - ATT_WINDOW)
        start = pl.multiple_of(start, ATT_RADIUS)
        kw = k_ref[0, pl.ds(start, ATT_WINDOW), :]
        vw = v_ref[0, pl.ds(start, ATT_WINDOW), :]
        out, lse = _attn_unit(q, kw, vw, rel, q0 - start, head_of_col)
        o_ref[0, rows, :] = out.astype(o_ref.dtype)
        lse_ref[0, rows, :] = lse


def _attn_unit(q, kw, vw, rel, q_minus_start, head_of_col):
    nh = HEADS_PER_GROUP
    band = jnp.abs(q_minus_start + rel) <= ATT_RADIUS
    zero = jnp.zeros_like(q)
    q4 = jnp.concatenate([jnp.where(head_of_col == h, q, zero) for h in range(nh)], axis=0)
    s = lax.dot_general(q4, kw, (((1,), (1,)), ((), ())), preferred_element_type=F32)
    s = jnp.where(band, s, MASK_VALUE)
    m = jnp.max(s, axis=-1, keepdims=True)
    p = jnp.exp(s - m)
    den = jnp.sum(p, axis=-1, keepdims=True)
    pv = jnp.dot(p.astype(BF16), vw, preferred_element_type=F32) / den
    lse4 = m + jnp.log(den)
    out = jnp.zeros((ATT_TQ, ATT_OUT), F32)
    lse = jnp.zeros((ATT_TQ, ATT_OUT), F32)
    for h in range(nh):
        mine = head_of_col == h
        hrows = slice(h * ATT_TQ, (h + 1) * ATT_TQ)
        out = jnp.where(mine, pv[hrows], out)
        lse = jnp.where(mine, lse4[hrows], lse)
    return out, lse


ATT_CHUNK = 2048


def _attn_strided_kernel(q0_ref, q1_ref, k0_ref, k1_ref, v0_ref, v1_ref, o_ref, lse_ref, o_scr, lse_scr, *, dil):
    i = pl.program_id(1)
    ls = q0_ref.shape[1] // dil

    def gather(lo_ref, hi_ref, first, n):
        rows = pl.ds(first, n, stride=dil)
        return jnp.concatenate([lo_ref[0, rows, :], hi_ref[0, rows, :]], axis=1).astype(BF16)

    units = ATT_CHUNK // ATT_TQ
    shift = dil.bit_length() - 1
    head_of_col = lax.broadcasted_iota(jnp.int32, (1, ATT_OUT), 1) // HEAD_DIM
    row_iota = lax.broadcasted_iota(jnp.int32, (HEADS_PER_GROUP * ATT_TQ, ATT_WINDOW), 0) % ATT_TQ
    rel = row_iota - lax.broadcasted_iota(jnp.int32, (HEADS_PER_GROUP * ATT_TQ, ATT_WINDOW), 1)

    def unit(u, carry):
        r = u & (dil - 1)
        sb = u >> shift
        m0 = i * (ATT_CHUNK // dil) + sb * ATT_TQ
        start = jnp.clip(m0 - ATT_RADIUS, 0, ls - ATT_WINDOW)
        q = gather(q0_ref, q1_ref, m0 * dil + r, ATT_TQ)
        kw = gather(k0_ref, k1_ref, start * dil + r, ATT_WINDOW)
        vw = gather(v0_ref, v1_ref, start * dil + r, ATT_WINDOW)
        out, lse = _attn_unit(q, kw, vw, rel, m0 - start, head_of_col)
        dst = pl.ds(sb * ATT_TQ * dil + r, ATT_TQ, stride=dil)
        for half in range(2):
            lanes = slice(half * LANES, (half + 1) * LANES)
            o_scr[half, dst, :] = out[:, lanes]
            lse_scr[half, dst, :] = lse[:, lanes]
        return carry
    lax.fori_loop(0, units, unit, 0, unroll=2)
    for half in range(2):
        lanes = slice(half * LANES, (half + 1) * LANES)
        o_ref[0, :, lanes] = o_scr[half]
        lse_ref[0, :, lanes] = lse_scr[half]


def _attention_strided(qkv_g, dil, batch, seq_len):
    view = qkv_g.reshape(batch, seq_len, 3 * ATT_OUT)
    halves = [pl.BlockSpec((1, seq_len, LANES), lambda b, i, c=c: (b, 0, c), pipeline_mode=pl.Buffered(1))
              for c in range(3 * ATT_OUT // LANES)]
    o_spec = pl.BlockSpec((1, ATT_CHUNK, ATT_OUT), lambda b, i: (b, i, 0))
    kern = lambda *refs: _attn_strided_kernel(*refs, dil=dil)
    o, lse = pl.pallas_call(
        kern,
        out_shape=(jax.ShapeDtypeStruct((batch, seq_len, ATT_OUT), F32),
                   jax.ShapeDtypeStruct((batch, seq_len, ATT_OUT), F32)),
        grid=(batch, seq_len // ATT_CHUNK),
        in_specs=halves,
        out_specs=(o_spec, o_spec),
        scratch_shapes=[pltpu.VMEM((2, ATT_CHUNK, LANES), F32), pltpu.VMEM((2, ATT_CHUNK, LANES), F32)],
        compiler_params=_params("arbitrary", "arbitrary"),
        name=f"dilated_attn_d{dil}",
    )(*([view] * len(halves)))
    T = batch * seq_len
    return o.reshape(T, ATT_OUT), lse.reshape(T, ATT_OUT)


def _attention_group(qkv_g, dil, batch, seq_len):
    ls = seq_len // dil
    tq = ATT_SUB * ATT_TQ
    view = qkv_g.reshape(batch, ls, dil * 3 * ATT_OUT)

    def col(which):
        return lambda b, r, i: (b, 0, r * 3 + which)

    q_map = lambda b, r, i: (b, i, r * 3)
    o_map = lambda b, r, i: (b, i, r)
    o, lse = pl.pallas_call(
        _attn_kernel,
        out_shape=(jax.ShapeDtypeStruct((batch, ls, dil * ATT_OUT), BF16),
                   jax.ShapeDtypeStruct((batch, ls, dil * ATT_OUT), F32)),
        grid=(batch, dil, ls // tq),
        in_specs=[pl.BlockSpec((1, tq, ATT_OUT), q_map),
                  pl.BlockSpec((1, ls, ATT_OUT), col(1)),
                  pl.BlockSpec((1, ls, ATT_OUT), col(2))],
        out_specs=(pl.BlockSpec((1, tq, ATT_OUT), o_map), pl.BlockSpec((1, tq, ATT_OUT), o_map)),
        compiler_params=_params("arbitrary", "arbitrary", "arbitrary"),
        name=f"dilated_attn_d{dil}",
    )(view, view, view)
    T = batch * seq_len
    return o.reshape(T, ATT_OUT), lse.reshape(T, ATT_OUT)


def _merge_kernel(x_ref, yhy_ref, o1_ref, o2_ref, o3_ref, l1_ref, l2_ref, l3_ref, gh_ref, ga_ref,
                  wh_ref, wa_ref, wo_ref, g1_ref, out_ref):
    l1, l2, l3 = l1_ref[...], l2_ref[...], l3_ref[...]
    m = jnp.maximum(jnp.maximum(l1, l2), l3)
    e1, e2, e3 = jnp.exp(l1 - m), jnp.exp(l2 - m), jnp.exp(l3 - m)
    tot = e1 + e2 + e3
    y_at = (e1 * o1_ref[...].astype(F32) + e2 * o2_ref[...].astype(F32) + e3 * o3_ref[...].astype(F32)) / tot
    y_hy = yhy_ref[:, :FFT_R, :].reshape(x_ref.shape[0], yhy_ref.shape[2])
    a = jnp.dot(y_hy.astype(BF16), wh_ref[...], preferred_element_type=F32)
    b = jnp.dot(y_at.astype(BF16), wa_ref[...], preferred_element_type=F32)
    merged = _sigmoid(gh_ref[...].astype(F32)) * a + _sigmoid(ga_ref[...].astype(F32)) * b
    upd = jnp.dot(merged.astype(BF16), wo_ref[...], preferred_element_type=F32)
    out_ref[...] = x_ref[...] + g1_ref[0] * upd


def _merge(x, y_hy, attn, gates, w_br_h, w_br_a, w_out, g1, seq_len):
    T, D = x.shape
    tm = 512
    per_batch = seq_len // tm
    (o1, l1), (o2, l2), (o3, l3) = attn
    row = lambda w: pl.BlockSpec((tm, w), lambda i: (i, 0))
    full = lambda a: pl.BlockSpec(a.shape, lambda i: (0, 0))
    return pl.pallas_call(
        _merge_kernel,
        out_shape=jax.ShapeDtypeStruct((T, D), F32),
        grid=(T // tm,),
        in_specs=[row(D), pl.BlockSpec((tm // FFT_R, FFT_T_PITCH, HY_WIDTH), lambda i: (i, 0, 0)),
                  row(ATT_OUT), row(ATT_OUT), row(ATT_OUT),
                  row(ATT_OUT), row(ATT_OUT), row(ATT_OUT),
                  pl.BlockSpec((tm, D), lambda i: (i, 0)), pl.BlockSpec((tm, D), lambda i: (i, 1)),
                  full(w_br_h), full(w_br_a), full(w_out),
                  pl.BlockSpec((1, 1, D), lambda i: (i // per_batch, 0, 0))],
        out_specs=row(D),
        compiler_params=_params("arbitrary"),
        name="mixer_merge",
    )(x, y_hy, o1, o2, o3, l1, l2, l3, gates, gates, w_br_h, w_br_a, w_out, g1)


def _router_kernel(x_ref, g_ref, sc_ref, sh_ref, rwt_ref, rb_ref, hg_ref, stats_ref, cnt_ref, base_ref):
    i = pl.program_id(0)
    tm = x_ref.shape[0]

    @pl.when(i == 0)
    def _():
        base_ref[...] = jnp.zeros_like(base_ref)

    x = x_ref[...]
    r = lax.rsqrt(jnp.mean(x * x, axis=-1, keepdims=True) + NORM_EPS)
    h = (x * r) * g_ref[...]
    h = h * (1.0 + sc_ref[0]) + sh_ref[0]

    logits = lax.dot_general(rwt_ref[...], h, (((1,), (1,)), ((), ())),
                             precision=HIGHEST, preferred_element_type=F32)
    scores = _sigmoid(logits)
    biased = scores + rb_ref[...]

    def row(a, k):
        return a[k:k + 1, :]

    sel = jnp.zeros((1, tm), jnp.int32)
    best = None
    for g in range(N_GROUPS):
        a, b, c, d = (row(biased, 4 * g + k) for k in range(4))
        m_ab, n_ab = jnp.maximum(a, b), jnp.minimum(a, b)
        m_cd, n_cd = jnp.maximum(c, d), jnp.minimum(c, d)
        gs = jnp.maximum(m_ab, m_cd) + jnp.maximum(jnp.minimum(m_ab, m_cd), jnp.maximum(n_ab, n_cd))
        if g == 0:
            best = gs
        else:
            better = gs > best
            sel = jnp.where(better, g, sel)
            best = jnp.where(better, gs, best)

    v, u = [], []
    for k in range(EXPERTS_PER_GROUP):
        vk = jnp.zeros((1, tm), F32)
        uk = jnp.zeros((1, tm), F32)
        for g in range(N_GROUPS):
            vk = jnp.where(sel == g, row(biased, 4 * g + k), vk)
            uk = jnp.where(sel == g, row(scores, 4 * g + k), uk)
        v.append(vk)
        u.append(uk)

    i1 = jnp.zeros((1, tm), jnp.int32)
    b1 = v[0]
    for k in range(1, EXPERTS_PER_GROUP):
        gt = v[k] > b1
        i1 = jnp.where(gt, k, i1)
        b1 = jnp.where(gt, v[k], b1)
    i2 = jnp.zeros((1, tm), jnp.int32)
    b2 = jnp.full((1, tm), -jnp.inf, F32)
    for k in range(EXPERTS_PER_GROUP):
        cand = (i1 != k) & (v[k] > b2)
        i2 = jnp.where(cand, k, i2)
        b2 = jnp.where(cand, v[k], b2)

    lo = jnp.minimum(i1, i2)
    hi = jnp.maximum(i1, i2)
    pair = jnp.where(lo == 0, hi - 1, jnp.where(lo == 1, hi + 1, 5))
    bucket = sel * PAIRS_PER_GROUP + pair

    u_lo = jnp.zeros((1, tm), F32)
    u_hi = jnp.zeros((1, tm), F32)
    for k in range(EXPERTS_PER_GROUP):
        u_lo = jnp.where(lo == k, u[k], u_lo)
        u_hi = jnp.where(hi == k, u[k], u_hi)
    tot = u_lo + u_hi
    w_lo = u_lo / tot
    w_hi = u_hi / tot

    rows = lax.broadcasted_iota(jnp.int32, (BUCKET_ROWS, tm), 0)
    onehot = (rows == bucket).astype(F32)
    t_src = lax.broadcasted_iota(jnp.int32, (tm, tm), 0)
    t_dst = lax.broadcasted_iota(jnp.int32, (tm, tm), 1)
    before = (t_src < t_dst).astype(BF16)
    cum = jnp.dot(onehot.astype(BF16), before, preferred_element_type=F32)
    base = base_ref[...]
    rank = jnp.sum(onehot * (cum + base), axis=0, keepdims=True)
    base = base + jnp.sum(onehot, axis=1, keepdims=True)
    base_ref[...] = base
    cnt_ref[...] = jnp.broadcast_to(base, cnt_ref.shape)

    srow = lax.broadcasted_iota(jnp.int32, (8, tm), 0)
    stats_ref[...] = jnp.where(srow == 0, bucket.astype(F32), jnp.where(srow == 1, rank, 0.0))

    grow = lax.broadcasted_iota(jnp.int32, (GATE_COLS, tm), 0)
    gates_t = jnp.where(grow == 0, w_lo, jnp.where(grow == 1, w_hi, 0.0))
    hg_ref[:, :D_MODEL] = h
    hg_ref[:, D_MODEL:] = gates_t.T


def _router(x, norm_g, sc, sh, router_wt, router_b, seq_len):
    T, D = x.shape
    tm = ROUTER_TM
    per_batch = seq_len // tm
    return pl.pallas_call(
        _router_kernel,
        out_shape=(jax.ShapeDtypeStruct((T, D + GATE_COLS), F32),
                   jax.ShapeDtypeStruct((8, T), F32),
                   jax.ShapeDtypeStruct((BUCKET_ROWS, LANES), F32)),
        grid=(T // tm,),
        in_specs=[pl.BlockSpec((tm, D), lambda i: (i, 0)),
                  pl.BlockSpec((1, D), lambda i: (0, 0)),
                  pl.BlockSpec((1, 1, D), lambda i: (i // per_batch, 0, 0)),
                  pl.BlockSpec((1, 1, D), lambda i: (i // per_batch, 0, 0)),
                  pl.BlockSpec((N_EXPERTS, D), lambda i: (0, 0)),
                  pl.BlockSpec((N_EXPERTS, 1), lambda i: (0, 0))],
        out_specs=(pl.BlockSpec((tm, D + GATE_COLS), lambda i: (i, 0)),
                   pl.BlockSpec((8, tm), lambda i: (0, i)),
                   pl.BlockSpec((BUCKET_ROWS, LANES), lambda i: (0, 0))),
        scratch_shapes=[pltpu.VMEM((BUCKET_ROWS, 1), F32)],
        compiler_params=_params("arbitrary"),
        name="moe_router",
    )(x, norm_g, sc, sh, router_wt, router_b)


def _start_row_copies(n_rows, make_copy):
    group = 8

    def body(g, carry):
        base = pl.multiple_of(g * group, group)
        for k in range(group):
            make_copy(base + k).start(priority=k % 2)
        return carry

    lax.fori_loop(0, n_rows // group, body, 0)


def _dispatch_kernel(pos_ref, hg_ref, xs_init_hbm, xs_hbm, sem):
    del xs_init_hbm
    _start_row_copies(ROW_BLOCK, lambda r: pltpu.make_async_copy(
        hg_ref.at[pl.ds(r, 1), :], xs_hbm.at[pl.ds(pos_ref[r], 1), :], sem))
    pltpu.make_async_copy(hg_ref, xs_hbm.at[pl.ds(0, ROW_BLOCK), :], sem).wait()


def _dispatch(pos, hg, n_rows):
    T, W = hg.shape
    zeros = jnp.zeros((n_rows, W), F32)
    return pl.pallas_call(
        _dispatch_kernel,
        out_shape=jax.ShapeDtypeStruct((n_rows, W), F32),
        grid=(T // ROW_BLOCK,),
        in_specs=[pl.BlockSpec((ROW_BLOCK,), lambda i: (i,), memory_space=pltpu.SMEM),
                  pl.BlockSpec((ROW_BLOCK, W), lambda i: (i, 0)),
                  pl.BlockSpec(memory_space=pl.ANY)],
        out_specs=pl.BlockSpec(memory_space=pl.ANY),
        scratch_shapes=[pltpu.SemaphoreType.DMA],
        input_output_aliases={2: 0},
        compiler_params=_params("arbitrary"),
        name="moe_dispatch",
    )(pos, hg, zeros)


def _expert_kernel(e_lo_ref, e_hi_ref, n_used_ref, xs_ref, w1a, w3a, w2a, w1b, w3b, w2b, y_ref):
    del e_lo_ref, e_hi_ref
    used = pl.program_id(0) < n_used_ref[0]

    @pl.when(jnp.logical_not(used))
    def _():
        y_ref[...] = jnp.zeros_like(y_ref)

    @pl.when(used)
    def _():
        xb = xs_ref[:, :D_MODEL].astype(BF16)
        g_lo = xs_ref[:, D_MODEL:D_MODEL + 1]
        g_hi = xs_ref[:, D_MODEL + 1:D_MODEL + 2]

        def ffn(w1, w3, w2):
            a = jnp.dot(xb, w1[0], preferred_element_type=F32)
            b = jnp.dot(xb, w3[0], preferred_element_type=F32)
            act = (a * _sigmoid(a)) * b
            return jnp.dot(act.astype(BF16), w2[0], preferred_element_type=F32)

        y_ref[...] = g_lo * ffn(w1a, w3a, w2a) + g_hi * ffn(w1b, w3b, w2b)


def _experts(tile_lo, tile_hi, n_used, xs, w1, w3, w2):
    n_rows, W = xs.shape
    D, F = w1.shape[1], w1.shape[2]
    n_tiles = n_rows // EXPERT_TM

    def x_map(j, lo, hi, nu):
        return (jnp.minimum(j, nu[0] - 1), 0)

    def w_lo_map(j, lo, hi, nu):
        return (lo[j], 0, 0)

    def w_hi_map(j, lo, hi, nu):
        return (hi[j], 0, 0)

    grid_spec = pltpu.PrefetchScalarGridSpec(
        num_scalar_prefetch=3,
        grid=(n_tiles,),
        in_specs=[pl.BlockSpec((EXPERT_TM, W), x_map),
                  pl.BlockSpec((1, D, F), w_lo_map), pl.BlockSpec((1, D, F), w_lo_map),
                  pl.BlockSpec((1, F, D), w_lo_map),
                  pl.BlockSpec((1, D, F), w_hi_map), pl.BlockSpec((1, D, F), w_hi_map),
                  pl.BlockSpec((1, F, D), w_hi_map)],
        out_specs=pl.BlockSpec((EXPERT_TM, D), lambda j, lo, hi, nu: (j, 0)),
    )
    return pl.pallas_call(
        _expert_kernel,
        out_shape=jax.ShapeDtypeStruct((n_rows, D), F32),
        grid_spec=grid_spec,
        compiler_params=_params("arbitrary"),
        name="moe_experts",
    )(tile_lo, tile_hi, n_used, xs, w1, w3, w2, w1, w3, w2)


def _combine_kernel(pos_ref, x_ref, g2_ref, ys_hbm, o_ref, buf, sem):
    _start_row_copies(ROW_BLOCK, lambda r: pltpu.make_async_copy(
        ys_hbm.at[pl.ds(pos_ref[r], 1), :], buf.at[pl.ds(r, 1), :], sem))
    pltpu.make_async_copy(ys_hbm.at[pl.ds(0, ROW_BLOCK), :], buf, sem).wait()
    o_ref[...] = x_ref[...] + g2_ref[0] * buf[...]


def _combine(pos, x, g2, ys, seq_len):
    T, D = x.shape
    per_batch = seq_len // ROW_BLOCK
    return pl.pallas_call(
        _combine_kernel,
        out_shape=jax.ShapeDtypeStruct((T, D), F32),
        grid=(T // ROW_BLOCK,),
        in_specs=[pl.BlockSpec((ROW_BLOCK,), lambda i: (i,), memory_space=pltpu.SMEM),
                  pl.BlockSpec((ROW_BLOCK, D), lambda i: (i, 0)),
                  pl.BlockSpec((1, 1, D), lambda i: (i // per_batch, 0, 0)),
                  pl.BlockSpec(memory_space=pl.ANY)],
        out_specs=pl.BlockSpec((ROW_BLOCK, D), lambda i: (i, 0)),
        scratch_shapes=[pltpu.VMEM((ROW_BLOCK, D), F32), pltpu.SemaphoreType.DMA],
        compiler_params=_params("arbitrary"),
        name="moe_combine",
    )(pos, x, g2, ys)


def _moe_layer(x, norm_g, sc2, sh2, g2, router_wt, router_b, w1, w3, w2, seq_len):
    T, D = x.shape
    hg, stats, counts = _router(x, norm_g, sc2, sh2, router_wt, router_b, seq_len)

    cnt = counts[:N_BUCKETS, 0].astype(jnp.int32)
    padded = ((cnt + EXPERT_TM - 1) // EXPERT_TM) * EXPERT_TM
    ends = jnp.cumsum(padded)
    starts = ends - padded
    bucket = stats[0].astype(jnp.int32)
    pos = starts[bucket] + stats[1].astype(jnp.int32)
    n_tiles = T // EXPERT_TM + N_BUCKETS
    tile_row0 = jnp.arange(n_tiles, dtype=jnp.int32) * EXPERT_TM
    tile_bucket = jnp.sum((ends[None, :] <= tile_row0[:, None]).astype(jnp.int32), axis=1)
    tile_bucket = jnp.minimum(tile_bucket, N_BUCKETS - 1)
    grp, pair = tile_bucket // PAIRS_PER_GROUP, tile_bucket % PAIRS_PER_GROUP
    pair_lo = jnp.array([0, 0, 0, 1, 1, 2], jnp.int32)[pair]
    pair_hi = jnp.array([1, 2, 3, 2, 3, 3], jnp.int32)[pair]
    tile_lo = grp * EXPERTS_PER_GROUP + pair_lo
    tile_hi = grp * EXPERTS_PER_GROUP + pair_hi
    n_used = (ends[-1] // EXPERT_TM).astype(jnp.int32).reshape(1)

    xs = _dispatch(pos, hg, n_tiles * EXPERT_TM)
    ys = _experts(tile_lo, tile_hi, n_used, xs, w1, w3, w2)
    return _combine(pos, x, g2, ys, seq_len)


def _final_norm_kernel(x_ref, g_ref, o_ref):
    x = x_ref[...]
    r = lax.rsqrt(jnp.mean(x * x, axis=-1, keepdims=True) + NORM_EPS)
    o_ref[...] = (x * r) * g_ref[...]


def _final_norm(x, g):
    T, D = x.shape
    tm = 1024
    return pl.pallas_call(
        _final_norm_kernel,
        out_shape=jax.ShapeDtypeStruct((T, D), F32),
        grid=(T // tm,),
        in_specs=[pl.BlockSpec((tm, D), lambda i: (i, 0)), pl.BlockSpec((1, D), lambda i: (0, 0))],
        out_specs=pl.BlockSpec((tm, D), lambda i: (i, 0)),
        compiler_params=_params("arbitrary"),
        name="final_norm",
    )(x, g)


def kernel(x, c, norm1_g, norm2_g, w_ada, b_ada, w_in, w_sc, b_sc, hf_w1, hf_b1, hf_w2, hf_b2, hf_w3, hf_b3,
           hf_w4, hf_freq, hy_bias, w_br_h, w_br_a, w_out, router_w, router_bias, moe_w1, moe_w3, moe_w2, final_g):
    B, L, D = x.shape
    T = B * L
    C = HY_WIDTH
    xt = x.reshape(T, D)

    rope = _rope_tables(L)
    feat, deltas = _hyena_features(L)
    mats = _dft_matrices()

    c_pad = jnp.pad(c, ((0, 8 - B), (0, 0)))
    mod = _ada(c_pad, w_ada, b_ada)[:, :B]
    router_wt = router_w.T
    router_b = router_bias.reshape(N_EXPERTS, 1)

    for i in range(DEPTH):
        sh1, sc1, g1, sh2, sc2, g2 = (mod[i, :, k * D:(k + 1) * D].reshape(B, 1, D) for k in range(6))

        h = _norm_mod(xt, norm1_g[i].reshape(1, D), sc1, sh1, L)
        w_in_b = w_in[i].astype(BF16)
        u = _proj(h, w_in_b, 0, HY_COLS, tn=HY_WIDTH)
        n_groups = len(ATT_GROUPS)
        qkv = [_proj(h, w_in_b, HY_COLS + g * ATT_OUT, 3 * ATT_OUT, rope=rope, seq_len=L, col_step=n_groups,
                     out_dtype=BF16 if dil == 1 else F32) for g, (_, dil) in enumerate(ATT_GROUPS)]
        gates = _proj(h, w_in_b, HY_COLS + QKV_COLS, GATE_COLS2, tn=D_MODEL // 2)

        k2 = _hyena_filter(feat, _pad2(hf_w1[i], LANES, LANES), _pad2(hf_b1[i][None], 1, LANES),
                           _pad2(hf_w2[i], LANES, LANES), _pad2(hf_b2[i][None], 1, LANES),
                           _pad2(hf_w3[i], LANES, LANES), _pad2(hf_b3[i][None], 1, LANES),
                           _pad2(hf_w4[i], LANES, 2 * C), _pad2(hf_freq[i][None], 1, LANES), deltas)
        kf = _filter_fft(k2.reshape(-1, 2 * C), mats)
        x0c, zv = _short_conv(u, w_sc[i], b_sc[i], L)
        lp = L // FFT_R * FFT_T_PITCH
        y_hy = _long_conv(zv.reshape(B, lp, C), x0c.reshape(B, L, C), kf, hy_bias[i], mats)
        y_hy = y_hy.reshape(T // FFT_R, FFT_T_PITCH, C)

        attn = [(_attention_group if dil == 1 else _attention_strided)(qkv[g], dil, B, L)
                for g, (_, dil) in enumerate(ATT_GROUPS)]
        xt = _merge(xt, y_hy, attn, gates, w_br_h[i].astype(BF16), w_br_a[i].astype(BF16),
                    w_out[i].astype(BF16), g1, L)

        xt = _moe_layer(xt, norm2_g[i].reshape(1, D), sc2, sh2, g2, router_wt, router_b,
                        moe_w1[i].astype(BF16), moe_w3[i].astype(BF16), moe_w2[i].astype(BF16), L)
    return _final_norm(xt, final_g.reshape(1, D)).reshape(B, L, D)
```

```python
import math

import jax
import jax.numpy as jnp
import numpy as np
from jax import lax
from jax.experimental import pallas as pl
from jax.experimental.pallas import tpu as pltpu

D_MODEL = 1024
DEPTH = 2
HY_WIDTH = 768
HY_EMB = 33
HY_FFN = 64
HY_FAST_DECAY_PCT = 0.3
HY_SLOW_DECAY_PCT = 1.5
HY_TARGET = 1e-2
HEAD_DIM = 64
ATT_GROUPS = ((128, 1), (512, 4), (2048, 16))
HEADS_PER_GROUP = 4
N_HEADS = HEADS_PER_GROUP * len(ATT_GROUPS)
ATT_WIDTH = N_HEADS * HEAD_DIM
ATT_OUT = HEADS_PER_GROUP * HEAD_DIM
ROPE_THETA = 10000.0
IN_SPLITS = (3 * HY_WIDTH, ATT_WIDTH, ATT_WIDTH, ATT_WIDTH, D_MODEL, D_MODEL)
N_EXPERTS = 16
N_GROUPS = 4
EXPERTS_PER_GROUP = N_EXPERTS // N_GROUPS
D_EXPERT = 512
NORM_EPS = 1e-6
MASK_VALUE = -1e30

LANES = 128
MXU_DIM = 256
VMEM_LIMIT_BYTES = 56 * 1024 * 1024

F32 = jnp.float32
BF16 = jnp.bfloat16
HIGHEST = lax.Precision.HIGHEST

PROJ_TM = 2048
PROJ_TN = MXU_DIM
HY_COLS = 3 * HY_WIDTH
QKV_COLS = 3 * ATT_WIDTH
GATE_COLS2 = 2 * D_MODEL

FFT_R = 128
FFT_KH = FFT_R // 2 + 1
FFT_KP = 72
FFT_SLAB = 2 * FFT_KP
FFT_G_PITCH = FFT_SLAB + 8
FFT_T_PITCH = FFT_R + 8
CONV_CB = 128
FFT_UNROLL_TIME = 8
FFT_UNROLL_FREQ = 5

PAIRS_PER_GROUP = 6
N_BUCKETS = N_GROUPS * PAIRS_PER_GROUP
BUCKET_ROWS = 32
ROUTER_TM = 512
EXPERT_TM = 256
ROW_BLOCK = 1024
GATE_COLS = LANES


def _params(*sem):
    return pltpu.CompilerParams(dimension_semantics=sem, vmem_limit_bytes=VMEM_LIMIT_BYTES)


def _sigmoid(x):
    return 1.0 / (1.0 + jnp.exp(-x))


def _store_time_padded(o_ref, val):
    groups = val.shape[0] // FFT_R
    o_ref[:, :FFT_R, :] = val.reshape(groups, FFT_R, val.shape[1])
    o_ref[:, FFT_R:, :] = jnp.zeros((groups, FFT_T_PITCH - FFT_R, val.shape[1]), o_ref.dtype)


def _time_padded_shape(rows, cols):
    return jax.ShapeDtypeStruct((rows // FFT_R, FFT_T_PITCH, cols), F32)


def _ada_kernel(c_ref, w_ref, b_ref, o_ref):
    c = c_ref[...]
    c_act = c * _sigmoid(c)
    o_ref[0] = jnp.dot(c_act, w_ref[0], precision=HIGHEST, preferred_element_type=F32) + b_ref[0]


def _ada(c_pad, w_ada, b_ada):
    depth, D, N = w_ada.shape
    rows = c_pad.shape[0]
    tn = N // 4
    return pl.pallas_call(
        _ada_kernel,
        out_shape=jax.ShapeDtypeStruct((depth, rows, N), F32),
        grid=(depth, N // tn),
        in_specs=[pl.BlockSpec((rows, D), lambda l, j: (0, 0)),
                  pl.BlockSpec((1, D, tn), lambda l, j: (l, 0, j)),
                  pl.BlockSpec((1, 1, tn), lambda l, j: (l, 0, j))],
        out_specs=pl.BlockSpec((1, rows, tn), lambda l, j: (l, 0, j)),
        compiler_params=_params("arbitrary", "arbitrary"),
        name="ada_mod",
    )(c_pad, w_ada, b_ada.reshape(depth, 1, N))


def _norm_mod_kernel(x_ref, g_ref, sc_ref, sh_ref, o_ref):
    x = x_ref[...]
    r = lax.rsqrt(jnp.mean(x * x, axis=-1, keepdims=True) + NORM_EPS)
    h = (x * r) * g_ref[...]
    o_ref[...] = (h * (1.0 + sc_ref[0]) + sh_ref[0]).astype(o_ref.dtype)


def _norm_mod(x, g, sc, sh, seq_len):
    T, D = x.shape
    tm = 1024
    per_batch = seq_len // tm
    return pl.pallas_call(
        _norm_mod_kernel,
        out_shape=jax.ShapeDtypeStruct((T, D), BF16),
        grid=(T // tm,),
        in_specs=[pl.BlockSpec((tm, D), lambda i: (i, 0)),
                  pl.BlockSpec((1, D), lambda i: (0, 0)),
                  pl.BlockSpec((1, 1, D), lambda i: (i // per_batch, 0, 0)),
                  pl.BlockSpec((1, 1, D), lambda i: (i // per_batch, 0, 0))],
        out_specs=pl.BlockSpec((tm, D), lambda i: (i, 0)),
        compiler_params=_params("arbitrary"),
        name="norm_mod",
    )(x, g, sc, sh)


def _proj_kernel(h_ref, w_ref, o_ref):
    o_ref[...] = jnp.dot(h_ref[...], w_ref[0], preferred_element_type=F32).astype(o_ref.dtype)


def _proj(h, w, layer, col0, n_cols, tn):
    T, D = h.shape
    tm = PROJ_TM
    assert col0 % tn == 0 and n_cols % tn == 0
    off = col0 // tn
    return pl.pallas_call(
        _proj_kernel,
        out_shape=jax.ShapeDtypeStruct((T, n_cols), BF16),
        grid=(T // tm, n_cols // tn),
        in_specs=[pl.BlockSpec((tm, D), lambda i, j: (i, 0)),
                  pl.BlockSpec((1, D, tn), lambda i, j: (layer, 0, off + j))],
        out_specs=pl.BlockSpec((tm, tn), lambda i, j: (i, j)),
        compiler_params=_params("arbitrary", "arbitrary"),
        name="proj",
    )(h, w)


def _proj_qkv_kernel(h_ref, w_ref, cos_ref, sin_ref, rot_ref, *out_refs):
    j = pl.program_id(1)
    kind = j % 3
    grp = j // 3
    acc = jnp.dot(h_ref[...], w_ref[0], preferred_element_type=F32)

    def emit(val):
        for g, o_ref in enumerate(out_refs):
            @pl.when(grp == g)
            def _(o_ref=o_ref):
                o_ref[...] = val.astype(o_ref.dtype)

    @pl.when(kind < 2)
    def _():
        cos = jnp.concatenate([cos_ref[...], cos_ref[...]], axis=1)
        sin = jnp.concatenate([sin_ref[...], sin_ref[...]], axis=1)
        swapped = jnp.dot(acc.astype(BF16), rot_ref[...], preferred_element_type=F32)
        scale = jnp.where(kind == 0, HEAD_DIM ** -0.5, 1.0)
        emit((acc * cos + swapped * sin) * scale)

    @pl.when(kind == 2)
    def _():
        emit(acc)


def _proj_qkv(h, w, layer, rope, seq_len, out_dtypes):
    T, D = h.shape
    tm, tn = PROJ_TM, PROJ_TN
    n_groups = len(out_dtypes)
    off = HY_COLS // tn
    cos_t, sin_t, rot = rope
    per_batch = seq_len // tm
    tab_spec = pl.BlockSpec((tm, LANES), lambda i, j: (i % per_batch, 0))
    out_specs = tuple(pl.BlockSpec((tm, tn), lambda i, j, g=g: (i, jnp.clip(j - 3 * g, 0, 2)))
                      for g in range(n_groups))
    return pl.pallas_call(
        _proj_qkv_kernel,
        out_shape=tuple(jax.ShapeDtypeStruct((T, 3 * tn), dt) for dt in out_dtypes),
        grid=(T // tm, 3 * n_groups),
        in_specs=[pl.BlockSpec((tm, D), lambda i, j: (i, 0)),
                  pl.BlockSpec((1, D, tn), lambda i, j: (layer, 0, off + j // 3 + n_groups * (j % 3))),
                  tab_spec, tab_spec,
                  pl.BlockSpec((tn, tn), lambda i, j: (0, 0))],
        out_specs=out_specs,
        compiler_params=_params("arbitrary", "arbitrary"),
        name="proj_qkv",
    )(h, w, cos_t, sin_t, rot)


def _rope_tables(seq_len):
    half = HEAD_DIM // 2
    inv = ROPE_THETA ** (-jnp.arange(half, dtype=F32) / half)
    ang = jnp.arange(seq_len, dtype=F32)[:, None] * inv[None, :]
    reps = LANES // half
    cos_t = jnp.tile(jnp.cos(ang), (1, reps))
    sin_t = jnp.tile(jnp.sin(ang), (1, reps))
    rot = np.zeros((PROJ_TN, PROJ_TN), np.float32)
    for j in range(PROJ_TN):
        if j % HEAD_DIM < half:
            rot[j + half, j] = -1.0
        else:
            rot[j - half, j] = 1.0
    return cos_t, sin_t, jnp.asarray(rot, BF16)


def _hyena_filter_kernel(feat_ref, w1_ref, b1_ref, w2_ref, b2_ref, w3_ref, b3_ref, w4_ref, fr_ref,
                         dl_ref, o_ref):
    i = pl.program_id(0)
    tm = feat_ref.shape[0]
    z = feat_ref[...]
    fr = fr_ref[...]
    h = jnp.sin(fr * (jnp.dot(z, w1_ref[...], precision=HIGHEST, preferred_element_type=F32) + b1_ref[...]))
    h = jnp.sin(fr * (jnp.dot(h, w2_ref[...], precision=HIGHEST, preferred_element_type=F32) + b2_ref[...]))
    h = jnp.sin(fr * (jnp.dot(h, w3_ref[...], precision=HIGHEST, preferred_element_type=F32) + b3_ref[...]))
    k = jnp.dot(h, w4_ref[...], precision=HIGHEST, preferred_element_type=F32)
    decay = jnp.exp(-z[:, 0:1] * jnp.abs(dl_ref[...]))
    pos = i * tm + lax.broadcasted_iota(jnp.int32, (tm, 1), 0)
    is_bwd = lax.broadcasted_iota(jnp.int32, (1, k.shape[1]), 1) >= HY_WIDTH
    _store_time_padded(o_ref, jnp.where((pos == 0) & is_bwd, 0.0, k * decay))


def _hyena_filter(feat, w1, b1, w2, b2, w3, b3, w4, freq, deltas2):
    n_rows = feat.shape[0]
    C2 = 2 * HY_WIDTH
    tm = 1024
    full = lambda i: (0, 0)
    return pl.pallas_call(
        _hyena_filter_kernel,
        out_shape=_time_padded_shape(n_rows, C2),
        grid=(n_rows // tm,),
        in_specs=[pl.BlockSpec((tm, LANES), lambda i: (i, 0)),
                  pl.BlockSpec((LANES, LANES), full), pl.BlockSpec((1, LANES), full),
                  pl.BlockSpec((LANES, LANES), full), pl.BlockSpec((1, LANES), full),
                  pl.BlockSpec((LANES, LANES), full), pl.BlockSpec((1, LANES), full),
                  pl.BlockSpec((LANES, C2), full),
                  pl.BlockSpec((1, LANES), full),
                  pl.BlockSpec((1, C2), full)],
        out_specs=pl.BlockSpec((tm // FFT_R, FFT_T_PITCH, C2), lambda i: (i, 0, 0)),
        compiler_params=_params("arbitrary"),
        name="hyena_filter",
    )(feat, w1, b1, w2, b2, w3, b3, w4, freq, deltas2)


def _hyena_features(seq_len):
    L = seq_len
    t = jnp.linspace(0.0, 1.0, L, dtype=F32)[:, None]
    bands = (HY_EMB - 1) // 2
    w = 2.0 * math.pi * jnp.arange(L, dtype=F32)[:, None] / L
    f = jnp.linspace(1e-4, bands - 1, bands, dtype=F32)[None, :]
    z = jnp.concatenate([t, jnp.cos(f * w), -jnp.sin(f * w)], axis=-1)
    z = jnp.pad(z, ((0, 0), (0, LANES - HY_EMB)))
    max_decay = math.log(HY_TARGET) / HY_FAST_DECAY_PCT
    min_decay = math.log(HY_TARGET) / HY_SLOW_DECAY_PCT
    deltas = jnp.linspace(min_decay, max_decay, HY_WIDTH, dtype=F32)[None, :]
    return z, jnp.concatenate([deltas, deltas], axis=1)


def _pad2(a, rows, cols):
    return jnp.pad(a, ((0, rows - a.shape[0]), (0, cols - a.shape[1])))


def _dft_matrices():
    R, KH, KP = FFT_R, FFT_KH, FFT_KP
    N = R * R
    n1 = np.arange(R)[:, None, None]
    k2 = np.arange(KP)[None, :, None]
    n2 = np.arange(R)[None, None, :]
    phase = 2.0 * np.pi * ((n2 * k2 % R) / R + (n1 * k2) / N)
    live = (k2 < KH)
    a1 = np.concatenate([np.cos(phase) * live, -np.sin(phase) * live], axis=1)
    wgt = np.where((k2 == 0) | (k2 == R // 2), 1.0, 2.0) * live / N
    b1 = np.concatenate([np.cos(phase) * wgt, -np.sin(phase) * wgt], axis=1)
    b1 = np.transpose(b1, (0, 2, 1))[:, :R // 2, :]
    th = 2.0 * np.pi * (np.arange(R)[:, None] * np.arange(R)[None, :] % R) / R
    c, s = np.cos(th), np.sin(th)
    w2f = np.block([[c, s], [-s, c]])
    w2i = np.block([[c, -s], [s, c]])
    as_bf = lambda a: jnp.asarray(a.astype(np.float32)).astype(BF16)
    return dict(a1=as_bf(a1[:, :, :R // 2]), b1=as_bf(b1), w2f=as_bf(w2f), w2i=as_bf(w2i))


def _fft_stage1(src_ref, a1_ref, g_ref, n_rows):
    def body(n1, carry):
        xs = src_ref[pl.ds(n1, n_rows, stride=FFT_T_PITCH), :].astype(BF16)
        slab = jnp.dot(a1_ref[n1], xs, preferred_element_type=F32)
        g_ref[pl.ds(pl.multiple_of(n1 * FFT_G_PITCH, 8), FFT_SLAB), :] = slab
        return carry
    lax.fori_loop(0, FFT_R, body, 0, unroll=FFT_UNROLL_TIME)


def _load_freq_rows(g_ref, k2):
    re = g_ref[pl.ds(k2, FFT_R, stride=FFT_G_PITCH), :]
    im = g_ref[pl.ds(FFT_KP + k2, FFT_R, stride=FFT_G_PITCH), :]
    return jnp.concatenate([re, im], axis=0)


def _filter_fft_kernel(fwd_ref, bwd_ref, a1_ref, w2_ref, o_ref, gf_ref, gb_ref):
    n2_rows = fwd_ref.shape[0] // FFT_T_PITCH
    cb = fwd_ref.shape[1]

    def stage1(n1, carry):
        rows = pl.ds(n1, n2_rows, stride=FFT_T_PITCH)
        xs = jnp.concatenate([fwd_ref[rows, :], bwd_ref[rows, :]], axis=1).astype(BF16)
        slab = jnp.dot(a1_ref[n1], xs, preferred_element_type=F32)
        dst = pl.ds(pl.multiple_of(n1 * FFT_G_PITCH, 8), FFT_SLAB)
        gf_ref[dst, :] = slab[:, :cb]
        gb_ref[dst, :] = slab[:, cb:]
        return carry
    lax.fori_loop(0, FFT_R, stage1, 0, unroll=FFT_UNROLL_TIME)

    def body(k2, carry):
        gk = jnp.concatenate([_load_freq_rows(gf_ref, k2), _load_freq_rows(gb_ref, k2)], axis=1).astype(BF16)
        x = jnp.dot(w2_ref[...], gk, preferred_element_type=F32)
        f, b = x[:, :cb], x[:, cb:]
        spec = jnp.concatenate([f[:FFT_R] + b[:FFT_R], f[FFT_R:] - b[FFT_R:]], axis=0)
        o_ref[k2] = spec.astype(o_ref.dtype)
        return carry
    lax.fori_loop(0, FFT_KH, body, 0, unroll=FFT_UNROLL_FREQ)


def _filter_fft(k2, mats):
    n_rows = k2.shape[0]
    C = HY_WIDTH
    cb = CONV_CB
    ncb = C // cb
    return pl.pallas_call(
        _filter_fft_kernel,
        out_shape=jax.ShapeDtypeStruct((FFT_KH, 2 * FFT_R, C), BF16),
        grid=(ncb,),
        in_specs=[pl.BlockSpec((n_rows, cb), lambda j: (0, j)),
                  pl.BlockSpec((n_rows, cb), lambda j: (0, ncb + j)),
                  pl.BlockSpec((FFT_R, FFT_SLAB, FFT_R // 2), lambda j: (0, 0, 0)),
                  pl.BlockSpec((2 * FFT_R, 2 * FFT_R), lambda j: (0, 0))],
        out_specs=pl.BlockSpec((FFT_KH, 2 * FFT_R, cb), lambda j: (0, 0, j)),
        scratch_shapes=[pltpu.VMEM((FFT_R * FFT_G_PITCH, cb), F32), pltpu.VMEM((FFT_R * FFT_G_PITCH, cb), F32)],
        compiler_params=_params("arbitrary"),
        name="hyena_filter_fft",
    )(k2, k2, mats["a1"], mats["w2f"])


def _short_conv_kernel(x0_ref, x1_ref, v_ref, p0_ref, p1_ref, pv_ref, n0_ref, n1_ref, nv_ref,
                       w0_ref, w1_ref, wv_ref, b0_ref, b1_ref, bv_ref, x0c_ref, zv_ref, scr, *, tiles_per_seq):
    i = pl.program_id(0)
    tm = x0_ref.shape[0]
    first = (i % tiles_per_seq) == 0
    last = (i % tiles_per_seq) == tiles_per_seq - 1
    halo = p0_ref.shape[0]

    def conv(u_ref, prev_ref, next_ref, w_ref, b_ref):
        prev_row = jnp.where(first, 0.0, prev_ref[halo - 1:halo, :].astype(F32))
        next_row = jnp.where(last, 0.0, next_ref[0:1, :].astype(F32))
        scr[7:8, :] = prev_row
        scr[8:8 + tm, :] = u_ref[...].astype(F32)
        scr[8 + tm:9 + tm, :] = next_row
        w = w_ref[...]
        return (scr[pl.ds(7, tm), :] * w[0:1] + scr[pl.ds(8, tm), :] * w[1:2]
                + scr[pl.ds(9, tm), :] * w[2:3] + b_ref[...])

    x0c_ref[...] = conv(x0_ref, p0_ref, n0_ref, w0_ref, b0_ref).astype(x0c_ref.dtype)
    x1c = conv(x1_ref, p1_ref, n1_ref, w1_ref, b1_ref)
    vc = conv(v_ref, pv_ref, nv_ref, wv_ref, bv_ref)
    _store_time_padded(zv_ref, vc * x1c)


def _short_conv(u, w_sc, b_sc, seq_len):
    T = u.shape[0]
    C = HY_WIDTH
    tm, cb, halo = 1024, 256, 16
    ncb = C // cb
    tiles_per_seq = seq_len // tm
    hb = tm // halo
    n_halo = T // halo

    def part(p):
        return pl.BlockSpec((tm, cb), lambda i, j: (i, p * ncb + j))

    def prev(p):
        return pl.BlockSpec((halo, cb), lambda i, j: (jnp.maximum(i * hb - 1, 0), p * ncb + j))

    def nxt(p):
        return pl.BlockSpec((halo, cb), lambda i, j: (jnp.minimum((i + 1) * hb, n_halo - 1), p * ncb + j))

    def wpart(p):
        return pl.BlockSpec((3, cb), lambda i, j: (0, p * ncb + j))

    def bpart(p):
        return pl.BlockSpec((1, cb), lambda i, j: (0, p * ncb + j))

    kern = lambda *refs: _short_conv_kernel(*refs, tiles_per_seq=tiles_per_seq)
    b2 = b_sc.reshape(1, 3 * C)
    return pl.pallas_call(
        kern,
        out_shape=(jax.ShapeDtypeStruct((T, C), BF16), _time_padded_shape(T, C)),
        grid=(T // tm, ncb),
        in_specs=[part(0), part(1), part(2), prev(0), prev(1), prev(2), nxt(0), nxt(1), nxt(2),
                  wpart(0), wpart(1), wpart(2), bpart(0), bpart(1), bpart(2)],
        out_specs=(pl.BlockSpec((tm, cb), lambda i, j: (i, j)),
                   pl.BlockSpec((tm // FFT_R, FFT_T_PITCH, cb), lambda i, j: (i, 0, j))),
        scratch_shapes=[pltpu.VMEM((tm + 16, cb), F32)],
        compiler_params=_params("arbitrary", "arbitrary"),
        name="hyena_short_conv",
    )(u, u, u, u, u, u, u, u, u, w_sc, w_sc, w_sc, b2, b2, b2)


def _long_conv_kernel(zv_ref, x0_ref, kf_ref, bias_ref, a1_ref, b1_ref, w2f_ref, w2i_ref, o_ref, g_ref):
    n2_rows = zv_ref.shape[1] // FFT_T_PITCH
    zv2 = zv_ref.at[0]
    o2 = o_ref.at[0]
    _fft_stage1(zv2, a1_ref, g_ref, n2_rows)

    def freq_body(k2, carry):
        gk = _load_freq_rows(g_ref, k2).astype(BF16)
        x = jnp.dot(w2f_ref[...], gk, preferred_element_type=F32)
        kf = kf_ref[k2].astype(F32)
        xr, xi = x[:FFT_R], x[FFT_R:]
        kr, ki = kf[:FFT_R], kf[FFT_R:]
        p = jnp.concatenate([xr * kr - xi * ki, xr * ki + xi * kr], axis=0).astype(BF16)
        hk = jnp.dot(w2i_ref[...], p, preferred_element_type=F32)
        g_ref[pl.ds(k2, FFT_R, stride=FFT_G_PITCH), :] = hk[:FFT_R]
        g_ref[pl.ds(FFT_KP + k2, FFT_R, stride=FFT_G_PITCH), :] = hk[FFT_R:]
        return carry
    lax.fori_loop(0, FFT_KH, freq_body, 0, unroll=FFT_UNROLL_FREQ)

    def time_body(n1, carry):
        slab = g_ref[pl.ds(pl.multiple_of(n1 * FFT_G_PITCH, 8), FFT_SLAB), :].astype(BF16)
        o2[pl.ds(n1, n2_rows, stride=FFT_T_PITCH), :] = jnp.dot(b1_ref[n1], slab, preferred_element_type=F32)
        return carry
    lax.fori_loop(0, FFT_R, time_body, 0, unroll=FFT_UNROLL_TIME)

    bias = bias_ref[...]
    pad_zeros = jnp.zeros((FFT_T_PITCH - FFT_R, o_ref.shape[2]), F32)

    def out_body(g, carry):
        prow = pl.multiple_of(g * FFT_T_PITCH, 8)
        rows = pl.ds(prow, FFT_R)
        x0 = x0_ref[0, pl.ds(pl.multiple_of(g * FFT_R, FFT_R), FFT_R), :].astype(F32)
        o2[rows, :] = (o2[rows, :] + zv2[rows, :] * bias) * x0
        o2[pl.ds(prow + FFT_R, FFT_T_PITCH - FFT_R), :] = pad_zeros
        return carry
    lax.fori_loop(0, n2_rows, out_body, 0, unroll=4)


def _long_conv(zv, x0c, kf, hy_bias, mats):
    B, Lp, C = zv.shape
    L = x0c.shape[1]
    cb = CONV_CB
    blk = lambda j, b: (b, 0, j)
    return pl.pallas_call(
        _long_conv_kernel,
        out_shape=jax.ShapeDtypeStruct((B, Lp, C), F32),
        grid=(C // cb, B),
        in_specs=[pl.BlockSpec((1, Lp, cb), blk),
                  pl.BlockSpec((1, L, cb), blk),
                  pl.BlockSpec((FFT_KH, 2 * FFT_R, cb), lambda j, b: (0, 0, j)),
                  pl.BlockSpec((1, cb), lambda j, b: (0, j)),
                  pl.BlockSpec((FFT_R, FFT_SLAB, FFT_R // 2), lambda j, b: (0, 0, 0)),
                  pl.BlockSpec((FFT_R, FFT_R // 2, FFT_SLAB), lambda j, b: (0, 0, 0)),
                  pl.BlockSpec((2 * FFT_R, 2 * FFT_R), lambda j, b: (0, 0)),
                  pl.BlockSpec((2 * FFT_R, 2 * FFT_R), lambda j, b: (0, 0))],
        out_specs=pl.BlockSpec((1, Lp, cb), blk),
        scratch_shapes=[pltpu.VMEM((FFT_R * FFT_G_PITCH, cb), F32)],
        compiler_params=_params("arbitrary", "arbitrary"),
        name="hyena_long_conv",
    )(zv, x0c, kf, hy_bias.reshape(1, C), mats["a1"], mats["b1"], mats["w2f"], mats["w2i"])


ATT_TQ = 128
ATT_RADIUS = 64
ATT_WINDOW = ATT_TQ + 2 * ATT_RADIUS


ATT_SUB = 4


def _attn_kernel(q_ref, k_ref, v_ref, o_ref, lse_ref):
    i = pl.program_id(2)
    ls = k_ref.shape[1]
    nh = HEADS_PER_GROUP
    head_of_col = lax.broadcasted_iota(jnp.int32, (1, ATT_OUT), 1) // HEAD_DIM
    row_iota = lax.broadcasted_iota(jnp.int32, (nh * ATT_TQ, ATT_WINDOW), 0) % ATT_TQ
    col_iota = lax.broadcasted_iota(jnp.int32, (nh * ATT_TQ, ATT_WINDOW), 1)
    rel = row_iota - col_iota
    for sub in range(ATT_SUB):
        q0 = (i * ATT_SUB + sub) * ATT_TQ
        rows = slice(sub * ATT_TQ, (sub + 1) * ATT_TQ)
        q = q_ref[0, rows, :]
        start = jnp.clip(q0 - ATT_RADIUS, 0, ls - ATT_WINDOW)
        start = pl.multiple_of(start, ATT_RADIUS)
        kw = k_ref[0, pl.ds(start, ATT_WINDOW), :]
        vw = v_ref[0, pl.ds(start, ATT_WINDOW), :]
        out, lse = _attn_unit(q, kw, vw, rel, q0 - start, head_of_col)
        o_ref[0, rows, :] = out.astype(o_ref.dtype)
        lse_ref[0, rows, :] = lse


def _attn_unit(q, kw, vw, rel, q_minus_start, head_of_col):
    nh = HEADS_PER_GROUP
    band = jnp.abs(q_minus_start + rel) <= ATT_RADIUS
    zero = jnp.zeros_like(q)
    q4 = jnp.concatenate([jnp.where(head_of_col == h, q, zero) for h in range(nh)], axis=0)
    s = lax.dot_general(q4, kw, (((1,), (1,)), ((), ())), preferred_element_type=F32)
    s = jnp.where(band, s, MASK_VALUE)
    m = jnp.max(s, axis=-1, keepdims=True)
    p = jnp.exp(s - m)
    den = jnp.sum(p, axis=-1, keepdims=True)
    pv = jnp.dot(p.astype(BF16), vw, preferred_element_type=F32) / den
    lse4 = m + jnp.log(den)
    out = jnp.zeros((ATT_TQ, ATT_OUT), F32)
    lse = jnp.zeros((ATT_TQ, ATT_OUT), F32)
    for h in range(nh):
        mine = head_of_col == h
        hrows = slice(h * ATT_TQ, (h + 1) * ATT_TQ)
        out = jnp.where(mine, pv[hrows], out)
        lse = jnp.where(mine, lse4[hrows], lse)
    return out, lse


ATT_CHUNK = 2048


def _attn_strided_kernel(q0_ref, q1_ref, k0_ref, k1_ref, v0_ref, v1_ref, o_ref, lse_ref, o_scr, lse_scr, *, dil):
    i = pl.program_id(1)
    ls = q0_ref.shape[1] // dil

    def gather(lo_ref, hi_ref, first, n):
        rows = pl.ds(first, n, stride=dil)
        return jnp.concatenate([lo_ref[0, rows, :], hi_ref[0, rows, :]], axis=1).astype(BF16)

    units = ATT_CHUNK // ATT_TQ
    shift = dil.bit_length() - 1
    head_of_col = lax.broadcasted_iota(jnp.int32, (1, ATT_OUT), 1) // HEAD_DIM
    row_iota = lax.broadcasted_iota(jnp.int32, (HEADS_PER_GROUP * ATT_TQ, ATT_WINDOW), 0) % ATT_TQ
    rel = row_iota - lax.broadcasted_iota(jnp.int32, (HEADS_PER_GROUP * ATT_TQ, ATT_WINDOW), 1)

    def unit(u, carry):
        r = u & (dil - 1)
        sb = u >> shift
        m0 = i * (ATT_CHUNK // dil) + sb * ATT_TQ
        start = jnp.clip(m0 - ATT_RADIUS, 0, ls - ATT_WINDOW)
        q = gather(q0_ref, q1_ref, m0 * dil + r, ATT_TQ)
        kw = gather(k0_ref, k1_ref, start * dil + r, ATT_WINDOW)
        vw = gather(v0_ref, v1_ref, start * dil + r, ATT_WINDOW)
        out, lse = _attn_unit(q, kw, vw, rel, m0 - start, head_of_col)
        dst = pl.ds(sb * ATT_TQ * dil + r, ATT_TQ, stride=dil)
        for half in range(2):
            lanes = slice(half * LANES, (half + 1) * LANES)
            o_scr[half, dst, :] = out[:, lanes]
            lse_scr[half, dst, :] = lse[:, lanes]
        return carry
    lax.fori_loop(0, units, unit, 0, unroll=2)
    for half in range(2):
        lanes = slice(half * LANES, (half + 1) * LANES)
        o_ref[0, :, lanes] = o_scr[half]
        lse_ref[0, :, lanes] = lse_scr[half]


def _attention_strided(qkv_g, dil, batch, seq_len):
    view = qkv_g.reshape(batch, seq_len, 3 * ATT_OUT)
    halves = [pl.BlockSpec((1, seq_len, LANES), lambda b, i, c=c: (b, 0, c), pipeline_mode=pl.Buffered(1))
              for c in range(3 * ATT_OUT // LANES)]
    o_spec = pl.BlockSpec((1, ATT_CHUNK, ATT_OUT), lambda b, i: (b, i, 0))
    kern = lambda *refs: _attn_strided_kernel(*refs, dil=dil)
    o, lse = pl.pallas_call(
        kern,
        out_shape=(jax.ShapeDtypeStruct((batch, seq_len, ATT_OUT), F32),
                   jax.ShapeDtypeStruct((batch, seq_len, ATT_OUT), F32)),
        grid=(batch, seq_len // ATT_CHUNK),
        in_specs=halves,
        out_specs=(o_spec, o_spec),
        scratch_shapes=[pltpu.VMEM((2, ATT_CHUNK, LANES), F32), pltpu.VMEM((2, ATT_CHUNK, LANES), F32)],
        compiler_params=_params("arbitrary", "arbitrary"),
        name=f"dilated_attn_d{dil}",
    )(*([view] * len(halves)))
    T = batch * seq_len
    return o.reshape(T, ATT_OUT), lse.reshape(T, ATT_OUT)


def _attention_group(qkv_g, dil, batch, seq_len):
    ls = seq_len // dil
    tq = ATT_SUB * ATT_TQ
    view = qkv_g.reshape(batch, ls, dil * 3 * ATT_OUT)

    def col(which):
        return lambda b, r, i: (b, 0, r * 3 + which)

    q_map = lambda b, r, i: (b, i, r * 3)
    o_map = lambda b, r, i: (b, i, r)
    o, lse = pl.pallas_call(
        _attn_kernel,
        out_shape=(jax.ShapeDtypeStruct((batch, ls, dil * ATT_OUT), BF16),
                   jax.ShapeDtypeStruct((batch, ls, dil * ATT_OUT), F32)),
        grid=(batch, dil, ls // tq),
        in_specs=[pl.BlockSpec((1, tq, ATT_OUT), q_map),
                  pl.BlockSpec((1, ls, ATT_OUT), col(1)),
                  pl.BlockSpec((1, ls, ATT_OUT), col(2))],
        out_specs=(pl.BlockSpec((1, tq, ATT_OUT), o_map), pl.BlockSpec((1, tq, ATT_OUT), o_map)),
        compiler_params=_params("arbitrary", "arbitrary", "arbitrary"),
        name=f"dilated_attn_d{dil}",
    )(view, view, view)
    T = batch * seq_len
    return o.reshape(T, ATT_OUT), lse.reshape(T, ATT_OUT)


def _merge_kernel(x_ref, yhy_ref, o1_ref, o2_ref, o3_ref, l1_ref, l2_ref, l3_ref, gh_ref, ga_ref,
                  wh_ref, wa_ref, wo_ref, g1_ref, out_ref):
    l1, l2, l3 = l1_ref[...], l2_ref[...], l3_ref[...]
    m = jnp.maximum(jnp.maximum(l1, l2), l3)
    e1, e2, e3 = jnp.exp(l1 - m), jnp.exp(l2 - m), jnp.exp(l3 - m)
    tot = e1 + e2 + e3
    y_at = (e1 * o1_ref[...].astype(F32) + e2 * o2_ref[...].astype(F32) + e3 * o3_ref[...].astype(F32)) / tot
    y_hy = yhy_ref[:, :FFT_R, :].reshape(x_ref.shape[0], yhy_ref.shape[2])
    a = jnp.dot(y_hy.astype(BF16), wh_ref[0], preferred_element_type=F32)
    b = jnp.dot(y_at.astype(BF16), wa_ref[0], preferred_element_type=F32)
    merged = _sigmoid(gh_ref[...].astype(F32)) * a + _sigmoid(ga_ref[...].astype(F32)) * b
    upd = jnp.dot(merged.astype(BF16), wo_ref[0], preferred_element_type=F32)
    out_ref[...] = x_ref[...] + g1_ref[0] * upd


def _merge(x, y_hy, attn, gates, w_br_h, w_br_a, w_out, layer, g1, seq_len):
    T, D = x.shape
    tm = 512
    per_batch = seq_len // tm
    (o1, l1), (o2, l2), (o3, l3) = attn
    row = lambda w: pl.BlockSpec((tm, w), lambda i: (i, 0))
    full = lambda a: pl.BlockSpec((1,) + a.shape[1:], lambda i: (layer, 0, 0))
    return pl.pallas_call(
        _merge_kernel,
        out_shape=jax.ShapeDtypeStruct((T, D), F32),
        grid=(T // tm,),
        in_specs=[row(D), pl.BlockSpec((tm // FFT_R, FFT_T_PITCH, HY_WIDTH), lambda i: (i, 0, 0)),
                  row(ATT_OUT), row(ATT_OUT), row(ATT_OUT),
                  row(ATT_OUT), row(ATT_OUT), row(ATT_OUT),
                  pl.BlockSpec((tm, D), lambda i: (i, 0)), pl.BlockSpec((tm, D), lambda i: (i, 1)),
                  full(w_br_h), full(w_br_a), full(w_out),
                  pl.BlockSpec((1, 1, D), lambda i: (i // per_batch, 0, 0))],
        out_specs=row(D),
        compiler_params=_params("arbitrary"),
        name="mixer_merge",
    )(x, y_hy, o1, o2, o3, l1, l2, l3, gates, gates, w_br_h, w_br_a, w_out, g1)


def _router_kernel(x_ref, g_ref, sc_ref, sh_ref, rwt_ref, rb_ref, hg_ref, stats_ref, cnt_ref, base_ref):
    i = pl.program_id(0)
    tm = x_ref.shape[0]

    @pl.when(i == 0)
    def _():
        base_ref[...] = jnp.zeros_like(base_ref)

    x = x_ref[...]
    r = lax.rsqrt(jnp.mean(x * x, axis=-1, keepdims=True) + NORM_EPS)
    h = (x * r) * g_ref[...]
    h = h * (1.0 + sc_ref[0]) + sh_ref[0]

    logits = lax.dot_general(rwt_ref[...], h, (((1,), (1,)), ((), ())),
                             precision=HIGHEST, preferred_element_type=F32)
    scores = _sigmoid(logits)
    biased = scores + rb_ref[...]

    def row(a, k):
        return a[k:k + 1, :]

    sel = jnp.zeros((1, tm), jnp.int32)
    best = None
    for g in range(N_GROUPS):
        a, b, c, d = (row(biased, 4 * g + k) for k in range(4))
        m_ab, n_ab = jnp.maximum(a, b), jnp.minimum(a, b)
        m_cd, n_cd = jnp.maximum(c, d), jnp.minimum(c, d)
        gs = jnp.maximum(m_ab, m_cd) + jnp.maximum(jnp.minimum(m_ab, m_cd), jnp.maximum(n_ab, n_cd))
        if g == 0:
            best = gs
        else:
            better = gs > best
            sel = jnp.where(better, g, sel)
            best = jnp.where(better, gs, best)

    v, u = [], []
    for k in range(EXPERTS_PER_GROUP):
        vk = jnp.zeros((1, tm), F32)
        uk = jnp.zeros((1, tm), F32)
        for g in range(N_GROUPS):
            vk = jnp.where(sel == g, row(biased, 4 * g + k), vk)
            uk = jnp.where(sel == g, row(scores, 4 * g + k), uk)
        v.append(vk)
        u.append(uk)

    i1 = jnp.zeros((1, tm), jnp.int32)
    b1 = v[0]
    for k in range(1, EXPERTS_PER_GROUP):
        gt = v[k] > b1
        i1 = jnp.where(gt, k, i1)
        b1 = jnp.where(gt, v[k], b1)
    i2 = jnp.zeros((1, tm), jnp.int32)
    b2 = jnp.full((1, tm), -jnp.inf, F32)
    for k in range(EXPERTS_PER_GROUP):
        cand = (i1 != k) & (v[k] > b2)
        i2 = jnp.where(cand, k, i2)
        b2 = jnp.where(cand, v[k], b2)

    lo = jnp.minimum(i1, i2)
    hi = jnp.maximum(i1, i2)
    pair = jnp.where(lo == 0, hi - 1, jnp.where(lo == 1, hi + 1, 5))
    bucket = sel * PAIRS_PER_GROUP + pair

    u_lo = jnp.zeros((1, tm), F32)
    u_hi = jnp.zeros((1, tm), F32)
    for k in range(EXPERTS_PER_GROUP):
        u_lo = jnp.where(lo == k, u[k], u_lo)
        u_hi = jnp.where(hi == k, u[k], u_hi)
    tot = u_lo + u_hi
    w_lo = u_lo / tot
    w_hi = u_hi / tot

    rows = lax.broadcasted_iota(jnp.int32, (BUCKET_ROWS, tm), 0)
    onehot = (rows == bucket).astype(F32)
    t_src = lax.broadcasted_iota(jnp.int32, (tm, tm), 0)
    t_dst = lax.broadcasted_iota(jnp.int32, (tm, tm), 1)
    before = (t_src < t_dst).astype(BF16)
    cum = jnp.dot(onehot.astype(BF16), before, preferred_element_type=F32)
    base = base_ref[...]
    rank = jnp.sum(onehot * (cum + base), axis=0, keepdims=True)
    base = base + jnp.sum(onehot, axis=1, keepdims=True)
    base_ref[...] = base
    cnt_ref[...] = jnp.broadcast_to(base, cnt_ref.shape)

    srow = lax.broadcasted_iota(jnp.int32, (8, tm), 0)
    stats_ref[...] = jnp.where(srow == 0, bucket.astype(F32), jnp.where(srow == 1, rank, 0.0))

    grow = lax.broadcasted_iota(jnp.int32, (GATE_COLS, tm), 0)
    gates_t = jnp.where(grow == 0, w_lo, jnp.where(grow == 1, w_hi, 0.0))
    hg_ref[:, :D_MODEL] = h
    hg_ref[:, D_MODEL:] = gates_t.T


def _router(x, norm_g, sc, sh, router_wt, router_b, seq_len):
    T, D = x.shape
    tm = ROUTER_TM
    per_batch = seq_len // tm
    return pl.pallas_call(
        _router_kernel,
        out_shape=(jax.ShapeDtypeStruct((T, D + GATE_COLS), F32),
                   jax.ShapeDtypeStruct((8, T), F32),
                   jax.ShapeDtypeStruct((BUCKET_ROWS, LANES), F32)),
        grid=(T // tm,),
        in_specs=[pl.BlockSpec((tm, D), lambda i: (i, 0)),
                  pl.BlockSpec((1, D), lambda i: (0, 0)),
                  pl.BlockSpec((1, 1, D), lambda i: (i // per_batch, 0, 0)),
                  pl.BlockSpec((1, 1, D), lambda i: (i // per_batch, 0, 0)),
                  pl.BlockSpec((N_EXPERTS, D), lambda i: (0, 0)),
                  pl.BlockSpec((N_EXPERTS, 1), lambda i: (0, 0))],
        out_specs=(pl.BlockSpec((tm, D + GATE_COLS), lambda i: (i, 0)),
                   pl.BlockSpec((8, tm), lambda i: (0, i)),
                   pl.BlockSpec((BUCKET_ROWS, LANES), lambda i: (0, 0))),
        scratch_shapes=[pltpu.VMEM((BUCKET_ROWS, 1), F32)],
        compiler_params=_params("arbitrary"),
        name="moe_router",
    )(x, norm_g, sc, sh, router_wt, router_b)


def _start_row_copies(n_rows, make_copy):
    group = 8

    def body(g, carry):
        base = pl.multiple_of(g * group, group)
        for k in range(group):
            make_copy(base + k).start(priority=k % 2)
        return carry

    lax.fori_loop(0, n_rows // group, body, 0)


def _dispatch_kernel(pos_ref, hg_ref, xs_init_hbm, xs_hbm, sem):
    del xs_init_hbm
    _start_row_copies(ROW_BLOCK, lambda r: pltpu.make_async_copy(
        hg_ref.at[pl.ds(r, 1), :], xs_hbm.at[pl.ds(pos_ref[r], 1), :], sem))
    pltpu.make_async_copy(hg_ref, xs_hbm.at[pl.ds(0, ROW_BLOCK), :], sem).wait()


def _dispatch(pos, hg, n_rows):
    T, W = hg.shape
    zeros = jnp.zeros((n_rows, W), F32)
    return pl.pallas_call(
        _dispatch_kernel,
        out_shape=jax.ShapeDtypeStruct((n_rows, W), F32),
        grid=(T // ROW_BLOCK,),
        in_specs=[pl.BlockSpec((ROW_BLOCK,), lambda i: (i,), memory_space=pltpu.SMEM),
                  pl.BlockSpec((ROW_BLOCK, W), lambda i: (i, 0)),
                  pl.BlockSpec(memory_space=pl.ANY)],
        out_specs=pl.BlockSpec(memory_space=pl.ANY),
        scratch_shapes=[pltpu.SemaphoreType.DMA],
        input_output_aliases={2: 0},
        compiler_params=_params("arbitrary"),
        name="moe_dispatch",
    )(pos, hg, zeros)


def _expert_kernel(e_lo_ref, e_hi_ref, n_used_ref, xs_ref, w1a, w3a, w2a, w1b, w3b, w2b, y_ref):
    del e_lo_ref, e_hi_ref
    used = pl.program_id(0) < n_used_ref[0]

    @pl.when(jnp.logical_not(used))
    def _():
        y_ref[...] = jnp.zeros_like(y_ref)

    @pl.when(used)
    def _():
        xb = xs_ref[:, :D_MODEL].astype(BF16)
        g_lo = xs_ref[:, D_MODEL:D_MODEL + 1]
        g_hi = xs_ref[:, D_MODEL + 1:D_MODEL + 2]

        def ffn(w1, w3, w2):
            a = jnp.dot(xb, w1[0, 0], preferred_element_type=F32)
            b = jnp.dot(xb, w3[0, 0], preferred_element_type=F32)
            act = (a * _sigmoid(a)) * b
            return jnp.dot(act.astype(BF16), w2[0, 0], preferred_element_type=F32)

        y_ref[...] = g_lo * ffn(w1a, w3a, w2a) + g_hi * ffn(w1b, w3b, w2b)


def _experts(tile_lo, tile_hi, n_used, xs, w1, w3, w2, layer):
    n_rows, W = xs.shape
    D, F = w1.shape[2], w1.shape[3]
    n_tiles = n_rows // EXPERT_TM

    def x_map(j, lo, hi, nu):
        return (jnp.minimum(j, nu[0] - 1), 0)

    def w_lo_map(j, lo, hi, nu):
        return (layer, lo[j], 0, 0)

    def w_hi_map(j, lo, hi, nu):
        return (layer, hi[j], 0, 0)

    grid_spec = pltpu.PrefetchScalarGridSpec(
        num_scalar_prefetch=3,
        grid=(n_tiles,),
        in_specs=[pl.BlockSpec((EXPERT_TM, W), x_map),
                  pl.BlockSpec((1, 1, D, F), w_lo_map), pl.BlockSpec((1, 1, D, F), w_lo_map),
                  pl.BlockSpec((1, 1, F, D), w_lo_map),
                  pl.BlockSpec((1, 1, D, F), w_hi_map), pl.BlockSpec((1, 1, D, F), w_hi_map),
                  pl.BlockSpec((1, 1, F, D), w_hi_map)],
        out_specs=pl.BlockSpec((EXPERT_TM, D), lambda j, lo, hi, nu: (j, 0)),
    )
    return pl.pallas_call(
        _expert_kernel,
        out_shape=jax.ShapeDtypeStruct((n_rows, D), F32),
        grid_spec=grid_spec,
        compiler_params=_params("arbitrary"),
        name="moe_experts",
    )(tile_lo, tile_hi, n_used, xs, w1, w3, w2, w1, w3, w2)


def _combine_kernel(pos_ref, x_ref, g2_ref, ys_hbm, o_ref, buf, sem):
    _start_row_copies(ROW_BLOCK, lambda r: pltpu.make_async_copy(
        ys_hbm.at[pl.ds(pos_ref[r], 1), :], buf.at[pl.ds(r, 1), :], sem))
    pltpu.make_async_copy(ys_hbm.at[pl.ds(0, ROW_BLOCK), :], buf, sem).wait()
    o_ref[...] = x_ref[...] + g2_ref[0] * buf[...]


def _combine(pos, x, g2, ys, seq_len):
    T, D = x.shape
    per_batch = seq_len // ROW_BLOCK
    return pl.pallas_call(
        _combine_kernel,
        out_shape=jax.ShapeDtypeStruct((T, D), F32),
        grid=(T // ROW_BLOCK,),
        in_specs=[pl.BlockSpec((ROW_BLOCK,), lambda i: (i,), memory_space=pltpu.SMEM),
                  pl.BlockSpec((ROW_BLOCK, D), lambda i: (i, 0)),
                  pl.BlockSpec((1, 1, D), lambda i: (i // per_batch, 0, 0)),
                  pl.BlockSpec(memory_space=pl.ANY)],
        out_specs=pl.BlockSpec((ROW_BLOCK, D), lambda i: (i, 0)),
        scratch_shapes=[pltpu.VMEM((ROW_BLOCK, D), F32), pltpu.SemaphoreType.DMA],
        compiler_params=_params("arbitrary"),
        name="moe_combine",
    )(pos, x, g2, ys)


def _moe_layer(x, norm_g, sc2, sh2, g2, router_wt, router_b, w1, w3, w2, layer, seq_len):
    T, D = x.shape
    hg, stats, counts = _router(x, norm_g, sc2, sh2, router_wt, router_b, seq_len)

    cnt = counts[:N_BUCKETS, 0].astype(jnp.int32)
    padded = ((cnt + EXPERT_TM - 1) // EXPERT_TM) * EXPERT_TM
    ends = jnp.cumsum(padded)
    starts = ends - padded
    bucket = stats[0].astype(jnp.int32)
    pos = starts[bucket] + stats[1].astype(jnp.int32)
    n_tiles = T // EXPERT_TM + N_BUCKETS
    tile_row0 = jnp.arange(n_tiles, dtype=jnp.int32) * EXPERT_TM
    tile_bucket = jnp.sum((ends[None, :] <= tile_row0[:, None]).astype(jnp.int32), axis=1)
    tile_bucket = jnp.minimum(tile_bucket, N_BUCKETS - 1)
    grp, pair = tile_bucket // PAIRS_PER_GROUP, tile_bucket % PAIRS_PER_GROUP
    pair_lo = jnp.array([0, 0, 0, 1, 1, 2], jnp.int32)[pair]
    pair_hi = jnp.array([1, 2, 3, 2, 3, 3], jnp.int32)[pair]
    tile_lo = grp * EXPERTS_PER_GROUP + pair_lo
    tile_hi = grp * EXPERTS_PER_GROUP + pair_hi
    n_used = (ends[-1] // EXPERT_TM).astype(jnp.int32).reshape(1)

    xs = _dispatch(pos, hg, n_tiles * EXPERT_TM)
    ys = _experts(tile_lo, tile_hi, n_used, xs, w1, w3, w2, layer)
    return _combine(pos, x, g2, ys, seq_len)


def _final_norm_kernel(x_ref, g_ref, o_ref):
    x = x_ref[...]
    r = lax.rsqrt(jnp.mean(x * x, axis=-1, keepdims=True) + NORM_EPS)
    o_ref[...] = (x * r) * g_ref[...]


def _final_norm(x, g):
    T, D = x.shape
    tm = 1024
    return pl.pallas_call(
        _final_norm_kernel,
        out_shape=jax.ShapeDtypeStruct((T, D), F32),
        grid=(T // tm,),
        in_specs=[pl.BlockSpec((tm, D), lambda i: (i, 0)), pl.BlockSpec((1, D), lambda i: (0, 0))],
        out_specs=pl.BlockSpec((tm, D), lambda i: (i, 0)),
        compiler_params=_params("arbitrary"),
        name="final_norm",
    )(x, g)


def kernel(x, c, norm1_g, norm2_g, w_ada, b_ada, w_in, w_sc, b_sc, hf_w1, hf_b1, hf_w2, hf_b2, hf_w3, hf_b3,
           hf_w4, hf_freq, hy_bias, w_br_h, w_br_a, w_out, router_w, router_bias, moe_w1, moe_w3, moe_w2, final_g):
    B, L, D = x.shape
    T = B * L
    C = HY_WIDTH
    xt = x.reshape(T, D)

    rope = _rope_tables(L)
    feat, deltas = _hyena_features(L)
    mats = _dft_matrices()

    c_pad = jnp.pad(c, ((0, 8 - B), (0, 0)))
    mod = _ada(c_pad, w_ada, b_ada)[:, :B]
    router_wt = router_w.T
    router_b = router_bias.reshape(N_EXPERTS, 1)
    w_in_b, w_br_h_b, w_br_a_b, w_out_b = (w.astype(BF16) for w in (w_in, w_br_h, w_br_a, w_out))
    moe_w1_b, moe_w3_b, moe_w2_b = (w.astype(BF16) for w in (moe_w1, moe_w3, moe_w2))

    for i in range(DEPTH):
        sh1, sc1, g1, sh2, sc2, g2 = (mod[i, :, k * D:(k + 1) * D].reshape(B, 1, D) for k in range(6))

        h = _norm_mod(xt, norm1_g[i].reshape(1, D), sc1, sh1, L)
        u = _proj(h, w_in_b, i, 0, HY_COLS, tn=HY_WIDTH)
        qkv = _proj_qkv(h, w_in_b, i, rope, L, [BF16 if dil == 1 else F32 for _, dil in ATT_GROUPS])
        gates = _proj(h, w_in_b, i, HY_COLS + QKV_COLS, GATE_COLS2, tn=D_MODEL // 2)

        k2 = _hyena_filter(feat, _pad2(hf_w1[i], LANES, LANES), _pad2(hf_b1[i][None], 1, LANES),
                           _pad2(hf_w2[i], LANES, LANES), _pad2(hf_b2[i][None], 1, LANES),
                           _pad2(hf_w3[i], LANES, LANES), _pad2(hf_b3[i][None], 1, LANES),
                           _pad2(hf_w4[i], LANES, 2 * C), _pad2(hf_freq[i][None], 1, LANES), deltas)
        kf = _filter_fft(k2.reshape(-1, 2 * C), mats)
        x0c, zv = _short_conv(u, w_sc[i], b_sc[i], L)
        lp = L // FFT_R * FFT_T_PITCH
        y_hy = _long_conv(zv.reshape(B, lp, C), x0c.reshape(B, L, C), kf, hy_bias[i], mats)
        y_hy = y_hy.reshape(T // FFT_R, FFT_T_PITCH, C)

        attn = [(_attention_group if dil == 1 else _attention_strided)(qkv[g], dil, B, L)
                for g, (_, dil) in enumerate(ATT_GROUPS)]
        xt = _merge(xt, y_hy, attn, gates, w_br_h_b, w_br_a_b, w_out_b, i, g1, L)

        xt = _moe_layer(xt, norm2_g[i].reshape(1, D), sc2, sh2, g2, router_wt, router_b,
                        moe_w1_b, moe_w3_b, moe_w2_b, i, L)
    return _final_norm(xt, final_g.reshape(1, D)).reshape(B, L, D)
```

```python
import math

import jax
import jax.numpy as jnp
import numpy as np
from jax import lax
from jax.experimental import pallas as pl
from jax.experimental.pallas import tpu as pltpu

D_MODEL = 1024
DEPTH = 2
HY_WIDTH = 768
HY_EMB = 33
HY_FFN = 64
HY_FAST_DECAY_PCT = 0.3
HY_SLOW_DECAY_PCT = 1.5
HY_TARGET = 1e-2
HEAD_DIM = 64
ATT_GROUPS = ((128, 1), (512, 4), (2048, 16))
HEADS_PER_GROUP = 4
N_HEADS = HEADS_PER_GROUP * len(ATT_GROUPS)
ATT_WIDTH = N_HEADS * HEAD_DIM
ATT_OUT = HEADS_PER_GROUP * HEAD_DIM
ROPE_THETA = 10000.0
IN_SPLITS = (3 * HY_WIDTH, ATT_WIDTH, ATT_WIDTH, ATT_WIDTH, D_MODEL, D_MODEL)
N_EXPERTS = 16
N_GROUPS = 4
EXPERTS_PER_GROUP = N_EXPERTS // N_GROUPS
D_EXPERT = 512
NORM_EPS = 1e-6
MASK_VALUE = -1e30

LANES = 128
MXU_DIM = 256
VMEM_LIMIT_BYTES = 56 * 1024 * 1024

F32 = jnp.float32
BF16 = jnp.bfloat16
HIGHEST = lax.Precision.HIGHEST

PROJ_TM = 2048
PROJ_TN = MXU_DIM
PROJ_CHUNK = 512
HY_COLS = 3 * HY_WIDTH
QKV_COLS = 3 * ATT_WIDTH
GATE_COLS2 = 2 * D_MODEL

FFT_R = 128
FFT_KH = FFT_R // 2 + 1
FFT_KP = 72
FFT_SLAB = 2 * FFT_KP
FFT_G_PITCH = FFT_SLAB + 8
FFT_T_PITCH = FFT_R + 8
CONV_CB = 128
FFT_UNROLL_TIME = 8
FFT_UNROLL_FREQ = 5

PAIRS_PER_GROUP = 6
N_BUCKETS = N_GROUPS * PAIRS_PER_GROUP
BUCKET_ROWS = 32
ROUTER_TM = 512
EXPERT_TM = 256
ROW_BLOCK = 1024
GATE_COLS = LANES


def _params(*sem):
    return pltpu.CompilerParams(dimension_semantics=sem, vmem_limit_bytes=VMEM_LIMIT_BYTES)


def _sigmoid(x):
    return 1.0 / (1.0 + jnp.exp(-x))


def _store_time_padded(o_ref, val):
    groups = val.shape[0] // FFT_R
    o_ref[:, :FFT_R, :] = val.reshape(groups, FFT_R, val.shape[1])
    o_ref[:, FFT_R:, :] = jnp.zeros((groups, FFT_T_PITCH - FFT_R, val.shape[1]), o_ref.dtype)


def _time_padded_shape(rows, cols):
    return jax.ShapeDtypeStruct((rows // FFT_R, FFT_T_PITCH, cols), F32)


def _ada_kernel(c_ref, w_ref, b_ref, o_ref):
    c = c_ref[...]
    c_act = c * _sigmoid(c)
    o_ref[0] = jnp.dot(c_act, w_ref[0], precision=HIGHEST, preferred_element_type=F32) + b_ref[0]


def _ada(c_pad, w_ada, b_ada):
    depth, D, N = w_ada.shape
    rows = c_pad.shape[0]
    tn = N // 4
    return pl.pallas_call(
        _ada_kernel,
        out_shape=jax.ShapeDtypeStruct((depth, rows, N), F32),
        grid=(depth, N // tn),
        in_specs=[pl.BlockSpec((rows, D), lambda l, j: (0, 0)),
                  pl.BlockSpec((1, D, tn), lambda l, j: (l, 0, j)),
                  pl.BlockSpec((1, 1, tn), lambda l, j: (l, 0, j))],
        out_specs=pl.BlockSpec((1, rows, tn), lambda l, j: (l, 0, j)),
        compiler_params=_params("arbitrary", "arbitrary"),
        name="ada_mod",
    )(c_pad, w_ada, b_ada.reshape(depth, 1, N))


def _norm_mod_kernel(x_ref, g_ref, sc_ref, sh_ref, o_ref):
    x = x_ref[...]
    r = lax.rsqrt(jnp.mean(x * x, axis=-1, keepdims=True) + NORM_EPS)
    h = (x * r) * g_ref[...]
    o_ref[...] = (h * (1.0 + sc_ref[0]) + sh_ref[0]).astype(o_ref.dtype)


def _norm_mod(x, g, sc, sh, seq_len):
    T, D = x.shape
    tm = 1024
    per_batch = seq_len // tm
    return pl.pallas_call(
        _norm_mod_kernel,
        out_shape=jax.ShapeDtypeStruct((T, D), BF16),
        grid=(T // tm,),
        in_specs=[pl.BlockSpec((tm, D), lambda i: (i, 0)),
                  pl.BlockSpec((1, D), lambda i: (0, 0)),
                  pl.BlockSpec((1, 1, D), lambda i: (i // per_batch, 0, 0)),
                  pl.BlockSpec((1, 1, D), lambda i: (i // per_batch, 0, 0))],
        out_specs=pl.BlockSpec((tm, D), lambda i: (i, 0)),
        compiler_params=_params("arbitrary"),
        name="norm_mod",
    )(x, g, sc, sh)


def _proj_kernel(h_ref, w_ref, o_ref):
    o_ref[...] = jnp.dot(h_ref[...], w_ref[0], preferred_element_type=F32).astype(o_ref.dtype)


def _proj(h, w, layer, col0, n_cols, tn):
    T, D = h.shape
    tm = PROJ_TM
    assert col0 % tn == 0 and n_cols % tn == 0
    off = col0 // tn
    return pl.pallas_call(
        _proj_kernel,
        out_shape=jax.ShapeDtypeStruct((T, n_cols), BF16),
        grid=(T // tm, n_cols // tn),
        in_specs=[pl.BlockSpec((tm, D), lambda i, j: (i, 0)),
                  pl.BlockSpec((1, D, tn), lambda i, j: (layer, 0, off + j))],
        out_specs=pl.BlockSpec((tm, tn), lambda i, j: (i, j)),
        compiler_params=_params("arbitrary", "arbitrary"),
        name="proj",
    )(h, w)


def _proj_qkv_kernel(h_ref, wq_ref, wk_ref, wv_ref, cos_ref, sin_ref, rot_ref, o_ref):
    tm, tn = h_ref.shape[0], PROJ_TN
    rot = rot_ref[...]
    for c in range(tm // PROJ_CHUNK):
        rows = slice(c * PROJ_CHUNK, (c + 1) * PROJ_CHUNK)
        hc = h_ref[rows, :]
        cos = jnp.concatenate([cos_ref[rows, :], cos_ref[rows, :]], axis=1)
        sin = jnp.concatenate([sin_ref[rows, :], sin_ref[rows, :]], axis=1)

        def roped(w_ref):
            acc = jnp.dot(hc, w_ref[0], preferred_element_type=F32)
            swapped = jnp.dot(acc.astype(BF16), rot, preferred_element_type=F32)
            return acc * cos + swapped * sin

        o_ref[rows, 0:tn] = (roped(wq_ref) * HEAD_DIM ** -0.5).astype(o_ref.dtype)
        o_ref[rows, tn:2 * tn] = roped(wk_ref).astype(o_ref.dtype)
        o_ref[rows, 2 * tn:3 * tn] = jnp.dot(hc, wv_ref[0], preferred_element_type=F32).astype(o_ref.dtype)


def _proj_qkv(h, w, layer, group, rope, seq_len, out_dtype):
    T, D = h.shape
    tm, tn = PROJ_TM, PROJ_TN
    n_groups = len(ATT_GROUPS)
    off = HY_COLS // tn + group
    cos_t, sin_t, rot = rope
    per_batch = seq_len // tm
    tab_spec = pl.BlockSpec((tm, LANES), lambda i: (i % per_batch, 0))
    w_spec = lambda kind: pl.BlockSpec((1, D, tn), lambda i: (layer, 0, off + n_groups * kind))
    return pl.pallas_call(
        _proj_qkv_kernel,
        out_shape=jax.ShapeDtypeStruct((T, 3 * tn), out_dtype),
        grid=(T // tm,),
        in_specs=[pl.BlockSpec((tm, D), lambda i: (i, 0)), w_spec(0), w_spec(1), w_spec(2),
                  tab_spec, tab_spec, pl.BlockSpec((tn, tn), lambda i: (0, 0))],
        out_specs=pl.BlockSpec((tm, 3 * tn), lambda i: (i, 0)),
        compiler_params=_params("arbitrary"),
        name="proj_qkv",
    )(h, w, w, w, cos_t, sin_t, rot)


def _rope_tables(seq_len):
    half = HEAD_DIM // 2
    inv = ROPE_THETA ** (-jnp.arange(half, dtype=F32) / half)
    ang = jnp.arange(seq_len, dtype=F32)[:, None] * inv[None, :]
    reps = LANES // half
    cos_t = jnp.tile(jnp.cos(ang), (1, reps))
    sin_t = jnp.tile(jnp.sin(ang), (1, reps))
    rot = np.zeros((PROJ_TN, PROJ_TN), np.float32)
    for j in range(PROJ_TN):
        if j % HEAD_DIM < half:
            rot[j + half, j] = -1.0
        else:
            rot[j - half, j] = 1.0
    return cos_t, sin_t, jnp.asarray(rot, BF16)


def _hyena_filter_kernel(feat_ref, w1_ref, b1_ref, w2_ref, b2_ref, w3_ref, b3_ref, w4_ref, fr_ref,
                         dl_ref, o_ref):
    i = pl.program_id(0)
    tm = feat_ref.shape[0]
    z = feat_ref[...]
    fr = fr_ref[...]
    h = jnp.sin(fr * (jnp.dot(z, w1_ref[...], precision=HIGHEST, preferred_element_type=F32) + b1_ref[...]))
    h = jnp.sin(fr * (jnp.dot(h, w2_ref[...], precision=HIGHEST, preferred_element_type=F32) + b2_ref[...]))
    h = jnp.sin(fr * (jnp.dot(h, w3_ref[...], precision=HIGHEST, preferred_element_type=F32) + b3_ref[...]))
    k = jnp.dot(h, w4_ref[...], precision=HIGHEST, preferred_element_type=F32)
    decay = jnp.exp(-z[:, 0:1] * jnp.abs(dl_ref[...]))
    pos = i * tm + lax.broadcasted_iota(jnp.int32, (tm, 1), 0)
    is_bwd = lax.broadcasted_iota(jnp.int32, (1, k.shape[1]), 1) >= HY_WIDTH
    _store_time_padded(o_ref, jnp.where((pos == 0) & is_bwd, 0.0, k * decay))


def _hyena_filter(feat, w1, b1, w2, b2, w3, b3, w4, freq, deltas2):
    n_rows = feat.shape[0]
    C2 = 2 * HY_WIDTH
    tm = 1024
    full = lambda i: (0, 0)
    return pl.pallas_call(
        _hyena_filter_kernel,
        out_shape=_time_padded_shape(n_rows, C2),
        grid=(n_rows // tm,),
        in_specs=[pl.BlockSpec((tm, LANES), lambda i: (i, 0)),
                  pl.BlockSpec((LANES, LANES), full), pl.BlockSpec((1, LANES), full),
                  pl.BlockSpec((LANES, LANES), full), pl.BlockSpec((1, LANES), full),
                  pl.BlockSpec((LANES, LANES), full), pl.BlockSpec((1, LANES), full),
                  pl.BlockSpec((LANES, C2), full),
                  pl.BlockSpec((1, LANES), full),
                  pl.BlockSpec((1, C2), full)],
        out_specs=pl.BlockSpec((tm // FFT_R, FFT_T_PITCH, C2), lambda i: (i, 0, 0)),
        compiler_params=_params("arbitrary"),
        name="hyena_filter",
    )(feat, w1, b1, w2, b2, w3, b3, w4, freq, deltas2)


def _hyena_features(seq_len):
    L = seq_len
    t = jnp.linspace(0.0, 1.0, L, dtype=F32)[:, None]
    bands = (HY_EMB - 1) // 2
    w = 2.0 * math.pi * jnp.arange(L, dtype=F32)[:, None] / L
    f = jnp.linspace(1e-4, bands - 1, bands, dtype=F32)[None, :]
    z = jnp.concatenate([t, jnp.cos(f * w), -jnp.sin(f * w)], axis=-1)
    z = jnp.pad(z, ((0, 0), (0, LANES - HY_EMB)))
    max_decay = math.log(HY_TARGET) / HY_FAST_DECAY_PCT
    min_decay = math.log(HY_TARGET) / HY_SLOW_DECAY_PCT
    deltas = jnp.linspace(min_decay, max_decay, HY_WIDTH, dtype=F32)[None, :]
    return z, jnp.concatenate([deltas, deltas], axis=1)


def _pad2(a, rows, cols):
    return jnp.pad(a, ((0, rows - a.shape[0]), (0, cols - a.shape[1])))


def _dft_matrices():
    R, KH, KP = FFT_R, FFT_KH, FFT_KP
    N = R * R
    n1 = np.arange(R)[:, None, None]
    k2 = np.arange(KP)[None, :, None]
    n2 = np.arange(R)[None, None, :]
    phase = 2.0 * np.pi * ((n2 * k2 % R) / R + (n1 * k2) / N)
    live = (k2 < KH)
    a1 = np.concatenate([np.cos(phase) * live, -np.sin(phase) * live], axis=1)
    wgt = np.where((k2 == 0) | (k2 == R // 2), 1.0, 2.0) * live / N
    b1 = np.concatenate([np.cos(phase) * wgt, -np.sin(phase) * wgt], axis=1)
    b1 = np.transpose(b1, (0, 2, 1))[:, :R // 2, :]
    th = 2.0 * np.pi * (np.arange(R)[:, None] * np.arange(R)[None, :] % R) / R
    c, s = np.cos(th), np.sin(th)
    w2f = np.block([[c, s], [-s, c]])
    w2i = np.block([[c, -s], [s, c]])
    as_bf = lambda a: jnp.asarray(a.astype(np.float32)).astype(BF16)
    return dict(a1=as_bf(a1[:, :, :R // 2]), b1=as_bf(b1), w2f=as_bf(w2f), w2i=as_bf(w2i))


def _fft_stage1(src_ref, a1_ref, g_ref, n_rows):
    def body(n1, carry):
        xs = src_ref[pl.ds(n1, n_rows, stride=FFT_T_PITCH), :].astype(BF16)
        slab = jnp.dot(a1_ref[n1], xs, preferred_element_type=F32)
        g_ref[pl.ds(pl.multiple_of(n1 * FFT_G_PITCH, 8), FFT_SLAB), :] = slab
        return carry
    lax.fori_loop(0, FFT_R, body, 0, unroll=FFT_UNROLL_TIME)


def _load_freq_rows(g_ref, k2):
    re = g_ref[pl.ds(k2, FFT_R, stride=FFT_G_PITCH), :]
    im = g_ref[pl.ds(FFT_KP + k2, FFT_R, stride=FFT_G_PITCH), :]
    return jnp.concatenate([re, im], axis=0)


def _filter_fft_kernel(fwd_ref, bwd_ref, a1_ref, w2_ref, o_ref, gf_ref, gb_ref):
    n2_rows = fwd_ref.shape[0] // FFT_T_PITCH
    cb = fwd_ref.shape[1]

    def stage1(n1, carry):
        rows = pl.ds(n1, n2_rows, stride=FFT_T_PITCH)
        xs = jnp.concatenate([fwd_ref[rows, :], bwd_ref[rows, :]], axis=1).astype(BF16)
        slab = jnp.dot(a1_ref[n1], xs, preferred_element_type=F32)
        dst = pl.ds(pl.multiple_of(n1 * FFT_G_PITCH, 8), FFT_SLAB)
        gf_ref[dst, :] = slab[:, :cb]
        gb_ref[dst, :] = slab[:, cb:]
        return carry
    lax.fori_loop(0, FFT_R, stage1, 0, unroll=FFT_UNROLL_TIME)

    def body(k2, carry):
        gk = jnp.concatenate([_load_freq_rows(gf_ref, k2), _load_freq_rows(gb_ref, k2)], axis=1).astype(BF16)
        x = jnp.dot(w2_ref[...], gk, preferred_element_type=F32)
        f, b = x[:, :cb], x[:, cb:]
        spec = jnp.concatenate([f[:FFT_R] + b[:FFT_R], f[FFT_R:] - b[FFT_R:]], axis=0)
        o_ref[k2] = spec.astype(o_ref.dtype)
        return carry
    lax.fori_loop(0, FFT_KH, body, 0, unroll=FFT_UNROLL_FREQ)


def _filter_fft(k2, mats):
    n_rows = k2.shape[0]
    C = HY_WIDTH
    cb = CONV_CB
    ncb = C // cb
    return pl.pallas_call(
        _filter_fft_kernel,
        out_shape=jax.ShapeDtypeStruct((FFT_KH, 2 * FFT_R, C), BF16),
        grid=(ncb,),
        in_specs=[pl.BlockSpec((n_rows, cb), lambda j: (0, j)),
                  pl.BlockSpec((n_rows, cb), lambda j: (0, ncb + j)),
                  pl.BlockSpec((FFT_R, FFT_SLAB, FFT_R // 2), lambda j: (0, 0, 0)),
                  pl.BlockSpec((2 * FFT_R, 2 * FFT_R), lambda j: (0, 0))],
        out_specs=pl.BlockSpec((FFT_KH, 2 * FFT_R, cb), lambda j: (0, 0, j)),
        scratch_shapes=[pltpu.VMEM((FFT_R * FFT_G_PITCH, cb), F32), pltpu.VMEM((FFT_R * FFT_G_PITCH, cb), F32)],
        compiler_params=_params("arbitrary"),
        name="hyena_filter_fft",
    )(k2, k2, mats["a1"], mats["w2f"])


def _short_conv_kernel(x0_ref, x1_ref, v_ref, p0_ref, p1_ref, pv_ref, n0_ref, n1_ref, nv_ref,
                       w0_ref, w1_ref, wv_ref, b0_ref, b1_ref, bv_ref, x0c_ref, zv_ref, scr, *, tiles_per_seq):
    i = pl.program_id(0)
    tm = x0_ref.shape[0]
    first = (i % tiles_per_seq) == 0
    last = (i % tiles_per_seq) == tiles_per_seq - 1
    halo = p0_ref.shape[0]

    def conv(u_ref, prev_ref, next_ref, w_ref, b_ref):
        prev_row = jnp.where(first, 0.0, prev_ref[halo - 1:halo, :].astype(F32))
        next_row = jnp.where(last, 0.0, next_ref[0:1, :].astype(F32))
        scr[7:8, :] = prev_row
        scr[8:8 + tm, :] = u_ref[...].astype(F32)
        scr[8 + tm:9 + tm, :] = next_row
        w = w_ref[...]
        return (scr[pl.ds(7, tm), :] * w[0:1] + scr[pl.ds(8, tm), :] * w[1:2]
                + scr[pl.ds(9, tm), :] * w[2:3] + b_ref[...])

    x0c_ref[...] = conv(x0_ref, p0_ref, n0_ref, w0_ref, b0_ref).astype(x0c_ref.dtype)
    x1c = conv(x1_ref, p1_ref, n1_ref, w1_ref, b1_ref)
    vc = conv(v_ref, pv_ref, nv_ref, wv_ref, bv_ref)
    _store_time_padded(zv_ref, vc * x1c)


def _short_conv(u, w_sc, b_sc, seq_len):
    T = u.shape[0]
    C = HY_WIDTH
    tm, cb, halo = 1024, 256, 16
    ncb = C // cb
    tiles_per_seq = seq_len // tm
    hb = tm // halo
    n_halo = T // halo

    def part(p):
        return pl.BlockSpec((tm, cb), lambda i, j: (i, p * ncb + j))

    def prev(p):
        return pl.BlockSpec((halo, cb), lambda i, j: (jnp.maximum(i * hb - 1, 0), p * ncb + j))

    def nxt(p):
        return pl.BlockSpec((halo, cb), lambda i, j: (jnp.minimum((i + 1) * hb, n_halo - 1), p * ncb + j))

    def wpart(p):
        return pl.BlockSpec((3, cb), lambda i, j: (0, p * ncb + j))

    def bpart(p):
        return pl.BlockSpec((1, cb), lambda i, j: (0, p * ncb + j))

    kern = lambda *refs: _short_conv_kernel(*refs, tiles_per_seq=tiles_per_seq)
    b2 = b_sc.reshape(1, 3 * C)
    return pl.pallas_call(
        kern,
        out_shape=(jax.ShapeDtypeStruct((T, C), BF16), _time_padded_shape(T, C)),
        grid=(T // tm, ncb),
        in_specs=[part(0), part(1), part(2), prev(0), prev(1), prev(2), nxt(0), nxt(1), nxt(2),
                  wpart(0), wpart(1), wpart(2), bpart(0), bpart(1), bpart(2)],
        out_specs=(pl.BlockSpec((tm, cb), lambda i, j: (i, j)),
                   pl.BlockSpec((tm // FFT_R, FFT_T_PITCH, cb), lambda i, j: (i, 0, j))),
        scratch_shapes=[pltpu.VMEM((tm + 16, cb), F32)],
        compiler_params=_params("arbitrary", "arbitrary"),
        name="hyena_short_conv",
    )(u, u, u, u, u, u, u, u, u, w_sc, w_sc, w_sc, b2, b2, b2)


def _long_conv_kernel(zv_ref, x0_ref, kf_ref, bias_ref, a1_ref, b1_ref, w2f_ref, w2i_ref, o_ref, g_ref):
    n2_rows = zv_ref.shape[1] // FFT_T_PITCH
    zv2 = zv_ref.at[0]
    o2 = o_ref.at[0]
    _fft_stage1(zv2, a1_ref, g_ref, n2_rows)

    def freq_body(k2, carry):
        gk = _load_freq_rows(g_ref, k2).astype(BF16)
        x = jnp.dot(w2f_ref[...], gk, preferred_element_type=F32)
        kf = kf_ref[k2].astype(F32)
        xr, xi = x[:FFT_R], x[FFT_R:]
        kr, ki = kf[:FFT_R], kf[FFT_R:]
        p = jnp.concatenate([xr * kr - xi * ki, xr * ki + xi * kr], axis=0).astype(BF16)
        hk = jnp.dot(w2i_ref[...], p, preferred_element_type=F32)
        g_ref[pl.ds(k2, FFT_R, stride=FFT_G_PITCH), :] = hk[:FFT_R]
        g_ref[pl.ds(FFT_KP + k2, FFT_R, stride=FFT_G_PITCH), :] = hk[FFT_R:]
        return carry
    lax.fori_loop(0, FFT_KH, freq_body, 0, unroll=FFT_UNROLL_FREQ)

    def time_body(n1, carry):
        slab = g_ref[pl.ds(pl.multiple_of(n1 * FFT_G_PITCH, 8), FFT_SLAB), :].astype(BF16)
        o2[pl.ds(n1, n2_rows, stride=FFT_T_PITCH), :] = jnp.dot(b1_ref[n1], slab, preferred_element_type=F32)
        return carry
    lax.fori_loop(0, FFT_R, time_body, 0, unroll=FFT_UNROLL_TIME)

    bias = bias_ref[...]
    pad_zeros = jnp.zeros((FFT_T_PITCH - FFT_R, o_ref.shape[2]), F32)

    def out_body(g, carry):
        prow = pl.multiple_of(g * FFT_T_PITCH, 8)
        rows = pl.ds(prow, FFT_R)
        x0 = x0_ref[0, pl.ds(pl.multiple_of(g * FFT_R, FFT_R), FFT_R), :].astype(F32)
        o2[rows, :] = (o2[rows, :] + zv2[rows, :] * bias) * x0
        o2[pl.ds(prow + FFT_R, FFT_T_PITCH - FFT_R), :] = pad_zeros
        return carry
    lax.fori_loop(0, n2_rows, out_body, 0, unroll=4)


def _long_conv(zv, x0c, kf, hy_bias, mats):
    B, Lp, C = zv.shape
    L = x0c.shape[1]
    cb = CONV_CB
    blk = lambda j, b: (b, 0, j)
    return pl.pallas_call(
        _long_conv_kernel,
        out_shape=jax.ShapeDtypeStruct((B, Lp, C), F32),
        grid=(C // cb, B),
        in_specs=[pl.BlockSpec((1, Lp, cb), blk),
                  pl.BlockSpec((1, L, cb), blk),
                  pl.BlockSpec((FFT_KH, 2 * FFT_R, cb), lambda j, b: (0, 0, j)),
                  pl.BlockSpec((1, cb), lambda j, b: (0, j)),
                  pl.BlockSpec((FFT_R, FFT_SLAB, FFT_R // 2), lambda j, b: (0, 0, 0)),
                  pl.BlockSpec((FFT_R, FFT_R // 2, FFT_SLAB), lambda j, b: (0, 0, 0)),
                  pl.BlockSpec((2 * FFT_R, 2 * FFT_R), lambda j, b: (0, 0)),
                  pl.BlockSpec((2 * FFT_R, 2 * FFT_R), lambda j, b: (0, 0))],
        out_specs=pl.BlockSpec((1, Lp, cb), blk),
        scratch_shapes=[pltpu.VMEM((FFT_R * FFT_G_PITCH, cb), F32)],
        compiler_params=_params("arbitrary", "arbitrary"),
        name="hyena_long_conv",
    )(zv, x0c, kf, hy_bias.reshape(1, C), mats["a1"], mats["b1"], mats["w2f"], mats["w2i"])


ATT_TQ = 128
ATT_RADIUS = 64
ATT_WINDOW = ATT_TQ + 2 * ATT_RADIUS


ATT_SUB = 4


def _attn_kernel(q_ref, k_ref, v_ref, o_ref, lse_ref):
    i = pl.program_id(2)
    ls = k_ref.shape[1]
    nh = HEADS_PER_GROUP
    head_of_col = lax.broadcasted_iota(jnp.int32, (1, ATT_OUT), 1) // HEAD_DIM
    row_iota = lax.broadcasted_iota(jnp.int32, (nh * ATT_TQ, ATT_WINDOW), 0) % ATT_TQ
    col_iota = lax.broadcasted_iota(jnp.int32, (nh * ATT_TQ, ATT_WINDOW), 1)
    rel = row_iota - col_iota
    for sub in range(ATT_SUB):
        q0 = (i * ATT_SUB + sub) * ATT_TQ
        rows = slice(sub * ATT_TQ, (sub + 1) * ATT_TQ)
        q = q_ref[0, rows, :]
        start = jnp.clip(q0 - ATT_RADIUS, 0, ls - ATT_WINDOW)
        start = pl.multiple_of(start, ATT_RADIUS)
        kw = k_ref[0, pl.ds(start, ATT_WINDOW), :]
        vw = v_ref[0, pl.ds(start, ATT_WINDOW), :]
        out, lse = _attn_unit(q, kw, vw, rel, q0 - start, head_of_col)
        o_ref[0, rows, :] = out.astype(o_ref.dtype)
        lse_ref[0, rows, :] = lse


def _attn_unit(q, kw, vw, rel, q_minus_start, head_of_col):
    nh = HEADS_PER_GROUP
    band = jnp.abs(q_minus_start + rel) <= ATT_RADIUS
    zero = jnp.zeros_like(q)
    q4 = jnp.concatenate([jnp.where(head_of_col == h, q, zero) for h in range(nh)], axis=0)
    s = lax.dot_general(q4, kw, (((1,), (1,)), ((), ())), preferred_element_type=F32)
    s = jnp.where(band, s, MASK_VALUE)
    m = jnp.max(s, axis=-1, keepdims=True)
    p = jnp.exp(s - m)
    den = jnp.sum(p, axis=-1, keepdims=True)
    pv = jnp.dot(p.astype(BF16), vw, preferred_element_type=F32) / den
    lse4 = m + jnp.log(den)
    out = jnp.zeros((ATT_TQ, ATT_OUT), F32)
    lse = jnp.zeros((ATT_TQ, ATT_OUT), F32)
    for h in range(nh):
        mine = head_of_col == h
        hrows = slice(h * ATT_TQ, (h + 1) * ATT_TQ)
        out = jnp.where(mine, pv[hrows], out)
        lse = jnp.where(mine, lse4[hrows], lse)
    return out, lse


ATT_CHUNK = 2048


def _attn_strided_kernel(q0_ref, q1_ref, k0_ref, k1_ref, v0_ref, v1_ref, o_ref, lse_ref, o_scr, lse_scr, *, dil):
    i = pl.program_id(1)
    ls = q0_ref.shape[1] // dil

    def gather(lo_ref, hi_ref, first, n):
        rows = pl.ds(first, n, stride=dil)
        return jnp.concatenate([lo_ref[0, rows, :], hi_ref[0, rows, :]], axis=1).astype(BF16)

    units = ATT_CHUNK // ATT_TQ
    shift = dil.bit_length() - 1
    head_of_col = lax.broadcasted_iota(jnp.int32, (1, ATT_OUT), 1) // HEAD_DIM
    row_iota = lax.broadcasted_iota(jnp.int32, (HEADS_PER_GROUP * ATT_TQ, ATT_WINDOW), 0) % ATT_TQ
    rel = row_iota - lax.broadcasted_iota(jnp.int32, (HEADS_PER_GROUP * ATT_TQ, ATT_WINDOW), 1)

    def unit(u, carry):
        r = u & (dil - 1)
        sb = u >> shift
        m0 = i * (ATT_CHUNK // dil) + sb * ATT_TQ
        start = jnp.clip(m0 - ATT_RADIUS, 0, ls - ATT_WINDOW)
        q = gather(q0_ref, q1_ref, m0 * dil + r, ATT_TQ)
        kw = gather(k0_ref, k1_ref, start * dil + r, ATT_WINDOW)
        vw = gather(v0_ref, v1_ref, start * dil + r, ATT_WINDOW)
        out, lse = _attn_unit(q, kw, vw, rel, m0 - start, head_of_col)
        dst = pl.ds(sb * ATT_TQ * dil + r, ATT_TQ, stride=dil)
        for half in range(2):
            lanes = slice(half * LANES, (half + 1) * LANES)
            o_scr[half, dst, :] = out[:, lanes]
            lse_scr[half, dst, :] = lse[:, lanes]
        return carry
    lax.fori_loop(0, units, unit, 0, unroll=2)
    for half in range(2):
        lanes = slice(half * LANES, (half + 1) * LANES)
        o_ref[0, :, lanes] = o_scr[half]
        lse_ref[0, :, lanes] = lse_scr[half]


def _attention_strided(qkv_g, dil, batch, seq_len):
    view = qkv_g.reshape(batch, seq_len, 3 * ATT_OUT)
    halves = [pl.BlockSpec((1, seq_len, LANES), lambda b, i, c=c: (b, 0, c), pipeline_mode=pl.Buffered(1))
              for c in range(3 * ATT_OUT // LANES)]
    o_spec = pl.BlockSpec((1, ATT_CHUNK, ATT_OUT), lambda b, i: (b, i, 0))
    kern = lambda *refs: _attn_strided_kernel(*refs, dil=dil)
    o, lse = pl.pallas_call(
        kern,
        out_shape=(jax.ShapeDtypeStruct((batch, seq_len, ATT_OUT), F32),
                   jax.ShapeDtypeStruct((batch, seq_len, ATT_OUT), F32)),
        grid=(batch, seq_len // ATT_CHUNK),
        in_specs=halves,
        out_specs=(o_spec, o_spec),
        scratch_shapes=[pltpu.VMEM((2, ATT_CHUNK, LANES), F32), pltpu.VMEM((2, ATT_CHUNK, LANES), F32)],
        compiler_params=_params("arbitrary", "arbitrary"),
        name=f"dilated_attn_d{dil}",
    )(*([view] * len(halves)))
    T = batch * seq_len
    return o.reshape(T, ATT_OUT), lse.reshape(T, ATT_OUT)


def _attention_group(qkv_g, dil, batch, seq_len):
    ls = seq_len // dil
    tq = ATT_SUB * ATT_TQ
    view = qkv_g.reshape(batch, ls, dil * 3 * ATT_OUT)

    def col(which):
        return lambda b, r, i: (b, 0, r * 3 + which)

    q_map = lambda b, r, i: (b, i, r * 3)
    o_map = lambda b, r, i: (b, i, r)
    o, lse = pl.pallas_call(
        _attn_kernel,
        out_shape=(jax.ShapeDtypeStruct((batch, ls, dil * ATT_OUT), BF16),
                   jax.ShapeDtypeStruct((batch, ls, dil * ATT_OUT), F32)),
        grid=(batch, dil, ls // tq),
        in_specs=[pl.BlockSpec((1, tq, ATT_OUT), q_map),
                  pl.BlockSpec((1, ls, ATT_OUT), col(1)),
                  pl.BlockSpec((1, ls, ATT_OUT), col(2))],
        out_specs=(pl.BlockSpec((1, tq, ATT_OUT), o_map), pl.BlockSpec((1, tq, ATT_OUT), o_map)),
        compiler_params=_params("arbitrary", "arbitrary", "arbitrary"),
        name=f"dilated_attn_d{dil}",
    )(view, view, view)
    T = batch * seq_len
    return o.reshape(T, ATT_OUT), lse.reshape(T, ATT_OUT)


def _merge_kernel(x_ref, yhy_ref, o1_ref, o2_ref, o3_ref, l1_ref, l2_ref, l3_ref, gh_ref, ga_ref,
                  wh_ref, wa_ref, wo_ref, g1_ref, out_ref):
    l1, l2, l3 = l1_ref[...], l2_ref[...], l3_ref[...]
    m = jnp.maximum(jnp.maximum(l1, l2), l3)
    e1, e2, e3 = jnp.exp(l1 - m), jnp.exp(l2 - m), jnp.exp(l3 - m)
    tot = e1 + e2 + e3
    y_at = (e1 * o1_ref[...].astype(F32) + e2 * o2_ref[...].astype(F32) + e3 * o3_ref[...].astype(F32)) / tot
    y_hy = yhy_ref[:, :FFT_R, :].reshape(x_ref.shape[0], yhy_ref.shape[2])
    a = jnp.dot(y_hy.astype(BF16), wh_ref[0], preferred_element_type=F32)
    b = jnp.dot(y_at.astype(BF16), wa_ref[0], preferred_element_type=F32)
    merged = _sigmoid(gh_ref[...].astype(F32)) * a + _sigmoid(ga_ref[...].astype(F32)) * b
    upd = jnp.dot(merged.astype(BF16), wo_ref[0], preferred_element_type=F32)
    out_ref[...] = x_ref[...] + g1_ref[0] * upd


def _merge(x, y_hy, attn, gates, w_br_h, w_br_a, w_out, layer, g1, seq_len):
    T, D = x.shape
    tm = 512
    per_batch = seq_len // tm
    (o1, l1), (o2, l2), (o3, l3) = attn
    row = lambda w: pl.BlockSpec((tm, w), lambda i: (i, 0))
    full = lambda a: pl.BlockSpec((1,) + a.shape[1:], lambda i: (layer, 0, 0))
    return pl.pallas_call(
        _merge_kernel,
        out_shape=jax.ShapeDtypeStruct((T, D), F32),
        grid=(T // tm,),
        in_specs=[row(D), pl.BlockSpec((tm // FFT_R, FFT_T_PITCH, HY_WIDTH), lambda i: (i, 0, 0)),
                  row(ATT_OUT), row(ATT_OUT), row(ATT_OUT),
                  row(ATT_OUT), row(ATT_OUT), row(ATT_OUT),
                  pl.BlockSpec((tm, D), lambda i: (i, 0)), pl.BlockSpec((tm, D), lambda i: (i, 1)),
                  full(w_br_h), full(w_br_a), full(w_out),
                  pl.BlockSpec((1, 1, D), lambda i: (i // per_batch, 0, 0))],
        out_specs=row(D),
        compiler_params=_params("arbitrary"),
        name="mixer_merge",
    )(x, y_hy, o1, o2, o3, l1, l2, l3, gates, gates, w_br_h, w_br_a, w_out, g1)


def _router_kernel(x_ref, g_ref, sc_ref, sh_ref, rwt_ref, rb_ref, hg_ref, stats_ref, cnt_ref, base_ref):
    i = pl.program_id(0)
    tm = x_ref.shape[0]

    @pl.when(i == 0)
    def _():
        base_ref[...] = jnp.zeros_like(base_ref)

    x = x_ref[...]
    r = lax.rsqrt(jnp.mean(x * x, axis=-1, keepdims=True) + NORM_EPS)
    h = (x * r) * g_ref[...]
    h = h * (1.0 + sc_ref[0]) + sh_ref[0]

    logits = lax.dot_general(rwt_ref[...], h, (((1,), (1,)), ((), ())),
                             precision=HIGHEST, preferred_element_type=F32)
    scores = _sigmoid(logits)
    biased = scores + rb_ref[...]

    def row(a, k):
        return a[k:k + 1, :]

    sel = jnp.zeros((1, tm), jnp.int32)
    best = None
    for g in range(N_GROUPS):
        a, b, c, d = (row(biased, 4 * g + k) for k in range(4))
        m_ab, n_ab = jnp.maximum(a, b), jnp.minimum(a, b)
        m_cd, n_cd = jnp.maximum(c, d), jnp.minimum(c, d)
        gs = jnp.maximum(m_ab, m_cd) + jnp.maximum(jnp.minimum(m_ab, m_cd), jnp.maximum(n_ab, n_cd))
        if g == 0:
            best = gs
        else:
            better = gs > best
            sel = jnp.where(better, g, sel)
            best = jnp.where(better, gs, best)

    v, u = [], []
    for k in range(EXPERTS_PER_GROUP):
        vk = jnp.zeros((1, tm), F32)
        uk = jnp.zeros((1, tm), F32)
        for g in range(N_GROUPS):
            vk = jnp.where(sel == g, row(biased, 4 * g + k), vk)
            uk = jnp.where(sel == g, row(scores, 4 * g + k), uk)
        v.append(vk)
        u.append(uk)

    i1 = jnp.zeros((1, tm), jnp.int32)
    b1 = v[0]
    for k in range(1, EXPERTS_PER_GROUP):
        gt = v[k] > b1
        i1 = jnp.where(gt, k, i1)
        b1 = jnp.where(gt, v[k], b1)
    i2 = jnp.zeros((1, tm), jnp.int32)
    b2 = jnp.full((1, tm), -jnp.inf, F32)
    for k in range(EXPERTS_PER_GROUP):
        cand = (i1 != k) & (v[k] > b2)
        i2 = jnp.where(cand, k, i2)
        b2 = jnp.where(cand, v[k], b2)

    lo = jnp.minimum(i1, i2)
    hi = jnp.maximum(i1, i2)
    pair = jnp.where(lo == 0, hi - 1, jnp.where(lo == 1, hi + 1, 5))
    bucket = sel * PAIRS_PER_GROUP + pair

    u_lo = jnp.zeros((1, tm), F32)
    u_hi = jnp.zeros((1, tm), F32)
    for k in range(EXPERTS_PER_GROUP):
        u_lo = jnp.where(lo == k, u[k], u_lo)
        u_hi = jnp.where(hi == k, u[k], u_hi)
    tot = u_lo + u_hi
    w_lo = u_lo / tot
    w_hi = u_hi / tot

    rows = lax.broadcasted_iota(jnp.int32, (BUCKET_ROWS, tm), 0)
    onehot = (rows == bucket).astype(F32)
    t_src = lax.broadcasted_iota(jnp.int32, (tm, tm), 0)
    t_dst = lax.broadcasted_iota(jnp.int32, (tm, tm), 1)
    before = (t_src < t_dst).astype(BF16)
    cum = jnp.dot(onehot.astype(BF16), before, preferred_element_type=F32)
    base = base_ref[...]
    rank = jnp.sum(onehot * (cum + base), axis=0, keepdims=True)
    base = base + jnp.sum(onehot, axis=1, keepdims=True)
    base_ref[...] = base
    cnt_ref[...] = jnp.broadcast_to(base, cnt_ref.shape)

    srow = lax.broadcasted_iota(jnp.int32, (8, tm), 0)
    stats_ref[...] = jnp.where(srow == 0, bucket.astype(F32), jnp.where(srow == 1, rank, 0.0))

    grow = lax.broadcasted_iota(jnp.int32, (GATE_COLS, tm), 0)
    gates_t = jnp.where(grow == 0, w_lo, jnp.where(grow == 1, w_hi, 0.0))
    hg_ref[:, :D_MODEL] = h
    hg_ref[:, D_MODEL:] = gates_t.T


def _router(x, norm_g, sc, sh, router_wt, router_b, seq_len):
    T, D = x.shape
    tm = ROUTER_TM
    per_batch = seq_len // tm
    return pl.pallas_call(
        _router_kernel,
        out_shape=(jax.ShapeDtypeStruct((T, D + GATE_COLS), F32),
                   jax.ShapeDtypeStruct((8, T), F32),
                   jax.ShapeDtypeStruct((BUCKET_ROWS, LANES), F32)),
        grid=(T // tm,),
        in_specs=[pl.BlockSpec((tm, D), lambda i: (i, 0)),
                  pl.BlockSpec((1, D), lambda i: (0, 0)),
                  pl.BlockSpec((1, 1, D), lambda i: (i // per_batch, 0, 0)),
                  pl.BlockSpec((1, 1, D), lambda i: (i // per_batch, 0, 0)),
                  pl.BlockSpec((N_EXPERTS, D), lambda i: (0, 0)),
                  pl.BlockSpec((N_EXPERTS, 1), lambda i: (0, 0))],
        out_specs=(pl.BlockSpec((tm, D + GATE_COLS), lambda i: (i, 0)),
                   pl.BlockSpec((8, tm), lambda i: (0, i)),
                   pl.BlockSpec((BUCKET_ROWS, LANES), lambda i: (0, 0))),
        scratch_shapes=[pltpu.VMEM((BUCKET_ROWS, 1), F32)],
        compiler_params=_params("arbitrary"),
        name="moe_router",
    )(x, norm_g, sc, sh, router_wt, router_b)


def _start_row_copies(n_rows, make_copy):
    group = 8

    def body(g, carry):
        base = pl.multiple_of(g * group, group)
        for k in range(group):
            make_copy(base + k).start(priority=k % 2)
        return carry

    lax.fori_loop(0, n_rows // group, body, 0)


def _dispatch_kernel(pos_ref, hg_ref, xs_init_hbm, xs_hbm, sem):
    del xs_init_hbm
    _start_row_copies(ROW_BLOCK, lambda r: pltpu.make_async_copy(
        hg_ref.at[pl.ds(r, 1), :], xs_hbm.at[pl.ds(pos_ref[r], 1), :], sem))
    pltpu.make_async_copy(hg_ref, xs_hbm.at[pl.ds(0, ROW_BLOCK), :], sem).wait()


def _dispatch(pos, hg, n_rows):
    T, W = hg.shape
    zeros = jnp.zeros((n_rows, W), F32)
    return pl.pallas_call(
        _dispatch_kernel,
        out_shape=jax.ShapeDtypeStruct((n_rows, W), F32),
        grid=(T // ROW_BLOCK,),
        in_specs=[pl.BlockSpec((ROW_BLOCK,), lambda i: (i,), memory_space=pltpu.SMEM),
                  pl.BlockSpec((ROW_BLOCK, W), lambda i: (i, 0)),
                  pl.BlockSpec(memory_space=pl.ANY)],
        out_specs=pl.BlockSpec(memory_space=pl.ANY),
        scratch_shapes=[pltpu.SemaphoreType.DMA],
        input_output_aliases={2: 0},
        compiler_params=_params("arbitrary"),
        name="moe_dispatch",
    )(pos, hg, zeros)


def _expert_kernel(e_lo_ref, e_hi_ref, n_used_ref, xs_ref, w1a, w3a, w2a, w1b, w3b, w2b, y_ref):
    del e_lo_ref, e_hi_ref
    used = pl.program_id(0) < n_used_ref[0]

    @pl.when(jnp.logical_not(used))
    def _():
        y_ref[...] = jnp.zeros_like(y_ref)

    @pl.when(used)
    def _():
        xb = xs_ref[:, :D_MODEL].astype(BF16)
        g_lo = xs_ref[:, D_MODEL:D_MODEL + 1]
        g_hi = xs_ref[:, D_MODEL + 1:D_MODEL + 2]

        def ffn(w1, w3, w2):
            a = jnp.dot(xb, w1[0, 0], preferred_element_type=F32)
            b = jnp.dot(xb, w3[0, 0], preferred_element_type=F32)
            act = (a * _sigmoid(a)) * b
            return jnp.dot(act.astype(BF16), w2[0, 0], preferred_element_type=F32)

        y_ref[...] = g_lo * ffn(w1a, w3a, w2a) + g_hi * ffn(w1b, w3b, w2b)


def _experts(tile_lo, tile_hi, n_used, xs, w1, w3, w2, layer):
    n_rows, W = xs.shape
    D, F = w1.shape[2], w1.shape[3]
    n_tiles = n_rows // EXPERT_TM

    def x_map(j, lo, hi, nu):
        return (jnp.minimum(j, nu[0] - 1), 0)

    def w_lo_map(j, lo, hi, nu):
        return (layer, lo[j], 0, 0)

    def w_hi_map(j, lo, hi, nu):
        return (layer, hi[j], 0, 0)

    grid_spec = pltpu.PrefetchScalarGridSpec(
        num_scalar_prefetch=3,
        grid=(n_tiles,),
        in_specs=[pl.BlockSpec((EXPERT_TM, W), x_map),
                  pl.BlockSpec((1, 1, D, F), w_lo_map), pl.BlockSpec((1, 1, D, F), w_lo_map),
                  pl.BlockSpec((1, 1, F, D), w_lo_map),
                  pl.BlockSpec((1, 1, D, F), w_hi_map), pl.BlockSpec((1, 1, D, F), w_hi_map),
                  pl.BlockSpec((1, 1, F, D), w_hi_map)],
        out_specs=pl.BlockSpec((EXPERT_TM, D), lambda j, lo, hi, nu: (j, 0)),
    )
    return pl.pallas_call(
        _expert_kernel,
        out_shape=jax.ShapeDtypeStruct((n_rows, D), F32),
        grid_spec=grid_spec,
        compiler_params=_params("arbitrary"),
        name="moe_experts",
    )(tile_lo, tile_hi, n_used, xs, w1, w3, w2, w1, w3, w2)


def _combine_kernel(pos_ref, x_ref, g2_ref, ng_ref, sc_ref, sh_ref, ys_hbm, *refs):
    *out_refs, buf, sem = refs
    _start_row_copies(ROW_BLOCK, lambda r: pltpu.make_async_copy(
        ys_hbm.at[pl.ds(pos_ref[r], 1), :], buf.at[pl.ds(r, 1), :], sem))
    pltpu.make_async_copy(ys_hbm.at[pl.ds(0, ROW_BLOCK), :], buf, sem).wait()
    xn = x_ref[...] + g2_ref[0] * buf[...]
    r = lax.rsqrt(jnp.mean(xn * xn, axis=-1, keepdims=True) + NORM_EPS)
    h = (xn * r) * ng_ref[...]
    if len(out_refs) == 2:
        out_refs[0][...] = xn
        h = h * (1.0 + sc_ref[0]) + sh_ref[0]
    out_refs[-1][...] = h.astype(out_refs[-1].dtype)


def _combine(pos, x, g2, ys, norm_g, sc, sh, seq_len, last):
    T, D = x.shape
    per_batch = seq_len // ROW_BLOCK
    row = pl.BlockSpec((ROW_BLOCK, D), lambda i: (i, 0))
    per_b = pl.BlockSpec((1, 1, D), lambda i: (i // per_batch, 0, 0))
    if last:
        out_shape, out_specs = jax.ShapeDtypeStruct((T, D), F32), row
    else:
        out_shape = (jax.ShapeDtypeStruct((T, D), F32), jax.ShapeDtypeStruct((T, D), BF16))
        out_specs = (row, row)
    return pl.pallas_call(
        _combine_kernel,
        out_shape=out_shape,
        grid=(T // ROW_BLOCK,),
        in_specs=[pl.BlockSpec((ROW_BLOCK,), lambda i: (i,), memory_space=pltpu.SMEM),
                  row, per_b, pl.BlockSpec((1, D), lambda i: (0, 0)), per_b, per_b,
                  pl.BlockSpec(memory_space=pl.ANY)],
        out_specs=out_specs,
        scratch_shapes=[pltpu.VMEM((ROW_BLOCK, D), F32), pltpu.SemaphoreType.DMA],
        compiler_params=_params("arbitrary"),
        name="moe_combine",
    )(pos, x, g2, norm_g, sc, sh, ys)


def _moe_layer(x, norm_g, sc2, sh2, g2, router_wt, router_b, w1, w3, w2, layer, seq_len, next_norm, last):
    T, D = x.shape
    hg, stats, counts = _router(x, norm_g, sc2, sh2, router_wt, router_b, seq_len)

    cnt = counts[:N_BUCKETS, 0].astype(jnp.int32)
    padded = ((cnt + EXPERT_TM - 1) // EXPERT_TM) * EXPERT_TM
    ends = jnp.cumsum(padded)
    starts = ends - padded
    bucket = stats[0].astype(jnp.int32)
    pos = starts[bucket] + stats[1].astype(jnp.int32)
    n_tiles = T // EXPERT_TM + N_BUCKETS
    tile_row0 = jnp.arange(n_tiles, dtype=jnp.int32) * EXPERT_TM
    tile_bucket = jnp.sum((ends[None, :] <= tile_row0[:, None]).astype(jnp.int32), axis=1)
    tile_bucket = jnp.minimum(tile_bucket, N_BUCKETS - 1)
    grp, pair = tile_bucket // PAIRS_PER_GROUP, tile_bucket % PAIRS_PER_GROUP
    pair_lo = jnp.array([0, 0, 0, 1, 1, 2], jnp.int32)[pair]
    pair_hi = jnp.array([1, 2, 3, 2, 3, 3], jnp.int32)[pair]
    tile_lo = grp * EXPERTS_PER_GROUP + pair_lo
    tile_hi = grp * EXPERTS_PER_GROUP + pair_hi
    n_used = (ends[-1] // EXPERT_TM).astype(jnp.int32).reshape(1)

    xs = _dispatch(pos, hg, n_tiles * EXPERT_TM)
    ys = _experts(tile_lo, tile_hi, n_used, xs, w1, w3, w2, layer)
    return _combine(pos, x, g2, ys, *next_norm, seq_len, last)


def kernel(x, c, norm1_g, norm2_g, w_ada, b_ada, w_in, w_sc, b_sc, hf_w1, hf_b1, hf_w2, hf_b2, hf_w3, hf_b3,
           hf_w4, hf_freq, hy_bias, w_br_h, w_br_a, w_out, router_w, router_bias, moe_w1, moe_w3, moe_w2, final_g):
    B, L, D = x.shape
    T = B * L
    C = HY_WIDTH
    xt = x.reshape(T, D)

    rope = _rope_tables(L)
    feat, deltas = _hyena_features(L)
    mats = _dft_matrices()

    c_pad = jnp.pad(c, ((0, 8 - B), (0, 0)))
    mod = _ada(c_pad, w_ada, b_ada)[:, :B]
    router_wt = router_w.T
    router_b = router_bias.reshape(N_EXPERTS, 1)
    w_in_b, w_br_h_b, w_br_a_b, w_out_b = (w.astype(BF16) for w in (w_in, w_br_h, w_br_a, w_out))
    moe_w1_b, moe_w3_b, moe_w2_b = (w.astype(BF16) for w in (moe_w1, moe_w3, moe_w2))

    mods = [[mod[i, :, k * D:(k + 1) * D].reshape(B, 1, D) for k in range(6)] for i in range(DEPTH)]
    h = _norm_mod(xt, norm1_g[0].reshape(1, D), mods[0][1], mods[0][0], L)
    for i in range(DEPTH):
        sh1, sc1, g1, sh2, sc2, g2 = mods[i]

        u = _proj(h, w_in_b, i, 0, HY_COLS, tn=HY_WIDTH)
        qkv = [_proj_qkv(h, w_in_b, i, g, rope, L, BF16 if dil == 1 else F32)
               for g, (_, dil) in enumerate(ATT_GROUPS)]
        gates = _proj(h, w_in_b, i, HY_COLS + QKV_COLS, GATE_COLS2, tn=D_MODEL // 2)

        k2 = _hyena_filter(feat, _pad2(hf_w1[i], LANES, LANES), _pad2(hf_b1[i][None], 1, LANES),
                           _pad2(hf_w2[i], LANES, LANES), _pad2(hf_b2[i][None], 1, LANES),
                           _pad2(hf_w3[i], LANES, LANES), _pad2(hf_b3[i][None], 1, LANES),
                           _pad2(hf_w4[i], LANES, 2 * C), _pad2(hf_freq[i][None], 1, LANES), deltas)
        kf = _filter_fft(k2.reshape(-1, 2 * C), mats)
        x0c, zv = _short_conv(u, w_sc[i], b_sc[i], L)
        lp = L // FFT_R * FFT_T_PITCH
        y_hy = _long_conv(zv.reshape(B, lp, C), x0c.reshape(B, L, C), kf, hy_bias[i], mats)
        y_hy = y_hy.reshape(T // FFT_R, FFT_T_PITCH, C)

        attn = [(_attention_group if dil == 1 else _attention_strided)(qkv[g], dil, B, L)
                for g, (_, dil) in enumerate(ATT_GROUPS)]
        xt = _merge(xt, y_hy, attn, gates, w_br_h_b, w_br_a_b, w_out_b, i, g1, L)

        last = i == DEPTH - 1
        if last:
            next_norm = (final_g.reshape(1, D), sc2, sh2)
        else:
            next_norm = (norm1_g[i + 1].reshape(1, D), mods[i + 1][1], mods[i + 1][0])
        res = _moe_layer(xt, norm2_g[i].reshape(1, D), sc2, sh2, g2, router_wt, router_b,
                         moe_w1_b, moe_w3_b, moe_w2_b, i, L, next_norm, last)
        if last:
            return res.reshape(B, L, D)
        xt, h = res
```

```python
import math

import jax
import jax.numpy as jnp
import numpy as np
from jax import lax
from jax.experimental import pallas as pl
from jax.experimental.pallas import tpu as pltpu

D_MODEL = 1024
DEPTH = 2
HY_WIDTH = 768
HY_EMB = 33
HY_FFN = 64
HY_FAST_DECAY_PCT = 0.3
HY_SLOW_DECAY_PCT = 1.5
HY_TARGET = 1e-2
HEAD_DIM = 64
ATT_GROUPS = ((128, 1), (512, 4), (2048, 16))
HEADS_PER_GROUP = 4
N_HEADS = HEADS_PER_GROUP * len(ATT_GROUPS)
ATT_WIDTH = N_HEADS * HEAD_DIM
ATT_OUT = HEADS_PER_GROUP * HEAD_DIM
ROPE_THETA = 10000.0
IN_SPLITS = (3 * HY_WIDTH, ATT_WIDTH, ATT_WIDTH, ATT_WIDTH, D_MODEL, D_MODEL)
N_EXPERTS = 16
N_GROUPS = 4
EXPERTS_PER_GROUP = N_EXPERTS // N_GROUPS
D_EXPERT = 512
NORM_EPS = 1e-6
MASK_VALUE = -1e30

LANES = 128
MXU_DIM = 256
VMEM_LIMIT_BYTES = 56 * 1024 * 1024

F32 = jnp.float32
BF16 = jnp.bfloat16
HIGHEST = lax.Precision.HIGHEST

PROJ_TM = 2048
PROJ_TN = MXU_DIM
PROJ_CHUNK = 512
HY_COLS = 3 * HY_WIDTH
QKV_COLS = 3 * ATT_WIDTH
GATE_COLS2 = 2 * D_MODEL

FFT_R = 128
FFT_KH = FFT_R // 2 + 1
FFT_KP = 72
FFT_SLAB = 2 * FFT_KP
FFT_G_PITCH = FFT_SLAB + 8
FFT_T_PITCH = FFT_R + 8
CONV_CB = 128
FFT_UNROLL_TIME = 8
FFT_UNROLL_FREQ = 5

PAIRS_PER_GROUP = 6
N_BUCKETS = N_GROUPS * PAIRS_PER_GROUP
BUCKET_ROWS = 32
ROUTER_TM = 512
EXPERT_TM = 256
ROW_BLOCK = 1024
GATE_COLS = LANES


def _params(*sem):
    return pltpu.CompilerParams(dimension_semantics=sem, vmem_limit_bytes=VMEM_LIMIT_BYTES)


def _sigmoid(x):
    return 1.0 / (1.0 + jnp.exp(-x))


def _store_time_padded(o_ref, val):
    groups = val.shape[0] // FFT_R
    o_ref[:, :FFT_R, :] = val.reshape(groups, FFT_R, val.shape[1])
    o_ref[:, FFT_R:, :] = jnp.zeros((groups, FFT_T_PITCH - FFT_R, val.shape[1]), o_ref.dtype)


def _time_padded_shape(rows, cols):
    return jax.ShapeDtypeStruct((rows // FFT_R, FFT_T_PITCH, cols), F32)


def _ada_kernel(c_ref, w_ref, b_ref, o_ref):
    c = c_ref[...]
    c_act = c * _sigmoid(c)
    o_ref[0] = jnp.dot(c_act, w_ref[0], precision=HIGHEST, preferred_element_type=F32) + b_ref[0]


def _ada(c_pad, w_ada, b_ada):
    depth, D, N = w_ada.shape
    rows = c_pad.shape[0]
    tn = N // 4
    return pl.pallas_call(
        _ada_kernel,
        out_shape=jax.ShapeDtypeStruct((depth, rows, N), F32),
        grid=(depth, N // tn),
        in_specs=[pl.BlockSpec((rows, D), lambda l, j: (0, 0)),
                  pl.BlockSpec((1, D, tn), lambda l, j: (l, 0, j)),
                  pl.BlockSpec((1, 1, tn), lambda l, j: (l, 0, j))],
        out_specs=pl.BlockSpec((1, rows, tn), lambda l, j: (l, 0, j)),
        compiler_params=_params("arbitrary", "arbitrary"),
        name="ada_mod",
    )(c_pad, w_ada, b_ada.reshape(depth, 1, N))


def _norm_mod_kernel(x_ref, g_ref, sc_ref, sh_ref, o_ref):
    x = x_ref[...]
    r = lax.rsqrt(jnp.mean(x * x, axis=-1, keepdims=True) + NORM_EPS)
    h = (x * r) * g_ref[...]
    o_ref[...] = (h * (1.0 + sc_ref[0]) + sh_ref[0]).astype(o_ref.dtype)


def _norm_mod(x, g, sc, sh, seq_len):
    T, D = x.shape
    tm = 1024
    per_batch = seq_len // tm
    return pl.pallas_call(
        _norm_mod_kernel,
        out_shape=jax.ShapeDtypeStruct((T, D), BF16),
        grid=(T // tm,),
        in_specs=[pl.BlockSpec((tm, D), lambda i: (i, 0)),
                  pl.BlockSpec((1, D), lambda i: (0, 0)),
                  pl.BlockSpec((1, 1, D), lambda i: (i // per_batch, 0, 0)),
                  pl.BlockSpec((1, 1, D), lambda i: (i // per_batch, 0, 0))],
        out_specs=pl.BlockSpec((tm, D), lambda i: (i, 0)),
        compiler_params=_params("arbitrary"),
        name="norm_mod",
    )(x, g, sc, sh)


def _proj_kernel(h_ref, w_ref, o_ref):
    o_ref[...] = jnp.dot(h_ref[...], w_ref[0], preferred_element_type=F32).astype(o_ref.dtype)


def _proj(h, w, layer, col0, n_cols, tn):
    T, D = h.shape
    tm = PROJ_TM
    assert col0 % tn == 0 and n_cols % tn == 0
    off = col0 // tn
    return pl.pallas_call(
        _proj_kernel,
        out_shape=jax.ShapeDtypeStruct((T, n_cols), BF16),
        grid=(T // tm, n_cols // tn),
        in_specs=[pl.BlockSpec((tm, D), lambda i, j: (i, 0)),
                  pl.BlockSpec((1, D, tn), lambda i, j: (layer, 0, off + j))],
        out_specs=pl.BlockSpec((tm, tn), lambda i, j: (i, j)),
        compiler_params=_params("arbitrary", "arbitrary"),
        name="proj",
    )(h, w)


def _proj_qkv_kernel(h_ref, wq_ref, wk_ref, wv_ref, cos_ref, sin_ref, rot_ref, o_ref):
    tm, tn = h_ref.shape[0], PROJ_TN
    rot = rot_ref[...]
    for c in range(tm // PROJ_CHUNK):
        rows = slice(c * PROJ_CHUNK, (c + 1) * PROJ_CHUNK)
        hc = h_ref[rows, :]
        cos = jnp.concatenate([cos_ref[rows, :], cos_ref[rows, :]], axis=1)
        sin = jnp.concatenate([sin_ref[rows, :], sin_ref[rows, :]], axis=1)

        def roped(w_ref):
            acc = jnp.dot(hc, w_ref[0], preferred_element_type=F32)
            swapped = jnp.dot(acc.astype(BF16), rot, preferred_element_type=F32)
            return acc * cos + swapped * sin

        o_ref[rows, 0:tn] = (roped(wq_ref) * HEAD_DIM ** -0.5).astype(o_ref.dtype)
        o_ref[rows, tn:2 * tn] = roped(wk_ref).astype(o_ref.dtype)
        o_ref[rows, 2 * tn:3 * tn] = jnp.dot(hc, wv_ref[0], preferred_element_type=F32).astype(o_ref.dtype)


def _proj_qkv(h, w, layer, group, rope, seq_len, out_dtype):
    T, D = h.shape
    tm, tn = PROJ_TM, PROJ_TN
    n_groups = len(ATT_GROUPS)
    off = HY_COLS // tn + group
    cos_t, sin_t, rot = rope
    per_batch = seq_len // tm
    tab_spec = pl.BlockSpec((tm, LANES), lambda i: (i % per_batch, 0))
    w_spec = lambda kind: pl.BlockSpec((1, D, tn), lambda i: (layer, 0, off + n_groups * kind))
    return pl.pallas_call(
        _proj_qkv_kernel,
        out_shape=jax.ShapeDtypeStruct((T, 3 * tn), out_dtype),
        grid=(T // tm,),
        in_specs=[pl.BlockSpec((tm, D), lambda i: (i, 0)), w_spec(0), w_spec(1), w_spec(2),
                  tab_spec, tab_spec, pl.BlockSpec((tn, tn), lambda i: (0, 0))],
        out_specs=pl.BlockSpec((tm, 3 * tn), lambda i: (i, 0)),
        compiler_params=_params("arbitrary"),
        name="proj_qkv",
    )(h, w, w, w, cos_t, sin_t, rot)


def _rope_tables(seq_len):
    half = HEAD_DIM // 2
    inv = ROPE_THETA ** (-jnp.arange(half, dtype=F32) / half)
    ang = jnp.arange(seq_len, dtype=F32)[:, None] * inv[None, :]
    reps = LANES // half
    cos_t = jnp.tile(jnp.cos(ang), (1, reps))
    sin_t = jnp.tile(jnp.sin(ang), (1, reps))
    rot = np.zeros((PROJ_TN, PROJ_TN), np.float32)
    for j in range(PROJ_TN):
        if j % HEAD_DIM < half:
            rot[j + half, j] = -1.0
        else:
            rot[j - half, j] = 1.0
    return cos_t, sin_t, jnp.asarray(rot, BF16)


def _hyena_filter_kernel(feat_ref, w1_ref, b1_ref, w2_ref, b2_ref, w3_ref, b3_ref, w4_ref, fr_ref,
                         dl_ref, o_ref):
    i = pl.program_id(0)
    tm = feat_ref.shape[0]
    z = feat_ref[...]
    fr = fr_ref[...]
    h = jnp.sin(fr * (jnp.dot(z, w1_ref[...], precision=HIGHEST, preferred_element_type=F32) + b1_ref[...]))
    h = jnp.sin(fr * (jnp.dot(h, w2_ref[...], precision=HIGHEST, preferred_element_type=F32) + b2_ref[...]))
    h = jnp.sin(fr * (jnp.dot(h, w3_ref[...], precision=HIGHEST, preferred_element_type=F32) + b3_ref[...]))
    k = jnp.dot(h, w4_ref[...], precision=HIGHEST, preferred_element_type=F32)
    decay = jnp.exp(-z[:, 0:1] * jnp.abs(dl_ref[...]))
    pos = i * tm + lax.broadcasted_iota(jnp.int32, (tm, 1), 0)
    is_bwd = lax.broadcasted_iota(jnp.int32, (1, k.shape[1]), 1) >= HY_WIDTH
    _store_time_padded(o_ref, jnp.where((pos == 0) & is_bwd, 0.0, k * decay))


def _hyena_filter(feat, w1, b1, w2, b2, w3, b3, w4, freq, deltas2):
    n_rows = feat.shape[0]
    C2 = 2 * HY_WIDTH
    tm = 1024
    full = lambda i: (0, 0)
    return pl.pallas_call(
        _hyena_filter_kernel,
        out_shape=_time_padded_shape(n_rows, C2),
        grid=(n_rows // tm,),
        in_specs=[pl.BlockSpec((tm, LANES), lambda i: (i, 0)),
                  pl.BlockSpec((LANES, LANES), full), pl.BlockSpec((1, LANES), full),
                  pl.BlockSpec((LANES, LANES), full), pl.BlockSpec((1, LANES), full),
                  pl.BlockSpec((LANES, LANES), full), pl.BlockSpec((1, LANES), full),
                  pl.BlockSpec((LANES, C2), full),
                  pl.BlockSpec((1, LANES), full),
                  pl.BlockSpec((1, C2), full)],
        out_specs=pl.BlockSpec((tm // FFT_R, FFT_T_PITCH, C2), lambda i: (i, 0, 0)),
        compiler_params=_params("arbitrary"),
        name="hyena_filter",
    )(feat, w1, b1, w2, b2, w3, b3, w4, freq, deltas2)


def _hyena_features(seq_len):
    L = seq_len
    t = jnp.linspace(0.0, 1.0, L, dtype=F32)[:, None]
    bands = (HY_EMB - 1) // 2
    w = 2.0 * math.pi * jnp.arange(L, dtype=F32)[:, None] / L
    f = jnp.linspace(1e-4, bands - 1, bands, dtype=F32)[None, :]
    z = jnp.concatenate([t, jnp.cos(f * w), -jnp.sin(f * w)], axis=-1)
    z = jnp.pad(z, ((0, 0), (0, LANES - HY_EMB)))
    max_decay = math.log(HY_TARGET) / HY_FAST_DECAY_PCT
    min_decay = math.log(HY_TARGET) / HY_SLOW_DECAY_PCT
    deltas = jnp.linspace(min_decay, max_decay, HY_WIDTH, dtype=F32)[None, :]
    return z, jnp.concatenate([deltas, deltas], axis=1)


def _pad2(a, rows, cols):
    return jnp.pad(a, ((0, rows - a.shape[0]), (0, cols - a.shape[1])))


def _dft_matrices():
    R, KH, KP = FFT_R, FFT_KH, FFT_KP
    N = R * R
    n1 = np.arange(R)[:, None, None]
    k2 = np.arange(KP)[None, :, None]
    n2 = np.arange(R)[None, None, :]
    phase = 2.0 * np.pi * ((n2 * k2 % R) / R + (n1 * k2) / N)
    live = (k2 < KH)
    a1 = np.concatenate([np.cos(phase) * live, -np.sin(phase) * live], axis=1)
    wgt = np.where((k2 == 0) | (k2 == R // 2), 1.0, 2.0) * live / N
    b1 = np.concatenate([np.cos(phase) * wgt, -np.sin(phase) * wgt], axis=1)
    b1 = np.transpose(b1, (0, 2, 1))[:, :R // 2, :]
    th = 2.0 * np.pi * (np.arange(R)[:, None] * np.arange(R)[None, :] % R) / R
    c, s = np.cos(th), np.sin(th)
    w2f = np.block([[c, s], [-s, c]])
    w2i = np.block([[c, -s], [s, c]])
    as_bf = lambda a: jnp.asarray(a.astype(np.float32)).astype(BF16)
    return dict(a1=as_bf(a1[:, :, :R // 2]), b1=as_bf(b1), w2f=as_bf(w2f), w2i=as_bf(w2i))


def _fft_stage1(src_ref, a1_ref, g_ref, n_rows):
    def body(n1, carry):
        xs = src_ref[pl.ds(n1, n_rows, stride=FFT_T_PITCH), :].astype(BF16)
        slab = jnp.dot(a1_ref[n1], xs, preferred_element_type=F32)
        g_ref[pl.ds(pl.multiple_of(n1 * FFT_G_PITCH, 8), FFT_SLAB), :] = slab
        return carry
    lax.fori_loop(0, FFT_R, body, 0, unroll=FFT_UNROLL_TIME)


def _load_freq_rows(g_ref, k2):
    re = g_ref[pl.ds(k2, FFT_R, stride=FFT_G_PITCH), :]
    im = g_ref[pl.ds(FFT_KP + k2, FFT_R, stride=FFT_G_PITCH), :]
    return jnp.concatenate([re, im], axis=0)


def _filter_fft_kernel(fwd_ref, bwd_ref, a1_ref, w2_ref, o_ref, gf_ref, gb_ref):
    n2_rows = fwd_ref.shape[0] // FFT_T_PITCH
    cb = fwd_ref.shape[1]

    def stage1(n1, carry):
        rows = pl.ds(n1, n2_rows, stride=FFT_T_PITCH)
        xs = jnp.concatenate([fwd_ref[rows, :], bwd_ref[rows, :]], axis=1).astype(BF16)
        slab = jnp.dot(a1_ref[n1], xs, preferred_element_type=F32)
        dst = pl.ds(pl.multiple_of(n1 * FFT_G_PITCH, 8), FFT_SLAB)
        gf_ref[dst, :] = slab[:, :cb]
        gb_ref[dst, :] = slab[:, cb:]
        return carry
    lax.fori_loop(0, FFT_R, stage1, 0, unroll=FFT_UNROLL_TIME)

    def body(k2, carry):
        gk = jnp.concatenate([_load_freq_rows(gf_ref, k2), _load_freq_rows(gb_ref, k2)], axis=1).astype(BF16)
        x = jnp.dot(w2_ref[...], gk, preferred_element_type=F32)
        f, b = x[:, :cb], x[:, cb:]
        spec = jnp.concatenate([f[:FFT_R] + b[:FFT_R], f[FFT_R:] - b[FFT_R:]], axis=0)
        o_ref[k2] = spec.astype(o_ref.dtype)
        return carry
    lax.fori_loop(0, FFT_KH, body, 0, unroll=FFT_UNROLL_FREQ)


def _filter_fft(k2, mats):
    n_rows = k2.shape[0]
    C = HY_WIDTH
    cb = CONV_CB
    ncb = C // cb
    return pl.pallas_call(
        _filter_fft_kernel,
        out_shape=jax.ShapeDtypeStruct((FFT_KH, 2 * FFT_R, C), BF16),
        grid=(ncb,),
        in_specs=[pl.BlockSpec((n_rows, cb), lambda j: (0, j)),
                  pl.BlockSpec((n_rows, cb), lambda j: (0, ncb + j)),
                  pl.BlockSpec((FFT_R, FFT_SLAB, FFT_R // 2), lambda j: (0, 0, 0)),
                  pl.BlockSpec((2 * FFT_R, 2 * FFT_R), lambda j: (0, 0))],
        out_specs=pl.BlockSpec((FFT_KH, 2 * FFT_R, cb), lambda j: (0, 0, j)),
        scratch_shapes=[pltpu.VMEM((FFT_R * FFT_G_PITCH, cb), F32), pltpu.VMEM((FFT_R * FFT_G_PITCH, cb), F32)],
        compiler_params=_params("arbitrary"),
        name="hyena_filter_fft",
    )(k2, k2, mats["a1"], mats["w2f"])


SC_HALO = 16


def _long_conv_kernel(x0_ref, x1_ref, v_ref, w0_ref, w1_ref, wv_ref, b0_ref, b1s_ref, bv_ref, kf_ref, bias_ref,
                      a1_ref, b1_ref, w2f_ref, w2i_ref, o_ref, g_ref, zv_ref, y_ref, stage_ref):
    seq_len = x0_ref.shape[1]
    n_groups = seq_len // FFT_R

    def short_conv(part, u_ref, w_ref, b_ref, g):
        row0 = pl.multiple_of(g * FFT_R, FFT_R)
        lo = pl.multiple_of(jnp.maximum(row0 - SC_HALO, 0), SC_HALO)
        hi = pl.multiple_of(jnp.minimum(row0 + FFT_R, seq_len - SC_HALO), SC_HALO)
        st = stage_ref.at[part]
        st[0:SC_HALO, :] = jnp.where(g > 0, u_ref[0, pl.ds(lo, SC_HALO), :].astype(F32), 0.0)
        st[SC_HALO:SC_HALO + FFT_R, :] = u_ref[0, pl.ds(row0, FFT_R), :].astype(F32)
        st[SC_HALO + FFT_R:, :] = jnp.where(g < n_groups - 1, u_ref[0, pl.ds(hi, SC_HALO), :].astype(F32), 0.0)
        w = w_ref[...]
        return (st[SC_HALO - 1:SC_HALO - 1 + FFT_R, :] * w[0:1] + st[SC_HALO:SC_HALO + FFT_R, :] * w[1:2]
                + st[SC_HALO + 1:SC_HALO + 1 + FFT_R, :] * w[2:3] + b_ref[...])

    def gate_body(g, carry):
        zv = short_conv(2, v_ref, wv_ref, bv_ref, g) * short_conv(1, x1_ref, w1_ref, b1s_ref, g)
        zv_ref[pl.ds(pl.multiple_of(g * FFT_T_PITCH, 8), FFT_R), :] = zv
        return carry
    lax.fori_loop(0, n_groups, gate_body, 0, unroll=2)

    _fft_stage1(zv_ref, a1_ref, g_ref, n_groups)

    def freq_body(k2, carry):
        gk = _load_freq_rows(g_ref, k2).astype(BF16)
        x = jnp.dot(w2f_ref[...], gk, preferred_element_type=F32)
        kf = kf_ref[k2].astype(F32)
        xr, xi = x[:FFT_R], x[FFT_R:]
        kr, ki = kf[:FFT_R], kf[FFT_R:]
        p = jnp.concatenate([xr * kr - xi * ki, xr * ki + xi * kr], axis=0).astype(BF16)
        hk = jnp.dot(w2i_ref[...], p, preferred_element_type=F32)
        g_ref[pl.ds(k2, FFT_R, stride=FFT_G_PITCH), :] = hk[:FFT_R]
        g_ref[pl.ds(FFT_KP + k2, FFT_R, stride=FFT_G_PITCH), :] = hk[FFT_R:]
        return carry
    lax.fori_loop(0, FFT_KH, freq_body, 0, unroll=FFT_UNROLL_FREQ)

    def time_body(n1, carry):
        slab = g_ref[pl.ds(pl.multiple_of(n1 * FFT_G_PITCH, 8), FFT_SLAB), :].astype(BF16)
        y_ref[pl.ds(n1, n_groups, stride=FFT_T_PITCH), :] = jnp.dot(b1_ref[n1], slab, preferred_element_type=F32)
        return carry
    lax.fori_loop(0, FFT_R, time_body, 0, unroll=FFT_UNROLL_TIME)

    bias = bias_ref[...]

    def out_body(g, carry):
        rows = pl.ds(pl.multiple_of(g * FFT_T_PITCH, 8), FFT_R)
        x0 = short_conv(0, x0_ref, w0_ref, b0_ref, g)
        y = (y_ref[rows, :] + zv_ref[rows, :] * bias) * x0
        o_ref[0, pl.ds(pl.multiple_of(g * FFT_R, FFT_R), FFT_R), :] = y.astype(o_ref.dtype)
        return carry
    lax.fori_loop(0, n_groups, out_body, 0, unroll=2)


def _long_conv(u, w_sc, b_sc, kf, hy_bias, mats):
    B, L, _ = u.shape
    C = HY_WIDTH
    cb = CONV_CB
    ncb = C // cb
    lp = L // FFT_R * FFT_T_PITCH
    part = lambda p: pl.BlockSpec((1, L, cb), lambda j, b: (b, 0, p * ncb + j))
    wpart = lambda p: pl.BlockSpec((3, cb), lambda j, b: (0, p * ncb + j))
    bpart = lambda p: pl.BlockSpec((1, cb), lambda j, b: (0, p * ncb + j))
    b2 = b_sc.reshape(1, 3 * C)
    return pl.pallas_call(
        _long_conv_kernel,
        out_shape=jax.ShapeDtypeStruct((B, L, C), BF16),
        grid=(ncb, B),
        in_specs=[part(0), part(1), part(2), wpart(0), wpart(1), wpart(2), bpart(0), bpart(1), bpart(2),
                  pl.BlockSpec((FFT_KH, 2 * FFT_R, cb), lambda j, b: (0, 0, j)),
                  pl.BlockSpec((1, cb), lambda j, b: (0, j)),
                  pl.BlockSpec((FFT_R, FFT_SLAB, FFT_R // 2), lambda j, b: (0, 0, 0)),
                  pl.BlockSpec((FFT_R, FFT_R // 2, FFT_SLAB), lambda j, b: (0, 0, 0)),
                  pl.BlockSpec((2 * FFT_R, 2 * FFT_R), lambda j, b: (0, 0)),
                  pl.BlockSpec((2 * FFT_R, 2 * FFT_R), lambda j, b: (0, 0))],
        out_specs=pl.BlockSpec((1, L, cb), lambda j, b: (b, 0, j)),
        scratch_shapes=[pltpu.VMEM((FFT_R * FFT_G_PITCH, cb), F32), pltpu.VMEM((lp, cb), F32),
                        pltpu.VMEM((lp, cb), F32), pltpu.VMEM((3, FFT_R + 2 * SC_HALO, cb), F32)],
        compiler_params=_params("arbitrary", "arbitrary"),
        name="hyena_long_conv",
    )(u, u, u, w_sc, w_sc, w_sc, b2, b2, b2, kf, hy_bias.reshape(1, C),
      mats["a1"], mats["b1"], mats["w2f"], mats["w2i"])


ATT_TQ = 128
ATT_RADIUS = 64
ATT_WINDOW = ATT_TQ + 2 * ATT_RADIUS


ATT_SUB = 4


def _attn_kernel(q_ref, k_ref, v_ref, o_ref, lse_ref):
    i = pl.program_id(2)
    ls = k_ref.shape[1]
    nh = HEADS_PER_GROUP
    head_of_col = lax.broadcasted_iota(jnp.int32, (1, ATT_OUT), 1) // HEAD_DIM
    row_iota = lax.broadcasted_iota(jnp.int32, (nh * ATT_TQ, ATT_WINDOW), 0) % ATT_TQ
    col_iota = lax.broadcasted_iota(jnp.int32, (nh * ATT_TQ, ATT_WINDOW), 1)
    rel = row_iota - col_iota
    for sub in range(ATT_SUB):
        q0 = (i * ATT_SUB + sub) * ATT_TQ
        rows = slice(sub * ATT_TQ, (sub + 1) * ATT_TQ)
        q = q_ref[0, rows, :]
        start = jnp.clip(q0 - ATT_RADIUS, 0, ls - ATT_WINDOW)
        start = pl.multiple_of(start, ATT_RADIUS)
        kw = k_ref[0, pl.ds(start, ATT_WINDOW), :]
        vw = v_ref[0, pl.ds(start, ATT_WINDOW), :]
        out, lse = _attn_unit(q, kw, vw, rel, q0 - start, head_of_col)
        o_ref[0, rows, :] = out.astype(o_ref.dtype)
        lse_ref[0, rows, :] = lse


def _attn_unit(q, kw, vw, rel, q_minus_start, head_of_col):
    nh = HEADS_PER_GROUP
    band = jnp.abs(q_minus_start + rel) <= ATT_RADIUS
    zero = jnp.zeros_like(q)
    q4 = jnp.concatenate([jnp.where(head_of_col == h, q, zero) for h in range(nh)], axis=0)
    s = lax.dot_general(q4, kw, (((1,), (1,)), ((), ())), preferred_element_type=F32)
    s = jnp.where(band, s, MASK_VALUE)
    m = jnp.max(s, axis=-1, keepdims=True)
    p = jnp.exp(s - m)
    den = jnp.sum(p, axis=-1, keepdims=True)
    pv = jnp.dot(p.astype(BF16), vw, preferred_element_type=F32) / den
    lse4 = m + jnp.log(den)
    out = jnp.zeros((ATT_TQ, ATT_OUT), F32)
    lse = jnp.zeros((ATT_TQ, ATT_OUT), F32)
    for h in range(nh):
        mine = head_of_col == h
        hrows = slice(h * ATT_TQ, (h + 1) * ATT_TQ)
        out = jnp.where(mine, pv[hrows], out)
        lse = jnp.where(mine, lse4[hrows], lse)
    return out, lse


ATT_CHUNK = 2048


def _attn_strided_kernel(q0_ref, q1_ref, k0_ref, k1_ref, v0_ref, v1_ref, o_ref, lse_ref, o_scr, lse_scr, *, dil):
    i = pl.program_id(1)
    ls = q0_ref.shape[1] // dil

    def gather(lo_ref, hi_ref, first, n):
        rows = pl.ds(first, n, stride=dil)
        return jnp.concatenate([lo_ref[0, rows, :], hi_ref[0, rows, :]], axis=1).astype(BF16)

    units = ATT_CHUNK // ATT_TQ
    shift = dil.bit_length() - 1
    head_of_col = lax.broadcasted_iota(jnp.int32, (1, ATT_OUT), 1) // HEAD_DIM
    row_iota = lax.broadcasted_iota(jnp.int32, (HEADS_PER_GROUP * ATT_TQ, ATT_WINDOW), 0) % ATT_TQ
    rel = row_iota - lax.broadcasted_iota(jnp.int32, (HEADS_PER_GROUP * ATT_TQ, ATT_WINDOW), 1)

    def unit(u, carry):
        r = u & (dil - 1)
        sb = u >> shift
        m0 = i * (ATT_CHUNK // dil) + sb * ATT_TQ
        start = jnp.clip(m0 - ATT_RADIUS, 0, ls - ATT_WINDOW)
        q = gather(q0_ref, q1_ref, m0 * dil + r, ATT_TQ)
        kw = gather(k0_ref, k1_ref, start * dil + r, ATT_WINDOW)
        vw = gather(v0_ref, v1_ref, start * dil + r, ATT_WINDOW)
        out, lse = _attn_unit(q, kw, vw, rel, m0 - start, head_of_col)
        dst = pl.ds(sb * ATT_TQ * dil + r, ATT_TQ, stride=dil)
        for half in range(2):
            lanes = slice(half * LANES, (half + 1) * LANES)
            o_scr[half, dst, :] = out[:, lanes]
            lse_scr[half, dst, :] = lse[:, lanes]
        return carry
    lax.fori_loop(0, units, unit, 0, unroll=2)
    for half in range(2):
        lanes = slice(half * LANES, (half + 1) * LANES)
        o_ref[0, :, lanes] = o_scr[half]
        lse_ref[0, :, lanes] = lse_scr[half]


def _attention_strided(qkv_g, dil, batch, seq_len):
    view = qkv_g.reshape(batch, seq_len, 3 * ATT_OUT)
    halves = [pl.BlockSpec((1, seq_len, LANES), lambda b, i, c=c: (b, 0, c), pipeline_mode=pl.Buffered(1))
              for c in range(3 * ATT_OUT // LANES)]
    o_spec = pl.BlockSpec((1, ATT_CHUNK, ATT_OUT), lambda b, i: (b, i, 0))
    kern = lambda *refs: _attn_strided_kernel(*refs, dil=dil)
    o, lse = pl.pallas_call(
        kern,
        out_shape=(jax.ShapeDtypeStruct((batch, seq_len, ATT_OUT), F32),
                   jax.ShapeDtypeStruct((batch, seq_len, ATT_OUT), F32)),
        grid=(batch, seq_len // ATT_CHUNK),
        in_specs=halves,
        out_specs=(o_spec, o_spec),
        scratch_shapes=[pltpu.VMEM((2, ATT_CHUNK, LANES), F32), pltpu.VMEM((2, ATT_CHUNK, LANES), F32)],
        compiler_params=_params("arbitrary", "arbitrary"),
        name=f"dilated_attn_d{dil}",
    )(*([view] * len(halves)))
    T = batch * seq_len
    return o.reshape(T, ATT_OUT), lse.reshape(T, ATT_OUT)


def _attention_group(qkv_g, dil, batch, seq_len):
    ls = seq_len // dil
    tq = ATT_SUB * ATT_TQ
    view = qkv_g.reshape(batch, ls, dil * 3 * ATT_OUT)

    def col(which):
        return lambda b, r, i: (b, 0, r * 3 + which)

    q_map = lambda b, r, i: (b, i, r * 3)
    o_map = lambda b, r, i: (b, i, r)
    o, lse = pl.pallas_call(
        _attn_kernel,
        out_shape=(jax.ShapeDtypeStruct((batch, ls, dil * ATT_OUT), BF16),
                   jax.ShapeDtypeStruct((batch, ls, dil * ATT_OUT), F32)),
        grid=(batch, dil, ls // tq),
        in_specs=[pl.BlockSpec((1, tq, ATT_OUT), q_map),
                  pl.BlockSpec((1, ls, ATT_OUT), col(1)),
                  pl.BlockSpec((1, ls, ATT_OUT), col(2))],
        out_specs=(pl.BlockSpec((1, tq, ATT_OUT), o_map), pl.BlockSpec((1, tq, ATT_OUT), o_map)),
        compiler_params=_params("arbitrary", "arbitrary", "arbitrary"),
        name=f"dilated_attn_d{dil}",
    )(view, view, view)
    T = batch * seq_len
    return o.reshape(T, ATT_OUT), lse.reshape(T, ATT_OUT)


def _merge_kernel(x_ref, yhy_ref, o1_ref, o2_ref, o3_ref, l1_ref, l2_ref, l3_ref, gh_ref, ga_ref,
                  wh_ref, wa_ref, wo_ref, g1_ref, out_ref):
    l1, l2, l3 = l1_ref[...], l2_ref[...], l3_ref[...]
    m = jnp.maximum(jnp.maximum(l1, l2), l3)
    e1, e2, e3 = jnp.exp(l1 - m), jnp.exp(l2 - m), jnp.exp(l3 - m)
    tot = e1 + e2 + e3
    y_at = (e1 * o1_ref[...].astype(F32) + e2 * o2_ref[...].astype(F32) + e3 * o3_ref[...].astype(F32)) / tot
    a = jnp.dot(yhy_ref[...], wh_ref[0], preferred_element_type=F32)
    b = jnp.dot(y_at.astype(BF16), wa_ref[0], preferred_element_type=F32)
    merged = _sigmoid(gh_ref[...].astype(F32)) * a + _sigmoid(ga_ref[...].astype(F32)) * b
    upd = jnp.dot(merged.astype(BF16), wo_ref[0], preferred_element_type=F32)
    out_ref[...] = x_ref[...] + g1_ref[0] * upd


def _merge(x, y_hy, attn, gates, w_br_h, w_br_a, w_out, layer, g1, seq_len):
    T, D = x.shape
    tm = 512
    per_batch = seq_len // tm
    (o1, l1), (o2, l2), (o3, l3) = attn
    row = lambda w: pl.BlockSpec((tm, w), lambda i: (i, 0))
    full = lambda a: pl.BlockSpec((1,) + a.shape[1:], lambda i: (layer, 0, 0))
    return pl.pallas_call(
        _merge_kernel,
        out_shape=jax.ShapeDtypeStruct((T, D), F32),
        grid=(T // tm,),
        in_specs=[row(D), row(HY_WIDTH), row(ATT_OUT), row(ATT_OUT), row(ATT_OUT),
                  row(ATT_OUT), row(ATT_OUT), row(ATT_OUT),
                  pl.BlockSpec((tm, D), lambda i: (i, 0)), pl.BlockSpec((tm, D), lambda i: (i, 1)),
                  full(w_br_h), full(w_br_a), full(w_out),
                  pl.BlockSpec((1, 1, D), lambda i: (i // per_batch, 0, 0))],
        out_specs=row(D),
        compiler_params=_params("arbitrary"),
        name="mixer_merge",
    )(x, y_hy, o1, o2, o3, l1, l2, l3, gates, gates, w_br_h, w_br_a, w_out, g1)


def _router_kernel(x_ref, g_ref, sc_ref, sh_ref, rwt_ref, rb_ref, hg_ref, stats_ref, cnt_ref, base_ref):
    i = pl.program_id(0)
    tm = x_ref.shape[0]

    @pl.when(i == 0)
    def _():
        base_ref[...] = jnp.zeros_like(base_ref)

    x = x_ref[...]
    r = lax.rsqrt(jnp.mean(x * x, axis=-1, keepdims=True) + NORM_EPS)
    h = (x * r) * g_ref[...]
    h = h * (1.0 + sc_ref[0]) + sh_ref[0]

    logits = lax.dot_general(rwt_ref[...], h, (((1,), (1,)), ((), ())),
                             precision=HIGHEST, preferred_element_type=F32)
    scores = _sigmoid(logits)
    biased = scores + rb_ref[...]

    def row(a, k):
        return a[k:k + 1, :]

    sel = jnp.zeros((1, tm), jnp.int32)
    best = None
    for g in range(N_GROUPS):
        a, b, c, d = (row(biased, 4 * g + k) for k in range(4))
        m_ab, n_ab = jnp.maximum(a, b), jnp.minimum(a, b)
        m_cd, n_cd = jnp.maximum(c, d), jnp.minimum(c, d)
        gs = jnp.maximum(m_ab, m_cd) + jnp.maximum(jnp.minimum(m_ab, m_cd), jnp.maximum(n_ab, n_cd))
        if g == 0:
            best = gs
        else:
            better = gs > best
            sel = jnp.where(better, g, sel)
            best = jnp.where(better, gs, best)

    v, u = [], []
    for k in range(EXPERTS_PER_GROUP):
        vk = jnp.zeros((1, tm), F32)
        uk = jnp.zeros((1, tm), F32)
        for g in range(N_GROUPS):
            vk = jnp.where(sel == g, row(biased, 4 * g + k), vk)
            uk = jnp.where(sel == g, row(scores, 4 * g + k), uk)
        v.append(vk)
        u.append(uk)

    i1 = jnp.zeros((1, tm), jnp.int32)
    b1 = v[0]
    for k in range(1, EXPERTS_PER_GROUP):
        gt = v[k] > b1
        i1 = jnp.where(gt, k, i1)
        b1 = jnp.where(gt, v[k], b1)
    i2 = jnp.zeros((1, tm), jnp.int32)
    b2 = jnp.full((1, tm), -jnp.inf, F32)
    for k in range(EXPERTS_PER_GROUP):
        cand = (i1 != k) & (v[k] > b2)
        i2 = jnp.where(cand, k, i2)
        b2 = jnp.where(cand, v[k], b2)

    lo = jnp.minimum(i1, i2)
    hi = jnp.maximum(i1, i2)
    pair = jnp.where(lo == 0, hi - 1, jnp.where(lo == 1, hi + 1, 5))
    bucket = sel * PAIRS_PER_GROUP + pair

    u_lo = jnp.zeros((1, tm), F32)
    u_hi = jnp.zeros((1, tm), F32)
    for k in range(EXPERTS_PER_GROUP):
        u_lo = jnp.where(lo == k, u[k], u_lo)
        u_hi = jnp.where(hi == k, u[k], u_hi)
    tot = u_lo + u_hi
    w_lo = u_lo / tot
    w_hi = u_hi / tot

    rows = lax.broadcasted_iota(jnp.int32, (BUCKET_ROWS, tm), 0)
    onehot = (rows == bucket).astype(F32)
    t_src = lax.broadcasted_iota(jnp.int32, (tm, tm), 0)
    t_dst = lax.broadcasted_iota(jnp.int32, (tm, tm), 1)
    before = (t_src < t_dst).astype(BF16)
    cum = jnp.dot(onehot.astype(BF16), before, preferred_element_type=F32)
    base = base_ref[...]
    rank = jnp.sum(onehot * (cum + base), axis=0, keepdims=True)
    base = base + jnp.sum(onehot, axis=1, keepdims=True)
    base_ref[...] = base
    cnt_ref[...] = jnp.broadcast_to(base, cnt_ref.shape)

    srow = lax.broadcasted_iota(jnp.int32, (8, tm), 0)
    stats_ref[...] = jnp.where(srow == 0, bucket.astype(F32), jnp.where(srow == 1, rank, 0.0))

    grow = lax.broadcasted_iota(jnp.int32, (GATE_COLS, tm), 0)
    gates_t = jnp.where(grow == 0, w_lo, jnp.where(grow == 1, w_hi, 0.0))
    hg_ref[:, :D_MODEL] = h
    hg_ref[:, D_MODEL:] = gates_t.T


def _router(x, norm_g, sc, sh, router_wt, router_b, seq_len):
    T, D = x.shape
    tm = ROUTER_TM
    per_batch = seq_len // tm
    return pl.pallas_call(
        _router_kernel,
        out_shape=(jax.ShapeDtypeStruct((T, D + GATE_COLS), F32),
                   jax.ShapeDtypeStruct((8, T), F32),
                   jax.ShapeDtypeStruct((BUCKET_ROWS, LANES), F32)),
        grid=(T // tm,),
        in_specs=[pl.BlockSpec((tm, D), lambda i: (i, 0)),
                  pl.BlockSpec((1, D), lambda i: (0, 0)),
                  pl.BlockSpec((1, 1, D), lambda i: (i // per_batch, 0, 0)),
                  pl.BlockSpec((1, 1, D), lambda i: (i // per_batch, 0, 0)),
                  pl.BlockSpec((N_EXPERTS, D), lambda i: (0, 0)),
                  pl.BlockSpec((N_EXPERTS, 1), lambda i: (0, 0))],
        out_specs=(pl.BlockSpec((tm, D + GATE_COLS), lambda i: (i, 0)),
                   pl.BlockSpec((8, tm), lambda i: (0, i)),
                   pl.BlockSpec((BUCKET_ROWS, LANES), lambda i: (0, 0))),
        scratch_shapes=[pltpu.VMEM((BUCKET_ROWS, 1), F32)],
        compiler_params=_params("arbitrary"),
        name="moe_router",
    )(x, norm_g, sc, sh, router_wt, router_b)


def _start_row_copies(n_rows, make_copy):
    group = 8

    def body(g, carry):
        base = pl.multiple_of(g * group, group)
        for k in range(group):
            make_copy(base + k).start(priority=k % 2)
        return carry

    lax.fori_loop(0, n_rows // group, body, 0)


def _dispatch_kernel(pos_ref, hg_ref, xs_init_hbm, xs_hbm, sem):
    del xs_init_hbm
    _start_row_copies(ROW_BLOCK, lambda r: pltpu.make_async_copy(
        hg_ref.at[pl.ds(r, 1), :], xs_hbm.at[pl.ds(pos_ref[r], 1), :], sem))
    pltpu.make_async_copy(hg_ref, xs_hbm.at[pl.ds(0, ROW_BLOCK), :], sem).wait()


def _dispatch(pos, hg, n_rows):
    T, W = hg.shape
    zeros = jnp.zeros((n_rows, W), F32)
    return pl.pallas_call(
        _dispatch_kernel,
        out_shape=jax.ShapeDtypeStruct((n_rows, W), F32),
        grid=(T // ROW_BLOCK,),
        in_specs=[pl.BlockSpec((ROW_BLOCK,), lambda i: (i,), memory_space=pltpu.SMEM),
                  pl.BlockSpec((ROW_BLOCK, W), lambda i: (i, 0)),
                  pl.BlockSpec(memory_space=pl.ANY)],
        out_specs=pl.BlockSpec(memory_space=pl.ANY),
        scratch_shapes=[pltpu.SemaphoreType.DMA],
        input_output_aliases={2: 0},
        compiler_params=_params("arbitrary"),
        name="moe_dispatch",
    )(pos, hg, zeros)


def _expert_kernel(e_lo_ref, e_hi_ref, n_used_ref, xs_ref, w1a, w3a, w2a, w1b, w3b, w2b, y_ref):
    del e_lo_ref, e_hi_ref
    used = pl.program_id(0) < n_used_ref[0]

    @pl.when(jnp.logical_not(used))
    def _():
        y_ref[...] = jnp.zeros_like(y_ref)

    @pl.when(used)
    def _():
        xb = xs_ref[:, :D_MODEL].astype(BF16)
        g_lo = xs_ref[:, D_MODEL:D_MODEL + 1]
        g_hi = xs_ref[:, D_MODEL + 1:D_MODEL + 2]

        def ffn(w1, w3, w2):
            a = jnp.dot(xb, w1[0, 0], preferred_element_type=F32)
            b = jnp.dot(xb, w3[0, 0], preferred_element_type=F32)
            act = (a * _sigmoid(a)) * b
            return jnp.dot(act.astype(BF16), w2[0, 0], preferred_element_type=F32)

        y_ref[...] = g_lo * ffn(w1a, w3a, w2a) + g_hi * ffn(w1b, w3b, w2b)


def _experts(tile_lo, tile_hi, n_used, xs, w1, w3, w2, layer):
    n_rows, W = xs.shape
    D, F = w1.shape[2], w1.shape[3]
    n_tiles = n_rows // EXPERT_TM

    def x_map(j, lo, hi, nu):
        return (jnp.minimum(j, nu[0] - 1), 0)

    def w_lo_map(j, lo, hi, nu):
        return (layer, lo[j], 0, 0)

    def w_hi_map(j, lo, hi, nu):
        return (layer, hi[j], 0, 0)

    grid_spec = pltpu.PrefetchScalarGridSpec(
        num_scalar_prefetch=3,
        grid=(n_tiles,),
        in_specs=[pl.BlockSpec((EXPERT_TM, W), x_map),
                  pl.BlockSpec((1, 1, D, F), w_lo_map), pl.BlockSpec((1, 1, D, F), w_lo_map),
                  pl.BlockSpec((1, 1, F, D), w_lo_map),
                  pl.BlockSpec((1, 1, D, F), w_hi_map), pl.BlockSpec((1, 1, D, F), w_hi_map),
                  pl.BlockSpec((1, 1, F, D), w_hi_map)],
        out_specs=pl.BlockSpec((EXPERT_TM, D), lambda j, lo, hi, nu: (j, 0)),
    )
    return pl.pallas_call(
        _expert_kernel,
        out_shape=jax.ShapeDtypeStruct((n_rows, D), F32),
        grid_spec=grid_spec,
        compiler_params=_params("arbitrary"),
        name="moe_experts",
    )(tile_lo, tile_hi, n_used, xs, w1, w3, w2, w1, w3, w2)


def _combine_kernel(pos_ref, x_ref, g2_ref, ng_ref, sc_ref, sh_ref, ys_hbm, *refs):
    *out_refs, buf, sem = refs
    _start_row_copies(ROW_BLOCK, lambda r: pltpu.make_async_copy(
        ys_hbm.at[pl.ds(pos_ref[r], 1), :], buf.at[pl.ds(r, 1), :], sem))
    pltpu.make_async_copy(ys_hbm.at[pl.ds(0, ROW_BLOCK), :], buf, sem).wait()
    xn = x_ref[...] + g2_ref[0] * buf[...]
    r = lax.rsqrt(jnp.mean(xn * xn, axis=-1, keepdims=True) + NORM_EPS)
    h = (xn * r) * ng_ref[...]
    if len(out_refs) == 2:
        out_refs[0][...] = xn
        h = h * (1.0 + sc_ref[0]) + sh_ref[0]
    out_refs[-1][...] = h.astype(out_refs[-1].dtype)


def _combine(pos, x, g2, ys, norm_g, sc, sh, seq_len, last):
    T, D = x.shape
    per_batch = seq_len // ROW_BLOCK
    row = pl.BlockSpec((ROW_BLOCK, D), lambda i: (i, 0))
    per_b = pl.BlockSpec((1, 1, D), lambda i: (i // per_batch, 0, 0))
    if last:
        out_shape, out_specs = jax.ShapeDtypeStruct((T, D), F32), row
    else:
        out_shape = (jax.ShapeDtypeStruct((T, D), F32), jax.ShapeDtypeStruct((T, D), BF16))
        out_specs = (row, row)
    return pl.pallas_call(
        _combine_kernel,
        out_shape=out_shape,
        grid=(T // ROW_BLOCK,),
        in_specs=[pl.BlockSpec((ROW_BLOCK,), lambda i: (i,), memory_space=pltpu.SMEM),
                  row, per_b, pl.BlockSpec((1, D), lambda i: (0, 0)), per_b, per_b,
                  pl.BlockSpec(memory_space=pl.ANY)],
        out_specs=out_specs,
        scratch_shapes=[pltpu.VMEM((ROW_BLOCK, D), F32), pltpu.SemaphoreType.DMA],
        compiler_params=_params("arbitrary"),
        name="moe_combine",
    )(pos, x, g2, norm_g, sc, sh, ys)


def _moe_layer(x, norm_g, sc2, sh2, g2, router_wt, router_b, w1, w3, w2, layer, seq_len, next_norm, last):
    T, D = x.shape
    hg, stats, counts = _router(x, norm_g, sc2, sh2, router_wt, router_b, seq_len)

    cnt = counts[:N_BUCKETS, 0].astype(jnp.int32)
    padded = ((cnt + EXPERT_TM - 1) // EXPERT_TM) * EXPERT_TM
    ends = jnp.cumsum(padded)
    starts = ends - padded
    bucket = stats[0].astype(jnp.int32)
    pos = starts[bucket] + stats[1].astype(jnp.int32)
    n_tiles = T // EXPERT_TM + N_BUCKETS
    tile_row0 = jnp.arange(n_tiles, dtype=jnp.int32) * EXPERT_TM
    tile_bucket = jnp.sum((ends[None, :] <= tile_row0[:, None]).astype(jnp.int32), axis=1)
    tile_bucket = jnp.minimum(tile_bucket, N_BUCKETS - 1)
    grp, pair = tile_bucket // PAIRS_PER_GROUP, tile_bucket % PAIRS_PER_GROUP
    pair_lo = jnp.array([0, 0, 0, 1, 1, 2], jnp.int32)[pair]
    pair_hi = jnp.array([1, 2, 3, 2, 3, 3], jnp.int32)[pair]
    tile_lo = grp * EXPERTS_PER_GROUP + pair_lo
    tile_hi = grp * EXPERTS_PER_GROUP + pair_hi
    n_used = (ends[-1] // EXPERT_TM).astype(jnp.int32).reshape(1)

    xs = _dispatch(pos, hg, n_tiles * EXPERT_TM)
    ys = _experts(tile_lo, tile_hi, n_used, xs, w1, w3, w2, layer)
    return _combine(pos, x, g2, ys, *next_norm, seq_len, last)


def kernel(x, c, norm1_g, norm2_g, w_ada, b_ada, w_in, w_sc, b_sc, hf_w1, hf_b1, hf_w2, hf_b2, hf_w3, hf_b3,
           hf_w4, hf_freq, hy_bias, w_br_h, w_br_a, w_out, router_w, router_bias, moe_w1, moe_w3, moe_w2, final_g):
    B, L, D = x.shape
    T = B * L
    C = HY_WIDTH
    xt = x.reshape(T, D)

    rope = _rope_tables(L)
    feat, deltas = _hyena_features(L)
    mats = _dft_matrices()

    c_pad = jnp.pad(c, ((0, 8 - B), (0, 0)))
    mod = _ada(c_pad, w_ada, b_ada)[:, :B]
    router_wt = router_w.T
    router_b = router_bias.reshape(N_EXPERTS, 1)
    w_in_b, w_br_h_b, w_br_a_b, w_out_b = (w.astype(BF16) for w in (w_in, w_br_h, w_br_a, w_out))
    moe_w1_b, moe_w3_b, moe_w2_b = (w.astype(BF16) for w in (moe_w1, moe_w3, moe_w2))

    mods = [[mod[i, :, k * D:(k + 1) * D].reshape(B, 1, D) for k in range(6)] for i in range(DEPTH)]
    h = _norm_mod(xt, norm1_g[0].reshape(1, D), mods[0][1], mods[0][0], L)
    for i in range(DEPTH):
        sh1, sc1, g1, sh2, sc2, g2 = mods[i]

        u = _proj(h, w_in_b, i, 0, HY_COLS, tn=HY_WIDTH)
        qkv = [_proj_qkv(h, w_in_b, i, g, rope, L, BF16 if dil == 1 else F32)
               for g, (_, dil) in enumerate(ATT_GROUPS)]
        gates = _proj(h, w_in_b, i, HY_COLS + QKV_COLS, GATE_COLS2, tn=D_MODEL // 2)

        k2 = _hyena_filter(feat, _pad2(hf_w1[i], LANES, LANES), _pad2(hf_b1[i][None], 1, LANES),
                           _pad2(hf_w2[i], LANES, LANES), _pad2(hf_b2[i][None], 1, LANES),
                           _pad2(hf_w3[i], LANES, LANES), _pad2(hf_b3[i][None], 1, LANES),
                           _pad2(hf_w4[i], LANES, 2 * C), _pad2(hf_freq[i][None], 1, LANES), deltas)
        kf = _filter_fft(k2.reshape(-1, 2 * C), mats)
        y_hy = _long_conv(u.reshape(B, L, 3 * C), w_sc[i], b_sc[i], kf, hy_bias[i], mats).reshape(T, C)

        attn = [(_attention_group if dil == 1 else _attention_strided)(qkv[g], dil, B, L)
                for g, (_, dil) in enumerate(ATT_GROUPS)]
        xt = _merge(xt, y_hy, attn, gates, w_br_h_b, w_br_a_b, w_out_b, i, g1, L)

        last = i == DEPTH - 1
        if last:
            next_norm = (final_g.reshape(1, D), sc2, sh2)
        else:
            next_norm = (norm1_g[i + 1].reshape(1, D), mods[i + 1][1], mods[i + 1][0])
        res = _moe_layer(xt, norm2_g[i].reshape(1, D), sc2, sh2, g2, router_wt, router_b,
                         moe_w1_b, moe_w3_b, moe_w2_b, i, L, next_norm, last)
        if last:
            return res.reshape(B, L, D)
        xt, h = res
```

```python
import math

import jax
import jax.numpy as jnp
import numpy as np
from jax import lax
from jax.experimental import pallas as pl
from jax.experimental.pallas import tpu as pltpu

D_MODEL = 1024
DEPTH = 2
HY_WIDTH = 768
HY_EMB = 33
HY_FFN = 64
HY_FAST_DECAY_PCT = 0.3
HY_SLOW_DECAY_PCT = 1.5
HY_TARGET = 1e-2
HEAD_DIM = 64
ATT_GROUPS = ((128, 1), (512, 4), (2048, 16))
HEADS_PER_GROUP = 4
N_HEADS = HEADS_PER_GROUP * len(ATT_GROUPS)
ATT_WIDTH = N_HEADS * HEAD_DIM
ATT_OUT = HEADS_PER_GROUP * HEAD_DIM
ROPE_THETA = 10000.0
IN_SPLITS = (3 * HY_WIDTH, ATT_WIDTH, ATT_WIDTH, ATT_WIDTH, D_MODEL, D_MODEL)
N_EXPERTS = 16
N_GROUPS = 4
EXPERTS_PER_GROUP = N_EXPERTS // N_GROUPS
D_EXPERT = 512
NORM_EPS = 1e-6
MASK_VALUE = -1e30

LANES = 128
MXU_DIM = 256
VMEM_LIMIT_BYTES = 56 * 1024 * 1024

F32 = jnp.float32
BF16 = jnp.bfloat16
HIGHEST = lax.Precision.HIGHEST

PROJ_TM = 2048
PROJ_TN = MXU_DIM
PROJ_CHUNK = 512
HY_COLS = 3 * HY_WIDTH
QKV_COLS = 3 * ATT_WIDTH
GATE_COLS2 = 2 * D_MODEL

FFT_R = 128
FFT_KH = FFT_R // 2 + 1
FFT_KP = 72
FFT_SLAB = 2 * FFT_KP
FFT_G_PITCH = FFT_SLAB + 8
FFT_T_PITCH = FFT_R + 8
CONV_CB = 128
FFT_UNROLL_TIME = 8
FFT_UNROLL_FREQ = 5

PAIRS_PER_GROUP = 6
N_BUCKETS = N_GROUPS * PAIRS_PER_GROUP
BUCKET_ROWS = 32
ROUTER_TM = 512
EXPERT_TM = 256
ROW_BLOCK = 1024
GATE_COLS = LANES


def _params(*sem):
    return pltpu.CompilerParams(dimension_semantics=sem, vmem_limit_bytes=VMEM_LIMIT_BYTES)


def _sigmoid(x):
    return 1.0 / (1.0 + jnp.exp(-x))


def _store_time_padded(o_ref, val):
    groups = val.shape[0] // FFT_R
    o_ref[:, :FFT_R, :] = val.reshape(groups, FFT_R, val.shape[1])
    o_ref[:, FFT_R:, :] = jnp.zeros((groups, FFT_T_PITCH - FFT_R, val.shape[1]), o_ref.dtype)


def _time_padded_shape(rows, cols):
    return jax.ShapeDtypeStruct((rows // FFT_R, FFT_T_PITCH, cols), F32)


def _ada_kernel(c_ref, w_ref, b_ref, o_ref):
    c = c_ref[...]
    c_act = c * _sigmoid(c)
    o_ref[0] = jnp.dot(c_act, w_ref[0], precision=HIGHEST, preferred_element_type=F32) + b_ref[0]


def _ada(c_pad, w_ada, b_ada):
    depth, D, N = w_ada.shape
    rows = c_pad.shape[0]
    tn = N // 4
    return pl.pallas_call(
        _ada_kernel,
        out_shape=jax.ShapeDtypeStruct((depth, rows, N), F32),
        grid=(depth, N // tn),
        in_specs=[pl.BlockSpec((rows, D), lambda l, j: (0, 0)),
                  pl.BlockSpec((1, D, tn), lambda l, j: (l, 0, j)),
                  pl.BlockSpec((1, 1, tn), lambda l, j: (l, 0, j))],
        out_specs=pl.BlockSpec((1, rows, tn), lambda l, j: (l, 0, j)),
        compiler_params=_params("arbitrary", "arbitrary"),
        name="ada_mod",
    )(c_pad, w_ada, b_ada.reshape(depth, 1, N))


def _norm_mod_kernel(x_ref, g_ref, sc_ref, sh_ref, o_ref):
    x = x_ref[...]
    r = lax.rsqrt(jnp.mean(x * x, axis=-1, keepdims=True) + NORM_EPS)
    h = (x * r) * g_ref[...]
    o_ref[...] = (h * (1.0 + sc_ref[0]) + sh_ref[0]).astype(o_ref.dtype)


def _norm_mod(x, g, sc, sh, seq_len):
    T, D = x.shape
    tm = 1024
    per_batch = seq_len // tm
    return pl.pallas_call(
        _norm_mod_kernel,
        out_shape=jax.ShapeDtypeStruct((T, D), BF16),
        grid=(T // tm,),
        in_specs=[pl.BlockSpec((tm, D), lambda i: (i, 0)),
                  pl.BlockSpec((1, D), lambda i: (0, 0)),
                  pl.BlockSpec((1, 1, D), lambda i: (i // per_batch, 0, 0)),
                  pl.BlockSpec((1, 1, D), lambda i: (i // per_batch, 0, 0))],
        out_specs=pl.BlockSpec((tm, D), lambda i: (i, 0)),
        compiler_params=_params("arbitrary"),
        name="norm_mod",
    )(x, g, sc, sh)


def _proj_kernel(h_ref, w_ref, o_ref):
    o_ref[...] = jnp.dot(h_ref[...], w_ref[0], preferred_element_type=F32).astype(o_ref.dtype)


def _proj(h, w, layer, col0, n_cols, tn):
    T, D = h.shape
    tm = PROJ_TM
    assert col0 % tn == 0 and n_cols % tn == 0
    off = col0 // tn
    return pl.pallas_call(
        _proj_kernel,
        out_shape=jax.ShapeDtypeStruct((T, n_cols), BF16),
        grid=(T // tm, n_cols // tn),
        in_specs=[pl.BlockSpec((tm, D), lambda i, j: (i, 0)),
                  pl.BlockSpec((1, D, tn), lambda i, j: (layer, 0, off + j))],
        out_specs=pl.BlockSpec((tm, tn), lambda i, j: (i, j)),
        compiler_params=_params("arbitrary", "arbitrary"),
        name="proj",
    )(h, w)


def _proj_qkv_kernel(h_ref, wq_ref, wk_ref, wv_ref, cos_ref, sin_ref, rot_ref, o_ref):
    tm, tn = h_ref.shape[0], PROJ_TN
    rot = rot_ref[...]
    for c in range(tm // PROJ_CHUNK):
        rows = slice(c * PROJ_CHUNK, (c + 1) * PROJ_CHUNK)
        hc = h_ref[rows, :]
        cos = jnp.concatenate([cos_ref[rows, :], cos_ref[rows, :]], axis=1)
        sin = jnp.concatenate([sin_ref[rows, :], sin_ref[rows, :]], axis=1)

        def roped(w_ref):
            acc = jnp.dot(hc, w_ref[0], preferred_element_type=F32)
            swapped = jnp.dot(acc.astype(BF16), rot, preferred_element_type=F32)
            return acc * cos + swapped * sin

        o_ref[rows, 0:tn] = (roped(wq_ref) * HEAD_DIM ** -0.5).astype(o_ref.dtype)
        o_ref[rows, tn:2 * tn] = roped(wk_ref).astype(o_ref.dtype)
        o_ref[rows, 2 * tn:3 * tn] = jnp.dot(hc, wv_ref[0], preferred_element_type=F32).astype(o_ref.dtype)


def _proj_qkv(h, w, layer, group, rope, seq_len, out_dtype):
    T, D = h.shape
    tm, tn = PROJ_TM, PROJ_TN
    n_groups = len(ATT_GROUPS)
    off = HY_COLS // tn + group
    cos_t, sin_t, rot = rope
    per_batch = seq_len // tm
    tab_spec = pl.BlockSpec((tm, LANES), lambda i: (i % per_batch, 0))
    w_spec = lambda kind: pl.BlockSpec((1, D, tn), lambda i: (layer, 0, off + n_groups * kind))
    return pl.pallas_call(
        _proj_qkv_kernel,
        out_shape=jax.ShapeDtypeStruct((T, 3 * tn), out_dtype),
        grid=(T // tm,),
        in_specs=[pl.BlockSpec((tm, D), lambda i: (i, 0)), w_spec(0), w_spec(1), w_spec(2),
                  tab_spec, tab_spec, pl.BlockSpec((tn, tn), lambda i: (0, 0))],
        out_specs=pl.BlockSpec((tm, 3 * tn), lambda i: (i, 0)),
        compiler_params=_params("arbitrary"),
        name="proj_qkv",
    )(h, w, w, w, cos_t, sin_t, rot)


def _rope_tables(seq_len):
    half = HEAD_DIM // 2
    inv = ROPE_THETA ** (-jnp.arange(half, dtype=F32) / half)
    ang = jnp.arange(seq_len, dtype=F32)[:, None] * inv[None, :]
    reps = LANES // half
    cos_t = jnp.tile(jnp.cos(ang), (1, reps))
    sin_t = jnp.tile(jnp.sin(ang), (1, reps))
    rot = np.zeros((PROJ_TN, PROJ_TN), np.float32)
    for j in range(PROJ_TN):
        if j % HEAD_DIM < half:
            rot[j + half, j] = -1.0
        else:
            rot[j - half, j] = 1.0
    return cos_t, sin_t, jnp.asarray(rot, BF16)


def _hyena_filter_kernel(feat_ref, w1_ref, b1_ref, w2_ref, b2_ref, w3_ref, b3_ref, w4_ref, fr_ref,
                         dl_ref, o_ref):
    i = pl.program_id(0)
    tm = feat_ref.shape[0]
    z = feat_ref[...]
    fr = fr_ref[...]
    h = jnp.sin(fr * (jnp.dot(z, w1_ref[...], precision=HIGHEST, preferred_element_type=F32) + b1_ref[...]))
    h = jnp.sin(fr * (jnp.dot(h, w2_ref[...], precision=HIGHEST, preferred_element_type=F32) + b2_ref[...]))
    h = jnp.sin(fr * (jnp.dot(h, w3_ref[...], precision=HIGHEST, preferred_element_type=F32) + b3_ref[...]))
    k = jnp.dot(h, w4_ref[...], precision=HIGHEST, preferred_element_type=F32)
    decay = jnp.exp(-z[:, 0:1] * jnp.abs(dl_ref[...]))
    pos = i * tm + lax.broadcasted_iota(jnp.int32, (tm, 1), 0)
    is_bwd = lax.broadcasted_iota(jnp.int32, (1, k.shape[1]), 1) >= HY_WIDTH
    _store_time_padded(o_ref, jnp.where((pos == 0) & is_bwd, 0.0, k * decay))


def _hyena_filter(feat, w1, b1, w2, b2, w3, b3, w4, freq, deltas2):
    n_rows = feat.shape[0]
    C2 = 2 * HY_WIDTH
    tm = 1024
    full = lambda i: (0, 0)
    return pl.pallas_call(
        _hyena_filter_kernel,
        out_shape=_time_padded_shape(n_rows, C2),
        grid=(n_rows // tm,),
        in_specs=[pl.BlockSpec((tm, LANES), lambda i: (i, 0)),
                  pl.BlockSpec((LANES, LANES), full), pl.BlockSpec((1, LANES), full),
                  pl.BlockSpec((LANES, LANES), full), pl.BlockSpec((1, LANES), full),
                  pl.BlockSpec((LANES, LANES), full), pl.BlockSpec((1, LANES), full),
                  pl.BlockSpec((LANES, C2), full),
                  pl.BlockSpec((1, LANES), full),
                  pl.BlockSpec((1, C2), full)],
        out_specs=pl.BlockSpec((tm // FFT_R, FFT_T_PITCH, C2), lambda i: (i, 0, 0)),
        compiler_params=_params("arbitrary"),
        name="hyena_filter",
    )(feat, w1, b1, w2, b2, w3, b3, w4, freq, deltas2)


def _hyena_features(seq_len):
    L = seq_len
    t = jnp.linspace(0.0, 1.0, L, dtype=F32)[:, None]
    bands = (HY_EMB - 1) // 2
    w = 2.0 * math.pi * jnp.arange(L, dtype=F32)[:, None] / L
    f = jnp.linspace(1e-4, bands - 1, bands, dtype=F32)[None, :]
    z = jnp.concatenate([t, jnp.cos(f * w), -jnp.sin(f * w)], axis=-1)
    z = jnp.pad(z, ((0, 0), (0, LANES - HY_EMB)))
    max_decay = math.log(HY_TARGET) / HY_FAST_DECAY_PCT
    min_decay = math.log(HY_TARGET) / HY_SLOW_DECAY_PCT
    deltas = jnp.linspace(min_decay, max_decay, HY_WIDTH, dtype=F32)[None, :]
    return z, jnp.concatenate([deltas, deltas], axis=1)


def _pad2(a, rows, cols):
    return jnp.pad(a, ((0, rows - a.shape[0]), (0, cols - a.shape[1])))


def _dft_matrices():
    R, KH, KP = FFT_R, FFT_KH, FFT_KP
    N = R * R
    n1 = np.arange(R)[:, None, None]
    k2 = np.arange(KP)[None, :, None]
    n2 = np.arange(R)[None, None, :]
    phase = 2.0 * np.pi * ((n2 * k2 % R) / R + (n1 * k2) / N)
    live = (k2 < KH)
    a1 = np.concatenate([np.cos(phase) * live, -np.sin(phase) * live], axis=1)
    wgt = np.where((k2 == 0) | (k2 == R // 2), 1.0, 2.0) * live / N
    b1 = np.concatenate([np.cos(phase) * wgt, -np.sin(phase) * wgt], axis=1)
    b1 = np.transpose(b1, (0, 2, 1))[:, :R // 2, :]
    th = 2.0 * np.pi * (np.arange(R)[:, None] * np.arange(R)[None, :] % R) / R
    c, s = np.cos(th), np.sin(th)
    w2f = np.block([[c, s], [-s, c]])
    w2i = np.block([[c, -s], [s, c]])
    as_bf = lambda a: jnp.asarray(a.astype(np.float32)).astype(BF16)
    return dict(a1=as_bf(a1[:, :, :R // 2]), b1=as_bf(b1), w2f=as_bf(w2f), w2i=as_bf(w2i))


def _fft_stage1(src_ref, a1_ref, g_ref, n_rows):
    def body(n1, carry):
        xs = src_ref[pl.ds(n1, n_rows, stride=FFT_T_PITCH), :].astype(BF16)
        slab = jnp.dot(a1_ref[n1], xs, preferred_element_type=F32)
        g_ref[pl.ds(pl.multiple_of(n1 * FFT_G_PITCH, 8), FFT_SLAB), :] = slab
        return carry
    lax.fori_loop(0, FFT_R, body, 0, unroll=FFT_UNROLL_TIME)


def _load_freq_rows(g_ref, k2):
    re = g_ref[pl.ds(k2, FFT_R, stride=FFT_G_PITCH), :]
    im = g_ref[pl.ds(FFT_KP + k2, FFT_R, stride=FFT_G_PITCH), :]
    return jnp.concatenate([re, im], axis=0)


def _filter_fft_kernel(fwd_ref, bwd_ref, a1_ref, w2_ref, o_ref, gf_ref, gb_ref):
    n2_rows = fwd_ref.shape[0] // FFT_T_PITCH
    cb = fwd_ref.shape[1]

    def stage1(n1, carry):
        rows = pl.ds(n1, n2_rows, stride=FFT_T_PITCH)
        xs = jnp.concatenate([fwd_ref[rows, :], bwd_ref[rows, :]], axis=1).astype(BF16)
        slab = jnp.dot(a1_ref[n1], xs, preferred_element_type=F32)
        dst = pl.ds(pl.multiple_of(n1 * FFT_G_PITCH, 8), FFT_SLAB)
        gf_ref[dst, :] = slab[:, :cb]
        gb_ref[dst, :] = slab[:, cb:]
        return carry
    lax.fori_loop(0, FFT_R, stage1, 0, unroll=FFT_UNROLL_TIME)

    def body(k2, carry):
        gk = jnp.concatenate([_load_freq_rows(gf_ref, k2), _load_freq_rows(gb_ref, k2)], axis=1).astype(BF16)
        x = jnp.dot(w2_ref[...], gk, preferred_element_type=F32)
        f, b = x[:, :cb], x[:, cb:]
        spec = jnp.concatenate([f[:FFT_R] + b[:FFT_R], f[FFT_R:] - b[FFT_R:]], axis=0)
        o_ref[k2] = spec.astype(o_ref.dtype)
        return carry
    lax.fori_loop(0, FFT_KH, body, 0, unroll=FFT_UNROLL_FREQ)


def _filter_fft(k2, mats):
    n_rows = k2.shape[0]
    C = HY_WIDTH
    cb = CONV_CB
    ncb = C // cb
    return pl.pallas_call(
        _filter_fft_kernel,
        out_shape=jax.ShapeDtypeStruct((FFT_KH, 2 * FFT_R, C), BF16),
        grid=(ncb,),
        in_specs=[pl.BlockSpec((n_rows, cb), lambda j: (0, j)),
                  pl.BlockSpec((n_rows, cb), lambda j: (0, ncb + j)),
                  pl.BlockSpec((FFT_R, FFT_SLAB, FFT_R // 2), lambda j: (0, 0, 0)),
                  pl.BlockSpec((2 * FFT_R, 2 * FFT_R), lambda j: (0, 0))],
        out_specs=pl.BlockSpec((FFT_KH, 2 * FFT_R, cb), lambda j: (0, 0, j)),
        scratch_shapes=[pltpu.VMEM((FFT_R * FFT_G_PITCH, cb), F32), pltpu.VMEM((FFT_R * FFT_G_PITCH, cb), F32)],
        compiler_params=_params("arbitrary"),
        name="hyena_filter_fft",
    )(k2, k2, mats["a1"], mats["w2f"])


SC_HALO = 16


def _long_conv_kernel(x0_ref, x1_ref, v_ref, w0_ref, w1_ref, wv_ref, b0_ref, b1s_ref, bv_ref, kf_ref, bias_ref,
                      a1_ref, b1_ref, w2f_ref, w2i_ref, o_ref, g_ref, zv_ref, y_ref, stage_ref):
    seq_len = x0_ref.shape[1]
    n_groups = seq_len // FFT_R

    def short_conv(part, u_ref, w_ref, b_ref, g):
        row0 = pl.multiple_of(g * FFT_R, FFT_R)
        lo = pl.multiple_of(jnp.maximum(row0 - SC_HALO, 0), SC_HALO)
        hi = pl.multiple_of(jnp.minimum(row0 + FFT_R, seq_len - SC_HALO), SC_HALO)
        st = stage_ref.at[part]
        st[0:SC_HALO, :] = jnp.where(g > 0, u_ref[0, pl.ds(lo, SC_HALO), :].astype(F32), 0.0)
        st[SC_HALO:SC_HALO + FFT_R, :] = u_ref[0, pl.ds(row0, FFT_R), :].astype(F32)
        st[SC_HALO + FFT_R:, :] = jnp.where(g < n_groups - 1, u_ref[0, pl.ds(hi, SC_HALO), :].astype(F32), 0.0)
        w = w_ref[...]
        return (st[SC_HALO - 1:SC_HALO - 1 + FFT_R, :] * w[0:1] + st[SC_HALO:SC_HALO + FFT_R, :] * w[1:2]
                + st[SC_HALO + 1:SC_HALO + 1 + FFT_R, :] * w[2:3] + b_ref[...])

    def gate_body(g, carry):
        zv = short_conv(2, v_ref, wv_ref, bv_ref, g) * short_conv(1, x1_ref, w1_ref, b1s_ref, g)
        zv_ref[pl.ds(pl.multiple_of(g * FFT_T_PITCH, 8), FFT_R), :] = zv
        return carry
    lax.fori_loop(0, n_groups, gate_body, 0, unroll=2)

    _fft_stage1(zv_ref, a1_ref, g_ref, n_groups)

    def freq_body(k2, carry):
        gk = _load_freq_rows(g_ref, k2).astype(BF16)
        x = jnp.dot(w2f_ref[...], gk, preferred_element_type=F32)
        kf = kf_ref[k2].astype(F32)
        xr, xi = x[:FFT_R], x[FFT_R:]
        kr, ki = kf[:FFT_R], kf[FFT_R:]
        p = jnp.concatenate([xr * kr - xi * ki, xr * ki + xi * kr], axis=0).astype(BF16)
        hk = jnp.dot(w2i_ref[...], p, preferred_element_type=F32)
        g_ref[pl.ds(k2, FFT_R, stride=FFT_G_PITCH), :] = hk[:FFT_R]
        g_ref[pl.ds(FFT_KP + k2, FFT_R, stride=FFT_G_PITCH), :] = hk[FFT_R:]
        return carry
    lax.fori_loop(0, FFT_KH, freq_body, 0, unroll=FFT_UNROLL_FREQ)

    def time_body(n1, carry):
        slab = g_ref[pl.ds(pl.multiple_of(n1 * FFT_G_PITCH, 8), FFT_SLAB), :].astype(BF16)
        y_ref[pl.ds(n1, n_groups, stride=FFT_T_PITCH), :] = jnp.dot(b1_ref[n1], slab, preferred_element_type=F32)
        return carry
    lax.fori_loop(0, FFT_R, time_body, 0, unroll=FFT_UNROLL_TIME)

    bias = bias_ref[...]

    def out_body(g, carry):
        rows = pl.ds(pl.multiple_of(g * FFT_T_PITCH, 8), FFT_R)
        x0 = short_conv(0, x0_ref, w0_ref, b0_ref, g)
        y = (y_ref[rows, :] + zv_ref[rows, :] * bias) * x0
        o_ref[0, pl.ds(pl.multiple_of(g * FFT_R, FFT_R), FFT_R), :] = y.astype(o_ref.dtype)
        return carry
    lax.fori_loop(0, n_groups, out_body, 0, unroll=2)


def _long_conv(u, w_sc, b_sc, kf, hy_bias, mats):
    B, L, _ = u.shape
    C = HY_WIDTH
    cb = CONV_CB
    ncb = C // cb
    lp = L // FFT_R * FFT_T_PITCH
    part = lambda p: pl.BlockSpec((1, L, cb), lambda j, b: (b, 0, p * ncb + j))
    wpart = lambda p: pl.BlockSpec((3, cb), lambda j, b: (0, p * ncb + j))
    bpart = lambda p: pl.BlockSpec((1, cb), lambda j, b: (0, p * ncb + j))
    b2 = b_sc.reshape(1, 3 * C)
    return pl.pallas_call(
        _long_conv_kernel,
        out_shape=jax.ShapeDtypeStruct((B, L, C), BF16),
        grid=(ncb, B),
        in_specs=[part(0), part(1), part(2), wpart(0), wpart(1), wpart(2), bpart(0), bpart(1), bpart(2),
                  pl.BlockSpec((FFT_KH, 2 * FFT_R, cb), lambda j, b: (0, 0, j)),
                  pl.BlockSpec((1, cb), lambda j, b: (0, j)),
                  pl.BlockSpec((FFT_R, FFT_SLAB, FFT_R // 2), lambda j, b: (0, 0, 0)),
                  pl.BlockSpec((FFT_R, FFT_R // 2, FFT_SLAB), lambda j, b: (0, 0, 0)),
                  pl.BlockSpec((2 * FFT_R, 2 * FFT_R), lambda j, b: (0, 0)),
                  pl.BlockSpec((2 * FFT_R, 2 * FFT_R), lambda j, b: (0, 0))],
        out_specs=pl.BlockSpec((1, L, cb), lambda j, b: (b, 0, j)),
        scratch_shapes=[pltpu.VMEM((FFT_R * FFT_G_PITCH, cb), F32), pltpu.VMEM((lp, cb), F32),
                        pltpu.VMEM((lp, cb), F32), pltpu.VMEM((3, FFT_R + 2 * SC_HALO, cb), F32)],
        compiler_params=_params("arbitrary", "arbitrary"),
        name="hyena_long_conv",
    )(u, u, u, w_sc, w_sc, w_sc, b2, b2, b2, kf, hy_bias.reshape(1, C),
      mats["a1"], mats["b1"], mats["w2f"], mats["w2i"])


ATT_TQ = 128
ATT_RADIUS = 64
ATT_WINDOW = ATT_TQ + 2 * ATT_RADIUS


ATT_SUB = 4


def _attn_kernel(q_ref, k_ref, v_ref, o_ref, lse_ref):
    i = pl.program_id(2)
    ls = k_ref.shape[1]
    nh = HEADS_PER_GROUP
    head_of_col = lax.broadcasted_iota(jnp.int32, (1, ATT_OUT), 1) // HEAD_DIM
    row_iota = lax.broadcasted_iota(jnp.int32, (nh * ATT_TQ, ATT_WINDOW), 0) % ATT_TQ
    col_iota = lax.broadcasted_iota(jnp.int32, (nh * ATT_TQ, ATT_WINDOW), 1)
    rel = row_iota - col_iota
    for sub in range(ATT_SUB):
        q0 = (i * ATT_SUB + sub) * ATT_TQ
        rows = slice(sub * ATT_TQ, (sub + 1) * ATT_TQ)
        q = q_ref[0, rows, :]
        start = jnp.clip(q0 - ATT_RADIUS, 0, ls - ATT_WINDOW)
        start = pl.multiple_of(start, ATT_RADIUS)
        kw = k_ref[0, pl.ds(start, ATT_WINDOW), :]
        vw = v_ref[0, pl.ds(start, ATT_WINDOW), :]
        out, lse = _attn_unit(q, kw, vw, rel, q0 - start, head_of_col)
        o_ref[0, rows, :] = out.astype(o_ref.dtype)
        lse_ref[0, rows, :] = lse


def _attn_unit(q, kw, vw, rel, q_minus_start, head_of_col):
    nh = HEADS_PER_GROUP
    band = jnp.abs(q_minus_start + rel) <= ATT_RADIUS
    zero = jnp.zeros_like(q)
    q4 = jnp.concatenate([jnp.where(head_of_col == h, q, zero) for h in range(nh)], axis=0)
    s = lax.dot_general(q4, kw, (((1,), (1,)), ((), ())), preferred_element_type=F32)
    s = jnp.where(band, s, MASK_VALUE)
    m = jnp.max(s, axis=-1, keepdims=True)
    p = jnp.exp(s - m)
    den = jnp.sum(p, axis=-1, keepdims=True)
    pv = jnp.dot(p.astype(BF16), vw, preferred_element_type=F32) / den
    lse4 = m + jnp.log(den)
    out = jnp.zeros((ATT_TQ, ATT_OUT), F32)
    lse = jnp.zeros((ATT_TQ, ATT_OUT), F32)
    for h in range(nh):
        mine = head_of_col == h
        hrows = slice(h * ATT_TQ, (h + 1) * ATT_TQ)
        out = jnp.where(mine, pv[hrows], out)
        lse = jnp.where(mine, lse4[hrows], lse)
    return out, lse


ATT_CHUNK = 2048


def _attn_strided_kernel(q0_ref, q1_ref, k0_ref, k1_ref, v0_ref, v1_ref, o_ref, lse_ref, o_scr, lse_scr, *, dil):
    i = pl.program_id(1)
    ls = q0_ref.shape[1] // dil

    def gather(lo_ref, hi_ref, first, n):
        rows = pl.ds(first, n, stride=dil)
        return jnp.concatenate([lo_ref[0, rows, :], hi_ref[0, rows, :]], axis=1).astype(BF16)

    units = ATT_CHUNK // ATT_TQ
    shift = dil.bit_length() - 1
    head_of_col = lax.broadcasted_iota(jnp.int32, (1, ATT_OUT), 1) // HEAD_DIM
    row_iota = lax.broadcasted_iota(jnp.int32, (HEADS_PER_GROUP * ATT_TQ, ATT_WINDOW), 0) % ATT_TQ
    rel = row_iota - lax.broadcasted_iota(jnp.int32, (HEADS_PER_GROUP * ATT_TQ, ATT_WINDOW), 1)

    def unit(u, carry):
        r = u & (dil - 1)
        sb = u >> shift
        m0 = i * (ATT_CHUNK // dil) + sb * ATT_TQ
        start = jnp.clip(m0 - ATT_RADIUS, 0, ls - ATT_WINDOW)
        q = gather(q0_ref, q1_ref, m0 * dil + r, ATT_TQ)
        kw = gather(k0_ref, k1_ref, start * dil + r, ATT_WINDOW)
        vw = gather(v0_ref, v1_ref, start * dil + r, ATT_WINDOW)
        out, lse = _attn_unit(q, kw, vw, rel, m0 - start, head_of_col)
        dst = pl.ds(sb * ATT_TQ * dil + r, ATT_TQ, stride=dil)
        for half in range(2):
            lanes = slice(half * LANES, (half + 1) * LANES)
            o_scr[half, dst, :] = out[:, lanes]
            lse_scr[half, dst, :] = lse[:, lanes]
        return carry
    lax.fori_loop(0, units, unit, 0, unroll=2)
    for half in range(2):
        lanes = slice(half * LANES, (half + 1) * LANES)
        o_ref[0, :, lanes] = o_scr[half]
        lse_ref[0, :, lanes] = lse_scr[half]


def _attention_strided(qkv_g, dil, batch, seq_len):
    view = qkv_g.reshape(batch, seq_len, 3 * ATT_OUT)
    halves = [pl.BlockSpec((1, seq_len, LANES), lambda b, i, c=c: (b, 0, c), pipeline_mode=pl.Buffered(1))
              for c in range(3 * ATT_OUT // LANES)]
    o_spec = pl.BlockSpec((1, ATT_CHUNK, ATT_OUT), lambda b, i: (b, i, 0))
    kern = lambda *refs: _attn_strided_kernel(*refs, dil=dil)
    o, lse = pl.pallas_call(
        kern,
        out_shape=(jax.ShapeDtypeStruct((batch, seq_len, ATT_OUT), F32),
                   jax.ShapeDtypeStruct((batch, seq_len, ATT_OUT), F32)),
        grid=(batch, seq_len // ATT_CHUNK),
        in_specs=halves,
        out_specs=(o_spec, o_spec),
        scratch_shapes=[pltpu.VMEM((2, ATT_CHUNK, LANES), F32), pltpu.VMEM((2, ATT_CHUNK, LANES), F32)],
        compiler_params=_params("arbitrary", "arbitrary"),
        name=f"dilated_attn_d{dil}",
    )(*([view] * len(halves)))
    T = batch * seq_len
    return o.reshape(T, ATT_OUT), lse.reshape(T, ATT_OUT)


def _attention_group(qkv_g, dil, batch, seq_len):
    ls = seq_len // dil
    tq = ATT_SUB * ATT_TQ
    view = qkv_g.reshape(batch, ls, dil * 3 * ATT_OUT)

    def col(which):
        return lambda b, r, i: (b, 0, r * 3 + which)

    q_map = lambda b, r, i: (b, i, r * 3)
    o_map = lambda b, r, i: (b, i, r)
    o, lse = pl.pallas_call(
        _attn_kernel,
        out_shape=(jax.ShapeDtypeStruct((batch, ls, dil * ATT_OUT), BF16),
                   jax.ShapeDtypeStruct((batch, ls, dil * ATT_OUT), F32)),
        grid=(batch, dil, ls // tq),
        in_specs=[pl.BlockSpec((1, tq, ATT_OUT), q_map),
                  pl.BlockSpec((1, ls, ATT_OUT), col(1)),
                  pl.BlockSpec((1, ls, ATT_OUT), col(2))],
        out_specs=(pl.BlockSpec((1, tq, ATT_OUT), o_map), pl.BlockSpec((1, tq, ATT_OUT), o_map)),
        compiler_params=_params("arbitrary", "arbitrary", "arbitrary"),
        name=f"dilated_attn_d{dil}",
    )(view, view, view)
    T = batch * seq_len
    return o.reshape(T, ATT_OUT), lse.reshape(T, ATT_OUT)


def _merge_kernel(x_ref, yhy_ref, o1_ref, o2_ref, o3_ref, l1_ref, l2_ref, l3_ref, h_ref, wg_ref,
                  wh_ref, wa_ref, wo_ref, g1_ref, out_ref):
    l1, l2, l3 = l1_ref[...], l2_ref[...], l3_ref[...]
    m = jnp.maximum(jnp.maximum(l1, l2), l3)
    e1, e2, e3 = jnp.exp(l1 - m), jnp.exp(l2 - m), jnp.exp(l3 - m)
    tot = e1 + e2 + e3
    y_at = (e1 * o1_ref[...].astype(F32) + e2 * o2_ref[...].astype(F32) + e3 * o3_ref[...].astype(F32)) / tot
    a = jnp.dot(yhy_ref[...], wh_ref[0], preferred_element_type=F32)
    b = jnp.dot(y_at.astype(BF16), wa_ref[0], preferred_element_type=F32)
    d_model = x_ref.shape[1]
    gate = jnp.dot(h_ref[...], wg_ref[0], preferred_element_type=F32)
    merged = _sigmoid(gate[:, :d_model]) * a + _sigmoid(gate[:, d_model:]) * b
    upd = jnp.dot(merged.astype(BF16), wo_ref[0], preferred_element_type=F32)
    out_ref[...] = x_ref[...] + g1_ref[0] * upd


def _merge(x, y_hy, attn, h, w_gate, w_br_h, w_br_a, w_out, layer, g1, seq_len):
    T, D = x.shape
    tm = 512
    per_batch = seq_len // tm
    (o1, l1), (o2, l2), (o3, l3) = attn
    row = lambda w: pl.BlockSpec((tm, w), lambda i: (i, 0))
    full = lambda a: pl.BlockSpec((1,) + a.shape[1:], lambda i: (layer, 0, 0))
    return pl.pallas_call(
        _merge_kernel,
        out_shape=jax.ShapeDtypeStruct((T, D), F32),
        grid=(T // tm,),
        in_specs=[row(D), row(HY_WIDTH), row(ATT_OUT), row(ATT_OUT), row(ATT_OUT),
                  row(ATT_OUT), row(ATT_OUT), row(ATT_OUT),
                  row(D), full(w_gate), full(w_br_h), full(w_br_a), full(w_out),
                  pl.BlockSpec((1, 1, D), lambda i: (i // per_batch, 0, 0))],
        out_specs=row(D),
        compiler_params=_params("arbitrary"),
        name="mixer_merge",
    )(x, y_hy, o1, o2, o3, l1, l2, l3, h, w_gate, w_br_h, w_br_a, w_out, g1)


def _router_kernel(x_ref, g_ref, sc_ref, sh_ref, rwt_ref, rb_ref, hg_ref, stats_ref, cnt_ref, base_ref):
    i = pl.program_id(0)
    tm = x_ref.shape[0]

    @pl.when(i == 0)
    def _():
        base_ref[...] = jnp.zeros_like(base_ref)

    x = x_ref[...]
    r = lax.rsqrt(jnp.mean(x * x, axis=-1, keepdims=True) + NORM_EPS)
    h = (x * r) * g_ref[...]
    h = h * (1.0 + sc_ref[0]) + sh_ref[0]

    logits = lax.dot_general(rwt_ref[...], h, (((1,), (1,)), ((), ())),
                             precision=HIGHEST, preferred_element_type=F32)
    scores = _sigmoid(logits)
    biased = scores + rb_ref[...]

    def row(a, k):
        return a[k:k + 1, :]

    sel = jnp.zeros((1, tm), jnp.int32)
    best = None
    for g in range(N_GROUPS):
        a, b, c, d = (row(biased, 4 * g + k) for k in range(4))
        m_ab, n_ab = jnp.maximum(a, b), jnp.minimum(a, b)
        m_cd, n_cd = jnp.maximum(c, d), jnp.minimum(c, d)
        gs = jnp.maximum(m_ab, m_cd) + jnp.maximum(jnp.minimum(m_ab, m_cd), jnp.maximum(n_ab, n_cd))
        if g == 0:
            best = gs
        else:
            better = gs > best
            sel = jnp.where(better, g, sel)
            best = jnp.where(better, gs, best)

    v, u = [], []
    for k in range(EXPERTS_PER_GROUP):
        vk = jnp.zeros((1, tm), F32)
        uk = jnp.zeros((1, tm), F32)
        for g in range(N_GROUPS):
            vk = jnp.where(sel == g, row(biased, 4 * g + k), vk)
            uk = jnp.where(sel == g, row(scores, 4 * g + k), uk)
        v.append(vk)
        u.append(uk)

    i1 = jnp.zeros((1, tm), jnp.int32)
    b1 = v[0]
    for k in range(1, EXPERTS_PER_GROUP):
        gt = v[k] > b1
        i1 = jnp.where(gt, k, i1)
        b1 = jnp.where(gt, v[k], b1)
    i2 = jnp.zeros((1, tm), jnp.int32)
    b2 = jnp.full((1, tm), -jnp.inf, F32)
    for k in range(EXPERTS_PER_GROUP):
        cand = (i1 != k) & (v[k] > b2)
        i2 = jnp.where(cand, k, i2)
        b2 = jnp.where(cand, v[k], b2)

    lo = jnp.minimum(i1, i2)
    hi = jnp.maximum(i1, i2)
    pair = jnp.where(lo == 0, hi - 1, jnp.where(lo == 1, hi + 1, 5))
    bucket = sel * PAIRS_PER_GROUP + pair

    u_lo = jnp.zeros((1, tm), F32)
    u_hi = jnp.zeros((1, tm), F32)
    for k in range(EXPERTS_PER_GROUP):
        u_lo = jnp.where(lo == k, u[k], u_lo)
        u_hi = jnp.where(hi == k, u[k], u_hi)
    tot = u_lo + u_hi
    w_lo = u_lo / tot
    w_hi = u_hi / tot

    rows = lax.broadcasted_iota(jnp.int32, (BUCKET_ROWS, tm), 0)
    onehot = (rows == bucket).astype(F32)
    t_src = lax.broadcasted_iota(jnp.int32, (tm, tm), 0)
    t_dst = lax.broadcasted_iota(jnp.int32, (tm, tm), 1)
    before = (t_src < t_dst).astype(BF16)
    cum = jnp.dot(onehot.astype(BF16), before, preferred_element_type=F32)
    base = base_ref[...]
    rank = jnp.sum(onehot * (cum + base), axis=0, keepdims=True)
    base = base + jnp.sum(onehot, axis=1, keepdims=True)
    base_ref[...] = base
    cnt_ref[...] = jnp.broadcast_to(base, cnt_ref.shape)

    srow = lax.broadcasted_iota(jnp.int32, (8, tm), 0)
    stats_ref[...] = jnp.where(srow == 0, bucket.astype(F32), jnp.where(srow == 1, rank, 0.0))

    grow = lax.broadcasted_iota(jnp.int32, (GATE_COLS, tm), 0)
    gates_t = jnp.where(grow == 0, w_lo, jnp.where(grow == 1, w_hi, 0.0))
    hg_ref[:, :D_MODEL] = h
    hg_ref[:, D_MODEL:] = gates_t.T


def _router(x, norm_g, sc, sh, router_wt, router_b, seq_len):
    T, D = x.shape
    tm = ROUTER_TM
    per_batch = seq_len // tm
    return pl.pallas_call(
        _router_kernel,
        out_shape=(jax.ShapeDtypeStruct((T, D + GATE_COLS), F32),
                   jax.ShapeDtypeStruct((8, T), F32),
                   jax.ShapeDtypeStruct((BUCKET_ROWS, LANES), F32)),
        grid=(T // tm,),
        in_specs=[pl.BlockSpec((tm, D), lambda i: (i, 0)),
                  pl.BlockSpec((1, D), lambda i: (0, 0)),
                  pl.BlockSpec((1, 1, D), lambda i: (i // per_batch, 0, 0)),
                  pl.BlockSpec((1, 1, D), lambda i: (i // per_batch, 0, 0)),
                  pl.BlockSpec((N_EXPERTS, D), lambda i: (0, 0)),
                  pl.BlockSpec((N_EXPERTS, 1), lambda i: (0, 0))],
        out_specs=(pl.BlockSpec((tm, D + GATE_COLS), lambda i: (i, 0)),
                   pl.BlockSpec((8, tm), lambda i: (0, i)),
                   pl.BlockSpec((BUCKET_ROWS, LANES), lambda i: (0, 0))),
        scratch_shapes=[pltpu.VMEM((BUCKET_ROWS, 1), F32)],
        compiler_params=_params("arbitrary"),
        name="moe_router",
    )(x, norm_g, sc, sh, router_wt, router_b)


def _start_row_copies(n_rows, make_copy):
    group = 8

    def body(g, carry):
        base = pl.multiple_of(g * group, group)
        for k in range(group):
            make_copy(base + k).start(priority=k % 2)
        return carry

    lax.fori_loop(0, n_rows // group, body, 0)


def _dispatch_kernel(pos_ref, hg_ref, xs_init_hbm, xs_hbm, sem):
    del xs_init_hbm
    _start_row_copies(ROW_BLOCK, lambda r: pltpu.make_async_copy(
        hg_ref.at[pl.ds(r, 1), :], xs_hbm.at[pl.ds(pos_ref[r], 1), :], sem))
    pltpu.make_async_copy(hg_ref, xs_hbm.at[pl.ds(0, ROW_BLOCK), :], sem).wait()


def _dispatch(pos, hg, n_rows):
    T, W = hg.shape
    zeros = jnp.zeros((n_rows, W), F32)
    return pl.pallas_call(
        _dispatch_kernel,
        out_shape=jax.ShapeDtypeStruct((n_rows, W), F32),
        grid=(T // ROW_BLOCK,),
        in_specs=[pl.BlockSpec((ROW_BLOCK,), lambda i: (i,), memory_space=pltpu.SMEM),
                  pl.BlockSpec((ROW_BLOCK, W), lambda i: (i, 0)),
                  pl.BlockSpec(memory_space=pl.ANY)],
        out_specs=pl.BlockSpec(memory_space=pl.ANY),
        scratch_shapes=[pltpu.SemaphoreType.DMA],
        input_output_aliases={2: 0},
        compiler_params=_params("arbitrary"),
        name="moe_dispatch",
    )(pos, hg, zeros)


def _expert_kernel(e_lo_ref, e_hi_ref, n_used_ref, xs_ref, w1a, w3a, w2a, w1b, w3b, w2b, y_ref):
    del e_lo_ref, e_hi_ref
    used = pl.program_id(0) < n_used_ref[0]

    @pl.when(jnp.logical_not(used))
    def _():
        y_ref[...] = jnp.zeros_like(y_ref)

    @pl.when(used)
    def _():
        xb = xs_ref[:, :D_MODEL].astype(BF16)
        g_lo = xs_ref[:, D_MODEL:D_MODEL + 1]
        g_hi = xs_ref[:, D_MODEL + 1:D_MODEL + 2]

        def ffn(w1, w3, w2):
            a = jnp.dot(xb, w1[0, 0], preferred_element_type=F32)
            b = jnp.dot(xb, w3[0, 0], preferred_element_type=F32)
            act = (a * _sigmoid(a)) * b
            return jnp.dot(act.astype(BF16), w2[0, 0], preferred_element_type=F32)

        y_ref[...] = g_lo * ffn(w1a, w3a, w2a) + g_hi * ffn(w1b, w3b, w2b)


def _experts(tile_lo, tile_hi, n_used, xs, w1, w3, w2, layer):
    n_rows, W = xs.shape
    D, F = w1.shape[2], w1.shape[3]
    n_tiles = n_rows // EXPERT_TM

    def x_map(j, lo, hi, nu):
        return (jnp.minimum(j, nu[0] - 1), 0)

    def w_lo_map(j, lo, hi, nu):
        return (layer, lo[j], 0, 0)

    def w_hi_map(j, lo, hi, nu):
        return (layer, hi[j], 0, 0)

    grid_spec = pltpu.PrefetchScalarGridSpec(
        num_scalar_prefetch=3,
        grid=(n_tiles,),
        in_specs=[pl.BlockSpec((EXPERT_TM, W), x_map),
                  pl.BlockSpec((1, 1, D, F), w_lo_map), pl.BlockSpec((1, 1, D, F), w_lo_map),
                  pl.BlockSpec((1, 1, F, D), w_lo_map),
                  pl.BlockSpec((1, 1, D, F), w_hi_map), pl.BlockSpec((1, 1, D, F), w_hi_map),
                  pl.BlockSpec((1, 1, F, D), w_hi_map)],
        out_specs=pl.BlockSpec((EXPERT_TM, D), lambda j, lo, hi, nu: (j, 0)),
    )
    return pl.pallas_call(
        _expert_kernel,
        out_shape=jax.ShapeDtypeStruct((n_rows, D), F32),
        grid_spec=grid_spec,
        compiler_params=_params("arbitrary"),
        name="moe_experts",
    )(tile_lo, tile_hi, n_used, xs, w1, w3, w2, w1, w3, w2)


def _combine_kernel(pos_ref, x_ref, g2_ref, ng_ref, sc_ref, sh_ref, ys_hbm, *refs):
    *out_refs, buf, sem = refs
    _start_row_copies(ROW_BLOCK, lambda r: pltpu.make_async_copy(
        ys_hbm.at[pl.ds(pos_ref[r], 1), :], buf.at[pl.ds(r, 1), :], sem))
    pltpu.make_async_copy(ys_hbm.at[pl.ds(0, ROW_BLOCK), :], buf, sem).wait()
    xn = x_ref[...] + g2_ref[0] * buf[...]
    r = lax.rsqrt(jnp.mean(xn * xn, axis=-1, keepdims=True) + NORM_EPS)
    h = (xn * r) * ng_ref[...]
    if len(out_refs) == 2:
        out_refs[0][...] = xn
        h = h * (1.0 + sc_ref[0]) + sh_ref[0]
    out_refs[-1][...] = h.astype(out_refs[-1].dtype)


def _combine(pos, x, g2, ys, norm_g, sc, sh, seq_len, last):
    T, D = x.shape
    per_batch = seq_len // ROW_BLOCK
    row = pl.BlockSpec((ROW_BLOCK, D), lambda i: (i, 0))
    per_b = pl.BlockSpec((1, 1, D), lambda i: (i // per_batch, 0, 0))
    if last:
        out_shape, out_specs = jax.ShapeDtypeStruct((T, D), F32), row
    else:
        out_shape = (jax.ShapeDtypeStruct((T, D), F32), jax.ShapeDtypeStruct((T, D), BF16))
        out_specs = (row, row)
    return pl.pallas_call(
        _combine_kernel,
        out_shape=out_shape,
        grid=(T // ROW_BLOCK,),
        in_specs=[pl.BlockSpec((ROW_BLOCK,), lambda i: (i,), memory_space=pltpu.SMEM),
                  row, per_b, pl.BlockSpec((1, D), lambda i: (0, 0)), per_b, per_b,
                  pl.BlockSpec(memory_space=pl.ANY)],
        out_specs=out_specs,
        scratch_shapes=[pltpu.VMEM((ROW_BLOCK, D), F32), pltpu.SemaphoreType.DMA],
        compiler_params=_params("arbitrary"),
        name="moe_combine",
    )(pos, x, g2, norm_g, sc, sh, ys)


def _moe_layer(x, norm_g, sc2, sh2, g2, router_wt, router_b, w1, w3, w2, layer, seq_len, next_norm, last):
    T, D = x.shape
    hg, stats, counts = _router(x, norm_g, sc2, sh2, router_wt, router_b, seq_len)

    cnt = counts[:N_BUCKETS, 0].astype(jnp.int32)
    padded = ((cnt + EXPERT_TM - 1) // EXPERT_TM) * EXPERT_TM
    ends = jnp.cumsum(padded)
    starts = ends - padded
    bucket = stats[0].astype(jnp.int32)
    pos = starts[bucket] + stats[1].astype(jnp.int32)
    n_tiles = T // EXPERT_TM + N_BUCKETS
    tile_row0 = jnp.arange(n_tiles, dtype=jnp.int32) * EXPERT_TM
    tile_bucket = jnp.sum((ends[None, :] <= tile_row0[:, None]).astype(jnp.int32), axis=1)
    tile_bucket = jnp.minimum(tile_bucket, N_BUCKETS - 1)
    grp, pair = tile_bucket // PAIRS_PER_GROUP, tile_bucket % PAIRS_PER_GROUP
    pair_lo = jnp.array([0, 0, 0, 1, 1, 2], jnp.int32)[pair]
    pair_hi = jnp.array([1, 2, 3, 2, 3, 3], jnp.int32)[pair]
    tile_lo = grp * EXPERTS_PER_GROUP + pair_lo
    tile_hi = grp * EXPERTS_PER_GROUP + pair_hi
    n_used = (ends[-1] // EXPERT_TM).astype(jnp.int32).reshape(1)

    xs = _dispatch(pos, hg, n_tiles * EXPERT_TM)
    ys = _experts(tile_lo, tile_hi, n_used, xs, w1, w3, w2, layer)
    return _combine(pos, x, g2, ys, *next_norm, seq_len, last)


def kernel(x, c, norm1_g, norm2_g, w_ada, b_ada, w_in, w_sc, b_sc, hf_w1, hf_b1, hf_w2, hf_b2, hf_w3, hf_b3,
           hf_w4, hf_freq, hy_bias, w_br_h, w_br_a, w_out, router_w, router_bias, moe_w1, moe_w3, moe_w2, final_g):
    B, L, D = x.shape
    T = B * L
    C = HY_WIDTH
    xt = x.reshape(T, D)

    rope = _rope_tables(L)
    feat, deltas = _hyena_features(L)
    mats = _dft_matrices()

    c_pad = jnp.pad(c, ((0, 8 - B), (0, 0)))
    mod = _ada(c_pad, w_ada, b_ada)[:, :B]
    router_wt = router_w.T
    router_b = router_bias.reshape(N_EXPERTS, 1)
    w_in_b, w_br_h_b, w_br_a_b, w_out_b = (w.astype(BF16) for w in (w_in, w_br_h, w_br_a, w_out))
    moe_w1_b, moe_w3_b, moe_w2_b = (w.astype(BF16) for w in (moe_w1, moe_w3, moe_w2))
    w_gate_b = w_in_b[:, :, HY_COLS + QKV_COLS:]

    mods = [[mod[i, :, k * D:(k + 1) * D].reshape(B, 1, D) for k in range(6)] for i in range(DEPTH)]
    h = _norm_mod(xt, norm1_g[0].reshape(1, D), mods[0][1], mods[0][0], L)
    for i in range(DEPTH):
        sh1, sc1, g1, sh2, sc2, g2 = mods[i]

        u = _proj(h, w_in_b, i, 0, HY_COLS, tn=HY_WIDTH)
        qkv = [_proj_qkv(h, w_in_b, i, g, rope, L, BF16 if dil == 1 else F32)
               for g, (_, dil) in enumerate(ATT_GROUPS)]

        k2 = _hyena_filter(feat, _pad2(hf_w1[i], LANES, LANES), _pad2(hf_b1[i][None], 1, LANES),
                           _pad2(hf_w2[i], LANES, LANES), _pad2(hf_b2[i][None], 1, LANES),
                           _pad2(hf_w3[i], LANES, LANES), _pad2(hf_b3[i][None], 1, LANES),
                           _pad2(hf_w4[i], LANES, 2 * C), _pad2(hf_freq[i][None], 1, LANES), deltas)
        kf = _filter_fft(k2.reshape(-1, 2 * C), mats)
        y_hy = _long_conv(u.reshape(B, L, 3 * C), w_sc[i], b_sc[i], kf, hy_bias[i], mats).reshape(T, C)

        attn = [(_attention_group if dil == 1 else _attention_strided)(qkv[g], dil, B, L)
                for g, (_, dil) in enumerate(ATT_GROUPS)]
        xt = _merge(xt, y_hy, attn, h, w_gate_b, w_br_h_b, w_br_a_b, w_out_b, i, g1, L)

        last = i == DEPTH - 1
        if last:
            next_norm = (final_g.reshape(1, D), sc2, sh2)
        else:
            next_norm = (norm1_g[i + 1].reshape(1, D), mods[i + 1][1], mods[i + 1][0])
        res = _moe_layer(xt, norm2_g[i].reshape(1, D), sc2, sh2, g2, router_wt, router_b,
                         moe_w1_b, moe_w3_b, moe_w2_b, i, L, next_norm, last)
        if last:
            return res.reshape(B, L, D)
        xt, h = res
```

```python
import math

import jax
import jax.numpy as jnp
import numpy as np
from jax import lax
from jax.experimental import pallas as pl
from jax.experimental.pallas import tpu as pltpu

D_MODEL = 1024
DEPTH = 2
HY_WIDTH = 768
HY_EMB = 33
HY_FAST_DECAY_PCT = 0.3
HY_SLOW_DECAY_PCT = 1.5
HY_TARGET = 1e-2
HEAD_DIM = 64
ATT_GROUPS = ((128, 1), (512, 4), (2048, 16))
HEADS_PER_GROUP = 4
N_HEADS = HEADS_PER_GROUP * len(ATT_GROUPS)
ATT_WIDTH = N_HEADS * HEAD_DIM
ATT_OUT = HEADS_PER_GROUP * HEAD_DIM
ROPE_THETA = 10000.0
N_EXPERTS = 16
N_GROUPS = 4
EXPERTS_PER_GROUP = N_EXPERTS // N_GROUPS
NORM_EPS = 1e-6
MASK_VALUE = -1e30

LANES = 128
MXU_DIM = 256
VMEM_LIMIT_BYTES = 56 * 1024 * 1024

F32 = jnp.float32
BF16 = jnp.bfloat16
HIGHEST = lax.Precision.HIGHEST

PROJ_TM = 2048
PROJ_TN = MXU_DIM
PROJ_CHUNK = 512
HY_COLS = 3 * HY_WIDTH
QKV_COLS = 3 * ATT_WIDTH

FFT_R = 128
FFT_KH = FFT_R // 2 + 1
FFT_KP = 72
FFT_SLAB = 2 * FFT_KP
FFT_G_PITCH = FFT_SLAB + 8
FFT_T_PITCH = FFT_R + 8
CONV_CB = 128
FFT_UNROLL_TIME = 8
FFT_UNROLL_FREQ = 5

PAIRS_PER_GROUP = 6
N_BUCKETS = N_GROUPS * PAIRS_PER_GROUP
BUCKET_ROWS = 32
ROUTER_TM = 512
EXPERT_TM = 256
ROW_BLOCK = 1024
GATE_COLS = LANES


def _params(*sem):
    return pltpu.CompilerParams(dimension_semantics=sem, vmem_limit_bytes=VMEM_LIMIT_BYTES)


def _sigmoid(x):
    return 1.0 / (1.0 + jnp.exp(-x))


def _store_time_padded(o_ref, val):
    groups = val.shape[0] // FFT_R
    o_ref[:, :FFT_R, :] = val.reshape(groups, FFT_R, val.shape[1])
    o_ref[:, FFT_R:, :] = jnp.zeros((groups, FFT_T_PITCH - FFT_R, val.shape[1]), o_ref.dtype)


def _time_padded_shape(rows, cols):
    return jax.ShapeDtypeStruct((rows // FFT_R, FFT_T_PITCH, cols), F32)


def _ada_kernel(c_ref, w_ref, b_ref, o_ref):
    c = c_ref[...]
    c_act = c * _sigmoid(c)
    o_ref[0] = jnp.dot(c_act, w_ref[0], precision=HIGHEST, preferred_element_type=F32) + b_ref[0]


def _ada(c_pad, w_ada, b_ada):
    depth, D, N = w_ada.shape
    rows = c_pad.shape[0]
    tn = N // 4
    return pl.pallas_call(
        _ada_kernel,
        out_shape=jax.ShapeDtypeStruct((depth, rows, N), F32),
        grid=(depth, N // tn),
        in_specs=[pl.BlockSpec((rows, D), lambda l, j: (0, 0)),
                  pl.BlockSpec((1, D, tn), lambda l, j: (l, 0, j)),
                  pl.BlockSpec((1, 1, tn), lambda l, j: (l, 0, j))],
        out_specs=pl.BlockSpec((1, rows, tn), lambda l, j: (l, 0, j)),
        compiler_params=_params("arbitrary", "arbitrary"),
        name="ada_mod",
    )(c_pad, w_ada, b_ada.reshape(depth, 1, N))


def _norm_mod_kernel(x_ref, g_ref, sc_ref, sh_ref, o_ref):
    x = x_ref[...]
    r = lax.rsqrt(jnp.mean(x * x, axis=-1, keepdims=True) + NORM_EPS)
    h = (x * r) * g_ref[...]
    o_ref[...] = (h * (1.0 + sc_ref[0]) + sh_ref[0]).astype(o_ref.dtype)


def _norm_mod(x, g, sc, sh, seq_len):
    T, D = x.shape
    tm = 1024
    per_batch = seq_len // tm
    return pl.pallas_call(
        _norm_mod_kernel,
        out_shape=jax.ShapeDtypeStruct((T, D), BF16),
        grid=(T // tm,),
        in_specs=[pl.BlockSpec((tm, D), lambda i: (i, 0)),
                  pl.BlockSpec((1, D), lambda i: (0, 0)),
                  pl.BlockSpec((1, 1, D), lambda i: (i // per_batch, 0, 0)),
                  pl.BlockSpec((1, 1, D), lambda i: (i // per_batch, 0, 0))],
        out_specs=pl.BlockSpec((tm, D), lambda i: (i, 0)),
        compiler_params=_params("arbitrary"),
        name="norm_mod",
    )(x, g, sc, sh)


def _proj_kernel(h_ref, w_ref, o_ref):
    o_ref[...] = jnp.dot(h_ref[...], w_ref[0], preferred_element_type=F32).astype(o_ref.dtype)


def _proj(h, w, layer, col0, n_cols, tn):
    T, D = h.shape
    tm = PROJ_TM
    assert col0 % tn == 0 and n_cols % tn == 0
    off = col0 // tn
    return pl.pallas_call(
        _proj_kernel,
        out_shape=jax.ShapeDtypeStruct((T, n_cols), BF16),
        grid=(T // tm, n_cols // tn),
        in_specs=[pl.BlockSpec((tm, D), lambda i, j: (i, 0)),
                  pl.BlockSpec((1, D, tn), lambda i, j: (layer, 0, off + j))],
        out_specs=pl.BlockSpec((tm, tn), lambda i, j: (i, j)),
        compiler_params=_params("arbitrary", "arbitrary"),
        name="proj",
    )(h, w)


def _proj_qkv_kernel(h_ref, wq_ref, wk_ref, wv_ref, cos_ref, sin_ref, rot_ref, o_ref):
    tm, tn = h_ref.shape[0], PROJ_TN
    rot = rot_ref[...]
    for c in range(tm // PROJ_CHUNK):
        rows = slice(c * PROJ_CHUNK, (c + 1) * PROJ_CHUNK)
        hc = h_ref[rows, :]
        cos = jnp.concatenate([cos_ref[rows, :], cos_ref[rows, :]], axis=1)
        sin = jnp.concatenate([sin_ref[rows, :], sin_ref[rows, :]], axis=1)

        def roped(w_ref):
            acc = jnp.dot(hc, w_ref[0], preferred_element_type=F32)
            swapped = jnp.dot(acc.astype(BF16), rot, preferred_element_type=F32)
            return acc * cos + swapped * sin

        o_ref[rows, 0:tn] = (roped(wq_ref) * HEAD_DIM ** -0.5).astype(o_ref.dtype)
        o_ref[rows, tn:2 * tn] = roped(wk_ref).astype(o_ref.dtype)
        o_ref[rows, 2 * tn:3 * tn] = jnp.dot(hc, wv_ref[0], preferred_element_type=F32).astype(o_ref.dtype)


def _proj_qkv(h, w, layer, group, rope, seq_len, out_dtype):
    T, D = h.shape
    tm, tn = PROJ_TM, PROJ_TN
    n_groups = len(ATT_GROUPS)
    off = HY_COLS // tn + group
    cos_t, sin_t, rot = rope
    per_batch = seq_len // tm
    tab_spec = pl.BlockSpec((tm, LANES), lambda i: (i % per_batch, 0))
    w_spec = lambda kind: pl.BlockSpec((1, D, tn), lambda i: (layer, 0, off + n_groups * kind))
    return pl.pallas_call(
        _proj_qkv_kernel,
        out_shape=jax.ShapeDtypeStruct((T, 3 * tn), out_dtype),
        grid=(T // tm,),
        in_specs=[pl.BlockSpec((tm, D), lambda i: (i, 0)), w_spec(0), w_spec(1), w_spec(2),
                  tab_spec, tab_spec, pl.BlockSpec((tn, tn), lambda i: (0, 0))],
        out_specs=pl.BlockSpec((tm, 3 * tn), lambda i: (i, 0)),
        compiler_params=_params("arbitrary"),
        name="proj_qkv",
    )(h, w, w, w, cos_t, sin_t, rot)


def _rope_tables(seq_len):
    half = HEAD_DIM // 2
    inv = ROPE_THETA ** (-jnp.arange(half, dtype=F32) / half)
    ang = jnp.arange(seq_len, dtype=F32)[:, None] * inv[None, :]
    reps = LANES // half
    cos_t = jnp.tile(jnp.cos(ang), (1, reps))
    sin_t = jnp.tile(jnp.sin(ang), (1, reps))
    rot = np.zeros((PROJ_TN, PROJ_TN), np.float32)
    for j in range(PROJ_TN):
        if j % HEAD_DIM < half:
            rot[j + half, j] = -1.0
        else:
            rot[j - half, j] = 1.0
    return cos_t, sin_t, jnp.asarray(rot, BF16)


def _hyena_filter_kernel(feat_ref, w1_ref, b1_ref, w2_ref, b2_ref, w3_ref, b3_ref, w4_ref, fr_ref,
                         dl_ref, o_ref):
    i = pl.program_id(1)
    tm = feat_ref.shape[0]
    z = feat_ref[...]
    fr = fr_ref[0]
    h = jnp.sin(fr * (jnp.dot(z, w1_ref[0], precision=HIGHEST, preferred_element_type=F32) + b1_ref[0]))
    h = jnp.sin(fr * (jnp.dot(h, w2_ref[0], precision=HIGHEST, preferred_element_type=F32) + b2_ref[0]))
    h = jnp.sin(fr * (jnp.dot(h, w3_ref[0], precision=HIGHEST, preferred_element_type=F32) + b3_ref[0]))
    k = jnp.dot(h, w4_ref[0], precision=HIGHEST, preferred_element_type=F32)
    decay = jnp.exp(-z[:, 0:1] * jnp.abs(dl_ref[...]))
    pos = i * tm + lax.broadcasted_iota(jnp.int32, (tm, 1), 0)
    is_bwd = lax.broadcasted_iota(jnp.int32, (1, k.shape[1]), 1) >= HY_WIDTH
    _store_time_padded(o_ref.at[0], jnp.where((pos == 0) & is_bwd, 0.0, k * decay))


def _hyena_filter(feat, w1, b1, w2, b2, w3, b3, w4, freq, deltas2):
    n_rows = feat.shape[0]
    depth = w1.shape[0]
    C2 = 2 * HY_WIDTH
    tm = 1024
    per_layer = lambda r, c: pl.BlockSpec((1, r, c), lambda l, i: (l, 0, 0))
    shape = _time_padded_shape(n_rows, C2)
    return pl.pallas_call(
        _hyena_filter_kernel,
        out_shape=jax.ShapeDtypeStruct((depth,) + shape.shape, shape.dtype),
        grid=(depth, n_rows // tm),
        in_specs=[pl.BlockSpec((tm, LANES), lambda l, i: (i, 0)),
                  per_layer(LANES, LANES), per_layer(1, LANES),
                  per_layer(LANES, LANES), per_layer(1, LANES),
                  per_layer(LANES, LANES), per_layer(1, LANES),
                  per_layer(LANES, C2),
                  per_layer(1, LANES),
                  pl.BlockSpec((1, C2), lambda l, i: (0, 0))],
        out_specs=pl.BlockSpec((1, tm // FFT_R, FFT_T_PITCH, C2), lambda l, i: (l, i, 0, 0)),
        compiler_params=_params("arbitrary", "arbitrary"),
        name="hyena_filter",
    )(feat, w1, b1, w2, b2, w3, b3, w4, freq, deltas2)


def _hyena_features(seq_len):
    L = seq_len
    t = jnp.linspace(0.0, 1.0, L, dtype=F32)[:, None]
    bands = (HY_EMB - 1) // 2
    w = 2.0 * math.pi * jnp.arange(L, dtype=F32)[:, None] / L
    f = jnp.linspace(1e-4, bands - 1, bands, dtype=F32)[None, :]
    z = jnp.concatenate([t, jnp.cos(f * w), -jnp.sin(f * w)], axis=-1)
    z = jnp.pad(z, ((0, 0), (0, LANES - HY_EMB)))
    max_decay = math.log(HY_TARGET) / HY_FAST_DECAY_PCT
    min_decay = math.log(HY_TARGET) / HY_SLOW_DECAY_PCT
    deltas = jnp.linspace(min_decay, max_decay, HY_WIDTH, dtype=F32)[None, :]
    return z, jnp.concatenate([deltas, deltas], axis=1)


def _dft_matrices():
    R, KH, KP = FFT_R, FFT_KH, FFT_KP
    N = R * R
    n1 = np.arange(R)[:, None, None]
    k2 = np.arange(KP)[None, :, None]
    n2 = np.arange(R)[None, None, :]
    phase = 2.0 * np.pi * ((n2 * k2 % R) / R + (n1 * k2) / N)
    live = (k2 < KH)
    a1 = np.concatenate([np.cos(phase) * live, -np.sin(phase) * live], axis=1)
    wgt = np.where((k2 == 0) | (k2 == R // 2), 1.0, 2.0) * live / N
    b1 = np.concatenate([np.cos(phase) * wgt, -np.sin(phase) * wgt], axis=1)
    b1 = np.transpose(b1, (0, 2, 1))[:, :R // 2, :]
    th = 2.0 * np.pi * (np.arange(R)[:, None] * np.arange(R)[None, :] % R) / R
    c, s = np.cos(th), np.sin(th)
    w2f = np.block([[c, s], [-s, c]])
    w2i = np.block([[c, -s], [s, c]])
    as_bf = lambda a: jnp.asarray(a.astype(np.float32)).astype(BF16)
    return dict(a1=as_bf(a1[:, :, :R // 2]), b1=as_bf(b1), w2f=as_bf(w2f), w2i=as_bf(w2i))


def _fft_stage1(src_ref, a1_ref, g_ref, n_rows):
    def body(n1, carry):
        xs = src_ref[pl.ds(n1, n_rows, stride=FFT_T_PITCH), :].astype(BF16)
        slab = jnp.dot(a1_ref[n1], xs, preferred_element_type=F32)
        g_ref[pl.ds(pl.multiple_of(n1 * FFT_G_PITCH, 8), FFT_SLAB), :] = slab
        return carry
    lax.fori_loop(0, FFT_R, body, 0, unroll=FFT_UNROLL_TIME)


def _load_freq_rows(g_ref, k2):
    re = g_ref[pl.ds(k2, FFT_R, stride=FFT_G_PITCH), :]
    im = g_ref[pl.ds(FFT_KP + k2, FFT_R, stride=FFT_G_PITCH), :]
    return jnp.concatenate([re, im], axis=0)


def _filter_fft_kernel(fwd_ref, bwd_ref, a1_ref, w2_ref, o_ref, gf_ref, gb_ref):
    n2_rows = fwd_ref.shape[1] // FFT_T_PITCH
    cb = fwd_ref.shape[2]
    fwd, bwd = fwd_ref.at[0], bwd_ref.at[0]

    def stage1(n1, carry):
        rows = pl.ds(n1, n2_rows, stride=FFT_T_PITCH)
        xs = jnp.concatenate([fwd[rows, :], bwd[rows, :]], axis=1).astype(BF16)
        slab = jnp.dot(a1_ref[n1], xs, preferred_element_type=F32)
        dst = pl.ds(pl.multiple_of(n1 * FFT_G_PITCH, 8), FFT_SLAB)
        gf_ref[dst, :] = slab[:, :cb]
        gb_ref[dst, :] = slab[:, cb:]
        return carry
    lax.fori_loop(0, FFT_R, stage1, 0, unroll=FFT_UNROLL_TIME)

    def body(k2, carry):
        gk = jnp.concatenate([_load_freq_rows(gf_ref, k2), _load_freq_rows(gb_ref, k2)], axis=1).astype(BF16)
        x = jnp.dot(w2_ref[...], gk, preferred_element_type=F32)
        f, b = x[:, :cb], x[:, cb:]
        spec = jnp.concatenate([f[:FFT_R] + b[:FFT_R], f[FFT_R:] - b[FFT_R:]], axis=0)
        o_ref[0, k2] = spec.astype(o_ref.dtype)
        return carry
    lax.fori_loop(0, FFT_KH, body, 0, unroll=FFT_UNROLL_FREQ)


def _filter_fft(k2, mats):
    depth, n_rows, _ = k2.shape
    C = HY_WIDTH
    cb = CONV_CB
    ncb = C // cb
    return pl.pallas_call(
        _filter_fft_kernel,
        out_shape=jax.ShapeDtypeStruct((depth, FFT_KH, 2 * FFT_R, C), BF16),
        grid=(depth, ncb),
        in_specs=[pl.BlockSpec((1, n_rows, cb), lambda l, j: (l, 0, j)),
                  pl.BlockSpec((1, n_rows, cb), lambda l, j: (l, 0, ncb + j)),
                  pl.BlockSpec((FFT_R, FFT_SLAB, FFT_R // 2), lambda l, j: (0, 0, 0)),
                  pl.BlockSpec((2 * FFT_R, 2 * FFT_R), lambda l, j: (0, 0))],
        out_specs=pl.BlockSpec((1, FFT_KH, 2 * FFT_R, cb), lambda l, j: (l, 0, 0, j)),
        scratch_shapes=[pltpu.VMEM((FFT_R * FFT_G_PITCH, cb), F32), pltpu.VMEM((FFT_R * FFT_G_PITCH, cb), F32)],
        compiler_params=_params("arbitrary", "arbitrary"),
        name="hyena_filter_fft",
    )(k2, k2, mats["a1"], mats["w2f"])


SC_HALO = 16


def _long_conv_kernel(x0_ref, x1_ref, v_ref, w0_ref, w1_ref, wv_ref, b0_ref, b1s_ref, bv_ref, kf_ref, bias_ref,
                      a1_ref, b1_ref, w2f_ref, w2i_ref, o_ref, g_ref, zv_ref, y_ref, stage_ref):
    seq_len = x0_ref.shape[1]
    n_groups = seq_len // FFT_R

    def short_conv(part, u_ref, w_ref, b_ref, g):
        row0 = pl.multiple_of(g * FFT_R, FFT_R)
        lo = pl.multiple_of(jnp.maximum(row0 - SC_HALO, 0), SC_HALO)
        hi = pl.multiple_of(jnp.minimum(row0 + FFT_R, seq_len - SC_HALO), SC_HALO)
        st = stage_ref.at[part]
        st[0:SC_HALO, :] = jnp.where(g > 0, u_ref[0, pl.ds(lo, SC_HALO), :].astype(F32), 0.0)
        st[SC_HALO:SC_HALO + FFT_R, :] = u_ref[0, pl.ds(row0, FFT_R), :].astype(F32)
        st[SC_HALO + FFT_R:, :] = jnp.where(g < n_groups - 1, u_ref[0, pl.ds(hi, SC_HALO), :].astype(F32), 0.0)
        w = w_ref[...]
        return (st[SC_HALO - 1:SC_HALO - 1 + FFT_R, :] * w[0:1] + st[SC_HALO:SC_HALO + FFT_R, :] * w[1:2]
                + st[SC_HALO + 1:SC_HALO + 1 + FFT_R, :] * w[2:3] + b_ref[...])

    def gate_body(g, carry):
        zv = short_conv(2, v_ref, wv_ref, bv_ref, g) * short_conv(1, x1_ref, w1_ref, b1s_ref, g)
        zv_ref[pl.ds(pl.multiple_of(g * FFT_T_PITCH, 8), FFT_R), :] = zv
        return carry
    lax.fori_loop(0, n_groups, gate_body, 0, unroll=2)

    _fft_stage1(zv_ref, a1_ref, g_ref, n_groups)

    def freq_body(k2, carry):
        gk = _load_freq_rows(g_ref, k2).astype(BF16)
        x = jnp.dot(w2f_ref[...], gk, preferred_element_type=F32)
        kf = kf_ref[0, k2].astype(F32)
        xr, xi = x[:FFT_R], x[FFT_R:]
        kr, ki = kf[:FFT_R], kf[FFT_R:]
        p = jnp.concatenate([xr * kr - xi * ki, xr * ki + xi * kr], axis=0).astype(BF16)
        hk = jnp.dot(w2i_ref[...], p, preferred_element_type=F32)
        g_ref[pl.ds(k2, FFT_R, stride=FFT_G_PITCH), :] = hk[:FFT_R]
        g_ref[pl.ds(FFT_KP + k2, FFT_R, stride=FFT_G_PITCH), :] = hk[FFT_R:]
        return carry
    lax.fori_loop(0, FFT_KH, freq_body, 0, unroll=FFT_UNROLL_FREQ)

    def time_body(n1, carry):
        slab = g_ref[pl.ds(pl.multiple_of(n1 * FFT_G_PITCH, 8), FFT_SLAB), :].astype(BF16)
        y_ref[pl.ds(n1, n_groups, stride=FFT_T_PITCH), :] = jnp.dot(b1_ref[n1], slab, preferred_element_type=F32)
        return carry
    lax.fori_loop(0, FFT_R, time_body, 0, unroll=FFT_UNROLL_TIME)

    bias = bias_ref[...]

    def out_body(g, carry):
        rows = pl.ds(pl.multiple_of(g * FFT_T_PITCH, 8), FFT_R)
        x0 = short_conv(0, x0_ref, w0_ref, b0_ref, g)
        y = (y_ref[rows, :] + zv_ref[rows, :] * bias) * x0
        o_ref[0, pl.ds(pl.multiple_of(g * FFT_R, FFT_R), FFT_R), :] = y.astype(o_ref.dtype)
        return carry
    lax.fori_loop(0, n_groups, out_body, 0, unroll=2)


def _long_conv(u, w_sc, b_sc, kf, layer, hy_bias, mats):
    B, L, _ = u.shape
    C = HY_WIDTH
    cb = CONV_CB
    ncb = C // cb
    lp = L // FFT_R * FFT_T_PITCH
    part = lambda p: pl.BlockSpec((1, L, cb), lambda j, b: (b, 0, p * ncb + j))
    wpart = lambda p: pl.BlockSpec((3, cb), lambda j, b: (0, p * ncb + j))
    bpart = lambda p: pl.BlockSpec((1, cb), lambda j, b: (0, p * ncb + j))
    b2 = b_sc.reshape(1, 3 * C)
    return pl.pallas_call(
        _long_conv_kernel,
        out_shape=jax.ShapeDtypeStruct((B, L, C), BF16),
        grid=(ncb, B),
        in_specs=[part(0), part(1), part(2), wpart(0), wpart(1), wpart(2), bpart(0), bpart(1), bpart(2),
                  pl.BlockSpec((1, FFT_KH, 2 * FFT_R, cb), lambda j, b: (layer, 0, 0, j)),
                  pl.BlockSpec((1, cb), lambda j, b: (0, j)),
                  pl.BlockSpec((FFT_R, FFT_SLAB, FFT_R // 2), lambda j, b: (0, 0, 0)),
                  pl.BlockSpec((FFT_R, FFT_R // 2, FFT_SLAB), lambda j, b: (0, 0, 0)),
                  pl.BlockSpec((2 * FFT_R, 2 * FFT_R), lambda j, b: (0, 0)),
                  pl.BlockSpec((2 * FFT_R, 2 * FFT_R), lambda j, b: (0, 0))],
        out_specs=pl.BlockSpec((1, L, cb), lambda j, b: (b, 0, j)),
        scratch_shapes=[pltpu.VMEM((FFT_R * FFT_G_PITCH, cb), F32), pltpu.VMEM((lp, cb), F32),
                        pltpu.VMEM((lp, cb), F32), pltpu.VMEM((3, FFT_R + 2 * SC_HALO, cb), F32)],
        compiler_params=_params("arbitrary", "arbitrary"),
        name="hyena_long_conv",
    )(u, u, u, w_sc, w_sc, w_sc, b2, b2, b2, kf, hy_bias.reshape(1, C),
      mats["a1"], mats["b1"], mats["w2f"], mats["w2i"])


ATT_TQ = 128
ATT_RADIUS = 64
ATT_WINDOW = ATT_TQ + 2 * ATT_RADIUS


ATT_SUB = 4


def _attn_kernel(q_ref, k_ref, v_ref, o_ref, lse_ref):
    i = pl.program_id(2)
    ls = k_ref.shape[1]
    nh = HEADS_PER_GROUP
    head_of_col = lax.broadcasted_iota(jnp.int32, (1, ATT_OUT), 1) // HEAD_DIM
    row_iota = lax.broadcasted_iota(jnp.int32, (nh * ATT_TQ, ATT_WINDOW), 0) % ATT_TQ
    col_iota = lax.broadcasted_iota(jnp.int32, (nh * ATT_TQ, ATT_WINDOW), 1)
    rel = row_iota - col_iota
    for sub in range(ATT_SUB):
        q0 = (i * ATT_SUB + sub) * ATT_TQ
        rows = slice(sub * ATT_TQ, (sub + 1) * ATT_TQ)
        q = q_ref[0, rows, :]
        start = jnp.clip(q0 - ATT_RADIUS, 0, ls - ATT_WINDOW)
        start = pl.multiple_of(start, ATT_RADIUS)
        kw = k_ref[0, pl.ds(start, ATT_WINDOW), :]
        vw = v_ref[0, pl.ds(start, ATT_WINDOW), :]
        out, lse = _attn_unit(q, kw, vw, rel, q0 - start, head_of_col)
        o_ref[0, rows, :] = out.astype(o_ref.dtype)
        lse_ref[0, rows, :] = lse


def _attn_unit(q, kw, vw, rel, q_minus_start, head_of_col):
    nh = HEADS_PER_GROUP
    band = jnp.abs(q_minus_start + rel) <= ATT_RADIUS
    zero = jnp.zeros_like(q)
    q4 = jnp.concatenate([jnp.where(head_of_col == h, q, zero) for h in range(nh)], axis=0)
    s = lax.dot_general(q4, kw, (((1,), (1,)), ((), ())), preferred_element_type=F32)
    s = jnp.where(band, s, MASK_VALUE)
    m = jnp.max(s, axis=-1, keepdims=True)
    p = jnp.exp(s - m)
    den = jnp.sum(p, axis=-1, keepdims=True)
    pv = jnp.dot(p.astype(BF16), vw, preferred_element_type=F32) / den
    lse4 = m + jnp.log(den)
    out = jnp.zeros((ATT_TQ, ATT_OUT), F32)
    lse = jnp.zeros((ATT_TQ, ATT_OUT), F32)
    for h in range(nh):
        mine = head_of_col == h
        hrows = slice(h * ATT_TQ, (h + 1) * ATT_TQ)
        out = jnp.where(mine, pv[hrows], out)
        lse = jnp.where(mine, lse4[hrows], lse)
    return out, lse


ATT_CHUNK = 2048


def _attn_strided_kernel(q0_ref, q1_ref, k0_ref, k1_ref, v0_ref, v1_ref, o_ref, lse_ref, o_scr, lse_scr, *, dil):
    i = pl.program_id(1)
    ls = q0_ref.shape[1] // dil

    def gather(lo_ref, hi_ref, first, n):
        rows = pl.ds(first, n, stride=dil)
        return jnp.concatenate([lo_ref[0, rows, :], hi_ref[0, rows, :]], axis=1).astype(BF16)

    units = ATT_CHUNK // ATT_TQ
    shift = dil.bit_length() - 1
    head_of_col = lax.broadcasted_iota(jnp.int32, (1, ATT_OUT), 1) // HEAD_DIM
    row_iota = lax.broadcasted_iota(jnp.int32, (HEADS_PER_GROUP * ATT_TQ, ATT_WINDOW), 0) % ATT_TQ
    rel = row_iota - lax.broadcasted_iota(jnp.int32, (HEADS_PER_GROUP * ATT_TQ, ATT_WINDOW), 1)

    def unit(u, carry):
        r = u & (dil - 1)
        sb = u >> shift
        m0 = i * (ATT_CHUNK // dil) + sb * ATT_TQ
        start = jnp.clip(m0 - ATT_RADIUS, 0, ls - ATT_WINDOW)
        q = gather(q0_ref, q1_ref, m0 * dil + r, ATT_TQ)
        kw = gather(k0_ref, k1_ref, start * dil + r, ATT_WINDOW)
        vw = gather(v0_ref, v1_ref, start * dil + r, ATT_WINDOW)
        out, lse = _attn_unit(q, kw, vw, rel, m0 - start, head_of_col)
        dst = pl.ds(sb * ATT_TQ * dil + r, ATT_TQ, stride=dil)
        for half in range(2):
            lanes = slice(half * LANES, (half + 1) * LANES)
            o_scr[half, dst, :] = out[:, lanes]
            lse_scr[half, dst, :] = lse[:, lanes]
        return carry
    lax.fori_loop(0, units, unit, 0, unroll=2)
    for half in range(2):
        lanes = slice(half * LANES, (half + 1) * LANES)
        o_ref[0, :, lanes] = o_scr[half]
        lse_ref[0, :, lanes] = lse_scr[half]


def _attention_strided(qkv_g, dil, batch, seq_len):
    view = qkv_g.reshape(batch, seq_len, 3 * ATT_OUT)
    halves = [pl.BlockSpec((1, seq_len, LANES), lambda b, i, c=c: (b, 0, c), pipeline_mode=pl.Buffered(1))
              for c in range(3 * ATT_OUT // LANES)]
    o_spec = pl.BlockSpec((1, ATT_CHUNK, ATT_OUT), lambda b, i: (b, i, 0))
    kern = lambda *refs: _attn_strided_kernel(*refs, dil=dil)
    o, lse = pl.pallas_call(
        kern,
        out_shape=(jax.ShapeDtypeStruct((batch, seq_len, ATT_OUT), F32),
                   jax.ShapeDtypeStruct((batch, seq_len, ATT_OUT), F32)),
        grid=(batch, seq_len // ATT_CHUNK),
        in_specs=halves,
        out_specs=(o_spec, o_spec),
        scratch_shapes=[pltpu.VMEM((2, ATT_CHUNK, LANES), F32), pltpu.VMEM((2, ATT_CHUNK, LANES), F32)],
        compiler_params=_params("arbitrary", "arbitrary"),
        name=f"dilated_attn_d{dil}",
    )(*([view] * len(halves)))
    T = batch * seq_len
    return o.reshape(T, ATT_OUT), lse.reshape(T, ATT_OUT)


def _attention_group(qkv_g, dil, batch, seq_len):
    ls = seq_len // dil
    tq = ATT_SUB * ATT_TQ
    view = qkv_g.reshape(batch, ls, dil * 3 * ATT_OUT)

    def col(which):
        return lambda b, r, i: (b, 0, r * 3 + which)

    q_map = lambda b, r, i: (b, i, r * 3)
    o_map = lambda b, r, i: (b, i, r)
    o, lse = pl.pallas_call(
        _attn_kernel,
        out_shape=(jax.ShapeDtypeStruct((batch, ls, dil * ATT_OUT), BF16),
                   jax.ShapeDtypeStruct((batch, ls, dil * ATT_OUT), F32)),
        grid=(batch, dil, ls // tq),
        in_specs=[pl.BlockSpec((1, tq, ATT_OUT), q_map),
                  pl.BlockSpec((1, ls, ATT_OUT), col(1)),
                  pl.BlockSpec((1, ls, ATT_OUT), col(2))],
        out_specs=(pl.BlockSpec((1, tq, ATT_OUT), o_map), pl.BlockSpec((1, tq, ATT_OUT), o_map)),
        compiler_params=_params("arbitrary", "arbitrary", "arbitrary"),
        name=f"dilated_attn_d{dil}",
    )(view, view, view)
    T = batch * seq_len
    return o.reshape(T, ATT_OUT), lse.reshape(T, ATT_OUT)


def _merge_kernel(x_ref, yhy_ref, o1_ref, o2_ref, o3_ref, l1_ref, l2_ref, l3_ref, h_ref, wg_ref,
                  wh_ref, wa_ref, wo_ref, g1_ref, out_ref):
    l1, l2, l3 = l1_ref[...], l2_ref[...], l3_ref[...]
    m = jnp.maximum(jnp.maximum(l1, l2), l3)
    e1, e2, e3 = jnp.exp(l1 - m), jnp.exp(l2 - m), jnp.exp(l3 - m)
    tot = e1 + e2 + e3
    y_at = (e1 * o1_ref[...].astype(F32) + e2 * o2_ref[...].astype(F32) + e3 * o3_ref[...].astype(F32)) / tot
    a = jnp.dot(yhy_ref[...], wh_ref[0], preferred_element_type=F32)
    b = jnp.dot(y_at.astype(BF16), wa_ref[0], preferred_element_type=F32)
    d_model = x_ref.shape[1]
    gate = jnp.dot(h_ref[...], wg_ref[0], preferred_element_type=F32)
    merged = _sigmoid(gate[:, :d_model]) * a + _sigmoid(gate[:, d_model:]) * b
    upd = jnp.dot(merged.astype(BF16), wo_ref[0], preferred_element_type=F32)
    out_ref[...] = x_ref[...] + g1_ref[0] * upd


def _merge(x, y_hy, attn, h, w_gate, w_br_h, w_br_a, w_out, layer, g1, seq_len):
    T, D = x.shape
    tm = 512
    per_batch = seq_len // tm
    (o1, l1), (o2, l2), (o3, l3) = attn
    row = lambda w: pl.BlockSpec((tm, w), lambda i: (i, 0))
    full = lambda a: pl.BlockSpec((1,) + a.shape[1:], lambda i: (layer, 0, 0))
    return pl.pallas_call(
        _merge_kernel,
        out_shape=jax.ShapeDtypeStruct((T, D), F32),
        grid=(T // tm,),
        in_specs=[row(D), row(HY_WIDTH), row(ATT_OUT), row(ATT_OUT), row(ATT_OUT),
                  row(ATT_OUT), row(ATT_OUT), row(ATT_OUT),
                  row(D), full(w_gate), full(w_br_h), full(w_br_a), full(w_out),
                  pl.BlockSpec((1, 1, D), lambda i: (i // per_batch, 0, 0))],
        out_specs=row(D),
        compiler_params=_params("arbitrary"),
        name="mixer_merge",
    )(x, y_hy, o1, o2, o3, l1, l2, l3, h, w_gate, w_br_h, w_br_a, w_out, g1)


def _router_kernel(x_ref, g_ref, sc_ref, sh_ref, rwt_ref, rb_ref, hg_ref, stats_ref, cnt_ref, base_ref):
    i = pl.program_id(0)
    tm = x_ref.shape[0]

    @pl.when(i == 0)
    def _():
        base_ref[...] = jnp.zeros_like(base_ref)

    x = x_ref[...]
    r = lax.rsqrt(jnp.mean(x * x, axis=-1, keepdims=True) + NORM_EPS)
    h = (x * r) * g_ref[...]
    h = h * (1.0 + sc_ref[0]) + sh_ref[0]

    logits = lax.dot_general(rwt_ref[...], h, (((1,), (1,)), ((), ())),
                             precision=HIGHEST, preferred_element_type=F32)
    scores = _sigmoid(logits)
    biased = scores + rb_ref[...]

    def row(a, k):
        return a[k:k + 1, :]

    sel = jnp.zeros((1, tm), jnp.int32)
    best = None
    for g in range(N_GROUPS):
        a, b, c, d = (row(biased, 4 * g + k) for k in range(4))
        m_ab, n_ab = jnp.maximum(a, b), jnp.minimum(a, b)
        m_cd, n_cd = jnp.maximum(c, d), jnp.minimum(c, d)
        gs = jnp.maximum(m_ab, m_cd) + jnp.maximum(jnp.minimum(m_ab, m_cd), jnp.maximum(n_ab, n_cd))
        if g == 0:
            best = gs
        else:
            better = gs > best
            sel = jnp.where(better, g, sel)
            best = jnp.where(better, gs, best)

    v, u = [], []
    for k in range(EXPERTS_PER_GROUP):
        vk = jnp.zeros((1, tm), F32)
        uk = jnp.zeros((1, tm), F32)
        for g in range(N_GROUPS):
            vk = jnp.where(sel == g, row(biased, 4 * g + k), vk)
            uk = jnp.where(sel == g, row(scores, 4 * g + k), uk)
        v.append(vk)
        u.append(uk)

    i1 = jnp.zeros((1, tm), jnp.int32)
    b1 = v[0]
    for k in range(1, EXPERTS_PER_GROUP):
        gt = v[k] > b1
        i1 = jnp.where(gt, k, i1)
        b1 = jnp.where(gt, v[k], b1)
    i2 = jnp.zeros((1, tm), jnp.int32)
    b2 = jnp.full((1, tm), -jnp.inf, F32)
    for k in range(EXPERTS_PER_GROUP):
        cand = (i1 != k) & (v[k] > b2)
        i2 = jnp.where(cand, k, i2)
        b2 = jnp.where(cand, v[k], b2)

    lo = jnp.minimum(i1, i2)
    hi = jnp.maximum(i1, i2)
    pair = jnp.where(lo == 0, hi - 1, jnp.where(lo == 1, hi + 1, 5))
    bucket = sel * PAIRS_PER_GROUP + pair

    u_lo = jnp.zeros((1, tm), F32)
    u_hi = jnp.zeros((1, tm), F32)
    for k in range(EXPERTS_PER_GROUP):
        u_lo = jnp.where(lo == k, u[k], u_lo)
        u_hi = jnp.where(hi == k, u[k], u_hi)
    tot = u_lo + u_hi
    w_lo = u_lo / tot
    w_hi = u_hi / tot

    rows = lax.broadcasted_iota(jnp.int32, (BUCKET_ROWS, tm), 0)
    onehot = (rows == bucket).astype(F32)
    t_src = lax.broadcasted_iota(jnp.int32, (tm, tm), 0)
    t_dst = lax.broadcasted_iota(jnp.int32, (tm, tm), 1)
    before = (t_src < t_dst).astype(BF16)
    cum = jnp.dot(onehot.astype(BF16), before, preferred_element_type=F32)
    base = base_ref[...]
    rank = jnp.sum(onehot * (cum + base), axis=0, keepdims=True)
    base = base + jnp.sum(onehot, axis=1, keepdims=True)
    base_ref[...] = base
    cnt_ref[...] = jnp.broadcast_to(base, cnt_ref.shape)

    srow = lax.broadcasted_iota(jnp.int32, (8, tm), 0)
    stats_ref[...] = jnp.where(srow == 0, bucket.astype(F32), jnp.where(srow == 1, rank, 0.0))

    grow = lax.broadcasted_iota(jnp.int32, (GATE_COLS, tm), 0)
    gates_t = jnp.where(grow == 0, w_lo, jnp.where(grow == 1, w_hi, 0.0))
    hg_ref[:, :D_MODEL] = h
    hg_ref[:, D_MODEL:] = gates_t.T


def _router(x, norm_g, sc, sh, router_wt, router_b, seq_len):
    T, D = x.shape
    tm = ROUTER_TM
    per_batch = seq_len // tm
    return pl.pallas_call(
        _router_kernel,
        out_shape=(jax.ShapeDtypeStruct((T, D + GATE_COLS), F32),
                   jax.ShapeDtypeStruct((8, T), F32),
                   jax.ShapeDtypeStruct((BUCKET_ROWS, LANES), F32)),
        grid=(T // tm,),
        in_specs=[pl.BlockSpec((tm, D), lambda i: (i, 0)),
                  pl.BlockSpec((1, D), lambda i: (0, 0)),
                  pl.BlockSpec((1, 1, D), lambda i: (i // per_batch, 0, 0)),
                  pl.BlockSpec((1, 1, D), lambda i: (i // per_batch, 0, 0)),
                  pl.BlockSpec((N_EXPERTS, D), lambda i: (0, 0)),
                  pl.BlockSpec((N_EXPERTS, 1), lambda i: (0, 0))],
        out_specs=(pl.BlockSpec((tm, D + GATE_COLS), lambda i: (i, 0)),
                   pl.BlockSpec((8, tm), lambda i: (0, i)),
                   pl.BlockSpec((BUCKET_ROWS, LANES), lambda i: (0, 0))),
        scratch_shapes=[pltpu.VMEM((BUCKET_ROWS, 1), F32)],
        compiler_params=_params("arbitrary"),
        name="moe_router",
    )(x, norm_g, sc, sh, router_wt, router_b)


def _start_row_copies(n_rows, make_copy):
    group = 8

    def body(g, carry):
        base = pl.multiple_of(g * group, group)
        for k in range(group):
            make_copy(base + k).start(priority=k % 2)
        return carry

    lax.fori_loop(0, n_rows // group, body, 0)


def _dispatch_kernel(pos_ref, hg_ref, xs_init_hbm, xs_hbm, sem):
    del xs_init_hbm
    _start_row_copies(ROW_BLOCK, lambda r: pltpu.make_async_copy(
        hg_ref.at[pl.ds(r, 1), :], xs_hbm.at[pl.ds(pos_ref[r], 1), :], sem))
    pltpu.make_async_copy(hg_ref, xs_hbm.at[pl.ds(0, ROW_BLOCK), :], sem).wait()


def _dispatch(pos, hg, n_rows):
    T, W = hg.shape
    zeros = jnp.zeros((n_rows, W), F32)
    return pl.pallas_call(
        _dispatch_kernel,
        out_shape=jax.ShapeDtypeStruct((n_rows, W), F32),
        grid=(T // ROW_BLOCK,),
        in_specs=[pl.BlockSpec((ROW_BLOCK,), lambda i: (i,), memory_space=pltpu.SMEM),
                  pl.BlockSpec((ROW_BLOCK, W), lambda i: (i, 0)),
                  pl.BlockSpec(memory_space=pl.ANY)],
        out_specs=pl.BlockSpec(memory_space=pl.ANY),
        scratch_shapes=[pltpu.SemaphoreType.DMA],
        input_output_aliases={2: 0},
        compiler_params=_params("arbitrary"),
        name="moe_dispatch",
    )(pos, hg, zeros)


def _expert_kernel(e_lo_ref, e_hi_ref, n_used_ref, xs_ref, w1a, w3a, w2a, w1b, w3b, w2b, y_ref):
    del e_lo_ref, e_hi_ref
    used = pl.program_id(0) < n_used_ref[0]

    @pl.when(jnp.logical_not(used))
    def _():
        y_ref[...] = jnp.zeros_like(y_ref)

    @pl.when(used)
    def _():
        xb = xs_ref[:, :D_MODEL].astype(BF16)
        g_lo = xs_ref[:, D_MODEL:D_MODEL + 1]
        g_hi = xs_ref[:, D_MODEL + 1:D_MODEL + 2]

        def ffn(w1, w3, w2):
            a = jnp.dot(xb, w1[0, 0], preferred_element_type=F32)
            b = jnp.dot(xb, w3[0, 0], preferred_element_type=F32)
            act = (a * _sigmoid(a)) * b
            return jnp.dot(act.astype(BF16), w2[0, 0], preferred_element_type=F32)

        y_ref[...] = g_lo * ffn(w1a, w3a, w2a) + g_hi * ffn(w1b, w3b, w2b)


def _experts(tile_lo, tile_hi, n_used, xs, w1, w3, w2, layer):
    n_rows, W = xs.shape
    D, F = w1.shape[2], w1.shape[3]
    n_tiles = n_rows // EXPERT_TM

    def x_map(j, lo, hi, nu):
        return (jnp.minimum(j, nu[0] - 1), 0)

    def w_lo_map(j, lo, hi, nu):
        return (layer, lo[j], 0, 0)

    def w_hi_map(j, lo, hi, nu):
        return (layer, hi[j], 0, 0)

    grid_spec = pltpu.PrefetchScalarGridSpec(
        num_scalar_prefetch=3,
        grid=(n_tiles,),
        in_specs=[pl.BlockSpec((EXPERT_TM, W), x_map),
                  pl.BlockSpec((1, 1, D, F), w_lo_map), pl.BlockSpec((1, 1, D, F), w_lo_map),
                  pl.BlockSpec((1, 1, F, D), w_lo_map),
                  pl.BlockSpec((1, 1, D, F), w_hi_map), pl.BlockSpec((1, 1, D, F), w_hi_map),
                  pl.BlockSpec((1, 1, F, D), w_hi_map)],
        out_specs=pl.BlockSpec((EXPERT_TM, D), lambda j, lo, hi, nu: (j, 0)),
    )
    return pl.pallas_call(
        _expert_kernel,
        out_shape=jax.ShapeDtypeStruct((n_rows, D), F32),
        grid_spec=grid_spec,
        compiler_params=_params("arbitrary"),
        name="moe_experts",
    )(tile_lo, tile_hi, n_used, xs, w1, w3, w2, w1, w3, w2)


def _combine_kernel(pos_ref, x_ref, g2_ref, ng_ref, sc_ref, sh_ref, ys_hbm, *refs):
    *out_refs, buf, sem = refs
    _start_row_copies(ROW_BLOCK, lambda r: pltpu.make_async_copy(
        ys_hbm.at[pl.ds(pos_ref[r], 1), :], buf.at[pl.ds(r, 1), :], sem))
    pltpu.make_async_copy(ys_hbm.at[pl.ds(0, ROW_BLOCK), :], buf, sem).wait()
    xn = x_ref[...] + g2_ref[0] * buf[...]
    r = lax.rsqrt(jnp.mean(xn * xn, axis=-1, keepdims=True) + NORM_EPS)
    h = (xn * r) * ng_ref[...]
    if len(out_refs) == 2:
        out_refs[0][...] = xn
        h = h * (1.0 + sc_ref[0]) + sh_ref[0]
    out_refs[-1][...] = h.astype(out_refs[-1].dtype)


def _combine(pos, x, g2, ys, norm_g, sc, sh, seq_len, last):
    T, D = x.shape
    per_batch = seq_len // ROW_BLOCK
    row = pl.BlockSpec((ROW_BLOCK, D), lambda i: (i, 0))
    per_b = pl.BlockSpec((1, 1, D), lambda i: (i // per_batch, 0, 0))
    if last:
        out_shape, out_specs = jax.ShapeDtypeStruct((T, D), F32), row
    else:
        out_shape = (jax.ShapeDtypeStruct((T, D), F32), jax.ShapeDtypeStruct((T, D), BF16))
        out_specs = (row, row)
    return pl.pallas_call(
        _combine_kernel,
        out_shape=out_shape,
        grid=(T // ROW_BLOCK,),
        in_specs=[pl.BlockSpec((ROW_BLOCK,), lambda i: (i,), memory_space=pltpu.SMEM),
                  row, per_b, pl.BlockSpec((1, D), lambda i: (0, 0)), per_b, per_b,
                  pl.BlockSpec(memory_space=pl.ANY)],
        out_specs=out_specs,
        scratch_shapes=[pltpu.VMEM((ROW_BLOCK, D), F32), pltpu.SemaphoreType.DMA],
        compiler_params=_params("arbitrary"),
        name="moe_combine",
    )(pos, x, g2, norm_g, sc, sh, ys)


def _moe_layer(x, norm_g, sc2, sh2, g2, router_wt, router_b, w1, w3, w2, layer, seq_len, next_norm, last):
    T, D = x.shape
    hg, stats, counts = _router(x, norm_g, sc2, sh2, router_wt, router_b, seq_len)

    cnt = counts[:N_BUCKETS, 0].astype(jnp.int32)
    padded = ((cnt + EXPERT_TM - 1) // EXPERT_TM) * EXPERT_TM
    ends = jnp.cumsum(padded)
    starts = ends - padded
    bucket = stats[0].astype(jnp.int32)
    pos = starts[bucket] + stats[1].astype(jnp.int32)
    n_tiles = T // EXPERT_TM + N_BUCKETS
    tile_row0 = jnp.arange(n_tiles, dtype=jnp.int32) * EXPERT_TM
    tile_bucket = jnp.sum((ends[None, :] <= tile_row0[:, None]).astype(jnp.int32), axis=1)
    tile_bucket = jnp.minimum(tile_bucket, N_BUCKETS - 1)
    grp, pair = tile_bucket // PAIRS_PER_GROUP, tile_bucket % PAIRS_PER_GROUP
    pair_lo = jnp.array([0, 0, 0, 1, 1, 2], jnp.int32)[pair]
    pair_hi = jnp.array([1, 2, 3, 2, 3, 3], jnp.int32)[pair]
    tile_lo = grp * EXPERTS_PER_GROUP + pair_lo
    tile_hi = grp * EXPERTS_PER_GROUP + pair_hi
    n_used = (ends[-1] // EXPERT_TM).astype(jnp.int32).reshape(1)

    xs = _dispatch(pos, hg, n_tiles * EXPERT_TM)
    ys = _experts(tile_lo, tile_hi, n_used, xs, w1, w3, w2, layer)
    return _combine(pos, x, g2, ys, *next_norm, seq_len, last)


def kernel(x, c, norm1_g, norm2_g, w_ada, b_ada, w_in, w_sc, b_sc, hf_w1, hf_b1, hf_w2, hf_b2, hf_w3, hf_b3,
           hf_w4, hf_freq, hy_bias, w_br_h, w_br_a, w_out, router_w, router_bias, moe_w1, moe_w3, moe_w2, final_g):
    B, L, D = x.shape
    T = B * L
    C = HY_WIDTH
    xt = x.reshape(T, D)

    rope = _rope_tables(L)
    feat, deltas = _hyena_features(L)
    mats = _dft_matrices()

    c_pad = jnp.pad(c, ((0, 8 - B), (0, 0)))
    mod = _ada(c_pad, w_ada, b_ada)[:, :B]
    router_wt = router_w.T
    router_b = router_bias.reshape(N_EXPERTS, 1)
    w_in_b, w_br_h_b, w_br_a_b, w_out_b = (w.astype(BF16) for w in (w_in, w_br_h, w_br_a, w_out))
    moe_w1_b, moe_w3_b, moe_w2_b = (w.astype(BF16) for w in (moe_w1, moe_w3, moe_w2))
    w_gate_b = w_in_b[:, :, HY_COLS + QKV_COLS:]

    pad3 = lambda a, r, c: jnp.pad(a, ((0, 0), (0, r - a.shape[1]), (0, c - a.shape[2])))
    k2 = _hyena_filter(feat, pad3(hf_w1, LANES, LANES), pad3(hf_b1[:, None], 1, LANES),
                       pad3(hf_w2, LANES, LANES), pad3(hf_b2[:, None], 1, LANES),
                       pad3(hf_w3, LANES, LANES), pad3(hf_b3[:, None], 1, LANES),
                       pad3(hf_w4, LANES, 2 * C), pad3(hf_freq[:, None], 1, LANES), deltas)
    kf = _filter_fft(k2.reshape(DEPTH, -1, 2 * C), mats)

    mods = [[mod[i, :, k * D:(k + 1) * D].reshape(B, 1, D) for k in range(6)] for i in range(DEPTH)]
    h = _norm_mod(xt, norm1_g[0].reshape(1, D), mods[0][1], mods[0][0], L)
    for i in range(DEPTH):
        sh1, sc1, g1, sh2, sc2, g2 = mods[i]

        u = _proj(h, w_in_b, i, 0, HY_COLS, tn=HY_WIDTH)
        qkv = [_proj_qkv(h, w_in_b, i, g, rope, L, BF16 if dil == 1 else F32)
               for g, (_, dil) in enumerate(ATT_GROUPS)]

        y_hy = _long_conv(u.reshape(B, L, 3 * C), w_sc[i], b_sc[i], kf, i, hy_bias[i], mats).reshape(T, C)

        attn = [(_attention_group if dil == 1 else _attention_strided)(qkv[g], dil, B, L)
                for g, (_, dil) in enumerate(ATT_GROUPS)]
        xt = _merge(xt, y_hy, attn, h, w_gate_b, w_br_h_b, w_br_a_b, w_out_b, i, g1, L)

        last = i == DEPTH - 1
        if last:
            next_norm = (final_g.reshape(1, D), sc2, sh2)
        else:
            next_norm = (norm1_g[i + 1].reshape(1, D), mods[i + 1][1], mods[i + 1][0])
        res = _moe_layer(xt, norm2_g[i].reshape(1, D), sc2, sh2, g2, router_wt, router_b,
                         moe_w1_b, moe_w3_b, moe_w2_b, i, L, next_norm, last)
        if last:
            return res.reshape(B, L, D)
        xt, h = res
```

```python
import math

import jax
import jax.numpy as jnp
import numpy as np
from jax import lax
from jax.experimental import pallas as pl
from jax.experimental.pallas import tpu as pltpu

D_MODEL = 1024
DEPTH = 2
HY_WIDTH = 768
HY_EMB = 33
HY_FAST_DECAY_PCT = 0.3
HY_SLOW_DECAY_PCT = 1.5
HY_TARGET = 1e-2
HEAD_DIM = 64
ATT_GROUPS = ((128, 1), (512, 4), (2048, 16))
HEADS_PER_GROUP = 4
N_HEADS = HEADS_PER_GROUP * len(ATT_GROUPS)
ATT_WIDTH = N_HEADS * HEAD_DIM
ATT_OUT = HEADS_PER_GROUP * HEAD_DIM
ROPE_THETA = 10000.0
N_EXPERTS = 16
N_GROUPS = 4
EXPERTS_PER_GROUP = N_EXPERTS // N_GROUPS
NORM_EPS = 1e-6
MASK_VALUE = -1e30

LANES = 128
MXU_DIM = 256
VMEM_LIMIT_BYTES = 56 * 1024 * 1024

F32 = jnp.float32
BF16 = jnp.bfloat16
HIGHEST = lax.Precision.HIGHEST

PROJ_TM = 2048
PROJ_TN = MXU_DIM
QKV_TM = 1024
PROJ_CHUNK = 512
HY_COLS = 3 * HY_WIDTH
QKV_COLS = 3 * ATT_WIDTH

FFT_R = 128
FFT_KH = FFT_R // 2 + 1
FFT_KP = 72
FFT_SLAB = 2 * FFT_KP
FFT_G_PITCH = FFT_SLAB + 8
FFT_T_PITCH = FFT_R + 8
CONV_CB = 128
FILTER_TM = 1024
FFT_UNROLL_TIME = 8
FFT_UNROLL_FREQ = 5

PAIRS_PER_GROUP = 6
N_BUCKETS = N_GROUPS * PAIRS_PER_GROUP
BUCKET_ROWS = 32
ROUTER_TM = 512
EXPERT_TM = 256
ROW_BLOCK = 1024
GATE_COLS = LANES


def _params(*sem):
    return pltpu.CompilerParams(dimension_semantics=sem, vmem_limit_bytes=VMEM_LIMIT_BYTES)


def _sigmoid(x):
    return 1.0 / (1.0 + jnp.exp(-x))


def _store_time_padded(o_ref, val):
    groups = val.shape[0] // FFT_R
    o_ref[:, :FFT_R, :] = val.reshape(groups, FFT_R, val.shape[1])
    o_ref[:, FFT_R:, :] = jnp.zeros((groups, FFT_T_PITCH - FFT_R, val.shape[1]), o_ref.dtype)


def _time_padded_shape(rows, cols):
    return jax.ShapeDtypeStruct((rows // FFT_R, FFT_T_PITCH, cols), F32)


def _ada_kernel(c_ref, w_ref, b_ref, o_ref):
    c = c_ref[...]
    c_act = c * _sigmoid(c)
    o_ref[0] = jnp.dot(c_act, w_ref[0], precision=HIGHEST, preferred_element_type=F32) + b_ref[0]


def _ada(c_pad, w_ada, b_ada):
    depth, D, N = w_ada.shape
    rows = c_pad.shape[0]
    tn = N // 4
    return pl.pallas_call(
        _ada_kernel,
        out_shape=jax.ShapeDtypeStruct((depth, rows, N), F32),
        grid=(depth, N // tn),
        in_specs=[pl.BlockSpec((rows, D), lambda l, j: (0, 0)),
                  pl.BlockSpec((1, D, tn), lambda l, j: (l, 0, j)),
                  pl.BlockSpec((1, 1, tn), lambda l, j: (l, 0, j))],
        out_specs=pl.BlockSpec((1, rows, tn), lambda l, j: (l, 0, j)),
        compiler_params=_params("arbitrary", "arbitrary"),
        name="ada_mod",
    )(c_pad, w_ada, b_ada.reshape(depth, 1, N))


def _norm_mod_kernel(x_ref, g_ref, sc_ref, sh_ref, o_ref):
    x = x_ref[...]
    r = lax.rsqrt(jnp.mean(x * x, axis=-1, keepdims=True) + NORM_EPS)
    h = (x * r) * g_ref[...]
    o_ref[...] = (h * (1.0 + sc_ref[0]) + sh_ref[0]).astype(o_ref.dtype)


def _norm_mod(x, g, sc, sh, seq_len):
    T, D = x.shape
    tm = 1024
    per_batch = seq_len // tm
    return pl.pallas_call(
        _norm_mod_kernel,
        out_shape=jax.ShapeDtypeStruct((T, D), BF16),
        grid=(T // tm,),
        in_specs=[pl.BlockSpec((tm, D), lambda i: (i, 0)),
                  pl.BlockSpec((1, D), lambda i: (0, 0)),
                  pl.BlockSpec((1, 1, D), lambda i: (i // per_batch, 0, 0)),
                  pl.BlockSpec((1, 1, D), lambda i: (i // per_batch, 0, 0))],
        out_specs=pl.BlockSpec((tm, D), lambda i: (i, 0)),
        compiler_params=_params("arbitrary"),
        name="norm_mod",
    )(x, g, sc, sh)


def _proj_kernel(h_ref, w_ref, o_ref):
    o_ref[...] = jnp.dot(h_ref[...], w_ref[0], preferred_element_type=F32).astype(o_ref.dtype)


def _proj(h, w, layer, col0, n_cols, tn):
    T, D = h.shape
    tm = PROJ_TM
    assert col0 % tn == 0 and n_cols % tn == 0
    off = col0 // tn
    return pl.pallas_call(
        _proj_kernel,
        out_shape=jax.ShapeDtypeStruct((T, n_cols), BF16),
        grid=(T // tm, n_cols // tn),
        in_specs=[pl.BlockSpec((tm, D), lambda i, j: (i, 0)),
                  pl.BlockSpec((1, D, tn), lambda i, j: (layer, 0, off + j))],
        out_specs=pl.BlockSpec((tm, tn), lambda i, j: (i, j)),
        compiler_params=_params("arbitrary", "arbitrary"),
        name="proj",
    )(h, w)


def _proj_qkv_kernel(h_ref, wq_ref, wk_ref, wv_ref, cos_ref, sin_ref, rot_ref, o_ref):
    tm, tn = h_ref.shape[0], PROJ_TN
    rot = rot_ref[...]
    for c in range(tm // PROJ_CHUNK):
        rows = slice(c * PROJ_CHUNK, (c + 1) * PROJ_CHUNK)
        hc = h_ref[rows, :]
        cos = jnp.concatenate([cos_ref[rows, :], cos_ref[rows, :]], axis=1)
        sin = jnp.concatenate([sin_ref[rows, :], sin_ref[rows, :]], axis=1)

        def roped(w_ref):
            acc = jnp.dot(hc, w_ref[0], preferred_element_type=F32)
            swapped = jnp.dot(acc.astype(BF16), rot, preferred_element_type=F32)
            return acc * cos + swapped * sin

        o_ref[rows, 0:tn] = (roped(wq_ref) * HEAD_DIM ** -0.5).astype(o_ref.dtype)
        o_ref[rows, tn:2 * tn] = roped(wk_ref).astype(o_ref.dtype)
        o_ref[rows, 2 * tn:3 * tn] = jnp.dot(hc, wv_ref[0], preferred_element_type=F32).astype(o_ref.dtype)


def _proj_qkv(h, w, layer, group, rope, seq_len, out_dtype):
    T, D = h.shape
    tm, tn = QKV_TM, PROJ_TN
    n_groups = len(ATT_GROUPS)
    off = HY_COLS // tn + group
    cos_t, sin_t, rot = rope
    per_batch = seq_len // tm
    tab_spec = pl.BlockSpec((tm, LANES), lambda i: (i % per_batch, 0))
    w_spec = lambda kind: pl.BlockSpec((1, D, tn), lambda i: (layer, 0, off + n_groups * kind))
    return pl.pallas_call(
        _proj_qkv_kernel,
        out_shape=jax.ShapeDtypeStruct((T, 3 * tn), out_dtype),
        grid=(T // tm,),
        in_specs=[pl.BlockSpec((tm, D), lambda i: (i, 0)), w_spec(0), w_spec(1), w_spec(2),
                  tab_spec, tab_spec, pl.BlockSpec((tn, tn), lambda i: (0, 0))],
        out_specs=pl.BlockSpec((tm, 3 * tn), lambda i: (i, 0)),
        compiler_params=_params("arbitrary"),
        name="proj_qkv",
    )(h, w, w, w, cos_t, sin_t, rot)


def _rope_tables(seq_len):
    half = HEAD_DIM // 2
    inv = ROPE_THETA ** (-jnp.arange(half, dtype=F32) / half)
    ang = jnp.arange(seq_len, dtype=F32)[:, None] * inv[None, :]
    reps = LANES // half
    cos_t = jnp.tile(jnp.cos(ang), (1, reps))
    sin_t = jnp.tile(jnp.sin(ang), (1, reps))
    rot = np.zeros((PROJ_TN, PROJ_TN), np.float32)
    for j in range(PROJ_TN):
        if j % HEAD_DIM < half:
            rot[j + half, j] = -1.0
        else:
            rot[j - half, j] = 1.0
    return cos_t, sin_t, jnp.asarray(rot, BF16)


def _hyena_filter_kernel(feat_ref, w1_ref, b1_ref, w2_ref, b2_ref, w3_ref, b3_ref, w4_ref, fr_ref,
                         dl_ref, o_ref):
    i = pl.program_id(1)
    half = feat_ref.shape[0]
    z = feat_ref[...]
    fr = fr_ref[0]
    h = jnp.sin(fr * (jnp.dot(z, w1_ref[0], precision=HIGHEST, preferred_element_type=F32) + b1_ref[0]))
    h = jnp.sin(fr * (jnp.dot(h, w2_ref[0], precision=HIGHEST, preferred_element_type=F32) + b2_ref[0]))
    h = jnp.sin(fr * (jnp.dot(h, w3_ref[0], precision=HIGHEST, preferred_element_type=F32) + b3_ref[0]))
    rate = jnp.abs(dl_ref[...])
    parts = []
    for s in range(2):
        k = jnp.dot(h, w4_ref[0, s], precision=HIGHEST, preferred_element_type=F32)
        t = z[:, s * (LANES // 2):s * (LANES // 2) + 1]
        parts.append(k * jnp.exp(-t * rate))
    val = jnp.concatenate(parts, axis=0)
    pos = i * 2 * half + lax.broadcasted_iota(jnp.int32, (2 * half, 1), 0)
    is_bwd = lax.broadcasted_iota(jnp.int32, (1, val.shape[1]), 1) >= HY_WIDTH
    _store_time_padded(o_ref.at[0], jnp.where((pos == 0) & is_bwd, 0.0, val))


def _hyena_filter(feat, w1, b1, w2, b2, w3, b3, w4, freq, deltas2):
    n_rows = 2 * feat.shape[0]
    depth = w1.shape[0]
    C2 = 2 * HY_WIDTH
    tm = FILTER_TM
    per_layer = lambda r, c: pl.BlockSpec((1, r, c), lambda l, i: (l, 0, 0))
    shape = _time_padded_shape(n_rows, C2)
    return pl.pallas_call(
        _hyena_filter_kernel,
        out_shape=jax.ShapeDtypeStruct((depth,) + shape.shape, shape.dtype),
        grid=(depth, n_rows // tm),
        in_specs=[pl.BlockSpec((tm // 2, LANES), lambda l, i: (i, 0)),
                  per_layer(LANES, LANES), per_layer(1, LANES),
                  per_layer(LANES, LANES), per_layer(1, LANES),
                  per_layer(LANES, LANES), per_layer(1, LANES),
                  pl.BlockSpec((1, 2, LANES, C2), lambda l, i: (l, 0, 0, 0)),
                  per_layer(1, LANES),
                  pl.BlockSpec((1, C2), lambda l, i: (0, 0))],
        out_specs=pl.BlockSpec((1, tm // FFT_R, FFT_T_PITCH, C2), lambda l, i: (l, i, 0, 0)),
        compiler_params=_params("arbitrary", "arbitrary"),
        name="hyena_filter",
    )(feat, w1, b1, w2, b2, w3, b3, w4, freq, deltas2)


def _hyena_features(seq_len):
    L = seq_len
    half = LANES // 2
    t = jnp.linspace(0.0, 1.0, L, dtype=F32)[:, None]
    bands = (HY_EMB - 1) // 2
    w = 2.0 * math.pi * jnp.arange(L, dtype=F32)[:, None] / L
    f = jnp.linspace(1e-4, bands - 1, bands, dtype=F32)[None, :]
    z = jnp.concatenate([t, jnp.cos(f * w), -jnp.sin(f * w)], axis=-1)
    z = jnp.pad(z, ((0, 0), (0, half - HY_EMB)))
    z = z.reshape(L // FILTER_TM, 2, FILTER_TM // 2, half).transpose(0, 2, 1, 3).reshape(L // 2, LANES)
    max_decay = math.log(HY_TARGET) / HY_FAST_DECAY_PCT
    min_decay = math.log(HY_TARGET) / HY_SLOW_DECAY_PCT
    deltas = jnp.linspace(min_decay, max_decay, HY_WIDTH, dtype=F32)[None, :]
    return z, jnp.concatenate([deltas, deltas], axis=1)


def _dft_matrices():
    R, KH, KP = FFT_R, FFT_KH, FFT_KP
    N = R * R
    n1 = np.arange(R)[:, None, None]
    k2 = np.arange(KP)[None, :, None]
    n2 = np.arange(R)[None, None, :]
    phase = 2.0 * np.pi * ((n2 * k2 % R) / R + (n1 * k2) / N)
    live = (k2 < KH)
    a1 = np.concatenate([np.cos(phase) * live, -np.sin(phase) * live], axis=1)
    wgt = np.where((k2 == 0) | (k2 == R // 2), 1.0, 2.0) * live / N
    b1 = np.concatenate([np.cos(phase) * wgt, -np.sin(phase) * wgt], axis=1)
    b1 = np.transpose(b1, (0, 2, 1))[:, :R // 2, :]
    th = 2.0 * np.pi * (np.arange(R)[:, None] * np.arange(R)[None, :] % R) / R
    c, s = np.cos(th), np.sin(th)
    w2f = np.block([[c, s], [-s, c]])
    w2i = np.block([[c, -s], [s, c]])
    as_bf = lambda a: jnp.asarray(a.astype(np.float32)).astype(BF16)
    return dict(a1=as_bf(a1[:, :, :R // 2]), b1=as_bf(b1), w2f=as_bf(w2f), w2i=as_bf(w2i))


def _fft_stage1(src_ref, a1_ref, g_ref, n_rows):
    def body(n1, carry):
        xs = src_ref[pl.ds(n1, n_rows, stride=FFT_T_PITCH), :].astype(BF16)
        slab = jnp.dot(a1_ref[n1], xs, preferred_element_type=F32)
        g_ref[pl.ds(pl.multiple_of(n1 * FFT_G_PITCH, 8), FFT_SLAB), :] = slab
        return carry
    lax.fori_loop(0, FFT_R, body, 0, unroll=FFT_UNROLL_TIME)


def _load_freq_rows(g_ref, k2):
    re = g_ref[pl.ds(k2, FFT_R, stride=FFT_G_PITCH), :]
    im = g_ref[pl.ds(FFT_KP + k2, FFT_R, stride=FFT_G_PITCH), :]
    return jnp.concatenate([re, im], axis=0)


def _filter_fft_kernel(fwd_ref, bwd_ref, a1_ref, w2_ref, o_ref, gf_ref, gb_ref):
    n2_rows = fwd_ref.shape[1] // FFT_T_PITCH
    cb = fwd_ref.shape[2]
    fwd, bwd = fwd_ref.at[0], bwd_ref.at[0]

    def stage1(n1, carry):
        rows = pl.ds(n1, n2_rows, stride=FFT_T_PITCH)
        xs = jnp.concatenate([fwd[rows, :], bwd[rows, :]], axis=1).astype(BF16)
        slab = jnp.dot(a1_ref[n1], xs, preferred_element_type=F32)
        dst = pl.ds(pl.multiple_of(n1 * FFT_G_PITCH, 8), FFT_SLAB)
        gf_ref[dst, :] = slab[:, :cb]
        gb_ref[dst, :] = slab[:, cb:]
        return carry
    lax.fori_loop(0, FFT_R, stage1, 0, unroll=FFT_UNROLL_TIME)

    def body(k2, carry):
        gk = jnp.concatenate([_load_freq_rows(gf_ref, k2), _load_freq_rows(gb_ref, k2)], axis=1).astype(BF16)
        x = jnp.dot(w2_ref[...], gk, preferred_element_type=F32)
        f, b = x[:, :cb], x[:, cb:]
        spec = jnp.concatenate([f[:FFT_R] + b[:FFT_R], f[FFT_R:] - b[FFT_R:]], axis=0)
        o_ref[0, k2] = spec.astype(o_ref.dtype)
        return carry
    lax.fori_loop(0, FFT_KH, body, 0, unroll=FFT_UNROLL_FREQ)


def _filter_fft(k2, mats):
    depth, n_rows, _ = k2.shape
    C = HY_WIDTH
    cb = CONV_CB
    ncb = C // cb
    return pl.pallas_call(
        _filter_fft_kernel,
        out_shape=jax.ShapeDtypeStruct((depth, FFT_KH, 2 * FFT_R, C), BF16),
        grid=(depth, ncb),
        in_specs=[pl.BlockSpec((1, n_rows, cb), lambda l, j: (l, 0, j)),
                  pl.BlockSpec((1, n_rows, cb), lambda l, j: (l, 0, ncb + j)),
                  pl.BlockSpec((FFT_R, FFT_SLAB, FFT_R // 2), lambda l, j: (0, 0, 0)),
                  pl.BlockSpec((2 * FFT_R, 2 * FFT_R), lambda l, j: (0, 0))],
        out_specs=pl.BlockSpec((1, FFT_KH, 2 * FFT_R, cb), lambda l, j: (l, 0, 0, j)),
        scratch_shapes=[pltpu.VMEM((FFT_R * FFT_G_PITCH, cb), F32), pltpu.VMEM((FFT_R * FFT_G_PITCH, cb), F32)],
        compiler_params=_params("arbitrary", "arbitrary"),
        name="hyena_filter_fft",
    )(k2, k2, mats["a1"], mats["w2f"])


SC_HALO = 16


def _long_conv_kernel(x0_ref, x1_ref, v_ref, w0_ref, w1_ref, wv_ref, b0_ref, b1s_ref, bv_ref, kf_ref, bias_ref,
                      a1_ref, b1_ref, w2f_ref, w2i_ref, o_ref, g_ref, zv_ref, y_ref, stage_ref):
    seq_len = x0_ref.shape[1]
    n_groups = seq_len // FFT_R

    def short_conv(part, u_ref, w_ref, b_ref, g):
        row0 = pl.multiple_of(g * FFT_R, FFT_R)
        lo = pl.multiple_of(jnp.maximum(row0 - SC_HALO, 0), SC_HALO)
        hi = pl.multiple_of(jnp.minimum(row0 + FFT_R, seq_len - SC_HALO), SC_HALO)
        st = stage_ref.at[part]
        st[0:SC_HALO, :] = jnp.where(g > 0, u_ref[0, pl.ds(lo, SC_HALO), :].astype(F32), 0.0)
        st[SC_HALO:SC_HALO + FFT_R, :] = u_ref[0, pl.ds(row0, FFT_R), :].astype(F32)
        st[SC_HALO + FFT_R:, :] = jnp.where(g < n_groups - 1, u_ref[0, pl.ds(hi, SC_HALO), :].astype(F32), 0.0)
        w = w_ref[...]
        return (st[SC_HALO - 1:SC_HALO - 1 + FFT_R, :] * w[0:1] + st[SC_HALO:SC_HALO + FFT_R, :] * w[1:2]
                + st[SC_HALO + 1:SC_HALO + 1 + FFT_R, :] * w[2:3] + b_ref[...])

    def gate_body(g, carry):
        zv = short_conv(2, v_ref, wv_ref, bv_ref, g) * short_conv(1, x1_ref, w1_ref, b1s_ref, g)
        zv_ref[pl.ds(pl.multiple_of(g * FFT_T_PITCH, 8), FFT_R), :] = zv
        return carry
    lax.fori_loop(0, n_groups, gate_body, 0, unroll=2)

    _fft_stage1(zv_ref, a1_ref, g_ref, n_groups)

    def freq_body(k2, carry):
        gk = _load_freq_rows(g_ref, k2).astype(BF16)
        x = jnp.dot(w2f_ref[...], gk, preferred_element_type=F32)
        kf = kf_ref[0, k2].astype(F32)
        xr, xi = x[:FFT_R], x[FFT_R:]
        kr, ki = kf[:FFT_R], kf[FFT_R:]
        p = jnp.concatenate([xr * kr - xi * ki, xr * ki + xi * kr], axis=0).astype(BF16)
        hk = jnp.dot(w2i_ref[...], p, preferred_element_type=F32)
        g_ref[pl.ds(k2, FFT_R, stride=FFT_G_PITCH), :] = hk[:FFT_R]
        g_ref[pl.ds(FFT_KP + k2, FFT_R, stride=FFT_G_PITCH), :] = hk[FFT_R:]
        return carry
    lax.fori_loop(0, FFT_KH, freq_body, 0, unroll=FFT_UNROLL_FREQ)

    def time_body(n1, carry):
        slab = g_ref[pl.ds(pl.multiple_of(n1 * FFT_G_PITCH, 8), FFT_SLAB), :].astype(BF16)
        y_ref[pl.ds(n1, n_groups, stride=FFT_T_PITCH), :] = jnp.dot(b1_ref[n1], slab, preferred_element_type=F32)
        return carry
    lax.fori_loop(0, FFT_R, time_body, 0, unroll=FFT_UNROLL_TIME)

    bias = bias_ref[...]

    def out_body(g, carry):
        rows = pl.ds(pl.multiple_of(g * FFT_T_PITCH, 8), FFT_R)
        x0 = short_conv(0, x0_ref, w0_ref, b0_ref, g)
        y = (y_ref[rows, :] + zv_ref[rows, :] * bias) * x0
        o_ref[0, pl.ds(pl.multiple_of(g * FFT_R, FFT_R), FFT_R), :] = y.astype(o_ref.dtype)
        return carry
    lax.fori_loop(0, n_groups, out_body, 0, unroll=2)


def _long_conv(u, w_sc, b_sc, kf, layer, hy_bias, mats):
    B, L, _ = u.shape
    C = HY_WIDTH
    cb = CONV_CB
    ncb = C // cb
    lp = L // FFT_R * FFT_T_PITCH
    part = lambda p: pl.BlockSpec((1, L, cb), lambda j, b: (b, 0, p * ncb + j))
    wpart = lambda p: pl.BlockSpec((3, cb), lambda j, b: (0, p * ncb + j))
    bpart = lambda p: pl.BlockSpec((1, cb), lambda j, b: (0, p * ncb + j))
    b2 = b_sc.reshape(1, 3 * C)
    return pl.pallas_call(
        _long_conv_kernel,
        out_shape=jax.ShapeDtypeStruct((B, L, C), BF16),
        grid=(ncb, B),
        in_specs=[part(0), part(1), part(2), wpart(0), wpart(1), wpart(2), bpart(0), bpart(1), bpart(2),
                  pl.BlockSpec((1, FFT_KH, 2 * FFT_R, cb), lambda j, b: (layer, 0, 0, j)),
                  pl.BlockSpec((1, cb), lambda j, b: (0, j)),
                  pl.BlockSpec((FFT_R, FFT_SLAB, FFT_R // 2), lambda j, b: (0, 0, 0)),
                  pl.BlockSpec((FFT_R, FFT_R // 2, FFT_SLAB), lambda j, b: (0, 0, 0)),
                  pl.BlockSpec((2 * FFT_R, 2 * FFT_R), lambda j, b: (0, 0)),
                  pl.BlockSpec((2 * FFT_R, 2 * FFT_R), lambda j, b: (0, 0))],
        out_specs=pl.BlockSpec((1, L, cb), lambda j, b: (b, 0, j)),
        scratch_shapes=[pltpu.VMEM((FFT_R * FFT_G_PITCH, cb), F32), pltpu.VMEM((lp, cb), F32),
                        pltpu.VMEM((lp, cb), F32), pltpu.VMEM((3, FFT_R + 2 * SC_HALO, cb), F32)],
        compiler_params=_params("arbitrary", "arbitrary"),
        name="hyena_long_conv",
    )(u, u, u, w_sc, w_sc, w_sc, b2, b2, b2, kf, hy_bias.reshape(1, C),
      mats["a1"], mats["b1"], mats["w2f"], mats["w2i"])


ATT_TQ = 128
ATT_RADIUS = 64
ATT_WINDOW = ATT_TQ + 2 * ATT_RADIUS


ATT_SUB = 4


def _attn_kernel(q_ref, k_ref, v_ref, o_ref, lse_ref):
    i = pl.program_id(2)
    ls = k_ref.shape[1]
    nh = HEADS_PER_GROUP
    head_of_col = lax.broadcasted_iota(jnp.int32, (1, ATT_OUT), 1) // HEAD_DIM
    row_iota = lax.broadcasted_iota(jnp.int32, (nh * ATT_TQ, ATT_WINDOW), 0) % ATT_TQ
    col_iota = lax.broadcasted_iota(jnp.int32, (nh * ATT_TQ, ATT_WINDOW), 1)
    rel = row_iota - col_iota
    for sub in range(ATT_SUB):
        q0 = (i * ATT_SUB + sub) * ATT_TQ
        rows = slice(sub * ATT_TQ, (sub + 1) * ATT_TQ)
        q = q_ref[0, rows, :]
        start = jnp.clip(q0 - ATT_RADIUS, 0, ls - ATT_WINDOW)
        start = pl.multiple_of(start, ATT_RADIUS)
        kw = k_ref[0, pl.ds(start, ATT_WINDOW), :]
        vw = v_ref[0, pl.ds(start, ATT_WINDOW), :]
        out, lse = _attn_unit(q, kw, vw, rel, q0 - start, head_of_col)
        o_ref[0, rows, :] = out.astype(o_ref.dtype)
        lse_ref[0, rows, :] = lse


def _attn_unit(q, kw, vw, rel, q_minus_start, head_of_col):
    nh = HEADS_PER_GROUP
    band = jnp.abs(q_minus_start + rel) <= ATT_RADIUS
    zero = jnp.zeros_like(q)
    q4 = jnp.concatenate([jnp.where(head_of_col == h, q, zero) for h in range(nh)], axis=0)
    s = lax.dot_general(q4, kw, (((1,), (1,)), ((), ())), preferred_element_type=F32)
    s = jnp.where(band, s, MASK_VALUE)
    m = jnp.max(s, axis=-1, keepdims=True)
    p = jnp.exp(s - m)
    den = jnp.sum(p, axis=-1, keepdims=True)
    pv = jnp.dot(p.astype(BF16), vw, preferred_element_type=F32) / den
    lse4 = m + jnp.log(den)
    out = jnp.zeros((ATT_TQ, ATT_OUT), F32)
    lse = jnp.zeros((ATT_TQ, ATT_OUT), F32)
    for h in range(nh):
        mine = head_of_col == h
        hrows = slice(h * ATT_TQ, (h + 1) * ATT_TQ)
        out = jnp.where(mine, pv[hrows], out)
        lse = jnp.where(mine, lse4[hrows], lse)
    return out, lse


ATT_CHUNK = 2048


def _attn_strided_kernel(q0_ref, q1_ref, k0_ref, k1_ref, v0_ref, v1_ref, o_ref, lse_ref, o_scr, lse_scr, *, dil):
    i = pl.program_id(1)
    ls = q0_ref.shape[1] // dil

    def gather(lo_ref, hi_ref, first, n):
        rows = pl.ds(first, n, stride=dil)
        return jnp.concatenate([lo_ref[0, rows, :], hi_ref[0, rows, :]], axis=1).astype(BF16)

    units = ATT_CHUNK // ATT_TQ
    shift = dil.bit_length() - 1
    head_of_col = lax.broadcasted_iota(jnp.int32, (1, ATT_OUT), 1) // HEAD_DIM
    row_iota = lax.broadcasted_iota(jnp.int32, (HEADS_PER_GROUP * ATT_TQ, ATT_WINDOW), 0) % ATT_TQ
    rel = row_iota - lax.broadcasted_iota(jnp.int32, (HEADS_PER_GROUP * ATT_TQ, ATT_WINDOW), 1)

    def unit(u, carry):
        r = u & (dil - 1)
        sb = u >> shift
        m0 = i * (ATT_CHUNK // dil) + sb * ATT_TQ
        start = jnp.clip(m0 - ATT_RADIUS, 0, ls - ATT_WINDOW)
        q = gather(q0_ref, q1_ref, m0 * dil + r, ATT_TQ)
        kw = gather(k0_ref, k1_ref, start * dil + r, ATT_WINDOW)
        vw = gather(v0_ref, v1_ref, start * dil + r, ATT_WINDOW)
        out, lse = _attn_unit(q, kw, vw, rel, m0 - start, head_of_col)
        dst = pl.ds(sb * ATT_TQ * dil + r, ATT_TQ, stride=dil)
        for half in range(2):
            lanes = slice(half * LANES, (half + 1) * LANES)
            o_scr[half, dst, :] = out[:, lanes]
            lse_scr[half, dst, :] = lse[:, lanes]
        return carry
    lax.fori_loop(0, units, unit, 0, unroll=2)
    for half in range(2):
        lanes = slice(half * LANES, (half + 1) * LANES)
        o_ref[0, :, lanes] = o_scr[half]
        lse_ref[0, :, lanes] = lse_scr[half]


def _attention_strided(qkv_g, dil, batch, seq_len):
    view = qkv_g.reshape(batch, seq_len, 3 * ATT_OUT)
    halves = [pl.BlockSpec((1, seq_len, LANES), lambda b, i, c=c: (b, 0, c), pipeline_mode=pl.Buffered(1))
              for c in range(3 * ATT_OUT // LANES)]
    o_spec = pl.BlockSpec((1, ATT_CHUNK, ATT_OUT), lambda b, i: (b, i, 0))
    kern = lambda *refs: _attn_strided_kernel(*refs, dil=dil)
    o, lse = pl.pallas_call(
        kern,
        out_shape=(jax.ShapeDtypeStruct((batch, seq_len, ATT_OUT), F32),
                   jax.ShapeDtypeStruct((batch, seq_len, ATT_OUT), F32)),
        grid=(batch, seq_len // ATT_CHUNK),
        in_specs=halves,
        out_specs=(o_spec, o_spec),
        scratch_shapes=[pltpu.VMEM((2, ATT_CHUNK, LANES), F32), pltpu.VMEM((2, ATT_CHUNK, LANES), F32)],
        compiler_params=_params("arbitrary", "arbitrary"),
        name=f"dilated_attn_d{dil}",
    )(*([view] * len(halves)))
    T = batch * seq_len
    return o.reshape(T, ATT_OUT), lse.reshape(T, ATT_OUT)


def _attention_group(qkv_g, dil, batch, seq_len):
    ls = seq_len // dil
    tq = ATT_SUB * ATT_TQ
    view = qkv_g.reshape(batch, ls, dil * 3 * ATT_OUT)

    def col(which):
        return lambda b, r, i: (b, 0, r * 3 + which)

    q_map = lambda b, r, i: (b, i, r * 3)
    o_map = lambda b, r, i: (b, i, r)
    o, lse = pl.pallas_call(
        _attn_kernel,
        out_shape=(jax.ShapeDtypeStruct((batch, ls, dil * ATT_OUT), BF16),
                   jax.ShapeDtypeStruct((batch, ls, dil * ATT_OUT), F32)),
        grid=(batch, dil, ls // tq),
        in_specs=[pl.BlockSpec((1, tq, ATT_OUT), q_map),
                  pl.BlockSpec((1, ls, ATT_OUT), col(1)),
                  pl.BlockSpec((1, ls, ATT_OUT), col(2))],
        out_specs=(pl.BlockSpec((1, tq, ATT_OUT), o_map), pl.BlockSpec((1, tq, ATT_OUT), o_map)),
        compiler_params=_params("arbitrary", "arbitrary", "arbitrary"),
        name=f"dilated_attn_d{dil}",
    )(view, view, view)
    T = batch * seq_len
    return o.reshape(T, ATT_OUT), lse.reshape(T, ATT_OUT)


def _merge_kernel(x_ref, yhy_ref, o1_ref, o2_ref, o3_ref, l1_ref, l2_ref, l3_ref, h_ref, wg_ref,
                  wh_ref, wa_ref, wo_ref, g1_ref, out_ref):
    l1, l2, l3 = l1_ref[...], l2_ref[...], l3_ref[...]
    m = jnp.maximum(jnp.maximum(l1, l2), l3)
    e1, e2, e3 = jnp.exp(l1 - m), jnp.exp(l2 - m), jnp.exp(l3 - m)
    tot = e1 + e2 + e3
    y_at = (e1 * o1_ref[...].astype(F32) + e2 * o2_ref[...].astype(F32) + e3 * o3_ref[...].astype(F32)) / tot
    a = jnp.dot(yhy_ref[...], wh_ref[0], preferred_element_type=F32)
    b = jnp.dot(y_at.astype(BF16), wa_ref[0], preferred_element_type=F32)
    d_model = x_ref.shape[1]
    gate = jnp.dot(h_ref[...], wg_ref[0], preferred_element_type=F32)
    merged = _sigmoid(gate[:, :d_model]) * a + _sigmoid(gate[:, d_model:]) * b
    upd = jnp.dot(merged.astype(BF16), wo_ref[0], preferred_element_type=F32)
    out_ref[...] = x_ref[...] + g1_ref[0] * upd


def _merge(x, y_hy, attn, h, w_gate, w_br_h, w_br_a, w_out, layer, g1, seq_len):
    T, D = x.shape
    tm = 512
    per_batch = seq_len // tm
    (o1, l1), (o2, l2), (o3, l3) = attn
    row = lambda w: pl.BlockSpec((tm, w), lambda i: (i, 0))
    full = lambda a: pl.BlockSpec((1,) + a.shape[1:], lambda i: (layer, 0, 0))
    return pl.pallas_call(
        _merge_kernel,
        out_shape=jax.ShapeDtypeStruct((T, D), F32),
        grid=(T // tm,),
        in_specs=[row(D), row(HY_WIDTH), row(ATT_OUT), row(ATT_OUT), row(ATT_OUT),
                  row(ATT_OUT), row(ATT_OUT), row(ATT_OUT),
                  row(D), full(w_gate), full(w_br_h), full(w_br_a), full(w_out),
                  pl.BlockSpec((1, 1, D), lambda i: (i // per_batch, 0, 0))],
        out_specs=row(D),
        compiler_params=_params("arbitrary"),
        name="mixer_merge",
    )(x, y_hy, o1, o2, o3, l1, l2, l3, h, w_gate, w_br_h, w_br_a, w_out, g1)


def _router_kernel(x_ref, g_ref, sc_ref, sh_ref, rwt_ref, rb_ref, hg_ref, stats_ref, cnt_ref, base_ref):
    i = pl.program_id(0)
    tm = x_ref.shape[0]

    @pl.when(i == 0)
    def _():
        base_ref[...] = jnp.zeros_like(base_ref)

    x = x_ref[...]
    r = lax.rsqrt(jnp.mean(x * x, axis=-1, keepdims=True) + NORM_EPS)
    h = (x * r) * g_ref[...]
    h = h * (1.0 + sc_ref[0]) + sh_ref[0]

    logits = lax.dot_general(rwt_ref[...], h, (((1,), (1,)), ((), ())),
                             precision=HIGHEST, preferred_element_type=F32)
    scores = _sigmoid(logits)
    biased = scores + rb_ref[...]

    def row(a, k):
        return a[k:k + 1, :]

    sel = jnp.zeros((1, tm), jnp.int32)
    best = None
    for g in range(N_GROUPS):
        a, b, c, d = (row(biased, 4 * g + k) for k in range(4))
        m_ab, n_ab = jnp.maximum(a, b), jnp.minimum(a, b)
        m_cd, n_cd = jnp.maximum(c, d), jnp.minimum(c, d)
        gs = jnp.maximum(m_ab, m_cd) + jnp.maximum(jnp.minimum(m_ab, m_cd), jnp.maximum(n_ab, n_cd))
        if g == 0:
            best = gs
        else:
            better = gs > best
            sel = jnp.where(better, g, sel)
            best = jnp.where(better, gs, best)

    v, u = [], []
    for k in range(EXPERTS_PER_GROUP):
        vk = jnp.zeros((1, tm), F32)
        uk = jnp.zeros((1, tm), F32)
        for g in range(N_GROUPS):
            vk = jnp.where(sel == g, row(biased, 4 * g + k), vk)
            uk = jnp.where(sel == g, row(scores, 4 * g + k), uk)
        v.append(vk)
        u.append(uk)

    i1 = jnp.zeros((1, tm), jnp.int32)
    b1 = v[0]
    for k in range(1, EXPERTS_PER_GROUP):
        gt = v[k] > b1
        i1 = jnp.where(gt, k, i1)
        b1 = jnp.where(gt, v[k], b1)
    i2 = jnp.zeros((1, tm), jnp.int32)
    b2 = jnp.full((1, tm), -jnp.inf, F32)
    for k in range(EXPERTS_PER_GROUP):
        cand = (i1 != k) & (v[k] > b2)
        i2 = jnp.where(cand, k, i2)
        b2 = jnp.where(cand, v[k], b2)

    lo = jnp.minimum(i1, i2)
    hi = jnp.maximum(i1, i2)
    pair = jnp.where(lo == 0, hi - 1, jnp.where(lo == 1, hi + 1, 5))
    bucket = sel * PAIRS_PER_GROUP + pair

    u_lo = jnp.zeros((1, tm), F32)
    u_hi = jnp.zeros((1, tm), F32)
    for k in range(EXPERTS_PER_GROUP):
        u_lo = jnp.where(lo == k, u[k], u_lo)
        u_hi = jnp.where(hi == k, u[k], u_hi)
    tot = u_lo + u_hi
    w_lo = u_lo / tot
    w_hi = u_hi / tot

    rows = lax.broadcasted_iota(jnp.int32, (BUCKET_ROWS, tm), 0)
    onehot = (rows == bucket).astype(F32)
    t_src = lax.broadcasted_iota(jnp.int32, (tm, tm), 0)
    t_dst = lax.broadcasted_iota(jnp.int32, (tm, tm), 1)
    before = (t_src < t_dst).astype(BF16)
    cum = jnp.dot(onehot.astype(BF16), before, preferred_element_type=F32)
    base = base_ref[...]
    rank = jnp.sum(onehot * (cum + base), axis=0, keepdims=True)
    base = base + jnp.sum(onehot, axis=1, keepdims=True)
    base_ref[...] = base
    cnt_ref[...] = jnp.broadcast_to(base, cnt_ref.shape)

    srow = lax.broadcasted_iota(jnp.int32, (8, tm), 0)
    stats_ref[...] = jnp.where(srow == 0, bucket.astype(F32), jnp.where(srow == 1, rank, 0.0))

    grow = lax.broadcasted_iota(jnp.int32, (GATE_COLS, tm), 0)
    gates_t = jnp.where(grow == 0, w_lo, jnp.where(grow == 1, w_hi, 0.0))
    hg_ref[:, :D_MODEL] = h
    hg_ref[:, D_MODEL:] = gates_t.T


def _router(x, norm_g, sc, sh, router_wt, router_b, seq_len):
    T, D = x.shape
    tm = ROUTER_TM
    per_batch = seq_len // tm
    return pl.pallas_call(
        _router_kernel,
        out_shape=(jax.ShapeDtypeStruct((T, D + GATE_COLS), F32),
                   jax.ShapeDtypeStruct((8, T), F32),
                   jax.ShapeDtypeStruct((BUCKET_ROWS, LANES), F32)),
        grid=(T // tm,),
        in_specs=[pl.BlockSpec((tm, D), lambda i: (i, 0)),
                  pl.BlockSpec((1, D), lambda i: (0, 0)),
                  pl.BlockSpec((1, 1, D), lambda i: (i // per_batch, 0, 0)),
                  pl.BlockSpec((1, 1, D), lambda i: (i // per_batch, 0, 0)),
                  pl.BlockSpec((N_EXPERTS, D), lambda i: (0, 0)),
                  pl.BlockSpec((N_EXPERTS, 1), lambda i: (0, 0))],
        out_specs=(pl.BlockSpec((tm, D + GATE_COLS), lambda i: (i, 0)),
                   pl.BlockSpec((8, tm), lambda i: (0, i)),
                   pl.BlockSpec((BUCKET_ROWS, LANES), lambda i: (0, 0))),
        scratch_shapes=[pltpu.VMEM((BUCKET_ROWS, 1), F32)],
        compiler_params=_params("arbitrary"),
        name="moe_router",
    )(x, norm_g, sc, sh, router_wt, router_b)


def _start_row_copies(n_rows, make_copy):
    group = 8

    def body(g, carry):
        base = pl.multiple_of(g * group, group)
        for k in range(group):
            make_copy(base + k).start(priority=k % 2)
        return carry

    lax.fori_loop(0, n_rows // group, body, 0)


def _dispatch_kernel(pos_ref, hg_ref, xs_init_hbm, xs_hbm, sem):
    del xs_init_hbm
    _start_row_copies(ROW_BLOCK, lambda r: pltpu.make_async_copy(
        hg_ref.at[pl.ds(r, 1), :], xs_hbm.at[pl.ds(pos_ref[r], 1), :], sem))
    pltpu.make_async_copy(hg_ref, xs_hbm.at[pl.ds(0, ROW_BLOCK), :], sem).wait()


def _dispatch(pos, hg, n_rows):
    T, W = hg.shape
    zeros = jnp.zeros((n_rows, W), F32)
    return pl.pallas_call(
        _dispatch_kernel,
        out_shape=jax.ShapeDtypeStruct((n_rows, W), F32),
        grid=(T // ROW_BLOCK,),
        in_specs=[pl.BlockSpec((ROW_BLOCK,), lambda i: (i,), memory_space=pltpu.SMEM),
                  pl.BlockSpec((ROW_BLOCK, W), lambda i: (i, 0)),
                  pl.BlockSpec(memory_space=pl.ANY)],
        out_specs=pl.BlockSpec(memory_space=pl.ANY),
        scratch_shapes=[pltpu.SemaphoreType.DMA],
        input_output_aliases={2: 0},
        compiler_params=_params("arbitrary"),
        name="moe_dispatch",
    )(pos, hg, zeros)


def _expert_kernel(e_lo_ref, e_hi_ref, n_used_ref, xs_ref, w1a, w3a, w2a, w1b, w3b, w2b, y_ref):
    del e_lo_ref, e_hi_ref
    used = pl.program_id(0) < n_used_ref[0]

    @pl.when(jnp.logical_not(used))
    def _():
        y_ref[...] = jnp.zeros_like(y_ref)

    @pl.when(used)
    def _():
        xb = xs_ref[:, :D_MODEL].astype(BF16)
        g_lo = xs_ref[:, D_MODEL:D_MODEL + 1]
        g_hi = xs_ref[:, D_MODEL + 1:D_MODEL + 2]

        def ffn(w1, w3, w2):
            a = jnp.dot(xb, w1[0, 0], preferred_element_type=F32)
            b = jnp.dot(xb, w3[0, 0], preferred_element_type=F32)
            act = (a * _sigmoid(a)) * b
            return jnp.dot(act.astype(BF16), w2[0, 0], preferred_element_type=F32)

        y_ref[...] = g_lo * ffn(w1a, w3a, w2a) + g_hi * ffn(w1b, w3b, w2b)


def _experts(tile_lo, tile_hi, n_used, xs, w1, w3, w2, layer):
    n_rows, W = xs.shape
    D, F = w1.shape[2], w1.shape[3]
    n_tiles = n_rows // EXPERT_TM

    def x_map(j, lo, hi, nu):
        return (jnp.minimum(j, nu[0] - 1), 0)

    def w_lo_map(j, lo, hi, nu):
        return (layer, lo[j], 0, 0)

    def w_hi_map(j, lo, hi, nu):
        return (layer, hi[j], 0, 0)

    grid_spec = pltpu.PrefetchScalarGridSpec(
        num_scalar_prefetch=3,
        grid=(n_tiles,),
        in_specs=[pl.BlockSpec((EXPERT_TM, W), x_map),
                  pl.BlockSpec((1, 1, D, F), w_lo_map), pl.BlockSpec((1, 1, D, F), w_lo_map),
                  pl.BlockSpec((1, 1, F, D), w_lo_map),
                  pl.BlockSpec((1, 1, D, F), w_hi_map), pl.BlockSpec((1, 1, D, F), w_hi_map),
                  pl.BlockSpec((1, 1, F, D), w_hi_map)],
        out_specs=pl.BlockSpec((EXPERT_TM, D), lambda j, lo, hi, nu: (j, 0)),
    )
    return pl.pallas_call(
        _expert_kernel,
        out_shape=jax.ShapeDtypeStruct((n_rows, D), F32),
        grid_spec=grid_spec,
        compiler_params=_params("arbitrary"),
        name="moe_experts",
    )(tile_lo, tile_hi, n_used, xs, w1, w3, w2, w1, w3, w2)


def _combine_kernel(pos_ref, x_ref, g2_ref, ng_ref, sc_ref, sh_ref, ys_hbm, *refs):
    *out_refs, buf, sem = refs
    _start_row_copies(ROW_BLOCK, lambda r: pltpu.make_async_copy(
        ys_hbm.at[pl.ds(pos_ref[r], 1), :], buf.at[pl.ds(r, 1), :], sem))
    pltpu.make_async_copy(ys_hbm.at[pl.ds(0, ROW_BLOCK), :], buf, sem).wait()
    xn = x_ref[...] + g2_ref[0] * buf[...]
    r = lax.rsqrt(jnp.mean(xn * xn, axis=-1, keepdims=True) + NORM_EPS)
    h = (xn * r) * ng_ref[...]
    if len(out_refs) == 2:
        out_refs[0][...] = xn
        h = h * (1.0 + sc_ref[0]) + sh_ref[0]
    out_refs[-1][...] = h.astype(out_refs[-1].dtype)


def _combine(pos, x, g2, ys, norm_g, sc, sh, seq_len, last):
    T, D = x.shape
    per_batch = seq_len // ROW_BLOCK
    row = pl.BlockSpec((ROW_BLOCK, D), lambda i: (i, 0))
    per_b = pl.BlockSpec((1, 1, D), lambda i: (i // per_batch, 0, 0))
    if last:
        out_shape, out_specs = jax.ShapeDtypeStruct((T, D), F32), row
    else:
        out_shape = (jax.ShapeDtypeStruct((T, D), F32), jax.ShapeDtypeStruct((T, D), BF16))
        out_specs = (row, row)
    return pl.pallas_call(
        _combine_kernel,
        out_shape=out_shape,
        grid=(T // ROW_BLOCK,),
        in_specs=[pl.BlockSpec((ROW_BLOCK,), lambda i: (i,), memory_space=pltpu.SMEM),
                  row, per_b, pl.BlockSpec((1, D), lambda i: (0, 0)), per_b, per_b,
                  pl.BlockSpec(memory_space=pl.ANY)],
        out_specs=out_specs,
        scratch_shapes=[pltpu.VMEM((ROW_BLOCK, D), F32), pltpu.SemaphoreType.DMA],
        compiler_params=_params("arbitrary"),
        name="moe_combine",
    )(pos, x, g2, norm_g, sc, sh, ys)


def _moe_layer(x, norm_g, sc2, sh2, g2, router_wt, router_b, w1, w3, w2, layer, seq_len, next_norm, last):
    T, D = x.shape
    hg, stats, counts = _router(x, norm_g, sc2, sh2, router_wt, router_b, seq_len)

    cnt = counts[:N_BUCKETS, 0].astype(jnp.int32)
    padded = ((cnt + EXPERT_TM - 1) // EXPERT_TM) * EXPERT_TM
    ends = jnp.cumsum(padded)
    starts = ends - padded
    bucket = stats[0].astype(jnp.int32)
    pos = starts[bucket] + stats[1].astype(jnp.int32)
    n_tiles = T // EXPERT_TM + N_BUCKETS
    tile_row0 = jnp.arange(n_tiles, dtype=jnp.int32) * EXPERT_TM
    tile_bucket = jnp.sum((ends[None, :] <= tile_row0[:, None]).astype(jnp.int32), axis=1)
    tile_bucket = jnp.minimum(tile_bucket, N_BUCKETS - 1)
    grp, pair = tile_bucket // PAIRS_PER_GROUP, tile_bucket % PAIRS_PER_GROUP
    pair_lo = jnp.array([0, 0, 0, 1, 1, 2], jnp.int32)[pair]
    pair_hi = jnp.array([1, 2, 3, 2, 3, 3], jnp.int32)[pair]
    tile_lo = grp * EXPERTS_PER_GROUP + pair_lo
    tile_hi = grp * EXPERTS_PER_GROUP + pair_hi
    n_used = (ends[-1] // EXPERT_TM).astype(jnp.int32).reshape(1)

    xs = _dispatch(pos, hg, n_tiles * EXPERT_TM)
    ys = _experts(tile_lo, tile_hi, n_used, xs, w1, w3, w2, layer)
    return _combine(pos, x, g2, ys, *next_norm, seq_len, last)


def kernel(x, c, norm1_g, norm2_g, w_ada, b_ada, w_in, w_sc, b_sc, hf_w1, hf_b1, hf_w2, hf_b2, hf_w3, hf_b3,
           hf_w4, hf_freq, hy_bias, w_br_h, w_br_a, w_out, router_w, router_bias, moe_w1, moe_w3, moe_w2, final_g):
    B, L, D = x.shape
    T = B * L
    C = HY_WIDTH
    xt = x.reshape(T, D)

    rope = _rope_tables(L)
    feat, deltas = _hyena_features(L)
    mats = _dft_matrices()

    c_pad = jnp.pad(c, ((0, 8 - B), (0, 0)))
    mod = _ada(c_pad, w_ada, b_ada)[:, :B]
    router_wt = router_w.T
    router_b = router_bias.reshape(N_EXPERTS, 1)
    w_in_b, w_br_h_b, w_br_a_b, w_out_b = (w.astype(BF16) for w in (w_in, w_br_h, w_br_a, w_out))
    moe_w1_b, moe_w3_b, moe_w2_b = (w.astype(BF16) for w in (moe_w1, moe_w3, moe_w2))
    w_gate_b = w_in_b[:, :, HY_COLS + QKV_COLS:]

    half = LANES // 2

    def diag2(w):
        wp = jnp.pad(w, ((0, 0), (0, half - w.shape[1]), (0, half - w.shape[2])))
        zero = jnp.zeros_like(wp)
        return jnp.concatenate([jnp.concatenate([wp, zero], axis=2), jnp.concatenate([zero, wp], axis=2)], axis=1)

    twice = lambda v: jnp.concatenate([v, v], axis=1)[:, None]
    w4_zero = jnp.zeros_like(hf_w4)
    w4_halves = jnp.stack([jnp.concatenate([hf_w4, w4_zero], axis=1),
                           jnp.concatenate([w4_zero, hf_w4], axis=1)], axis=1)
    k2 = _hyena_filter(feat, diag2(hf_w1), twice(hf_b1), diag2(hf_w2), twice(hf_b2), diag2(hf_w3), twice(hf_b3),
                       w4_halves, twice(hf_freq), deltas)
    kf = _filter_fft(k2.reshape(DEPTH, -1, 2 * C), mats)

    mods = [[mod[i, :, k * D:(k + 1) * D].reshape(B, 1, D) for k in range(6)] for i in range(DEPTH)]
    h = _norm_mod(xt, norm1_g[0].reshape(1, D), mods[0][1], mods[0][0], L)
    for i in range(DEPTH):
        sh1, sc1, g1, sh2, sc2, g2 = mods[i]

        u = _proj(h, w_in_b, i, 0, HY_COLS, tn=HY_WIDTH)
        qkv = [_proj_qkv(h, w_in_b, i, g, rope, L, BF16 if dil == 1 else F32)
               for g, (_, dil) in enumerate(ATT_GROUPS)]

        y_hy = _long_conv(u.reshape(B, L, 3 * C), w_sc[i], b_sc[i], kf, i, hy_bias[i], mats).reshape(T, C)

        attn = [(_attention_group if dil == 1 else _attention_strided)(qkv[g], dil, B, L)
                for g, (_, dil) in enumerate(ATT_GROUPS)]
        xt = _merge(xt, y_hy, attn, h, w_gate_b, w_br_h_b, w_br_a_b, w_out_b, i, g1, L)

        last = i == DEPTH - 1
        if last:
            next_norm = (final_g.reshape(1, D), sc2, sh2)
        else:
            next_norm = (norm1_g[i + 1].reshape(1, D), mods[i + 1][1], mods[i + 1][0])
        res = _moe_layer(xt, norm2_g[i].reshape(1, D), sc2, sh2, g2, router_wt, router_b,
                         moe_w1_b, moe_w3_b, moe_w2_b, i, L, next_norm, last)
        if last:
            return res.reshape(B, L, D)
        xt, h = res
```

```python
import math

import jax
import jax.numpy as jnp
import numpy as np
from jax import lax
from jax.experimental import pallas as pl
from jax.experimental.pallas import tpu as pltpu

D_MODEL = 1024
DEPTH = 2
HY_WIDTH = 768
HY_EMB = 33
HY_FAST_DECAY_PCT = 0.3
HY_SLOW_DECAY_PCT = 1.5
HY_TARGET = 1e-2
HEAD_DIM = 64
ATT_GROUPS = ((128, 1), (512, 4), (2048, 16))
HEADS_PER_GROUP = 4
N_HEADS = HEADS_PER_GROUP * len(ATT_GROUPS)
ATT_WIDTH = N_HEADS * HEAD_DIM
ATT_OUT = HEADS_PER_GROUP * HEAD_DIM
ROPE_THETA = 10000.0
N_EXPERTS = 16
N_GROUPS = 4
EXPERTS_PER_GROUP = N_EXPERTS // N_GROUPS
NORM_EPS = 1e-6
MASK_VALUE = -1e30

LANES = 128
MXU_DIM = 256
VMEM_LIMIT_BYTES = 56 * 1024 * 1024

F32 = jnp.float32
BF16 = jnp.bfloat16
HIGHEST = lax.Precision.HIGHEST

PROJ_TM = 2048
PROJ_TN = MXU_DIM
QKV_TM = 1024
PROJ_CHUNK = 512
HY_COLS = 3 * HY_WIDTH
QKV_COLS = 3 * ATT_WIDTH

FFT_R = 128
FFT_KH = FFT_R // 2 + 1
FFT_KP = 72
FFT_SLAB = 2 * FFT_KP
FFT_G_PITCH = FFT_SLAB + 8
FFT_T_PITCH = FFT_R + 8
CONV_CB = 128
FILTER_TM = 1024
FFT_UNROLL_TIME = 16
FFT_UNROLL_FREQ = 13

PAIRS_PER_GROUP = 6
N_BUCKETS = N_GROUPS * PAIRS_PER_GROUP
BUCKET_ROWS = 32
ROUTER_TM = 512
EXPERT_TM = 256
ROW_BLOCK = 1024
GATE_COLS = LANES


def _params(*sem):
    return pltpu.CompilerParams(dimension_semantics=sem, vmem_limit_bytes=VMEM_LIMIT_BYTES)


def _sigmoid(x):
    return 1.0 / (1.0 + jnp.exp(-x))


def _store_time_padded(o_ref, val):
    groups = val.shape[0] // FFT_R
    o_ref[:, :FFT_R, :] = val.reshape(groups, FFT_R, val.shape[1])
    o_ref[:, FFT_R:, :] = jnp.zeros((groups, FFT_T_PITCH - FFT_R, val.shape[1]), o_ref.dtype)


def _time_padded_shape(rows, cols):
    return jax.ShapeDtypeStruct((rows // FFT_R, FFT_T_PITCH, cols), F32)


def _ada_kernel(c_ref, w_ref, b_ref, o_ref):
    c = c_ref[...]
    c_act = c * _sigmoid(c)
    o_ref[0] = jnp.dot(c_act, w_ref[0], precision=HIGHEST, preferred_element_type=F32) + b_ref[0]


def _ada(c_pad, w_ada, b_ada):
    depth, D, N = w_ada.shape
    rows = c_pad.shape[0]
    tn = N // 4
    return pl.pallas_call(
        _ada_kernel,
        out_shape=jax.ShapeDtypeStruct((depth, rows, N), F32),
        grid=(depth, N // tn),
        in_specs=[pl.BlockSpec((rows, D), lambda l, j: (0, 0)),
                  pl.BlockSpec((1, D, tn), lambda l, j: (l, 0, j)),
                  pl.BlockSpec((1, 1, tn), lambda l, j: (l, 0, j))],
        out_specs=pl.BlockSpec((1, rows, tn), lambda l, j: (l, 0, j)),
        compiler_params=_params("arbitrary", "arbitrary"),
        name="ada_mod",
    )(c_pad, w_ada, b_ada.reshape(depth, 1, N))


def _norm_mod_kernel(x_ref, g_ref, sc_ref, sh_ref, o_ref):
    x = x_ref[...]
    r = lax.rsqrt(jnp.mean(x * x, axis=-1, keepdims=True) + NORM_EPS)
    h = (x * r) * g_ref[...]
    o_ref[...] = (h * (1.0 + sc_ref[0]) + sh_ref[0]).astype(o_ref.dtype)


def _norm_mod(x, g, sc, sh, seq_len):
    T, D = x.shape
    tm = 1024
    per_batch = seq_len // tm
    return pl.pallas_call(
        _norm_mod_kernel,
        out_shape=jax.ShapeDtypeStruct((T, D), BF16),
        grid=(T // tm,),
        in_specs=[pl.BlockSpec((tm, D), lambda i: (i, 0)),
                  pl.BlockSpec((1, D), lambda i: (0, 0)),
                  pl.BlockSpec((1, 1, D), lambda i: (i // per_batch, 0, 0)),
                  pl.BlockSpec((1, 1, D), lambda i: (i // per_batch, 0, 0))],
        out_specs=pl.BlockSpec((tm, D), lambda i: (i, 0)),
        compiler_params=_params("arbitrary"),
        name="norm_mod",
    )(x, g, sc, sh)


def _proj_kernel(h_ref, w_ref, o_ref):
    o_ref[...] = jnp.dot(h_ref[...], w_ref[0], preferred_element_type=F32).astype(o_ref.dtype)


def _proj(h, w, layer, col0, n_cols, tn):
    T, D = h.shape
    tm = PROJ_TM
    assert col0 % tn == 0 and n_cols % tn == 0
    off = col0 // tn
    return pl.pallas_call(
        _proj_kernel,
        out_shape=jax.ShapeDtypeStruct((T, n_cols), BF16),
        grid=(T // tm, n_cols // tn),
        in_specs=[pl.BlockSpec((tm, D), lambda i, j: (i, 0)),
                  pl.BlockSpec((1, D, tn), lambda i, j: (layer, 0, off + j))],
        out_specs=pl.BlockSpec((tm, tn), lambda i, j: (i, j)),
        compiler_params=_params("arbitrary", "arbitrary"),
        name="proj",
    )(h, w)


def _proj_qkv_kernel(h_ref, wq_ref, wk_ref, wv_ref, cos_ref, sin_ref, rot_ref, o_ref):
    tm, tn = h_ref.shape[0], PROJ_TN
    rot = rot_ref[...]
    for c in range(tm // PROJ_CHUNK):
        rows = slice(c * PROJ_CHUNK, (c + 1) * PROJ_CHUNK)
        hc = h_ref[rows, :]
        cos = jnp.concatenate([cos_ref[rows, :], cos_ref[rows, :]], axis=1)
        sin = jnp.concatenate([sin_ref[rows, :], sin_ref[rows, :]], axis=1)

        def roped(w_ref):
            acc = jnp.dot(hc, w_ref[0], preferred_element_type=F32)
            swapped = jnp.dot(acc.astype(BF16), rot, preferred_element_type=F32)
            return acc * cos + swapped * sin

        o_ref[rows, 0:tn] = (roped(wq_ref) * HEAD_DIM ** -0.5).astype(o_ref.dtype)
        o_ref[rows, tn:2 * tn] = roped(wk_ref).astype(o_ref.dtype)
        o_ref[rows, 2 * tn:3 * tn] = jnp.dot(hc, wv_ref[0], preferred_element_type=F32).astype(o_ref.dtype)


def _proj_qkv(h, w, layer, group, rope, seq_len, out_dtype):
    T, D = h.shape
    tm, tn = QKV_TM, PROJ_TN
    n_groups = len(ATT_GROUPS)
    off = HY_COLS // tn + group
    cos_t, sin_t, rot = rope
    per_batch = seq_len // tm
    tab_spec = pl.BlockSpec((tm, LANES), lambda i: (i % per_batch, 0))
    w_spec = lambda kind: pl.BlockSpec((1, D, tn), lambda i: (layer, 0, off + n_groups * kind))
    return pl.pallas_call(
        _proj_qkv_kernel,
        out_shape=jax.ShapeDtypeStruct((T, 3 * tn), out_dtype),
        grid=(T // tm,),
        in_specs=[pl.BlockSpec((tm, D), lambda i: (i, 0)), w_spec(0), w_spec(1), w_spec(2),
                  tab_spec, tab_spec, pl.BlockSpec((tn, tn), lambda i: (0, 0))],
        out_specs=pl.BlockSpec((tm, 3 * tn), lambda i: (i, 0)),
        compiler_params=_params("arbitrary"),
        name="proj_qkv",
    )(h, w, w, w, cos_t, sin_t, rot)


def _rope_tables(seq_len):
    half = HEAD_DIM // 2
    inv = ROPE_THETA ** (-jnp.arange(half, dtype=F32) / half)
    ang = jnp.arange(seq_len, dtype=F32)[:, None] * inv[None, :]
    reps = LANES // half
    cos_t = jnp.tile(jnp.cos(ang), (1, reps))
    sin_t = jnp.tile(jnp.sin(ang), (1, reps))
    rot = np.zeros((PROJ_TN, PROJ_TN), np.float32)
    for j in range(PROJ_TN):
        if j % HEAD_DIM < half:
            rot[j + half, j] = -1.0
        else:
            rot[j - half, j] = 1.0
    return cos_t, sin_t, jnp.asarray(rot, BF16)


def _hyena_filter_kernel(feat_ref, w1_ref, b1_ref, w2_ref, b2_ref, w3_ref, b3_ref, w4_ref, fr_ref,
                         dl_ref, o_ref):
    i = pl.program_id(1)
    half = feat_ref.shape[0]
    z = feat_ref[...]
    fr = fr_ref[0]
    h = jnp.sin(fr * (jnp.dot(z, w1_ref[0], precision=HIGHEST, preferred_element_type=F32) + b1_ref[0]))
    h = jnp.sin(fr * (jnp.dot(h, w2_ref[0], precision=HIGHEST, preferred_element_type=F32) + b2_ref[0]))
    h = jnp.sin(fr * (jnp.dot(h, w3_ref[0], precision=HIGHEST, preferred_element_type=F32) + b3_ref[0]))
    rate = jnp.abs(dl_ref[...])
    parts = []
    for s in range(2):
        k = jnp.dot(h, w4_ref[0, s], precision=HIGHEST, preferred_element_type=F32)
        t = z[:, s * (LANES // 2):s * (LANES // 2) + 1]
        parts.append(k * jnp.exp(-t * rate))
    val = jnp.concatenate(parts, axis=0)
    pos = i * 2 * half + lax.broadcasted_iota(jnp.int32, (2 * half, 1), 0)
    is_bwd = lax.broadcasted_iota(jnp.int32, (1, val.shape[1]), 1) >= HY_WIDTH
    _store_time_padded(o_ref.at[0], jnp.where((pos == 0) & is_bwd, 0.0, val))


def _hyena_filter(feat, w1, b1, w2, b2, w3, b3, w4, freq, deltas2):
    n_rows = 2 * feat.shape[0]
    depth = w1.shape[0]
    C2 = 2 * HY_WIDTH
    tm = FILTER_TM
    per_layer = lambda r, c: pl.BlockSpec((1, r, c), lambda l, i: (l, 0, 0))
    shape = _time_padded_shape(n_rows, C2)
    return pl.pallas_call(
        _hyena_filter_kernel,
        out_shape=jax.ShapeDtypeStruct((depth,) + shape.shape, shape.dtype),
        grid=(depth, n_rows // tm),
        in_specs=[pl.BlockSpec((tm // 2, LANES), lambda l, i: (i, 0)),
                  per_layer(LANES, LANES), per_layer(1, LANES),
                  per_layer(LANES, LANES), per_layer(1, LANES),
                  per_layer(LANES, LANES), per_layer(1, LANES),
                  pl.BlockSpec((1, 2, LANES, C2), lambda l, i: (l, 0, 0, 0)),
                  per_layer(1, LANES),
                  pl.BlockSpec((1, C2), lambda l, i: (0, 0))],
        out_specs=pl.BlockSpec((1, tm // FFT_R, FFT_T_PITCH, C2), lambda l, i: (l, i, 0, 0)),
        compiler_params=_params("arbitrary", "arbitrary"),
        name="hyena_filter",
    )(feat, w1, b1, w2, b2, w3, b3, w4, freq, deltas2)


def _hyena_features(seq_len):
    L = seq_len
    half = LANES // 2
    t = jnp.linspace(0.0, 1.0, L, dtype=F32)[:, None]
    bands = (HY_EMB - 1) // 2
    w = 2.0 * math.pi * jnp.arange(L, dtype=F32)[:, None] / L
    f = jnp.linspace(1e-4, bands - 1, bands, dtype=F32)[None, :]
    z = jnp.concatenate([t, jnp.cos(f * w), -jnp.sin(f * w)], axis=-1)
    z = jnp.pad(z, ((0, 0), (0, half - HY_EMB)))
    z = z.reshape(L // FILTER_TM, 2, FILTER_TM // 2, half).transpose(0, 2, 1, 3).reshape(L // 2, LANES)
    max_decay = math.log(HY_TARGET) / HY_FAST_DECAY_PCT
    min_decay = math.log(HY_TARGET) / HY_SLOW_DECAY_PCT
    deltas = jnp.linspace(min_decay, max_decay, HY_WIDTH, dtype=F32)[None, :]
    return z, jnp.concatenate([deltas, deltas], axis=1)


def _dft_matrices():
    R, KH, KP = FFT_R, FFT_KH, FFT_KP
    N = R * R
    n1 = np.arange(R)[:, None, None]
    k2 = np.arange(KP)[None, :, None]
    n2 = np.arange(R)[None, None, :]
    phase = 2.0 * np.pi * ((n2 * k2 % R) / R + (n1 * k2) / N)
    live = (k2 < KH)
    a1 = np.concatenate([np.cos(phase) * live, -np.sin(phase) * live], axis=1)
    wgt = np.where((k2 == 0) | (k2 == R // 2), 1.0, 2.0) * live / N
    b1 = np.concatenate([np.cos(phase) * wgt, -np.sin(phase) * wgt], axis=1)
    b1 = np.transpose(b1, (0, 2, 1))[:, :R // 2, :]
    th = 2.0 * np.pi * (np.arange(R)[:, None] * np.arange(R)[None, :] % R) / R
    c, s = np.cos(th), np.sin(th)
    w2f = np.block([[c, s], [-s, c]])
    w2i = np.block([[c, -s], [s, c]])
    as_bf = lambda a: jnp.asarray(a.astype(np.float32)).astype(BF16)
    return dict(a1=as_bf(a1[:, :, :R // 2]), b1=as_bf(b1), w2f=as_bf(w2f), w2i=as_bf(w2i))


def _fft_stage1(src_ref, a1_ref, g_ref, n_rows):
    def body(n1, carry):
        xs = src_ref[pl.ds(n1, n_rows, stride=FFT_T_PITCH), :].astype(BF16)
        slab = jnp.dot(a1_ref[n1], xs, preferred_element_type=F32)
        g_ref[pl.ds(pl.multiple_of(n1 * FFT_G_PITCH, 8), FFT_SLAB), :] = slab
        return carry
    lax.fori_loop(0, FFT_R, body, 0, unroll=FFT_UNROLL_TIME)


def _load_freq_rows(g_ref, k2):
    re = g_ref[pl.ds(k2, FFT_R, stride=FFT_G_PITCH), :]
    im = g_ref[pl.ds(FFT_KP + k2, FFT_R, stride=FFT_G_PITCH), :]
    return jnp.concatenate([re, im], axis=0)


def _filter_fft_kernel(fwd_ref, bwd_ref, a1_ref, w2_ref, o_ref, gf_ref, gb_ref):
    n2_rows = fwd_ref.shape[1] // FFT_T_PITCH
    cb = fwd_ref.shape[2]
    fwd, bwd = fwd_ref.at[0], bwd_ref.at[0]

    def stage1(n1, carry):
        rows = pl.ds(n1, n2_rows, stride=FFT_T_PITCH)
        xs = jnp.concatenate([fwd[rows, :], bwd[rows, :]], axis=1).astype(BF16)
        slab = jnp.dot(a1_ref[n1], xs, preferred_element_type=F32)
        dst = pl.ds(pl.multiple_of(n1 * FFT_G_PITCH, 8), FFT_SLAB)
        gf_ref[dst, :] = slab[:, :cb]
        gb_ref[dst, :] = slab[:, cb:]
        return carry
    lax.fori_loop(0, FFT_R, stage1, 0, unroll=FFT_UNROLL_TIME)

    def body(k2, carry):
        gk = jnp.concatenate([_load_freq_rows(gf_ref, k2), _load_freq_rows(gb_ref, k2)], axis=1).astype(BF16)
        x = jnp.dot(w2_ref[...], gk, preferred_element_type=F32)
        f, b = x[:, :cb], x[:, cb:]
        spec = jnp.concatenate([f[:FFT_R] + b[:FFT_R], f[FFT_R:] - b[FFT_R:]], axis=0)
        o_ref[0, k2] = spec.astype(o_ref.dtype)
        return carry
    lax.fori_loop(0, FFT_KH, body, 0, unroll=FFT_UNROLL_FREQ)


def _filter_fft(k2, mats):
    depth, n_rows, _ = k2.shape
    C = HY_WIDTH
    cb = CONV_CB
    ncb = C // cb
    return pl.pallas_call(
        _filter_fft_kernel,
        out_shape=jax.ShapeDtypeStruct((depth, FFT_KH, 2 * FFT_R, C), BF16),
        grid=(depth, ncb),
        in_specs=[pl.BlockSpec((1, n_rows, cb), lambda l, j: (l, 0, j)),
                  pl.BlockSpec((1, n_rows, cb), lambda l, j: (l, 0, ncb + j)),
                  pl.BlockSpec((FFT_R, FFT_SLAB, FFT_R // 2), lambda l, j: (0, 0, 0)),
                  pl.BlockSpec((2 * FFT_R, 2 * FFT_R), lambda l, j: (0, 0))],
        out_specs=pl.BlockSpec((1, FFT_KH, 2 * FFT_R, cb), lambda l, j: (l, 0, 0, j)),
        scratch_shapes=[pltpu.VMEM((FFT_R * FFT_G_PITCH, cb), F32), pltpu.VMEM((FFT_R * FFT_G_PITCH, cb), F32)],
        compiler_params=_params("arbitrary", "arbitrary"),
        name="hyena_filter_fft",
    )(k2, k2, mats["a1"], mats["w2f"])


SC_HALO = 16


def _long_conv_kernel(x0_ref, x1_ref, v_ref, w0_ref, w1_ref, wv_ref, b0_ref, b1s_ref, bv_ref, kf_ref, bias_ref,
                      a1_ref, b1_ref, w2f_ref, w2i_ref, o_ref, g_ref, zv_ref, y_ref, stage_ref):
    seq_len = x0_ref.shape[1]
    n_groups = seq_len // FFT_R

    def short_conv(part, u_ref, w_ref, b_ref, g):
        row0 = pl.multiple_of(g * FFT_R, FFT_R)
        lo = pl.multiple_of(jnp.maximum(row0 - SC_HALO, 0), SC_HALO)
        hi = pl.multiple_of(jnp.minimum(row0 + FFT_R, seq_len - SC_HALO), SC_HALO)
        st = stage_ref.at[part]
        st[0:SC_HALO, :] = jnp.where(g > 0, u_ref[0, pl.ds(lo, SC_HALO), :].astype(F32), 0.0)
        st[SC_HALO:SC_HALO + FFT_R, :] = u_ref[0, pl.ds(row0, FFT_R), :].astype(F32)
        st[SC_HALO + FFT_R:, :] = jnp.where(g < n_groups - 1, u_ref[0, pl.ds(hi, SC_HALO), :].astype(F32), 0.0)
        w = w_ref[...]
        return (st[SC_HALO - 1:SC_HALO - 1 + FFT_R, :] * w[0:1] + st[SC_HALO:SC_HALO + FFT_R, :] * w[1:2]
                + st[SC_HALO + 1:SC_HALO + 1 + FFT_R, :] * w[2:3] + b_ref[...])

    def gate_body(g, carry):
        zv = short_conv(2, v_ref, wv_ref, bv_ref, g) * short_conv(1, x1_ref, w1_ref, b1s_ref, g)
        zv_ref[pl.ds(pl.multiple_of(g * FFT_T_PITCH, 8), FFT_R), :] = zv
        return carry
    lax.fori_loop(0, n_groups, gate_body, 0, unroll=2)

    _fft_stage1(zv_ref, a1_ref, g_ref, n_groups)

    def freq_body(k2, carry):
        gk = _load_freq_rows(g_ref, k2).astype(BF16)
        x = jnp.dot(w2f_ref[...], gk, preferred_element_type=F32)
        kf = kf_ref[0, k2].astype(F32)
        xr, xi = x[:FFT_R], x[FFT_R:]
        kr, ki = kf[:FFT_R], kf[FFT_R:]
        p = jnp.concatenate([xr * kr - xi * ki, xr * ki + xi * kr], axis=0).astype(BF16)
        hk = jnp.dot(w2i_ref[...], p, preferred_element_type=F32)
        g_ref[pl.ds(k2, FFT_R, stride=FFT_G_PITCH), :] = hk[:FFT_R]
        g_ref[pl.ds(FFT_KP + k2, FFT_R, stride=FFT_G_PITCH), :] = hk[FFT_R:]
        return carry
    lax.fori_loop(0, FFT_KH, freq_body, 0, unroll=FFT_UNROLL_FREQ)

    def time_body(n1, carry):
        slab = g_ref[pl.ds(pl.multiple_of(n1 * FFT_G_PITCH, 8), FFT_SLAB), :].astype(BF16)
        y_ref[pl.ds(n1, n_groups, stride=FFT_T_PITCH), :] = jnp.dot(b1_ref[n1], slab, preferred_element_type=F32)
        return carry
    lax.fori_loop(0, FFT_R, time_body, 0, unroll=FFT_UNROLL_TIME)

    bias = bias_ref[...]

    def out_body(g, carry):
        rows = pl.ds(pl.multiple_of(g * FFT_T_PITCH, 8), FFT_R)
        x0 = short_conv(0, x0_ref, w0_ref, b0_ref, g)
        y = (y_ref[rows, :] + zv_ref[rows, :] * bias) * x0
        o_ref[0, pl.ds(pl.multiple_of(g * FFT_R, FFT_R), FFT_R), :] = y.astype(o_ref.dtype)
        return carry
    lax.fori_loop(0, n_groups, out_body, 0, unroll=2)


def _long_conv(u, w_sc, b_sc, kf, layer, hy_bias, mats):
    B, L, _ = u.shape
    C = HY_WIDTH
    cb = CONV_CB
    ncb = C // cb
    lp = L // FFT_R * FFT_T_PITCH
    part = lambda p: pl.BlockSpec((1, L, cb), lambda j, b: (b, 0, p * ncb + j))
    wpart = lambda p: pl.BlockSpec((3, cb), lambda j, b: (0, p * ncb + j))
    bpart = lambda p: pl.BlockSpec((1, cb), lambda j, b: (0, p * ncb + j))
    b2 = b_sc.reshape(1, 3 * C)
    return pl.pallas_call(
        _long_conv_kernel,
        out_shape=jax.ShapeDtypeStruct((B, L, C), BF16),
        grid=(ncb, B),
        in_specs=[part(0), part(1), part(2), wpart(0), wpart(1), wpart(2), bpart(0), bpart(1), bpart(2),
                  pl.BlockSpec((1, FFT_KH, 2 * FFT_R, cb), lambda j, b: (layer, 0, 0, j)),
                  pl.BlockSpec((1, cb), lambda j, b: (0, j)),
                  pl.BlockSpec((FFT_R, FFT_SLAB, FFT_R // 2), lambda j, b: (0, 0, 0)),
                  pl.BlockSpec((FFT_R, FFT_R // 2, FFT_SLAB), lambda j, b: (0, 0, 0)),
                  pl.BlockSpec((2 * FFT_R, 2 * FFT_R), lambda j, b: (0, 0)),
                  pl.BlockSpec((2 * FFT_R, 2 * FFT_R), lambda j, b: (0, 0))],
        out_specs=pl.BlockSpec((1, L, cb), lambda j, b: (b, 0, j)),
        scratch_shapes=[pltpu.VMEM((FFT_R * FFT_G_PITCH, cb), F32), pltpu.VMEM((lp, cb), F32),
                        pltpu.VMEM((lp, cb), F32), pltpu.VMEM((3, FFT_R + 2 * SC_HALO, cb), F32)],
        compiler_params=_params("arbitrary", "arbitrary"),
        name="hyena_long_conv",
    )(u, u, u, w_sc, w_sc, w_sc, b2, b2, b2, kf, hy_bias.reshape(1, C),
      mats["a1"], mats["b1"], mats["w2f"], mats["w2i"])


ATT_TQ = 128
ATT_RADIUS = 64
ATT_WINDOW = ATT_TQ + 2 * ATT_RADIUS


ATT_SUB = 4


def _attn_kernel(q_ref, k_ref, v_ref, o_ref, lse_ref):
    i = pl.program_id(2)
    ls = k_ref.shape[1]
    nh = HEADS_PER_GROUP
    head_of_col = lax.broadcasted_iota(jnp.int32, (1, ATT_OUT), 1) // HEAD_DIM
    row_iota = lax.broadcasted_iota(jnp.int32, (nh * ATT_TQ, ATT_WINDOW), 0) % ATT_TQ
    col_iota = lax.broadcasted_iota(jnp.int32, (nh * ATT_TQ, ATT_WINDOW), 1)
    rel = row_iota - col_iota
    for sub in range(ATT_SUB):
        q0 = (i * ATT_SUB + sub) * ATT_TQ
        rows = slice(sub * ATT_TQ, (sub + 1) * ATT_TQ)
        q = q_ref[0, rows, :]
        start = jnp.clip(q0 - ATT_RADIUS, 0, ls - ATT_WINDOW)
        start = pl.multiple_of(start, ATT_RADIUS)
        kw = k_ref[0, pl.ds(start, ATT_WINDOW), :]
        vw = v_ref[0, pl.ds(start, ATT_WINDOW), :]
        out, lse = _attn_unit(q, kw, vw, rel, q0 - start, head_of_col)
        o_ref[0, rows, :] = out.astype(o_ref.dtype)
        lse_ref[0, rows, :] = lse


def _attn_unit(q, kw, vw, rel, q_minus_start, head_of_col):
    nh = HEADS_PER_GROUP
    band = jnp.abs(q_minus_start + rel) <= ATT_RADIUS
    zero = jnp.zeros_like(q)
    q4 = jnp.concatenate([jnp.where(head_of_col == h, q, zero) for h in range(nh)], axis=0)
    s = lax.dot_general(q4, kw, (((1,), (1,)), ((), ())), preferred_element_type=F32)
    s = jnp.where(band, s, MASK_VALUE)
    m = jnp.max(s, axis=-1, keepdims=True)
    p = jnp.exp(s - m)
    den = jnp.sum(p, axis=-1, keepdims=True)
    pv = jnp.dot(p.astype(BF16), vw, preferred_element_type=F32) / den
    lse4 = m + jnp.log(den)
    out = jnp.zeros((ATT_TQ, ATT_OUT), F32)
    lse = jnp.zeros((ATT_TQ, ATT_OUT), F32)
    for h in range(nh):
        mine = head_of_col == h
        hrows = slice(h * ATT_TQ, (h + 1) * ATT_TQ)
        out = jnp.where(mine, pv[hrows], out)
        lse = jnp.where(mine, lse4[hrows], lse)
    return out, lse


ATT_CHUNK = 2048


def _attn_strided_kernel(q0_ref, q1_ref, k0_ref, k1_ref, v0_ref, v1_ref, o_ref, lse_ref, o_scr, lse_scr, *, dil):
    i = pl.program_id(1)
    ls = q0_ref.shape[1] // dil

    def gather(lo_ref, hi_ref, first, n):
        rows = pl.ds(first, n, stride=dil)
        return jnp.concatenate([lo_ref[0, rows, :], hi_ref[0, rows, :]], axis=1).astype(BF16)

    units = ATT_CHUNK // ATT_TQ
    shift = dil.bit_length() - 1
    head_of_col = lax.broadcasted_iota(jnp.int32, (1, ATT_OUT), 1) // HEAD_DIM
    row_iota = lax.broadcasted_iota(jnp.int32, (HEADS_PER_GROUP * ATT_TQ, ATT_WINDOW), 0) % ATT_TQ
    rel = row_iota - lax.broadcasted_iota(jnp.int32, (HEADS_PER_GROUP * ATT_TQ, ATT_WINDOW), 1)

    def unit(u, carry):
        r = u & (dil - 1)
        sb = u >> shift
        m0 = i * (ATT_CHUNK // dil) + sb * ATT_TQ
        start = jnp.clip(m0 - ATT_RADIUS, 0, ls - ATT_WINDOW)
        q = gather(q0_ref, q1_ref, m0 * dil + r, ATT_TQ)
        kw = gather(k0_ref, k1_ref, start * dil + r, ATT_WINDOW)
        vw = gather(v0_ref, v1_ref, start * dil + r, ATT_WINDOW)
        out, lse = _attn_unit(q, kw, vw, rel, m0 - start, head_of_col)
        dst = pl.ds(sb * ATT_TQ * dil + r, ATT_TQ, stride=dil)
        for half in range(2):
            lanes = slice(half * LANES, (half + 1) * LANES)
            o_scr[half, dst, :] = out[:, lanes]
            lse_scr[half, dst, :] = lse[:, lanes]
        return carry
    lax.fori_loop(0, units, unit, 0, unroll=2)
    for half in range(2):
        lanes = slice(half * LANES, (half + 1) * LANES)
        o_ref[0, :, lanes] = o_scr[half]
        lse_ref[0, :, lanes] = lse_scr[half]


def _attention_strided(qkv_g, dil, batch, seq_len):
    view = qkv_g.reshape(batch, seq_len, 3 * ATT_OUT)
    halves = [pl.BlockSpec((1, seq_len, LANES), lambda b, i, c=c: (b, 0, c), pipeline_mode=pl.Buffered(1))
              for c in range(3 * ATT_OUT // LANES)]
    o_spec = pl.BlockSpec((1, ATT_CHUNK, ATT_OUT), lambda b, i: (b, i, 0))
    kern = lambda *refs: _attn_strided_kernel(*refs, dil=dil)
    o, lse = pl.pallas_call(
        kern,
        out_shape=(jax.ShapeDtypeStruct((batch, seq_len, ATT_OUT), F32),
                   jax.ShapeDtypeStruct((batch, seq_len, ATT_OUT), F32)),
        grid=(batch, seq_len // ATT_CHUNK),
        in_specs=halves,
        out_specs=(o_spec, o_spec),
        scratch_shapes=[pltpu.VMEM((2, ATT_CHUNK, LANES), F32), pltpu.VMEM((2, ATT_CHUNK, LANES), F32)],
        compiler_params=_params("arbitrary", "arbitrary"),
        name=f"dilated_attn_d{dil}",
    )(*([view] * len(halves)))
    T = batch * seq_len
    return o.reshape(T, ATT_OUT), lse.reshape(T, ATT_OUT)


def _attention_group(qkv_g, dil, batch, seq_len):
    ls = seq_len // dil
    tq = ATT_SUB * ATT_TQ
    view = qkv_g.reshape(batch, ls, dil * 3 * ATT_OUT)

    def col(which):
        return lambda b, r, i: (b, 0, r * 3 + which)

    q_map = lambda b, r, i: (b, i, r * 3)
    o_map = lambda b, r, i: (b, i, r)
    o, lse = pl.pallas_call(
        _attn_kernel,
        out_shape=(jax.ShapeDtypeStruct((batch, ls, dil * ATT_OUT), BF16),
                   jax.ShapeDtypeStruct((batch, ls, dil * ATT_OUT), F32)),
        grid=(batch, dil, ls // tq),
        in_specs=[pl.BlockSpec((1, tq, ATT_OUT), q_map),
                  pl.BlockSpec((1, ls, ATT_OUT), col(1)),
                  pl.BlockSpec((1, ls, ATT_OUT), col(2))],
        out_specs=(pl.BlockSpec((1, tq, ATT_OUT), o_map), pl.BlockSpec((1, tq, ATT_OUT), o_map)),
        compiler_params=_params("arbitrary", "arbitrary", "arbitrary"),
        name=f"dilated_attn_d{dil}",
    )(view, view, view)
    T = batch * seq_len
    return o.reshape(T, ATT_OUT), lse.reshape(T, ATT_OUT)


def _merge_kernel(x_ref, yhy_ref, o1_ref, o2_ref, o3_ref, l1_ref, l2_ref, l3_ref, h_ref, wg_ref,
                  wh_ref, wa_ref, wo_ref, g1_ref, out_ref):
    l1, l2, l3 = l1_ref[...], l2_ref[...], l3_ref[...]
    m = jnp.maximum(jnp.maximum(l1, l2), l3)
    e1, e2, e3 = jnp.exp(l1 - m), jnp.exp(l2 - m), jnp.exp(l3 - m)
    tot = e1 + e2 + e3
    y_at = (e1 * o1_ref[...].astype(F32) + e2 * o2_ref[...].astype(F32) + e3 * o3_ref[...].astype(F32)) / tot
    a = jnp.dot(yhy_ref[...], wh_ref[0], preferred_element_type=F32)
    b = jnp.dot(y_at.astype(BF16), wa_ref[0], preferred_element_type=F32)
    d_model = x_ref.shape[1]
    gate = jnp.dot(h_ref[...], wg_ref[0], preferred_element_type=F32)
    merged = _sigmoid(gate[:, :d_model]) * a + _sigmoid(gate[:, d_model:]) * b
    upd = jnp.dot(merged.astype(BF16), wo_ref[0], preferred_element_type=F32)
    out_ref[...] = x_ref[...] + g1_ref[0] * upd


def _merge(x, y_hy, attn, h, w_gate, w_br_h, w_br_a, w_out, layer, g1, seq_len):
    T, D = x.shape
    tm = 512
    per_batch = seq_len // tm
    (o1, l1), (o2, l2), (o3, l3) = attn
    row = lambda w: pl.BlockSpec((tm, w), lambda i: (i, 0))
    full = lambda a: pl.BlockSpec((1,) + a.shape[1:], lambda i: (layer, 0, 0))
    return pl.pallas_call(
        _merge_kernel,
        out_shape=jax.ShapeDtypeStruct((T, D), F32),
        grid=(T // tm,),
        in_specs=[row(D), row(HY_WIDTH), row(ATT_OUT), row(ATT_OUT), row(ATT_OUT),
                  row(ATT_OUT), row(ATT_OUT), row(ATT_OUT),
                  row(D), full(w_gate), full(w_br_h), full(w_br_a), full(w_out),
                  pl.BlockSpec((1, 1, D), lambda i: (i // per_batch, 0, 0))],
        out_specs=row(D),
        compiler_params=_params("arbitrary"),
        name="mixer_merge",
    )(x, y_hy, o1, o2, o3, l1, l2, l3, h, w_gate, w_br_h, w_br_a, w_out, g1)


def _router_kernel(x_ref, g_ref, sc_ref, sh_ref, rwt_ref, rb_ref, hg_ref, stats_ref, cnt_ref, base_ref):
    i = pl.program_id(0)
    tm = x_ref.shape[0]

    @pl.when(i == 0)
    def _():
        base_ref[...] = jnp.zeros_like(base_ref)

    x = x_ref[...]
    r = lax.rsqrt(jnp.mean(x * x, axis=-1, keepdims=True) + NORM_EPS)
    h = (x * r) * g_ref[...]
    h = h * (1.0 + sc_ref[0]) + sh_ref[0]

    logits = lax.dot_general(rwt_ref[...], h, (((1,), (1,)), ((), ())),
                             precision=HIGHEST, preferred_element_type=F32)
    scores = _sigmoid(logits)
    biased = scores + rb_ref[...]

    def row(a, k):
        return a[k:k + 1, :]

    sel = jnp.zeros((1, tm), jnp.int32)
    best = None
    for g in range(N_GROUPS):
        a, b, c, d = (row(biased, 4 * g + k) for k in range(4))
        m_ab, n_ab = jnp.maximum(a, b), jnp.minimum(a, b)
        m_cd, n_cd = jnp.maximum(c, d), jnp.minimum(c, d)
        gs = jnp.maximum(m_ab, m_cd) + jnp.maximum(jnp.minimum(m_ab, m_cd), jnp.maximum(n_ab, n_cd))
        if g == 0:
            best = gs
        else:
            better = gs > best
            sel = jnp.where(better, g, sel)
            best = jnp.where(better, gs, best)

    v, u = [], []
    for k in range(EXPERTS_PER_GROUP):
        vk = jnp.zeros((1, tm), F32)
        uk = jnp.zeros((1, tm), F32)
        for g in range(N_GROUPS):
            vk = jnp.where(sel == g, row(biased, 4 * g + k), vk)
            uk = jnp.where(sel == g, row(scores, 4 * g + k), uk)
        v.append(vk)
        u.append(uk)

    i1 = jnp.zeros((1, tm), jnp.int32)
    b1 = v[0]
    for k in range(1, EXPERTS_PER_GROUP):
        gt = v[k] > b1
        i1 = jnp.where(gt, k, i1)
        b1 = jnp.where(gt, v[k], b1)
    i2 = jnp.zeros((1, tm), jnp.int32)
    b2 = jnp.full((1, tm), -jnp.inf, F32)
    for k in range(EXPERTS_PER_GROUP):
        cand = (i1 != k) & (v[k] > b2)
        i2 = jnp.where(cand, k, i2)
        b2 = jnp.where(cand, v[k], b2)

    lo = jnp.minimum(i1, i2)
    hi = jnp.maximum(i1, i2)
    pair = jnp.where(lo == 0, hi - 1, jnp.where(lo == 1, hi + 1, 5))
    bucket = sel * PAIRS_PER_GROUP + pair

    u_lo = jnp.zeros((1, tm), F32)
    u_hi = jnp.zeros((1, tm), F32)
    for k in range(EXPERTS_PER_GROUP):
        u_lo = jnp.where(lo == k, u[k], u_lo)
        u_hi = jnp.where(hi == k, u[k], u_hi)
    tot = u_lo + u_hi
    w_lo = u_lo / tot
    w_hi = u_hi / tot

    rows = lax.broadcasted_iota(jnp.int32, (BUCKET_ROWS, tm), 0)
    onehot = (rows == bucket).astype(F32)
    t_src = lax.broadcasted_iota(jnp.int32, (tm, tm), 0)
    t_dst = lax.broadcasted_iota(jnp.int32, (tm, tm), 1)
    before = (t_src < t_dst).astype(BF16)
    cum = jnp.dot(onehot.astype(BF16), before, preferred_element_type=F32)
    base = base_ref[...]
    rank = jnp.sum(onehot * (cum + base), axis=0, keepdims=True)
    base = base + jnp.sum(onehot, axis=1, keepdims=True)
    base_ref[...] = base
    cnt_ref[...] = jnp.broadcast_to(base, cnt_ref.shape)

    srow = lax.broadcasted_iota(jnp.int32, (8, tm), 0)
    stats_ref[...] = jnp.where(srow == 0, bucket.astype(F32), jnp.where(srow == 1, rank, 0.0))

    grow = lax.broadcasted_iota(jnp.int32, (GATE_COLS, tm), 0)
    gates_t = jnp.where(grow == 0, w_lo, jnp.where(grow == 1, w_hi, 0.0))
    hg_ref[:, :D_MODEL] = h
    hg_ref[:, D_MODEL:] = gates_t.T


def _router(x, norm_g, sc, sh, router_wt, router_b, seq_len):
    T, D = x.shape
    tm = ROUTER_TM
    per_batch = seq_len // tm
    return pl.pallas_call(
        _router_kernel,
        out_shape=(jax.ShapeDtypeStruct((T, D + GATE_COLS), F32),
                   jax.ShapeDtypeStruct((8, T), F32),
                   jax.ShapeDtypeStruct((BUCKET_ROWS, LANES), F32)),
        grid=(T // tm,),
        in_specs=[pl.BlockSpec((tm, D), lambda i: (i, 0)),
                  pl.BlockSpec((1, D), lambda i: (0, 0)),
                  pl.BlockSpec((1, 1, D), lambda i: (i // per_batch, 0, 0)),
                  pl.BlockSpec((1, 1, D), lambda i: (i // per_batch, 0, 0)),
                  pl.BlockSpec((N_EXPERTS, D), lambda i: (0, 0)),
                  pl.BlockSpec((N_EXPERTS, 1), lambda i: (0, 0))],
        out_specs=(pl.BlockSpec((tm, D + GATE_COLS), lambda i: (i, 0)),
                   pl.BlockSpec((8, tm), lambda i: (0, i)),
                   pl.BlockSpec((BUCKET_ROWS, LANES), lambda i: (0, 0))),
        scratch_shapes=[pltpu.VMEM((BUCKET_ROWS, 1), F32)],
        compiler_params=_params("arbitrary"),
        name="moe_router",
    )(x, norm_g, sc, sh, router_wt, router_b)


def _start_row_copies(n_rows, make_copy):
    group = 8

    def body(g, carry):
        base = pl.multiple_of(g * group, group)
        for k in range(group):
            make_copy(base + k).start(priority=k % 2)
        return carry

    lax.fori_loop(0, n_rows // group, body, 0)


def _dispatch_kernel(pos_ref, hg_ref, xs_init_hbm, xs_hbm, sem):
    del xs_init_hbm
    _start_row_copies(ROW_BLOCK, lambda r: pltpu.make_async_copy(
        hg_ref.at[pl.ds(r, 1), :], xs_hbm.at[pl.ds(pos_ref[r], 1), :], sem))
    pltpu.make_async_copy(hg_ref, xs_hbm.at[pl.ds(0, ROW_BLOCK), :], sem).wait()


def _dispatch(pos, hg, n_rows):
    T, W = hg.shape
    zeros = jnp.zeros((n_rows, W), F32)
    return pl.pallas_call(
        _dispatch_kernel,
        out_shape=jax.ShapeDtypeStruct((n_rows, W), F32),
        grid=(T // ROW_BLOCK,),
        in_specs=[pl.BlockSpec((ROW_BLOCK,), lambda i: (i,), memory_space=pltpu.SMEM),
                  pl.BlockSpec((ROW_BLOCK, W), lambda i: (i, 0)),
                  pl.BlockSpec(memory_space=pl.ANY)],
        out_specs=pl.BlockSpec(memory_space=pl.ANY),
        scratch_shapes=[pltpu.SemaphoreType.DMA],
        input_output_aliases={2: 0},
        compiler_params=_params("arbitrary"),
        name="moe_dispatch",
    )(pos, hg, zeros)


def _expert_kernel(e_lo_ref, e_hi_ref, n_used_ref, xs_ref, w1a, w3a, w2a, w1b, w3b, w2b, y_ref):
    del e_lo_ref, e_hi_ref
    used = pl.program_id(0) < n_used_ref[0]

    @pl.when(jnp.logical_not(used))
    def _():
        y_ref[...] = jnp.zeros_like(y_ref)

    @pl.when(used)
    def _():
        xb = xs_ref[:, :D_MODEL].astype(BF16)
        g_lo = xs_ref[:, D_MODEL:D_MODEL + 1]
        g_hi = xs_ref[:, D_MODEL + 1:D_MODEL + 2]

        def ffn(w1, w3, w2):
            a = jnp.dot(xb, w1[0, 0], preferred_element_type=F32)
            b = jnp.dot(xb, w3[0, 0], preferred_element_type=F32)
            act = (a * _sigmoid(a)) * b
            return jnp.dot(act.astype(BF16), w2[0, 0], preferred_element_type=F32)

        y_ref[...] = g_lo * ffn(w1a, w3a, w2a) + g_hi * ffn(w1b, w3b, w2b)


def _experts(tile_lo, tile_hi, n_used, xs, w1, w3, w2, layer):
    n_rows, W = xs.shape
    D, F = w1.shape[2], w1.shape[3]
    n_tiles = n_rows // EXPERT_TM

    def x_map(j, lo, hi, nu):
        return (jnp.minimum(j, nu[0] - 1), 0)

    def w_lo_map(j, lo, hi, nu):
        return (layer, lo[j], 0, 0)

    def w_hi_map(j, lo, hi, nu):
        return (layer, hi[j], 0, 0)

    grid_spec = pltpu.PrefetchScalarGridSpec(
        num_scalar_prefetch=3,
        grid=(n_tiles,),
        in_specs=[pl.BlockSpec((EXPERT_TM, W), x_map),
                  pl.BlockSpec((1, 1, D, F), w_lo_map), pl.BlockSpec((1, 1, D, F), w_lo_map),
                  pl.BlockSpec((1, 1, F, D), w_lo_map),
                  pl.BlockSpec((1, 1, D, F), w_hi_map), pl.BlockSpec((1, 1, D, F), w_hi_map),
                  pl.BlockSpec((1, 1, F, D), w_hi_map)],
        out_specs=pl.BlockSpec((EXPERT_TM, D), lambda j, lo, hi, nu: (j, 0)),
    )
    return pl.pallas_call(
        _expert_kernel,
        out_shape=jax.ShapeDtypeStruct((n_rows, D), F32),
        grid_spec=grid_spec,
        compiler_params=_params("arbitrary"),
        name="moe_experts",
    )(tile_lo, tile_hi, n_used, xs, w1, w3, w2, w1, w3, w2)


def _combine_kernel(pos_ref, x_ref, g2_ref, ng_ref, sc_ref, sh_ref, ys_hbm, *refs):
    *out_refs, buf, sem = refs
    _start_row_copies(ROW_BLOCK, lambda r: pltpu.make_async_copy(
        ys_hbm.at[pl.ds(pos_ref[r], 1), :], buf.at[pl.ds(r, 1), :], sem))
    pltpu.make_async_copy(ys_hbm.at[pl.ds(0, ROW_BLOCK), :], buf, sem).wait()
    xn = x_ref[...] + g2_ref[0] * buf[...]
    r = lax.rsqrt(jnp.mean(xn * xn, axis=-1, keepdims=True) + NORM_EPS)
    h = (xn * r) * ng_ref[...]
    if len(out_refs) == 2:
        out_refs[0][...] = xn
        h = h * (1.0 + sc_ref[0]) + sh_ref[0]
    out_refs[-1][...] = h.astype(out_refs[-1].dtype)


def _combine(pos, x, g2, ys, norm_g, sc, sh, seq_len, last):
    T, D = x.shape
    per_batch = seq_len // ROW_BLOCK
    row = pl.BlockSpec((ROW_BLOCK, D), lambda i: (i, 0))
    per_b = pl.BlockSpec((1, 1, D), lambda i: (i // per_batch, 0, 0))
    if last:
        out_shape, out_specs = jax.ShapeDtypeStruct((T, D), F32), row
    else:
        out_shape = (jax.ShapeDtypeStruct((T, D), F32), jax.ShapeDtypeStruct((T, D), BF16))
        out_specs = (row, row)
    return pl.pallas_call(
        _combine_kernel,
        out_shape=out_shape,
        grid=(T // ROW_BLOCK,),
        in_specs=[pl.BlockSpec((ROW_BLOCK,), lambda i: (i,), memory_space=pltpu.SMEM),
                  row, per_b, pl.BlockSpec((1, D), lambda i: (0, 0)), per_b, per_b,
                  pl.BlockSpec(memory_space=pl.ANY)],
        out_specs=out_specs,
        scratch_shapes=[pltpu.VMEM((ROW_BLOCK, D), F32), pltpu.SemaphoreType.DMA],
        compiler_params=_params("arbitrary"),
        name="moe_combine",
    )(pos, x, g2, norm_g, sc, sh, ys)


def _moe_layer(x, norm_g, sc2, sh2, g2, router_wt, router_b, w1, w3, w2, layer, seq_len, next_norm, last):
    T, D = x.shape
    hg, stats, counts = _router(x, norm_g, sc2, sh2, router_wt, router_b, seq_len)

    cnt = counts[:N_BUCKETS, 0].astype(jnp.int32)
    padded = ((cnt + EXPERT_TM - 1) // EXPERT_TM) * EXPERT_TM
    ends = jnp.cumsum(padded)
    starts = ends - padded
    bucket = stats[0].astype(jnp.int32)
    pos = starts[bucket] + stats[1].astype(jnp.int32)
    n_tiles = T // EXPERT_TM + N_BUCKETS
    tile_row0 = jnp.arange(n_tiles, dtype=jnp.int32) * EXPERT_TM
    tile_bucket = jnp.sum((ends[None, :] <= tile_row0[:, None]).astype(jnp.int32), axis=1)
    tile_bucket = jnp.minimum(tile_bucket, N_BUCKETS - 1)
    grp, pair = tile_bucket // PAIRS_PER_GROUP, tile_bucket % PAIRS_PER_GROUP
    pair_lo = jnp.array([0, 0, 0, 1, 1, 2], jnp.int32)[pair]
    pair_hi = jnp.array([1, 2, 3, 2, 3, 3], jnp.int32)[pair]
    tile_lo = grp * EXPERTS_PER_GROUP + pair_lo
    tile_hi = grp * EXPERTS_PER_GROUP + pair_hi
    n_used = (ends[-1] // EXPERT_TM).astype(jnp.int32).reshape(1)

    xs = _dispatch(pos, hg, n_tiles * EXPERT_TM)
    ys = _experts(tile_lo, tile_hi, n_used, xs, w1, w3, w2, layer)
    return _combine(pos, x, g2, ys, *next_norm, seq_len, last)


def kernel(x, c, norm1_g, norm2_g, w_ada, b_ada, w_in, w_sc, b_sc, hf_w1, hf_b1, hf_w2, hf_b2, hf_w3, hf_b3,
           hf_w4, hf_freq, hy_bias, w_br_h, w_br_a, w_out, router_w, router_bias, moe_w1, moe_w3, moe_w2, final_g):
    B, L, D = x.shape
    T = B * L
    C = HY_WIDTH
    xt = x.reshape(T, D)

    rope = _rope_tables(L)
    feat, deltas = _hyena_features(L)
    mats = _dft_matrices()

    c_pad = jnp.pad(c, ((0, 8 - B), (0, 0)))
    mod = _ada(c_pad, w_ada, b_ada)[:, :B]
    router_wt = router_w.T
    router_b = router_bias.reshape(N_EXPERTS, 1)
    w_in_b, w_br_h_b, w_br_a_b, w_out_b = (w.astype(BF16) for w in (w_in, w_br_h, w_br_a, w_out))
    moe_w1_b, moe_w3_b, moe_w2_b = (w.astype(BF16) for w in (moe_w1, moe_w3, moe_w2))
    w_gate_b = w_in_b[:, :, HY_COLS + QKV_COLS:]

    half = LANES // 2

    def diag2(w):
        wp = jnp.pad(w, ((0, 0), (0, half - w.shape[1]), (0, half - w.shape[2])))
        zero = jnp.zeros_like(wp)
        return jnp.concatenate([jnp.concatenate([wp, zero], axis=2), jnp.concatenate([zero, wp], axis=2)], axis=1)

    twice = lambda v: jnp.concatenate([v, v], axis=1)[:, None]
    w4_zero = jnp.zeros_like(hf_w4)
    w4_halves = jnp.stack([jnp.concatenate([hf_w4, w4_zero], axis=1),
                           jnp.concatenate([w4_zero, hf_w4], axis=1)], axis=1)
    k2 = _hyena_filter(feat, diag2(hf_w1), twice(hf_b1), diag2(hf_w2), twice(hf_b2), diag2(hf_w3), twice(hf_b3),
                       w4_halves, twice(hf_freq), deltas)
    kf = _filter_fft(k2.reshape(DEPTH, -1, 2 * C), mats)

    mods = [[mod[i, :, k * D:(k + 1) * D].reshape(B, 1, D) for k in range(6)] for i in range(DEPTH)]
    h = _norm_mod(xt, norm1_g[0].reshape(1, D), mods[0][1], mods[0][0], L)
    for i in range(DEPTH):
        sh1, sc1, g1, sh2, sc2, g2 = mods[i]

        u = _proj(h, w_in_b, i, 0, HY_COLS, tn=HY_WIDTH)
        qkv = [_proj_qkv(h, w_in_b, i, g, rope, L, BF16 if dil == 1 else F32)
               for g, (_, dil) in enumerate(ATT_GROUPS)]

        y_hy = _long_conv(u.reshape(B, L, 3 * C), w_sc[i], b_sc[i], kf, i, hy_bias[i], mats).reshape(T, C)

        attn = [(_attention_group if dil == 1 else _attention_strided)(qkv[g], dil, B, L)
                for g, (_, dil) in enumerate(ATT_GROUPS)]
        xt = _merge(xt, y_hy, attn, h, w_gate_b, w_br_h_b, w_br_a_b, w_out_b, i, g1, L)

        last = i == DEPTH - 1
        if last:
            next_norm = (final_g.reshape(1, D), sc2, sh2)
        else:
            next_norm = (norm1_g[i + 1].reshape(1, D), mods[i + 1][1], mods[i + 1][0])
        res = _moe_layer(xt, norm2_g[i].reshape(1, D), sc2, sh2, g2, router_wt, router_b,
                         moe_w1_b, moe_w3_b, moe_w2_b, i, L, next_norm, last)
        if last:
            return res.reshape(B, L, D)
        xt, h = res
```

```python
import math

import jax
import jax.numpy as jnp
import numpy as np
from jax import lax
from jax.experimental import pallas as pl
from jax.experimental.pallas import tpu as pltpu

D_MODEL = 1024
DEPTH = 2
HY_WIDTH = 768
HY_EMB = 33
HY_FAST_DECAY_PCT = 0.3
HY_SLOW_DECAY_PCT = 1.5
HY_TARGET = 1e-2
HEAD_DIM = 64
ATT_GROUPS = ((128, 1), (512, 4), (2048, 16))
HEADS_PER_GROUP = 4
N_HEADS = HEADS_PER_GROUP * len(ATT_GROUPS)
ATT_WIDTH = N_HEADS * HEAD_DIM
ATT_OUT = HEADS_PER_GROUP * HEAD_DIM
ROPE_THETA = 10000.0
N_EXPERTS = 16
N_GROUPS = 4
EXPERTS_PER_GROUP = N_EXPERTS // N_GROUPS
NORM_EPS = 1e-6
MASK_VALUE = -1e30

LANES = 128
MXU_DIM = 256
VMEM_LIMIT_BYTES = 56 * 1024 * 1024

F32 = jnp.float32
BF16 = jnp.bfloat16
HIGHEST = lax.Precision.HIGHEST

PROJ_TM = 2048
PROJ_TN = MXU_DIM
QKV_TM = 1024
PROJ_CHUNK = 512
HY_COLS = 3 * HY_WIDTH
QKV_COLS = 3 * ATT_WIDTH

FFT_R = 128
FFT_KH = FFT_R // 2 + 1
FFT_KP = 72
FFT_SLAB = 2 * FFT_KP
FFT_G_PITCH = FFT_SLAB + 8
FFT_T_PITCH = FFT_R + 8
CONV_CB = 128
FILTER_TM = 1024
FFT_UNROLL_TIME = 32
FFT_UNROLL_FREQ = 65

PAIRS_PER_GROUP = 6
N_BUCKETS = N_GROUPS * PAIRS_PER_GROUP
BUCKET_ROWS = 32
ROUTER_TM = 512
EXPERT_TM = 256
ROW_BLOCK = 1024
GATE_COLS = LANES


def _params(*sem):
    return pltpu.CompilerParams(dimension_semantics=sem, vmem_limit_bytes=VMEM_LIMIT_BYTES)


def _sigmoid(x):
    return 1.0 / (1.0 + jnp.exp(-x))


def _store_time_padded(o_ref, val):
    groups = val.shape[0] // FFT_R
    o_ref[:, :FFT_R, :] = val.reshape(groups, FFT_R, val.shape[1])
    o_ref[:, FFT_R:, :] = jnp.zeros((groups, FFT_T_PITCH - FFT_R, val.shape[1]), o_ref.dtype)


def _time_padded_shape(rows, cols):
    return jax.ShapeDtypeStruct((rows // FFT_R, FFT_T_PITCH, cols), F32)


def _ada_kernel(c_ref, w_ref, b_ref, o_ref):
    c = c_ref[...]
    c_act = c * _sigmoid(c)
    o_ref[0] = jnp.dot(c_act, w_ref[0], precision=HIGHEST, preferred_element_type=F32) + b_ref[0]


def _ada(c_pad, w_ada, b_ada):
    depth, D, N = w_ada.shape
    rows = c_pad.shape[0]
    tn = N // 4
    return pl.pallas_call(
        _ada_kernel,
        out_shape=jax.ShapeDtypeStruct((depth, rows, N), F32),
        grid=(depth, N // tn),
        in_specs=[pl.BlockSpec((rows, D), lambda l, j: (0, 0)),
                  pl.BlockSpec((1, D, tn), lambda l, j: (l, 0, j)),
                  pl.BlockSpec((1, 1, tn), lambda l, j: (l, 0, j))],
        out_specs=pl.BlockSpec((1, rows, tn), lambda l, j: (l, 0, j)),
        compiler_params=_params("arbitrary", "arbitrary"),
        name="ada_mod",
    )(c_pad, w_ada, b_ada.reshape(depth, 1, N))


def _norm_mod_kernel(x_ref, g_ref, sc_ref, sh_ref, o_ref):
    x = x_ref[...]
    r = lax.rsqrt(jnp.mean(x * x, axis=-1, keepdims=True) + NORM_EPS)
    h = (x * r) * g_ref[...]
    o_ref[...] = (h * (1.0 + sc_ref[0]) + sh_ref[0]).astype(o_ref.dtype)


def _norm_mod(x, g, sc, sh, seq_len):
    T, D = x.shape
    tm = 1024
    per_batch = seq_len // tm
    return pl.pallas_call(
        _norm_mod_kernel,
        out_shape=jax.ShapeDtypeStruct((T, D), BF16),
        grid=(T // tm,),
        in_specs=[pl.BlockSpec((tm, D), lambda i: (i, 0)),
                  pl.BlockSpec((1, D), lambda i: (0, 0)),
                  pl.BlockSpec((1, 1, D), lambda i: (i // per_batch, 0, 0)),
                  pl.BlockSpec((1, 1, D), lambda i: (i // per_batch, 0, 0))],
        out_specs=pl.BlockSpec((tm, D), lambda i: (i, 0)),
        compiler_params=_params("arbitrary"),
        name="norm_mod",
    )(x, g, sc, sh)


def _proj_kernel(h_ref, w_ref, o_ref):
    o_ref[...] = jnp.dot(h_ref[...], w_ref[0], preferred_element_type=F32).astype(o_ref.dtype)


def _proj(h, w, layer, col0, n_cols, tn):
    T, D = h.shape
    tm = PROJ_TM
    assert col0 % tn == 0 and n_cols % tn == 0
    off = col0 // tn
    return pl.pallas_call(
        _proj_kernel,
        out_shape=jax.ShapeDtypeStruct((T, n_cols), BF16),
        grid=(T // tm, n_cols // tn),
        in_specs=[pl.BlockSpec((tm, D), lambda i, j: (i, 0)),
                  pl.BlockSpec((1, D, tn), lambda i, j: (layer, 0, off + j))],
        out_specs=pl.BlockSpec((tm, tn), lambda i, j: (i, j)),
        compiler_params=_params("arbitrary", "arbitrary"),
        name="proj",
    )(h, w)


def _proj_qkv_kernel(h_ref, wq_ref, wk_ref, wv_ref, cos_ref, sin_ref, rot_ref, o_ref):
    tm, tn = h_ref.shape[0], PROJ_TN
    rot = rot_ref[...]
    for c in range(tm // PROJ_CHUNK):
        rows = slice(c * PROJ_CHUNK, (c + 1) * PROJ_CHUNK)
        hc = h_ref[rows, :]
        cos = jnp.concatenate([cos_ref[rows, :], cos_ref[rows, :]], axis=1)
        sin = jnp.concatenate([sin_ref[rows, :], sin_ref[rows, :]], axis=1)

        def roped(w_ref):
            acc = jnp.dot(hc, w_ref[0], preferred_element_type=F32)
            swapped = jnp.dot(acc.astype(BF16), rot, preferred_element_type=F32)
            return acc * cos + swapped * sin

        o_ref[rows, 0:tn] = (roped(wq_ref) * HEAD_DIM ** -0.5).astype(o_ref.dtype)
        o_ref[rows, tn:2 * tn] = roped(wk_ref).astype(o_ref.dtype)
        o_ref[rows, 2 * tn:3 * tn] = jnp.dot(hc, wv_ref[0], preferred_element_type=F32).astype(o_ref.dtype)


def _proj_qkv(h, w, layer, group, rope, seq_len, out_dtype):
    T, D = h.shape
    tm, tn = QKV_TM, PROJ_TN
    n_groups = len(ATT_GROUPS)
    off = HY_COLS // tn + group
    cos_t, sin_t, rot = rope
    per_batch = seq_len // tm
    tab_spec = pl.BlockSpec((tm, LANES), lambda i: (i % per_batch, 0))
    w_spec = lambda kind: pl.BlockSpec((1, D, tn), lambda i: (layer, 0, off + n_groups * kind))
    return pl.pallas_call(
        _proj_qkv_kernel,
        out_shape=jax.ShapeDtypeStruct((T, 3 * tn), out_dtype),
        grid=(T // tm,),
        in_specs=[pl.BlockSpec((tm, D), lambda i: (i, 0)), w_spec(0), w_spec(1), w_spec(2),
                  tab_spec, tab_spec, pl.BlockSpec((tn, tn), lambda i: (0, 0))],
        out_specs=pl.BlockSpec((tm, 3 * tn), lambda i: (i, 0)),
        compiler_params=_params("arbitrary"),
        name="proj_qkv",
    )(h, w, w, w, cos_t, sin_t, rot)


def _rope_tables(seq_len):
    half = HEAD_DIM // 2
    inv = ROPE_THETA ** (-jnp.arange(half, dtype=F32) / half)
    ang = jnp.arange(seq_len, dtype=F32)[:, None] * inv[None, :]
    reps = LANES // half
    cos_t = jnp.tile(jnp.cos(ang), (1, reps))
    sin_t = jnp.tile(jnp.sin(ang), (1, reps))
    rot = np.zeros((PROJ_TN, PROJ_TN), np.float32)
    for j in range(PROJ_TN):
        if j % HEAD_DIM < half:
            rot[j + half, j] = -1.0
        else:
            rot[j - half, j] = 1.0
    return cos_t, sin_t, jnp.asarray(rot, BF16)


def _hyena_filter_kernel(feat_ref, w1_ref, b1_ref, w2_ref, b2_ref, w3_ref, b3_ref, w4_ref, fr_ref,
                         dl_ref, o_ref):
    i = pl.program_id(1)
    half = feat_ref.shape[0]
    z = feat_ref[...]
    fr = fr_ref[0]
    h = jnp.sin(fr * (jnp.dot(z, w1_ref[0], precision=HIGHEST, preferred_element_type=F32) + b1_ref[0]))
    h = jnp.sin(fr * (jnp.dot(h, w2_ref[0], precision=HIGHEST, preferred_element_type=F32) + b2_ref[0]))
    h = jnp.sin(fr * (jnp.dot(h, w3_ref[0], precision=HIGHEST, preferred_element_type=F32) + b3_ref[0]))
    rate = jnp.abs(dl_ref[...])
    parts = []
    for s in range(2):
        k = jnp.dot(h, w4_ref[0, s], precision=HIGHEST, preferred_element_type=F32)
        t = z[:, s * (LANES // 2):s * (LANES // 2) + 1]
        parts.append(k * jnp.exp(-t * rate))
    val = jnp.concatenate(parts, axis=0)
    pos = i * 2 * half + lax.broadcasted_iota(jnp.int32, (2 * half, 1), 0)
    is_bwd = lax.broadcasted_iota(jnp.int32, (1, val.shape[1]), 1) >= HY_WIDTH
    _store_time_padded(o_ref.at[0], jnp.where((pos == 0) & is_bwd, 0.0, val))


def _hyena_filter(feat, w1, b1, w2, b2, w3, b3, w4, freq, deltas2):
    n_rows = 2 * feat.shape[0]
    depth = w1.shape[0]
    C2 = 2 * HY_WIDTH
    tm = FILTER_TM
    per_layer = lambda r, c: pl.BlockSpec((1, r, c), lambda l, i: (l, 0, 0))
    shape = _time_padded_shape(n_rows, C2)
    return pl.pallas_call(
        _hyena_filter_kernel,
        out_shape=jax.ShapeDtypeStruct((depth,) + shape.shape, shape.dtype),
        grid=(depth, n_rows // tm),
        in_specs=[pl.BlockSpec((tm // 2, LANES), lambda l, i: (i, 0)),
                  per_layer(LANES, LANES), per_layer(1, LANES),
                  per_layer(LANES, LANES), per_layer(1, LANES),
                  per_layer(LANES, LANES), per_layer(1, LANES),
                  pl.BlockSpec((1, 2, LANES, C2), lambda l, i: (l, 0, 0, 0)),
                  per_layer(1, LANES),
                  pl.BlockSpec((1, C2), lambda l, i: (0, 0))],
        out_specs=pl.BlockSpec((1, tm // FFT_R, FFT_T_PITCH, C2), lambda l, i: (l, i, 0, 0)),
        compiler_params=_params("arbitrary", "arbitrary"),
        name="hyena_filter",
    )(feat, w1, b1, w2, b2, w3, b3, w4, freq, deltas2)


def _hyena_features(seq_len):
    L = seq_len
    half = LANES // 2
    t = jnp.linspace(0.0, 1.0, L, dtype=F32)[:, None]
    bands = (HY_EMB - 1) // 2
    w = 2.0 * math.pi * jnp.arange(L, dtype=F32)[:, None] / L
    f = jnp.linspace(1e-4, bands - 1, bands, dtype=F32)[None, :]
    z = jnp.concatenate([t, jnp.cos(f * w), -jnp.sin(f * w)], axis=-1)
    z = jnp.pad(z, ((0, 0), (0, half - HY_EMB)))
    z = z.reshape(L // FILTER_TM, 2, FILTER_TM // 2, half).transpose(0, 2, 1, 3).reshape(L // 2, LANES)
    max_decay = math.log(HY_TARGET) / HY_FAST_DECAY_PCT
    min_decay = math.log(HY_TARGET) / HY_SLOW_DECAY_PCT
    deltas = jnp.linspace(min_decay, max_decay, HY_WIDTH, dtype=F32)[None, :]
    return z, jnp.concatenate([deltas, deltas], axis=1)


def _dft_matrices():
    R, KH, KP = FFT_R, FFT_KH, FFT_KP
    N = R * R
    n1 = np.arange(R)[:, None, None]
    k2 = np.arange(KP)[None, :, None]
    n2 = np.arange(R)[None, None, :]
    phase = 2.0 * np.pi * ((n2 * k2 % R) / R + (n1 * k2) / N)
    live = (k2 < KH)
    a1 = np.concatenate([np.cos(phase) * live, -np.sin(phase) * live], axis=1)
    wgt = np.where((k2 == 0) | (k2 == R // 2), 1.0, 2.0) * live / N
    b1 = np.concatenate([np.cos(phase) * wgt, -np.sin(phase) * wgt], axis=1)
    b1 = np.transpose(b1, (0, 2, 1))[:, :R // 2, :]
    th = 2.0 * np.pi * (np.arange(R)[:, None] * np.arange(R)[None, :] % R) / R
    c, s = np.cos(th), np.sin(th)
    w2f = np.block([[c, s], [-s, c]])
    w2i = np.block([[c, -s], [s, c]])
    as_bf = lambda a: jnp.asarray(a.astype(np.float32)).astype(BF16)
    return dict(a1=as_bf(a1[:, :, :R // 2]), b1=as_bf(b1), w2f=as_bf(w2f), w2i=as_bf(w2i))


def _fft_stage1(src_ref, a1_ref, g_ref, n_rows):
    def body(n1, carry):
        xs = src_ref[pl.ds(n1, n_rows, stride=FFT_T_PITCH), :].astype(BF16)
        slab = jnp.dot(a1_ref[n1], xs, preferred_element_type=F32)
        g_ref[pl.ds(pl.multiple_of(n1 * FFT_G_PITCH, 8), FFT_SLAB), :] = slab
        return carry
    lax.fori_loop(0, FFT_R, body, 0, unroll=FFT_UNROLL_TIME)


def _load_freq_rows(g_ref, k2):
    re = g_ref[pl.ds(k2, FFT_R, stride=FFT_G_PITCH), :]
    im = g_ref[pl.ds(FFT_KP + k2, FFT_R, stride=FFT_G_PITCH), :]
    return jnp.concatenate([re, im], axis=0)


def _filter_fft_kernel(fwd_ref, bwd_ref, a1_ref, w2_ref, o_ref, gf_ref, gb_ref):
    n2_rows = fwd_ref.shape[1] // FFT_T_PITCH
    cb = fwd_ref.shape[2]
    fwd, bwd = fwd_ref.at[0], bwd_ref.at[0]

    def stage1(n1, carry):
        rows = pl.ds(n1, n2_rows, stride=FFT_T_PITCH)
        xs = jnp.concatenate([fwd[rows, :], bwd[rows, :]], axis=1).astype(BF16)
        slab = jnp.dot(a1_ref[n1], xs, preferred_element_type=F32)
        dst = pl.ds(pl.multiple_of(n1 * FFT_G_PITCH, 8), FFT_SLAB)
        gf_ref[dst, :] = slab[:, :cb]
        gb_ref[dst, :] = slab[:, cb:]
        return carry
    lax.fori_loop(0, FFT_R, stage1, 0, unroll=FFT_UNROLL_TIME)

    def body(k2, carry):
        gk = jnp.concatenate([_load_freq_rows(gf_ref, k2), _load_freq_rows(gb_ref, k2)], axis=1).astype(BF16)
        x = jnp.dot(w2_ref[...], gk, preferred_element_type=F32)
        f, b = x[:, :cb], x[:, cb:]
        spec = jnp.concatenate([f[:FFT_R] + b[:FFT_R], f[FFT_R:] - b[FFT_R:]], axis=0)
        o_ref[0, k2] = spec.astype(o_ref.dtype)
        return carry
    lax.fori_loop(0, FFT_KH, body, 0, unroll=FFT_UNROLL_FREQ)


def _filter_fft(k2, mats):
    depth, n_rows, _ = k2.shape
    C = HY_WIDTH
    cb = CONV_CB
    ncb = C // cb
    return pl.pallas_call(
        _filter_fft_kernel,
        out_shape=jax.ShapeDtypeStruct((depth, FFT_KH, 2 * FFT_R, C), BF16),
        grid=(depth, ncb),
        in_specs=[pl.BlockSpec((1, n_rows, cb), lambda l, j: (l, 0, j)),
                  pl.BlockSpec((1, n_rows, cb), lambda l, j: (l, 0, ncb + j)),
                  pl.BlockSpec((FFT_R, FFT_SLAB, FFT_R // 2), lambda l, j: (0, 0, 0)),
                  pl.BlockSpec((2 * FFT_R, 2 * FFT_R), lambda l, j: (0, 0))],
        out_specs=pl.BlockSpec((1, FFT_KH, 2 * FFT_R, cb), lambda l, j: (l, 0, 0, j)),
        scratch_shapes=[pltpu.VMEM((FFT_R * FFT_G_PITCH, cb), F32), pltpu.VMEM((FFT_R * FFT_G_PITCH, cb), F32)],
        compiler_params=_params("arbitrary", "arbitrary"),
        name="hyena_filter_fft",
    )(k2, k2, mats["a1"], mats["w2f"])


SC_HALO = 16


def _long_conv_kernel(x0_ref, x1_ref, v_ref, w0_ref, w1_ref, wv_ref, b0_ref, b1s_ref, bv_ref, kf_ref, bias_ref,
                      a1_ref, b1_ref, w2f_ref, w2i_ref, o_ref, g_ref, zv_ref, y_ref, stage_ref):
    seq_len = x0_ref.shape[1]
    n_groups = seq_len // FFT_R

    def short_conv(part, u_ref, w_ref, b_ref, g):
        row0 = pl.multiple_of(g * FFT_R, FFT_R)
        lo = pl.multiple_of(jnp.maximum(row0 - SC_HALO, 0), SC_HALO)
        hi = pl.multiple_of(jnp.minimum(row0 + FFT_R, seq_len - SC_HALO), SC_HALO)
        st = stage_ref.at[part]
        st[0:SC_HALO, :] = jnp.where(g > 0, u_ref[0, pl.ds(lo, SC_HALO), :].astype(F32), 0.0)
        st[SC_HALO:SC_HALO + FFT_R, :] = u_ref[0, pl.ds(row0, FFT_R), :].astype(F32)
        st[SC_HALO + FFT_R:, :] = jnp.where(g < n_groups - 1, u_ref[0, pl.ds(hi, SC_HALO), :].astype(F32), 0.0)
        w = w_ref[...]
        return (st[SC_HALO - 1:SC_HALO - 1 + FFT_R, :] * w[0:1] + st[SC_HALO:SC_HALO + FFT_R, :] * w[1:2]
                + st[SC_HALO + 1:SC_HALO + 1 + FFT_R, :] * w[2:3] + b_ref[...])

    def gate_body(g, carry):
        zv = short_conv(2, v_ref, wv_ref, bv_ref, g) * short_conv(1, x1_ref, w1_ref, b1s_ref, g)
        zv_ref[pl.ds(pl.multiple_of(g * FFT_T_PITCH, 8), FFT_R), :] = zv
        return carry
    lax.fori_loop(0, n_groups, gate_body, 0, unroll=2)

    _fft_stage1(zv_ref, a1_ref, g_ref, n_groups)

    def freq_body(k2, carry):
        gk = _load_freq_rows(g_ref, k2).astype(BF16)
        x = jnp.dot(w2f_ref[...], gk, preferred_element_type=F32)
        kf = kf_ref[0, k2].astype(F32)
        xr, xi = x[:FFT_R], x[FFT_R:]
        kr, ki = kf[:FFT_R], kf[FFT_R:]
        p = jnp.concatenate([xr * kr - xi * ki, xr * ki + xi * kr], axis=0).astype(BF16)
        hk = jnp.dot(w2i_ref[...], p, preferred_element_type=F32)
        g_ref[pl.ds(k2, FFT_R, stride=FFT_G_PITCH), :] = hk[:FFT_R]
        g_ref[pl.ds(FFT_KP + k2, FFT_R, stride=FFT_G_PITCH), :] = hk[FFT_R:]
        return carry
    lax.fori_loop(0, FFT_KH, freq_body, 0, unroll=FFT_UNROLL_FREQ)

    def time_body(n1, carry):
        slab = g_ref[pl.ds(pl.multiple_of(n1 * FFT_G_PITCH, 8), FFT_SLAB), :].astype(BF16)
        y_ref[pl.ds(n1, n_groups, stride=FFT_T_PITCH), :] = jnp.dot(b1_ref[n1], slab, preferred_element_type=F32)
        return carry
    lax.fori_loop(0, FFT_R, time_body, 0, unroll=FFT_UNROLL_TIME)

    bias = bias_ref[...]

    def out_body(g, carry):
        rows = pl.ds(pl.multiple_of(g * FFT_T_PITCH, 8), FFT_R)
        x0 = short_conv(0, x0_ref, w0_ref, b0_ref, g)
        y = (y_ref[rows, :] + zv_ref[rows, :] * bias) * x0
        o_ref[0, pl.ds(pl.multiple_of(g * FFT_R, FFT_R), FFT_R), :] = y.astype(o_ref.dtype)
        return carry
    lax.fori_loop(0, n_groups, out_body, 0, unroll=2)


def _long_conv(u, w_sc, b_sc, kf, layer, hy_bias, mats):
    B, L, _ = u.shape
    C = HY_WIDTH
    cb = CONV_CB
    ncb = C // cb
    lp = L // FFT_R * FFT_T_PITCH
    part = lambda p: pl.BlockSpec((1, L, cb), lambda j, b: (b, 0, p * ncb + j))
    wpart = lambda p: pl.BlockSpec((3, cb), lambda j, b: (0, p * ncb + j))
    bpart = lambda p: pl.BlockSpec((1, cb), lambda j, b: (0, p * ncb + j))
    b2 = b_sc.reshape(1, 3 * C)
    return pl.pallas_call(
        _long_conv_kernel,
        out_shape=jax.ShapeDtypeStruct((B, L, C), BF16),
        grid=(ncb, B),
        in_specs=[part(0), part(1), part(2), wpart(0), wpart(1), wpart(2), bpart(0), bpart(1), bpart(2),
                  pl.BlockSpec((1, FFT_KH, 2 * FFT_R, cb), lambda j, b: (layer, 0, 0, j)),
                  pl.BlockSpec((1, cb), lambda j, b: (0, j)),
                  pl.BlockSpec((FFT_R, FFT_SLAB, FFT_R // 2), lambda j, b: (0, 0, 0)),
                  pl.BlockSpec((FFT_R, FFT_R // 2, FFT_SLAB), lambda j, b: (0, 0, 0)),
                  pl.BlockSpec((2 * FFT_R, 2 * FFT_R), lambda j, b: (0, 0)),
                  pl.BlockSpec((2 * FFT_R, 2 * FFT_R), lambda j, b: (0, 0))],
        out_specs=pl.BlockSpec((1, L, cb), lambda j, b: (b, 0, j)),
        scratch_shapes=[pltpu.VMEM((FFT_R * FFT_G_PITCH, cb), F32), pltpu.VMEM((lp, cb), F32),
                        pltpu.VMEM((lp, cb), F32), pltpu.VMEM((3, FFT_R + 2 * SC_HALO, cb), F32)],
        compiler_params=_params("arbitrary", "arbitrary"),
        name="hyena_long_conv",
    )(u, u, u, w_sc, w_sc, w_sc, b2, b2, b2, kf, hy_bias.reshape(1, C),
      mats["a1"], mats["b1"], mats["w2f"], mats["w2i"])


ATT_TQ = 128
ATT_RADIUS = 64
ATT_WINDOW = ATT_TQ + 2 * ATT_RADIUS


ATT_SUB = 4


def _attn_kernel(q_ref, k_ref, v_ref, o_ref, lse_ref):
    i = pl.program_id(2)
    ls = k_ref.shape[1]
    nh = HEADS_PER_GROUP
    head_of_col = lax.broadcasted_iota(jnp.int32, (1, ATT_OUT), 1) // HEAD_DIM
    row_iota = lax.broadcasted_iota(jnp.int32, (nh * ATT_TQ, ATT_WINDOW), 0) % ATT_TQ
    col_iota = lax.broadcasted_iota(jnp.int32, (nh * ATT_TQ, ATT_WINDOW), 1)
    rel = row_iota - col_iota
    for sub in range(ATT_SUB):
        q0 = (i * ATT_SUB + sub) * ATT_TQ
        rows = slice(sub * ATT_TQ, (sub + 1) * ATT_TQ)
        q = q_ref[0, rows, :]
        start = jnp.clip(q0 - ATT_RADIUS, 0, ls - ATT_WINDOW)
        start = pl.multiple_of(start, ATT_RADIUS)
        kw = k_ref[0, pl.ds(start, ATT_WINDOW), :]
        vw = v_ref[0, pl.ds(start, ATT_WINDOW), :]
        out, lse = _attn_unit(q, kw, vw, rel, q0 - start, head_of_col)
        o_ref[0, rows, :] = out.astype(o_ref.dtype)
        lse_ref[0, rows, :] = lse


def _attn_unit(q, kw, vw, rel, q_minus_start, head_of_col):
    nh = HEADS_PER_GROUP
    band = jnp.abs(q_minus_start + rel) <= ATT_RADIUS
    zero = jnp.zeros_like(q)
    q4 = jnp.concatenate([jnp.where(head_of_col == h, q, zero) for h in range(nh)], axis=0)
    s = lax.dot_general(q4, kw, (((1,), (1,)), ((), ())), preferred_element_type=F32)
    s = jnp.where(band, s, MASK_VALUE)
    m = jnp.max(s, axis=-1, keepdims=True)
    p = jnp.exp(s - m)
    den = jnp.sum(p, axis=-1, keepdims=True)
    pv = jnp.dot(p.astype(BF16), vw, preferred_element_type=F32) / den
    lse4 = m + jnp.log(den)
    out = jnp.zeros((ATT_TQ, ATT_OUT), F32)
    lse = jnp.zeros((ATT_TQ, ATT_OUT), F32)
    for h in range(nh):
        mine = head_of_col == h
        hrows = slice(h * ATT_TQ, (h + 1) * ATT_TQ)
        out = jnp.where(mine, pv[hrows], out)
        lse = jnp.where(mine, lse4[hrows], lse)
    return out, lse


ATT_CHUNK = 2048


def _attn_strided_kernel(q0_ref, q1_ref, k0_ref, k1_ref, v0_ref, v1_ref, o_ref, lse_ref, o_scr, lse_scr, *, dil):
    i = pl.program_id(1)
    ls = q0_ref.shape[1] // dil

    def gather(lo_ref, hi_ref, first, n):
        rows = pl.ds(first, n, stride=dil)
        return jnp.concatenate([lo_ref[0, rows, :], hi_ref[0, rows, :]], axis=1).astype(BF16)

    units = ATT_CHUNK // ATT_TQ
    shift = dil.bit_length() - 1
    head_of_col = lax.broadcasted_iota(jnp.int32, (1, ATT_OUT), 1) // HEAD_DIM
    row_iota = lax.broadcasted_iota(jnp.int32, (HEADS_PER_GROUP * ATT_TQ, ATT_WINDOW), 0) % ATT_TQ
    rel = row_iota - lax.broadcasted_iota(jnp.int32, (HEADS_PER_GROUP * ATT_TQ, ATT_WINDOW), 1)

    def unit(u, carry):
        r = u & (dil - 1)
        sb = u >> shift
        m0 = i * (ATT_CHUNK // dil) + sb * ATT_TQ
        start = jnp.clip(m0 - ATT_RADIUS, 0, ls - ATT_WINDOW)
        q = gather(q0_ref, q1_ref, m0 * dil + r, ATT_TQ)
        kw = gather(k0_ref, k1_ref, start * dil + r, ATT_WINDOW)
        vw = gather(v0_ref, v1_ref, start * dil + r, ATT_WINDOW)
        out, lse = _attn_unit(q, kw, vw, rel, m0 - start, head_of_col)
        dst = pl.ds(sb * ATT_TQ * dil + r, ATT_TQ, stride=dil)
        for half in range(2):
            lanes = slice(half * LANES, (half + 1) * LANES)
            o_scr[half, dst, :] = out[:, lanes]
            lse_scr[half, dst, :] = lse[:, lanes]
        return carry
    lax.fori_loop(0, units, unit, 0, unroll=2)
    for half in range(2):
        lanes = slice(half * LANES, (half + 1) * LANES)
        o_ref[0, :, lanes] = o_scr[half]
        lse_ref[0, :, lanes] = lse_scr[half]


def _attention_strided(qkv_g, dil, batch, seq_len):
    view = qkv_g.reshape(batch, seq_len, 3 * ATT_OUT)
    halves = [pl.BlockSpec((1, seq_len, LANES), lambda b, i, c=c: (b, 0, c), pipeline_mode=pl.Buffered(1))
              for c in range(3 * ATT_OUT // LANES)]
    o_spec = pl.BlockSpec((1, ATT_CHUNK, ATT_OUT), lambda b, i: (b, i, 0))
    kern = lambda *refs: _attn_strided_kernel(*refs, dil=dil)
    o, lse = pl.pallas_call(
        kern,
        out_shape=(jax.ShapeDtypeStruct((batch, seq_len, ATT_OUT), F32),
                   jax.ShapeDtypeStruct((batch, seq_len, ATT_OUT), F32)),
        grid=(batch, seq_len // ATT_CHUNK),
        in_specs=halves,
        out_specs=(o_spec, o_spec),
        scratch_shapes=[pltpu.VMEM((2, ATT_CHUNK, LANES), F32), pltpu.VMEM((2, ATT_CHUNK, LANES), F32)],
        compiler_params=_params("arbitrary", "arbitrary"),
        name=f"dilated_attn_d{dil}",
    )(*([view] * len(halves)))
    T = batch * seq_len
    return o.reshape(T, ATT_OUT), lse.reshape(T, ATT_OUT)


def _attention_group(qkv_g, dil, batch, seq_len):
    ls = seq_len // dil
    tq = ATT_SUB * ATT_TQ
    view = qkv_g.reshape(batch, ls, dil * 3 * ATT_OUT)

    def col(which):
        return lambda b, r, i: (b, 0, r * 3 + which)

    q_map = lambda b, r, i: (b, i, r * 3)
    o_map = lambda b, r, i: (b, i, r)
    o, lse = pl.pallas_call(
        _attn_kernel,
        out_shape=(jax.ShapeDtypeStruct((batch, ls, dil * ATT_OUT), BF16),
                   jax.ShapeDtypeStruct((batch, ls, dil * ATT_OUT), F32)),
        grid=(batch, dil, ls // tq),
        in_specs=[pl.BlockSpec((1, tq, ATT_OUT), q_map),
                  pl.BlockSpec((1, ls, ATT_OUT), col(1)),
                  pl.BlockSpec((1, ls, ATT_OUT), col(2))],
        out_specs=(pl.BlockSpec((1, tq, ATT_OUT), o_map), pl.BlockSpec((1, tq, ATT_OUT), o_map)),
        compiler_params=_params("arbitrary", "arbitrary", "arbitrary"),
        name=f"dilated_attn_d{dil}",
    )(view, view, view)
    T = batch * seq_len
    return o.reshape(T, ATT_OUT), lse.reshape(T, ATT_OUT)


def _merge_kernel(x_ref, yhy_ref, o1_ref, o2_ref, o3_ref, l1_ref, l2_ref, l3_ref, h_ref, wg_ref,
                  wh_ref, wa_ref, wo_ref, g1_ref, out_ref):
    l1, l2, l3 = l1_ref[...], l2_ref[...], l3_ref[...]
    m = jnp.maximum(jnp.maximum(l1, l2), l3)
    e1, e2, e3 = jnp.exp(l1 - m), jnp.exp(l2 - m), jnp.exp(l3 - m)
    tot = e1 + e2 + e3
    y_at = (e1 * o1_ref[...].astype(F32) + e2 * o2_ref[...].astype(F32) + e3 * o3_ref[...].astype(F32)) / tot
    a = jnp.dot(yhy_ref[...], wh_ref[0], preferred_element_type=F32)
    b = jnp.dot(y_at.astype(BF16), wa_ref[0], preferred_element_type=F32)
    d_model = x_ref.shape[1]
    gate = jnp.dot(h_ref[...], wg_ref[0], preferred_element_type=F32)
    merged = _sigmoid(gate[:, :d_model]) * a + _sigmoid(gate[:, d_model:]) * b
    upd = jnp.dot(merged.astype(BF16), wo_ref[0], preferred_element_type=F32)
    out_ref[...] = x_ref[...] + g1_ref[0] * upd


def _merge(x, y_hy, attn, h, w_gate, w_br_h, w_br_a, w_out, layer, g1, seq_len):
    T, D = x.shape
    tm = 512
    per_batch = seq_len // tm
    (o1, l1), (o2, l2), (o3, l3) = attn
    row = lambda w: pl.BlockSpec((tm, w), lambda i: (i, 0))
    full = lambda a: pl.BlockSpec((1,) + a.shape[1:], lambda i: (layer, 0, 0))
    return pl.pallas_call(
        _merge_kernel,
        out_shape=jax.ShapeDtypeStruct((T, D), F32),
        grid=(T // tm,),
        in_specs=[row(D), row(HY_WIDTH), row(ATT_OUT), row(ATT_OUT), row(ATT_OUT),
                  row(ATT_OUT), row(ATT_OUT), row(ATT_OUT),
                  row(D), full(w_gate), full(w_br_h), full(w_br_a), full(w_out),
                  pl.BlockSpec((1, 1, D), lambda i: (i // per_batch, 0, 0))],
        out_specs=row(D),
        compiler_params=_params("arbitrary"),
        name="mixer_merge",
    )(x, y_hy, o1, o2, o3, l1, l2, l3, h, w_gate, w_br_h, w_br_a, w_out, g1)


def _router_kernel(x_ref, g_ref, sc_ref, sh_ref, rwt_ref, rb_ref, hg_ref, stats_ref, cnt_ref, base_ref):
    i = pl.program_id(0)
    tm = x_ref.shape[0]

    @pl.when(i == 0)
    def _():
        base_ref[...] = jnp.zeros_like(base_ref)

    x = x_ref[...]
    r = lax.rsqrt(jnp.mean(x * x, axis=-1, keepdims=True) + NORM_EPS)
    h = (x * r) * g_ref[...]
    h = h * (1.0 + sc_ref[0]) + sh_ref[0]

    logits = lax.dot_general(rwt_ref[...], h, (((1,), (1,)), ((), ())),
                             precision=HIGHEST, preferred_element_type=F32)
    scores = _sigmoid(logits)
    biased = scores + rb_ref[...]

    def row(a, k):
        return a[k:k + 1, :]

    sel = jnp.zeros((1, tm), jnp.int32)
    best = None
    for g in range(N_GROUPS):
        a, b, c, d = (row(biased, 4 * g + k) for k in range(4))
        m_ab, n_ab = jnp.maximum(a, b), jnp.minimum(a, b)
        m_cd, n_cd = jnp.maximum(c, d), jnp.minimum(c, d)
        gs = jnp.maximum(m_ab, m_cd) + jnp.maximum(jnp.minimum(m_ab, m_cd), jnp.maximum(n_ab, n_cd))
        if g == 0:
            best = gs
        else:
            better = gs > best
            sel = jnp.where(better, g, sel)
            best = jnp.where(better, gs, best)

    v, u = [], []
    for k in range(EXPERTS_PER_GROUP):
        vk = jnp.zeros((1, tm), F32)
        uk = jnp.zeros((1, tm), F32)
        for g in range(N_GROUPS):
            vk = jnp.where(sel == g, row(biased, 4 * g + k), vk)
            uk = jnp.where(sel == g, row(scores, 4 * g + k), uk)
        v.append(vk)
        u.append(uk)

    i1 = jnp.zeros((1, tm), jnp.int32)
    b1 = v[0]
    for k in range(1, EXPERTS_PER_GROUP):
        gt = v[k] > b1
        i1 = jnp.where(gt, k, i1)
        b1 = jnp.where(gt, v[k], b1)
    i2 = jnp.zeros((1, tm), jnp.int32)
    b2 = jnp.full((1, tm), -jnp.inf, F32)
    for k in range(EXPERTS_PER_GROUP):
        cand = (i1 != k) & (v[k] > b2)
        i2 = jnp.where(cand, k, i2)
        b2 = jnp.where(cand, v[k], b2)

    lo = jnp.minimum(i1, i2)
    hi = jnp.maximum(i1, i2)
    pair = jnp.where(lo == 0, hi - 1, jnp.where(lo == 1, hi + 1, 5))
    bucket = sel * PAIRS_PER_GROUP + pair

    u_lo = jnp.zeros((1, tm), F32)
    u_hi = jnp.zeros((1, tm), F32)
    for k in range(EXPERTS_PER_GROUP):
        u_lo = jnp.where(lo == k, u[k], u_lo)
        u_hi = jnp.where(hi == k, u[k], u_hi)
    tot = u_lo + u_hi
    w_lo = u_lo / tot
    w_hi = u_hi / tot

    rows = lax.broadcasted_iota(jnp.int32, (BUCKET_ROWS, tm), 0)
    onehot = (rows == bucket).astype(F32)
    t_src = lax.broadcasted_iota(jnp.int32, (tm, tm), 0)
    t_dst = lax.broadcasted_iota(jnp.int32, (tm, tm), 1)
    before = (t_src < t_dst).astype(BF16)
    cum = jnp.dot(onehot.astype(BF16), before, preferred_element_type=F32)
    base = base_ref[...]
    rank = jnp.sum(onehot * (cum + base), axis=0, keepdims=True)
    base = base + jnp.sum(onehot, axis=1, keepdims=True)
    base_ref[...] = base
    cnt_ref[...] = jnp.broadcast_to(base, cnt_ref.shape)

    srow = lax.broadcasted_iota(jnp.int32, (8, tm), 0)
    stats_ref[...] = jnp.where(srow == 0, bucket.astype(F32), jnp.where(srow == 1, rank, 0.0))

    grow = lax.broadcasted_iota(jnp.int32, (GATE_COLS, tm), 0)
    gates_t = jnp.where(grow == 0, w_lo, jnp.where(grow == 1, w_hi, 0.0))
    hg_ref[:, :D_MODEL] = h
    hg_ref[:, D_MODEL:] = gates_t.T


def _router(x, norm_g, sc, sh, router_wt, router_b, seq_len):
    T, D = x.shape
    tm = ROUTER_TM
    per_batch = seq_len // tm
    return pl.pallas_call(
        _router_kernel,
        out_shape=(jax.ShapeDtypeStruct((T, D + GATE_COLS), F32),
                   jax.ShapeDtypeStruct((8, T), F32),
                   jax.ShapeDtypeStruct((BUCKET_ROWS, LANES), F32)),
        grid=(T // tm,),
        in_specs=[pl.BlockSpec((tm, D), lambda i: (i, 0)),
                  pl.BlockSpec((1, D), lambda i: (0, 0)),
                  pl.BlockSpec((1, 1, D), lambda i: (i // per_batch, 0, 0)),
                  pl.BlockSpec((1, 1, D), lambda i: (i // per_batch, 0, 0)),
                  pl.BlockSpec((N_EXPERTS, D), lambda i: (0, 0)),
                  pl.BlockSpec((N_EXPERTS, 1), lambda i: (0, 0))],
        out_specs=(pl.BlockSpec((tm, D + GATE_COLS), lambda i: (i, 0)),
                   pl.BlockSpec((8, tm), lambda i: (0, i)),
                   pl.BlockSpec((BUCKET_ROWS, LANES), lambda i: (0, 0))),
        scratch_shapes=[pltpu.VMEM((BUCKET_ROWS, 1), F32)],
        compiler_params=_params("arbitrary"),
        name="moe_router",
    )(x, norm_g, sc, sh, router_wt, router_b)


def _start_row_copies(n_rows, make_copy):
    group = 8

    def body(g, carry):
        base = pl.multiple_of(g * group, group)
        for k in range(group):
            make_copy(base + k).start(priority=k % 2)
        return carry

    lax.fori_loop(0, n_rows // group, body, 0)


def _dispatch_kernel(pos_ref, hg_ref, xs_init_hbm, xs_hbm, sem):
    del xs_init_hbm
    _start_row_copies(ROW_BLOCK, lambda r: pltpu.make_async_copy(
        hg_ref.at[pl.ds(r, 1), :], xs_hbm.at[pl.ds(pos_ref[r], 1), :], sem))
    pltpu.make_async_copy(hg_ref, xs_hbm.at[pl.ds(0, ROW_BLOCK), :], sem).wait()


def _dispatch(pos, hg, n_rows):
    T, W = hg.shape
    zeros = jnp.zeros((n_rows, W), F32)
    return pl.pallas_call(
        _dispatch_kernel,
        out_shape=jax.ShapeDtypeStruct((n_rows, W), F32),
        grid=(T // ROW_BLOCK,),
        in_specs=[pl.BlockSpec((ROW_BLOCK,), lambda i: (i,), memory_space=pltpu.SMEM),
                  pl.BlockSpec((ROW_BLOCK, W), lambda i: (i, 0)),
                  pl.BlockSpec(memory_space=pl.ANY)],
        out_specs=pl.BlockSpec(memory_space=pl.ANY),
        scratch_shapes=[pltpu.SemaphoreType.DMA],
        input_output_aliases={2: 0},
        compiler_params=_params("arbitrary"),
        name="moe_dispatch",
    )(pos, hg, zeros)


def _expert_kernel(e_lo_ref, e_hi_ref, n_used_ref, xs_ref, w1a, w3a, w2a, w1b, w3b, w2b, y_ref):
    del e_lo_ref, e_hi_ref
    used = pl.program_id(0) < n_used_ref[0]

    @pl.when(jnp.logical_not(used))
    def _():
        y_ref[...] = jnp.zeros_like(y_ref)

    @pl.when(used)
    def _():
        xb = xs_ref[:, :D_MODEL].astype(BF16)
        g_lo = xs_ref[:, D_MODEL:D_MODEL + 1]
        g_hi = xs_ref[:, D_MODEL + 1:D_MODEL + 2]

        def ffn(w1, w3, w2):
            a = jnp.dot(xb, w1[0, 0], preferred_element_type=F32)
            b = jnp.dot(xb, w3[0, 0], preferred_element_type=F32)
            act = (a * _sigmoid(a)) * b
            return jnp.dot(act.astype(BF16), w2[0, 0], preferred_element_type=F32)

        y_ref[...] = g_lo * ffn(w1a, w3a, w2a) + g_hi * ffn(w1b, w3b, w2b)


def _experts(tile_lo, tile_hi, n_used, xs, w1, w3, w2, layer):
    n_rows, W = xs.shape
    D, F = w1.shape[2], w1.shape[3]
    n_tiles = n_rows // EXPERT_TM

    def x_map(j, lo, hi, nu):
        return (jnp.minimum(j, nu[0] - 1), 0)

    def w_lo_map(j, lo, hi, nu):
        return (layer, lo[j], 0, 0)

    def w_hi_map(j, lo, hi, nu):
        return (layer, hi[j], 0, 0)

    grid_spec = pltpu.PrefetchScalarGridSpec(
        num_scalar_prefetch=3,
        grid=(n_tiles,),
        in_specs=[pl.BlockSpec((EXPERT_TM, W), x_map),
                  pl.BlockSpec((1, 1, D, F), w_lo_map), pl.BlockSpec((1, 1, D, F), w_lo_map),
                  pl.BlockSpec((1, 1, F, D), w_lo_map),
                  pl.BlockSpec((1, 1, D, F), w_hi_map), pl.BlockSpec((1, 1, D, F), w_hi_map),
                  pl.BlockSpec((1, 1, F, D), w_hi_map)],
        out_specs=pl.BlockSpec((EXPERT_TM, D), lambda j, lo, hi, nu: (j, 0)),
    )
    return pl.pallas_call(
        _expert_kernel,
        out_shape=jax.ShapeDtypeStruct((n_rows, D), F32),
        grid_spec=grid_spec,
        compiler_params=_params("arbitrary"),
        name="moe_experts",
    )(tile_lo, tile_hi, n_used, xs, w1, w3, w2, w1, w3, w2)


def _combine_kernel(pos_ref, x_ref, g2_ref, ng_ref, sc_ref, sh_ref, ys_hbm, *refs):
    *out_refs, buf, sem = refs
    _start_row_copies(ROW_BLOCK, lambda r: pltpu.make_async_copy(
        ys_hbm.at[pl.ds(pos_ref[r], 1), :], buf.at[pl.ds(r, 1), :], sem))
    pltpu.make_async_copy(ys_hbm.at[pl.ds(0, ROW_BLOCK), :], buf, sem).wait()
    xn = x_ref[...] + g2_ref[0] * buf[...]
    r = lax.rsqrt(jnp.mean(xn * xn, axis=-1, keepdims=True) + NORM_EPS)
    h = (xn * r) * ng_ref[...]
    if len(out_refs) == 2:
        out_refs[0][...] = xn
        h = h * (1.0 + sc_ref[0]) + sh_ref[0]
    out_refs[-1][...] = h.astype(out_refs[-1].dtype)


def _combine(pos, x, g2, ys, norm_g, sc, sh, seq_len, last):
    T, D = x.shape
    per_batch = seq_len // ROW_BLOCK
    row = pl.BlockSpec((ROW_BLOCK, D), lambda i: (i, 0))
    per_b = pl.BlockSpec((1, 1, D), lambda i: (i // per_batch, 0, 0))
    if last:
        out_shape, out_specs = jax.ShapeDtypeStruct((T, D), F32), row
    else:
        out_shape = (jax.ShapeDtypeStruct((T, D), F32), jax.ShapeDtypeStruct((T, D), BF16))
        out_specs = (row, row)
    return pl.pallas_call(
        _combine_kernel,
        out_shape=out_shape,
        grid=(T // ROW_BLOCK,),
        in_specs=[pl.BlockSpec((ROW_BLOCK,), lambda i: (i,), memory_space=pltpu.SMEM),
                  row, per_b, pl.BlockSpec((1, D), lambda i: (0, 0)), per_b, per_b,
                  pl.BlockSpec(memory_space=pl.ANY)],
        out_specs=out_specs,
        scratch_shapes=[pltpu.VMEM((ROW_BLOCK, D), F32), pltpu.SemaphoreType.DMA],
        compiler_params=_params("arbitrary"),
        name="moe_combine",
    )(pos, x, g2, norm_g, sc, sh, ys)


def _moe_layer(x, norm_g, sc2, sh2, g2, router_wt, router_b, w1, w3, w2, layer, seq_len, next_norm, last):
    T, D = x.shape
    hg, stats, counts = _router(x, norm_g, sc2, sh2, router_wt, router_b, seq_len)

    cnt = counts[:N_BUCKETS, 0].astype(jnp.int32)
    padded = ((cnt + EXPERT_TM - 1) // EXPERT_TM) * EXPERT_TM
    ends = jnp.cumsum(padded)
    starts = ends - padded
    bucket = stats[0].astype(jnp.int32)
    pos = starts[bucket] + stats[1].astype(jnp.int32)
    n_tiles = T // EXPERT_TM + N_BUCKETS
    tile_row0 = jnp.arange(n_tiles, dtype=jnp.int32) * EXPERT_TM
    tile_bucket = jnp.sum((ends[None, :] <= tile_row0[:, None]).astype(jnp.int32), axis=1)
    tile_bucket = jnp.minimum(tile_bucket, N_BUCKETS - 1)
    grp, pair = tile_bucket // PAIRS_PER_GROUP, tile_bucket % PAIRS_PER_GROUP
    pair_lo = jnp.array([0, 0, 0, 1, 1, 2], jnp.int32)[pair]
    pair_hi = jnp.array([1, 2, 3, 2, 3, 3], jnp.int32)[pair]
    tile_lo = grp * EXPERTS_PER_GROUP + pair_lo
    tile_hi = grp * EXPERTS_PER_GROUP + pair_hi
    n_used = (ends[-1] // EXPERT_TM).astype(jnp.int32).reshape(1)

    xs = _dispatch(pos, hg, n_tiles * EXPERT_TM)
    ys = _experts(tile_lo, tile_hi, n_used, xs, w1, w3, w2, layer)
    return _combine(pos, x, g2, ys, *next_norm, seq_len, last)


def kernel(x, c, norm1_g, norm2_g, w_ada, b_ada, w_in, w_sc, b_sc, hf_w1, hf_b1, hf_w2, hf_b2, hf_w3, hf_b3,
           hf_w4, hf_freq, hy_bias, w_br_h, w_br_a, w_out, router_w, router_bias, moe_w1, moe_w3, moe_w2, final_g):
    B, L, D = x.shape
    T = B * L
    C = HY_WIDTH
    xt = x.reshape(T, D)

    rope = _rope_tables(L)
    feat, deltas = _hyena_features(L)
    mats = _dft_matrices()

    c_pad = jnp.pad(c, ((0, 8 - B), (0, 0)))
    mod = _ada(c_pad, w_ada, b_ada)[:, :B]
    router_wt = router_w.T
    router_b = router_bias.reshape(N_EXPERTS, 1)
    w_in_b, w_br_h_b, w_br_a_b, w_out_b = (w.astype(BF16) for w in (w_in, w_br_h, w_br_a, w_out))
    moe_w1_b, moe_w3_b, moe_w2_b = (w.astype(BF16) for w in (moe_w1, moe_w3, moe_w2))
    w_gate_b = w_in_b[:, :, HY_COLS + QKV_COLS:]

    half = LANES // 2

    def diag2(w):
        wp = jnp.pad(w, ((0, 0), (0, half - w.shape[1]), (0, half - w.shape[2])))
        zero = jnp.zeros_like(wp)
        return jnp.concatenate([jnp.concatenate([wp, zero], axis=2), jnp.concatenate([zero, wp], axis=2)], axis=1)

    twice = lambda v: jnp.concatenate([v, v], axis=1)[:, None]
    w4_zero = jnp.zeros_like(hf_w4)
    w4_halves = jnp.stack([jnp.concatenate([hf_w4, w4_zero], axis=1),
                           jnp.concatenate([w4_zero, hf_w4], axis=1)], axis=1)
    k2 = _hyena_filter(feat, diag2(hf_w1), twice(hf_b1), diag2(hf_w2), twice(hf_b2), diag2(hf_w3), twice(hf_b3),
                       w4_halves, twice(hf_freq), deltas)
    kf = _filter_fft(k2.reshape(DEPTH, -1, 2 * C), mats)

    mods = [[mod[i, :, k * D:(k + 1) * D].reshape(B, 1, D) for k in range(6)] for i in range(DEPTH)]
    h = _norm_mod(xt, norm1_g[0].reshape(1, D), mods[0][1], mods[0][0], L)
    for i in range(DEPTH):
        sh1, sc1, g1, sh2, sc2, g2 = mods[i]

        u = _proj(h, w_in_b, i, 0, HY_COLS, tn=HY_WIDTH)
        qkv = [_proj_qkv(h, w_in_b, i, g, rope, L, BF16 if dil == 1 else F32)
               for g, (_, dil) in enumerate(ATT_GROUPS)]

        y_hy = _long_conv(u.reshape(B, L, 3 * C), w_sc[i], b_sc[i], kf, i, hy_bias[i], mats).reshape(T, C)

        attn = [(_attention_group if dil == 1 else _attention_strided)(qkv[g], dil, B, L)
                for g, (_, dil) in enumerate(ATT_GROUPS)]
        xt = _merge(xt, y_hy, attn, h, w_gate_b, w_br_h_b, w_br_a_b, w_out_b, i, g1, L)

        last = i == DEPTH - 1
        if last:
            next_norm = (final_g.reshape(1, D), sc2, sh2)
        else:
            next_norm = (norm1_g[i + 1].reshape(1, D), mods[i + 1][1], mods[i + 1][0])
        res = _moe_layer(xt, norm2_g[i].reshape(1, D), sc2, sh2, g2, router_wt, router_b,
                         moe_w1_b, moe_w3_b, moe_w2_b, i, L, next_norm, last)
        if last:
            return res.reshape(B, L, D)
        xt, h = res
```

```python
import math

import jax
import jax.numpy as jnp
import numpy as np
from jax import lax
from jax.experimental import pallas as pl
from jax.experimental.pallas import tpu as pltpu

D_MODEL = 1024
DEPTH = 2
HY_WIDTH = 768
HY_EMB = 33
HY_FAST_DECAY_PCT = 0.3
HY_SLOW_DECAY_PCT = 1.5
HY_TARGET = 1e-2
HEAD_DIM = 64
ATT_GROUPS = ((128, 1), (512, 4), (2048, 16))
HEADS_PER_GROUP = 4
N_HEADS = HEADS_PER_GROUP * len(ATT_GROUPS)
ATT_WIDTH = N_HEADS * HEAD_DIM
ATT_OUT = HEADS_PER_GROUP * HEAD_DIM
ROPE_THETA = 10000.0
N_EXPERTS = 16
N_GROUPS = 4
EXPERTS_PER_GROUP = N_EXPERTS // N_GROUPS
NORM_EPS = 1e-6
MASK_VALUE = -1e30

LANES = 128
MXU_DIM = 256
VMEM_LIMIT_BYTES = 56 * 1024 * 1024

F32 = jnp.float32
BF16 = jnp.bfloat16
HIGHEST = lax.Precision.HIGHEST

PROJ_TM = 2048
PROJ_TN = MXU_DIM
QKV_TM = 1024
PROJ_CHUNK = 512
HY_COLS = 3 * HY_WIDTH
QKV_COLS = 3 * ATT_WIDTH

FFT_R = 128
FFT_KH = FFT_R // 2 + 1
FFT_KP = 72
FFT_SLAB = 2 * FFT_KP
FFT_G_PITCH = FFT_SLAB + 8
FFT_T_PITCH = FFT_R + 8
CONV_CB = 128
FILTER_TM = 1024
FFT_UNROLL_TIME = 32
FFT_UNROLL_FREQ = 65

PAIRS_PER_GROUP = 6
N_BUCKETS = N_GROUPS * PAIRS_PER_GROUP
BUCKET_ROWS = 32
ROUTER_TM = 512
EXPERT_TM = 256
ROW_BLOCK = 1024
GATE_COLS = LANES


def _params(*sem):
    return pltpu.CompilerParams(dimension_semantics=sem, vmem_limit_bytes=VMEM_LIMIT_BYTES)


def _sigmoid(x):
    return 1.0 / (1.0 + jnp.exp(-x))


def _store_time_padded(o_ref, val):
    groups = val.shape[0] // FFT_R
    o_ref[:, :FFT_R, :] = val.reshape(groups, FFT_R, val.shape[1])
    o_ref[:, FFT_R:, :] = jnp.zeros((groups, FFT_T_PITCH - FFT_R, val.shape[1]), o_ref.dtype)


def _time_padded_shape(rows, cols):
    return jax.ShapeDtypeStruct((rows // FFT_R, FFT_T_PITCH, cols), F32)


def _ada_kernel(c_ref, w_ref, b_ref, o_ref):
    c = c_ref[...]
    c_act = c * _sigmoid(c)
    o_ref[0] = jnp.dot(c_act, w_ref[0], precision=HIGHEST, preferred_element_type=F32) + b_ref[0]


def _ada(c_pad, w_ada, b_ada):
    depth, D, N = w_ada.shape
    rows = c_pad.shape[0]
    tn = N // 4
    return pl.pallas_call(
        _ada_kernel,
        out_shape=jax.ShapeDtypeStruct((depth, rows, N), F32),
        grid=(depth, N // tn),
        in_specs=[pl.BlockSpec((rows, D), lambda l, j: (0, 0)),
                  pl.BlockSpec((1, D, tn), lambda l, j: (l, 0, j)),
                  pl.BlockSpec((1, 1, tn), lambda l, j: (l, 0, j))],
        out_specs=pl.BlockSpec((1, rows, tn), lambda l, j: (l, 0, j)),
        compiler_params=_params("arbitrary", "arbitrary"),
        name="ada_mod",
    )(c_pad, w_ada, b_ada.reshape(depth, 1, N))


def _norm_mod_kernel(x_ref, g_ref, sc_ref, sh_ref, o_ref):
    x = x_ref[...]
    r = lax.rsqrt(jnp.mean(x * x, axis=-1, keepdims=True) + NORM_EPS)
    h = (x * r) * g_ref[...]
    o_ref[...] = (h * (1.0 + sc_ref[0]) + sh_ref[0]).astype(o_ref.dtype)


def _norm_mod(x, g, sc, sh, seq_len):
    T, D = x.shape
    tm = 1024
    per_batch = seq_len // tm
    return pl.pallas_call(
        _norm_mod_kernel,
        out_shape=jax.ShapeDtypeStruct((T, D), BF16),
        grid=(T // tm,),
        in_specs=[pl.BlockSpec((tm, D), lambda i: (i, 0)),
                  pl.BlockSpec((1, D), lambda i: (0, 0)),
                  pl.BlockSpec((1, 1, D), lambda i: (i // per_batch, 0, 0)),
                  pl.BlockSpec((1, 1, D), lambda i: (i // per_batch, 0, 0))],
        out_specs=pl.BlockSpec((tm, D), lambda i: (i, 0)),
        compiler_params=_params("arbitrary"),
        name="norm_mod",
    )(x, g, sc, sh)


def _proj_kernel(h_ref, w_ref, o_ref):
    o_ref[...] = jnp.dot(h_ref[...], w_ref[0], preferred_element_type=F32).astype(o_ref.dtype)


def _proj(h, w, layer, col0, n_cols, tn):
    T, D = h.shape
    tm = PROJ_TM
    assert col0 % tn == 0 and n_cols % tn == 0
    off = col0 // tn
    return pl.pallas_call(
        _proj_kernel,
        out_shape=jax.ShapeDtypeStruct((T, n_cols), BF16),
        grid=(T // tm, n_cols // tn),
        in_specs=[pl.BlockSpec((tm, D), lambda i, j: (i, 0)),
                  pl.BlockSpec((1, D, tn), lambda i, j: (layer, 0, off + j))],
        out_specs=pl.BlockSpec((tm, tn), lambda i, j: (i, j)),
        compiler_params=_params("arbitrary", "arbitrary"),
        name="proj",
    )(h, w)


def _proj_qkv_kernel(h_ref, wq_ref, wk_ref, wv_ref, cos_ref, sin_ref, rot_ref, o_ref):
    tm, tn = h_ref.shape[0], PROJ_TN
    rot = rot_ref[...]
    for c in range(tm // PROJ_CHUNK):
        rows = slice(c * PROJ_CHUNK, (c + 1) * PROJ_CHUNK)
        hc = h_ref[rows, :]
        cos = jnp.concatenate([cos_ref[rows, :], cos_ref[rows, :]], axis=1)
        sin = jnp.concatenate([sin_ref[rows, :], sin_ref[rows, :]], axis=1)

        def roped(w_ref):
            acc = jnp.dot(hc, w_ref[0], preferred_element_type=F32)
            swapped = jnp.dot(acc.astype(BF16), rot, preferred_element_type=F32)
            return acc * cos + swapped * sin

        o_ref[rows, 0:tn] = (roped(wq_ref) * HEAD_DIM ** -0.5).astype(o_ref.dtype)
        o_ref[rows, tn:2 * tn] = roped(wk_ref).astype(o_ref.dtype)
        o_ref[rows, 2 * tn:3 * tn] = jnp.dot(hc, wv_ref[0], preferred_element_type=F32).astype(o_ref.dtype)


def _proj_qkv(h, w, layer, group, rope, seq_len, out_dtype):
    T, D = h.shape
    tm, tn = QKV_TM, PROJ_TN
    n_groups = len(ATT_GROUPS)
    off = HY_COLS // tn + group
    cos_t, sin_t, rot = rope
    per_batch = seq_len // tm
    tab_spec = pl.BlockSpec((tm, LANES), lambda i: (i % per_batch, 0))
    w_spec = lambda kind: pl.BlockSpec((1, D, tn), lambda i: (layer, 0, off + n_groups * kind))
    return pl.pallas_call(
        _proj_qkv_kernel,
        out_shape=jax.ShapeDtypeStruct((T, 3 * tn), out_dtype),
        grid=(T // tm,),
        in_specs=[pl.BlockSpec((tm, D), lambda i: (i, 0)), w_spec(0), w_spec(1), w_spec(2),
                  tab_spec, tab_spec, pl.BlockSpec((tn, tn), lambda i: (0, 0))],
        out_specs=pl.BlockSpec((tm, 3 * tn), lambda i: (i, 0)),
        compiler_params=_params("arbitrary"),
        name="proj_qkv",
    )(h, w, w, w, cos_t, sin_t, rot)


def _rope_tables(seq_len):
    half = HEAD_DIM // 2
    inv = ROPE_THETA ** (-jnp.arange(half, dtype=F32) / half)
    ang = jnp.arange(seq_len, dtype=F32)[:, None] * inv[None, :]
    reps = LANES // half
    cos_t = jnp.tile(jnp.cos(ang), (1, reps))
    sin_t = jnp.tile(jnp.sin(ang), (1, reps))
    rot = np.zeros((PROJ_TN, PROJ_TN), np.float32)
    for j in range(PROJ_TN):
        if j % HEAD_DIM < half:
            rot[j + half, j] = -1.0
        else:
            rot[j - half, j] = 1.0
    return cos_t, sin_t, jnp.asarray(rot, BF16)


def _hyena_filter_kernel(feat_ref, w1_ref, b1_ref, w2_ref, b2_ref, w3_ref, b3_ref, w4_ref, fr_ref,
                         dl_ref, o_ref):
    i = pl.program_id(1)
    half = feat_ref.shape[0]
    z = feat_ref[...]
    fr = fr_ref[0]
    h = jnp.sin(fr * (jnp.dot(z, w1_ref[0], precision=HIGHEST, preferred_element_type=F32) + b1_ref[0]))
    h = jnp.sin(fr * (jnp.dot(h, w2_ref[0], precision=HIGHEST, preferred_element_type=F32) + b2_ref[0]))
    h = jnp.sin(fr * (jnp.dot(h, w3_ref[0], precision=HIGHEST, preferred_element_type=F32) + b3_ref[0]))
    rate = jnp.abs(dl_ref[...])
    parts = []
    for s in range(2):
        k = jnp.dot(h, w4_ref[0, s], precision=HIGHEST, preferred_element_type=F32)
        t = z[:, s * (LANES // 2):s * (LANES // 2) + 1]
        parts.append(k * jnp.exp(-t * rate))
    val = jnp.concatenate(parts, axis=0)
    pos = i * 2 * half + lax.broadcasted_iota(jnp.int32, (2 * half, 1), 0)
    is_bwd = lax.broadcasted_iota(jnp.int32, (1, val.shape[1]), 1) >= HY_WIDTH
    _store_time_padded(o_ref.at[0], jnp.where((pos == 0) & is_bwd, 0.0, val))


def _hyena_filter(feat, w1, b1, w2, b2, w3, b3, w4, freq, deltas2):
    n_rows = 2 * feat.shape[0]
    depth = w1.shape[0]
    C2 = 2 * HY_WIDTH
    tm = FILTER_TM
    per_layer = lambda r, c: pl.BlockSpec((1, r, c), lambda l, i: (l, 0, 0))
    shape = _time_padded_shape(n_rows, C2)
    return pl.pallas_call(
        _hyena_filter_kernel,
        out_shape=jax.ShapeDtypeStruct((depth,) + shape.shape, shape.dtype),
        grid=(depth, n_rows // tm),
        in_specs=[pl.BlockSpec((tm // 2, LANES), lambda l, i: (i, 0)),
                  per_layer(LANES, LANES), per_layer(1, LANES),
                  per_layer(LANES, LANES), per_layer(1, LANES),
                  per_layer(LANES, LANES), per_layer(1, LANES),
                  pl.BlockSpec((1, 2, LANES, C2), lambda l, i: (l, 0, 0, 0)),
                  per_layer(1, LANES),
                  pl.BlockSpec((1, C2), lambda l, i: (0, 0))],
        out_specs=pl.BlockSpec((1, tm // FFT_R, FFT_T_PITCH, C2), lambda l, i: (l, i, 0, 0)),
        compiler_params=_params("arbitrary", "arbitrary"),
        name="hyena_filter",
    )(feat, w1, b1, w2, b2, w3, b3, w4, freq, deltas2)


def _hyena_features(seq_len):
    L = seq_len
    half = LANES // 2
    t = jnp.linspace(0.0, 1.0, L, dtype=F32)[:, None]
    bands = (HY_EMB - 1) // 2
    w = 2.0 * math.pi * jnp.arange(L, dtype=F32)[:, None] / L
    f = jnp.linspace(1e-4, bands - 1, bands, dtype=F32)[None, :]
    z = jnp.concatenate([t, jnp.cos(f * w), -jnp.sin(f * w)], axis=-1)
    z = jnp.pad(z, ((0, 0), (0, half - HY_EMB)))
    z = z.reshape(L // FILTER_TM, 2, FILTER_TM // 2, half).transpose(0, 2, 1, 3).reshape(L // 2, LANES)
    max_decay = math.log(HY_TARGET) / HY_FAST_DECAY_PCT
    min_decay = math.log(HY_TARGET) / HY_SLOW_DECAY_PCT
    deltas = jnp.linspace(min_decay, max_decay, HY_WIDTH, dtype=F32)[None, :]
    return z, jnp.concatenate([deltas, deltas], axis=1)


def _dft_matrices():
    R, KH, KP = FFT_R, FFT_KH, FFT_KP
    N = R * R
    n1 = np.arange(R)[:, None, None]
    k2 = np.arange(KP)[None, :, None]
    n2 = np.arange(R)[None, None, :]
    phase = 2.0 * np.pi * ((n2 * k2 % R) / R + (n1 * k2) / N)
    live = (k2 < KH)
    a1 = np.concatenate([np.cos(phase) * live, -np.sin(phase) * live], axis=1)
    wgt = np.where((k2 == 0) | (k2 == R // 2), 1.0, 2.0) * live / N
    b1 = np.concatenate([np.cos(phase) * wgt, -np.sin(phase) * wgt], axis=1)
    b1 = np.transpose(b1, (0, 2, 1))[:, :R // 2, :]
    th = 2.0 * np.pi * (np.arange(R)[:, None] * np.arange(R)[None, :] % R) / R
    c, s = np.cos(th), np.sin(th)
    w2f = np.block([[c, s], [-s, c]])
    w2i = np.block([[c, -s], [s, c]])
    as_bf = lambda a: jnp.asarray(a.astype(np.float32)).astype(BF16)
    return dict(a1=as_bf(a1[:, :, :R // 2]), b1=as_bf(b1), w2f=as_bf(w2f), w2i=as_bf(w2i))


def _fft_stage1(src_ref, a1_ref, g_ref, n_rows):
    def body(n1, carry):
        xs = src_ref[pl.ds(n1, n_rows, stride=FFT_T_PITCH), :].astype(BF16)
        slab = jnp.dot(a1_ref[n1], xs, preferred_element_type=F32)
        g_ref[pl.ds(pl.multiple_of(n1 * FFT_G_PITCH, 8), FFT_SLAB), :] = slab
        return carry
    lax.fori_loop(0, FFT_R, body, 0, unroll=FFT_UNROLL_TIME)


def _load_freq_rows(g_ref, k2):
    re = g_ref[pl.ds(k2, FFT_R, stride=FFT_G_PITCH), :]
    im = g_ref[pl.ds(FFT_KP + k2, FFT_R, stride=FFT_G_PITCH), :]
    return jnp.concatenate([re, im], axis=0)


def _filter_fft_kernel(fwd_ref, bwd_ref, a1_ref, w2_ref, o_ref, gf_ref, gb_ref):
    n2_rows = fwd_ref.shape[1] // FFT_T_PITCH
    cb = fwd_ref.shape[2]
    fwd, bwd = fwd_ref.at[0], bwd_ref.at[0]

    def stage1(n1, carry):
        rows = pl.ds(n1, n2_rows, stride=FFT_T_PITCH)
        xs = jnp.concatenate([fwd[rows, :], bwd[rows, :]], axis=1).astype(BF16)
        slab = jnp.dot(a1_ref[n1], xs, preferred_element_type=F32)
        dst = pl.ds(pl.multiple_of(n1 * FFT_G_PITCH, 8), FFT_SLAB)
        gf_ref[dst, :] = slab[:, :cb]
        gb_ref[dst, :] = slab[:, cb:]
        return carry
    lax.fori_loop(0, FFT_R, stage1, 0, unroll=FFT_UNROLL_TIME)

    def body(k2, carry):
        gk = jnp.concatenate([_load_freq_rows(gf_ref, k2), _load_freq_rows(gb_ref, k2)], axis=1).astype(BF16)
        x = jnp.dot(w2_ref[...], gk, preferred_element_type=F32)
        f, b = x[:, :cb], x[:, cb:]
        spec = jnp.concatenate([f[:FFT_R] + b[:FFT_R], f[FFT_R:] - b[FFT_R:]], axis=0)
        o_ref[0, k2] = spec.astype(o_ref.dtype)
        return carry
    lax.fori_loop(0, FFT_KH, body, 0, unroll=FFT_UNROLL_FREQ)


def _filter_fft(k2, mats):
    depth, n_rows, _ = k2.shape
    C = HY_WIDTH
    cb = CONV_CB
    ncb = C // cb
    return pl.pallas_call(
        _filter_fft_kernel,
        out_shape=jax.ShapeDtypeStruct((depth, FFT_KH, 2 * FFT_R, C), BF16),
        grid=(depth, ncb),
        in_specs=[pl.BlockSpec((1, n_rows, cb), lambda l, j: (l, 0, j)),
                  pl.BlockSpec((1, n_rows, cb), lambda l, j: (l, 0, ncb + j)),
                  pl.BlockSpec((FFT_R, FFT_SLAB, FFT_R // 2), lambda l, j: (0, 0, 0)),
                  pl.BlockSpec((2 * FFT_R, 2 * FFT_R), lambda l, j: (0, 0))],
        out_specs=pl.BlockSpec((1, FFT_KH, 2 * FFT_R, cb), lambda l, j: (l, 0, 0, j)),
        scratch_shapes=[pltpu.VMEM((FFT_R * FFT_G_PITCH, cb), F32), pltpu.VMEM((FFT_R * FFT_G_PITCH, cb), F32)],
        compiler_params=_params("arbitrary", "arbitrary"),
        name="hyena_filter_fft",
    )(k2, k2, mats["a1"], mats["w2f"])


SC_HALO = 16


def _long_conv_kernel(x0_ref, x1_ref, v_ref, w0_ref, w1_ref, wv_ref, b0_ref, b1s_ref, bv_ref, kf_ref, bias_ref,
                      a1_ref, b1_ref, w2f_ref, w2i_ref, o_ref, g_ref, zv_ref, y_ref, stage_ref):
    seq_len = x0_ref.shape[1]
    n_groups = seq_len // FFT_R

    def short_conv(part, u_ref, w_ref, b_ref, g):
        row0 = pl.multiple_of(g * FFT_R, FFT_R)
        lo = pl.multiple_of(jnp.maximum(row0 - SC_HALO, 0), SC_HALO)
        hi = pl.multiple_of(jnp.minimum(row0 + FFT_R, seq_len - SC_HALO), SC_HALO)
        st = stage_ref.at[part]
        st[0:SC_HALO, :] = jnp.where(g > 0, u_ref[0, pl.ds(lo, SC_HALO), :].astype(F32), 0.0)
        st[SC_HALO:SC_HALO + FFT_R, :] = u_ref[0, pl.ds(row0, FFT_R), :].astype(F32)
        st[SC_HALO + FFT_R:, :] = jnp.where(g < n_groups - 1, u_ref[0, pl.ds(hi, SC_HALO), :].astype(F32), 0.0)
        w = w_ref[...]
        return (st[SC_HALO - 1:SC_HALO - 1 + FFT_R, :] * w[0:1] + st[SC_HALO:SC_HALO + FFT_R, :] * w[1:2]
                + st[SC_HALO + 1:SC_HALO + 1 + FFT_R, :] * w[2:3] + b_ref[...])

    def gate_body(g, carry):
        zv = short_conv(2, v_ref, wv_ref, bv_ref, g) * short_conv(1, x1_ref, w1_ref, b1s_ref, g)
        zv_ref[pl.ds(pl.multiple_of(g * FFT_T_PITCH, 8), FFT_R), :] = zv
        return carry
    lax.fori_loop(0, n_groups, gate_body, 0, unroll=2)

    _fft_stage1(zv_ref, a1_ref, g_ref, n_groups)

    def freq_body(k2, carry):
        gk = _load_freq_rows(g_ref, k2).astype(BF16)
        x = jnp.dot(w2f_ref[...], gk, preferred_element_type=F32)
        kf = kf_ref[0, k2].astype(F32)
        xr, xi = x[:FFT_R], x[FFT_R:]
        kr, ki = kf[:FFT_R], kf[FFT_R:]
        p = jnp.concatenate([xr * kr - xi * ki, xr * ki + xi * kr], axis=0).astype(BF16)
        hk = jnp.dot(w2i_ref[...], p, preferred_element_type=F32)
        g_ref[pl.ds(k2, FFT_R, stride=FFT_G_PITCH), :] = hk[:FFT_R]
        g_ref[pl.ds(FFT_KP + k2, FFT_R, stride=FFT_G_PITCH), :] = hk[FFT_R:]
        return carry
    lax.fori_loop(0, FFT_KH, freq_body, 0, unroll=FFT_UNROLL_FREQ)

    def time_body(n1, carry):
        slab = g_ref[pl.ds(pl.multiple_of(n1 * FFT_G_PITCH, 8), FFT_SLAB), :].astype(BF16)
        y_ref[pl.ds(n1, n_groups, stride=FFT_T_PITCH), :] = jnp.dot(b1_ref[n1], slab, preferred_element_type=F32)
        return carry
    lax.fori_loop(0, FFT_R, time_body, 0, unroll=FFT_UNROLL_TIME)

    bias = bias_ref[...]

    def out_body(g, carry):
        rows = pl.ds(pl.multiple_of(g * FFT_T_PITCH, 8), FFT_R)
        x0 = short_conv(0, x0_ref, w0_ref, b0_ref, g)
        y = (y_ref[rows, :] + zv_ref[rows, :] * bias) * x0
        o_ref[0, pl.ds(pl.multiple_of(g * FFT_R, FFT_R), FFT_R), :] = y.astype(o_ref.dtype)
        return carry
    lax.fori_loop(0, n_groups, out_body, 0, unroll=2)


def _long_conv(u, w_sc, b_sc, kf, layer, hy_bias, mats):
    B, L, _ = u.shape
    C = HY_WIDTH
    cb = CONV_CB
    ncb = C // cb
    lp = L // FFT_R * FFT_T_PITCH
    part = lambda p: pl.BlockSpec((1, L, cb), lambda j, b: (b, 0, p * ncb + j))
    wpart = lambda p: pl.BlockSpec((3, cb), lambda j, b: (0, p * ncb + j))
    bpart = lambda p: pl.BlockSpec((1, cb), lambda j, b: (0, p * ncb + j))
    b2 = b_sc.reshape(1, 3 * C)
    return pl.pallas_call(
        _long_conv_kernel,
        out_shape=jax.ShapeDtypeStruct((B, L, C), BF16),
        grid=(ncb, B),
        in_specs=[part(0), part(1), part(2), wpart(0), wpart(1), wpart(2), bpart(0), bpart(1), bpart(2),
                  pl.BlockSpec((1, FFT_KH, 2 * FFT_R, cb), lambda j, b: (layer, 0, 0, j)),
                  pl.BlockSpec((1, cb), lambda j, b: (0, j)),
                  pl.BlockSpec((FFT_R, FFT_SLAB, FFT_R // 2), lambda j, b: (0, 0, 0)),
                  pl.BlockSpec((FFT_R, FFT_R // 2, FFT_SLAB), lambda j, b: (0, 0, 0)),
                  pl.BlockSpec((2 * FFT_R, 2 * FFT_R), lambda j, b: (0, 0)),
                  pl.BlockSpec((2 * FFT_R, 2 * FFT_R), lambda j, b: (0, 0))],
        out_specs=pl.BlockSpec((1, L, cb), lambda j, b: (b, 0, j)),
        scratch_shapes=[pltpu.VMEM((FFT_R * FFT_G_PITCH, cb), F32), pltpu.VMEM((lp, cb), F32),
                        pltpu.VMEM((lp, cb), F32), pltpu.VMEM((3, FFT_R + 2 * SC_HALO, cb), F32)],
        compiler_params=_params("arbitrary", "arbitrary"),
        name="hyena_long_conv",
    )(u, u, u, w_sc, w_sc, w_sc, b2, b2, b2, kf, hy_bias.reshape(1, C),
      mats["a1"], mats["b1"], mats["w2f"], mats["w2i"])


ATT_TQ = 128
ATT_RADIUS = 64
ATT_WINDOW = ATT_TQ + 2 * ATT_RADIUS


ATT_SUB = 4


def _attn_kernel(q_ref, k_ref, v_ref, o_ref, lse_ref):
    i = pl.program_id(2)
    ls = k_ref.shape[1]
    nh = HEADS_PER_GROUP
    head_of_col = lax.broadcasted_iota(jnp.int32, (1, ATT_OUT), 1) // HEAD_DIM
    row_iota = lax.broadcasted_iota(jnp.int32, (nh * ATT_TQ, ATT_WINDOW), 0) % ATT_TQ
    col_iota = lax.broadcasted_iota(jnp.int32, (nh * ATT_TQ, ATT_WINDOW), 1)
    rel = row_iota - col_iota
    for sub in range(ATT_SUB):
        q0 = (i * ATT_SUB + sub) * ATT_TQ
        rows = slice(sub * ATT_TQ, (sub + 1) * ATT_TQ)
        q = q_ref[0, rows, :]
        start = jnp.clip(q0 - ATT_RADIUS, 0, ls - ATT_WINDOW)
        start = pl.multiple_of(start, ATT_RADIUS)
        kw = k_ref[0, pl.ds(start, ATT_WINDOW), :]
        vw = v_ref[0, pl.ds(start, ATT_WINDOW), :]
        out, lse = _attn_unit(q, kw, vw, rel, q0 - start, head_of_col)
        o_ref[0, rows, :] = out.astype(o_ref.dtype)
        lse_ref[0, rows, :] = lse


def _attn_unit(q, kw, vw, rel, q_minus_start, head_of_col):
    nh = HEADS_PER_GROUP
    band = jnp.abs(q_minus_start + rel) <= ATT_RADIUS
    zero = jnp.zeros_like(q)
    q4 = jnp.concatenate([jnp.where(head_of_col == h, q, zero) for h in range(nh)], axis=0)
    s = lax.dot_general(q4, kw, (((1,), (1,)), ((), ())), preferred_element_type=F32)
    s = jnp.where(band, s, MASK_VALUE)
    m = jnp.max(s, axis=-1, keepdims=True)
    p = jnp.exp(s - m)
    den = jnp.sum(p, axis=-1, keepdims=True)
    pv = jnp.dot(p.astype(BF16), vw, preferred_element_type=F32) / den
    lse4 = m + jnp.log(den)
    out = jnp.zeros((ATT_TQ, ATT_OUT), F32)
    lse = jnp.zeros((ATT_TQ, ATT_OUT), F32)
    for h in range(nh):
        mine = head_of_col == h
        hrows = slice(h * ATT_TQ, (h + 1) * ATT_TQ)
        out = jnp.where(mine, pv[hrows], out)
        lse = jnp.where(mine, lse4[hrows], lse)
    return out, lse


ATT_CHUNK = 2048


def _attn_strided_kernel(q0_ref, q1_ref, k0_ref, k1_ref, v0_ref, v1_ref, o_ref, lse_ref, o_scr, lse_scr, *, dil):
    i = pl.program_id(1)
    ls = q0_ref.shape[1] // dil

    def gather(lo_ref, hi_ref, first, n):
        rows = pl.ds(first, n, stride=dil)
        return jnp.concatenate([lo_ref[0, rows, :], hi_ref[0, rows, :]], axis=1).astype(BF16)

    units = ATT_CHUNK // ATT_TQ
    shift = dil.bit_length() - 1
    head_of_col = lax.broadcasted_iota(jnp.int32, (1, ATT_OUT), 1) // HEAD_DIM
    row_iota = lax.broadcasted_iota(jnp.int32, (HEADS_PER_GROUP * ATT_TQ, ATT_WINDOW), 0) % ATT_TQ
    rel = row_iota - lax.broadcasted_iota(jnp.int32, (HEADS_PER_GROUP * ATT_TQ, ATT_WINDOW), 1)

    def unit(u, carry):
        r = u & (dil - 1)
        sb = u >> shift
        m0 = i * (ATT_CHUNK // dil) + sb * ATT_TQ
        start = jnp.clip(m0 - ATT_RADIUS, 0, ls - ATT_WINDOW)
        q = gather(q0_ref, q1_ref, m0 * dil + r, ATT_TQ)
        kw = gather(k0_ref, k1_ref, start * dil + r, ATT_WINDOW)
        vw = gather(v0_ref, v1_ref, start * dil + r, ATT_WINDOW)
        out, lse = _attn_unit(q, kw, vw, rel, m0 - start, head_of_col)
        dst = pl.ds(sb * ATT_TQ * dil + r, ATT_TQ, stride=dil)
        for half in range(2):
            lanes = slice(half * LANES, (half + 1) * LANES)
            o_scr[half, dst, :] = out[:, lanes]
            lse_scr[half, dst, :] = lse[:, lanes]
        return carry
    lax.fori_loop(0, units, unit, 0, unroll=4 if dil <= 4 else 2)
    for half in range(2):
        lanes = slice(half * LANES, (half + 1) * LANES)
        o_ref[0, :, lanes] = o_scr[half]
        lse_ref[0, :, lanes] = lse_scr[half]


def _attention_strided(qkv_g, dil, batch, seq_len):
    view = qkv_g.reshape(batch, seq_len, 3 * ATT_OUT)
    halves = [pl.BlockSpec((1, seq_len, LANES), lambda b, i, c=c: (b, 0, c), pipeline_mode=pl.Buffered(1))
              for c in range(3 * ATT_OUT // LANES)]
    o_spec = pl.BlockSpec((1, ATT_CHUNK, ATT_OUT), lambda b, i: (b, i, 0))
    kern = lambda *refs: _attn_strided_kernel(*refs, dil=dil)
    o, lse = pl.pallas_call(
        kern,
        out_shape=(jax.ShapeDtypeStruct((batch, seq_len, ATT_OUT), F32),
                   jax.ShapeDtypeStruct((batch, seq_len, ATT_OUT), F32)),
        grid=(batch, seq_len // ATT_CHUNK),
        in_specs=halves,
        out_specs=(o_spec, o_spec),
        scratch_shapes=[pltpu.VMEM((2, ATT_CHUNK, LANES), F32), pltpu.VMEM((2, ATT_CHUNK, LANES), F32)],
        compiler_params=_params("arbitrary", "arbitrary"),
        name=f"dilated_attn_d{dil}",
    )(*([view] * len(halves)))
    T = batch * seq_len
    return o.reshape(T, ATT_OUT), lse.reshape(T, ATT_OUT)


def _attention_group(qkv_g, dil, batch, seq_len):
    ls = seq_len // dil
    tq = ATT_SUB * ATT_TQ
    view = qkv_g.reshape(batch, ls, dil * 3 * ATT_OUT)

    def col(which):
        return lambda b, r, i: (b, 0, r * 3 + which)

    q_map = lambda b, r, i: (b, i, r * 3)
    o_map = lambda b, r, i: (b, i, r)
    o, lse = pl.pallas_call(
        _attn_kernel,
        out_shape=(jax.ShapeDtypeStruct((batch, ls, dil * ATT_OUT), BF16),
                   jax.ShapeDtypeStruct((batch, ls, dil * ATT_OUT), F32)),
        grid=(batch, dil, ls // tq),
        in_specs=[pl.BlockSpec((1, tq, ATT_OUT), q_map),
                  pl.BlockSpec((1, ls, ATT_OUT), col(1)),
                  pl.BlockSpec((1, ls, ATT_OUT), col(2))],
        out_specs=(pl.BlockSpec((1, tq, ATT_OUT), o_map), pl.BlockSpec((1, tq, ATT_OUT), o_map)),
        compiler_params=_params("arbitrary", "arbitrary", "arbitrary"),
        name=f"dilated_attn_d{dil}",
    )(view, view, view)
    T = batch * seq_len
    return o.reshape(T, ATT_OUT), lse.reshape(T, ATT_OUT)


def _merge_kernel(x_ref, yhy_ref, o1_ref, o2_ref, o3_ref, l1_ref, l2_ref, l3_ref, h_ref, wg_ref,
                  wh_ref, wa_ref, wo_ref, g1_ref, out_ref):
    l1, l2, l3 = l1_ref[...], l2_ref[...], l3_ref[...]
    m = jnp.maximum(jnp.maximum(l1, l2), l3)
    e1, e2, e3 = jnp.exp(l1 - m), jnp.exp(l2 - m), jnp.exp(l3 - m)
    tot = e1 + e2 + e3
    y_at = (e1 * o1_ref[...].astype(F32) + e2 * o2_ref[...].astype(F32) + e3 * o3_ref[...].astype(F32)) / tot
    a = jnp.dot(yhy_ref[...], wh_ref[0], preferred_element_type=F32)
    b = jnp.dot(y_at.astype(BF16), wa_ref[0], preferred_element_type=F32)
    d_model = x_ref.shape[1]
    gate = jnp.dot(h_ref[...], wg_ref[0], preferred_element_type=F32)
    merged = _sigmoid(gate[:, :d_model]) * a + _sigmoid(gate[:, d_model:]) * b
    upd = jnp.dot(merged.astype(BF16), wo_ref[0], preferred_element_type=F32)
    out_ref[...] = x_ref[...] + g1_ref[0] * upd


def _merge(x, y_hy, attn, h, w_gate, w_br_h, w_br_a, w_out, layer, g1, seq_len):
    T, D = x.shape
    tm = 512
    per_batch = seq_len // tm
    (o1, l1), (o2, l2), (o3, l3) = attn
    row = lambda w: pl.BlockSpec((tm, w), lambda i: (i, 0))
    full = lambda a: pl.BlockSpec((1,) + a.shape[1:], lambda i: (layer, 0, 0))
    return pl.pallas_call(
        _merge_kernel,
        out_shape=jax.ShapeDtypeStruct((T, D), F32),
        grid=(T // tm,),
        in_specs=[row(D), row(HY_WIDTH), row(ATT_OUT), row(ATT_OUT), row(ATT_OUT),
                  row(ATT_OUT), row(ATT_OUT), row(ATT_OUT),
                  row(D), full(w_gate), full(w_br_h), full(w_br_a), full(w_out),
                  pl.BlockSpec((1, 1, D), lambda i: (i // per_batch, 0, 0))],
        out_specs=row(D),
        compiler_params=_params("arbitrary"),
        name="mixer_merge",
    )(x, y_hy, o1, o2, o3, l1, l2, l3, h, w_gate, w_br_h, w_br_a, w_out, g1)


def _router_kernel(x_ref, g_ref, sc_ref, sh_ref, rwt_ref, rb_ref, hg_ref, stats_ref, cnt_ref, base_ref):
    i = pl.program_id(0)
    tm = x_ref.shape[0]

    @pl.when(i == 0)
    def _():
        base_ref[...] = jnp.zeros_like(base_ref)

    x = x_ref[...]
    r = lax.rsqrt(jnp.mean(x * x, axis=-1, keepdims=True) + NORM_EPS)
    h = (x * r) * g_ref[...]
    h = h * (1.0 + sc_ref[0]) + sh_ref[0]

    logits = lax.dot_general(rwt_ref[...], h, (((1,), (1,)), ((), ())),
                             precision=HIGHEST, preferred_element_type=F32)
    scores = _sigmoid(logits)
    biased = scores + rb_ref[...]

    def row(a, k):
        return a[k:k + 1, :]

    sel = jnp.zeros((1, tm), jnp.int32)
    best = None
    for g in range(N_GROUPS):
        a, b, c, d = (row(biased, 4 * g + k) for k in range(4))
        m_ab, n_ab = jnp.maximum(a, b), jnp.minimum(a, b)
        m_cd, n_cd = jnp.maximum(c, d), jnp.minimum(c, d)
        gs = jnp.maximum(m_ab, m_cd) + jnp.maximum(jnp.minimum(m_ab, m_cd), jnp.maximum(n_ab, n_cd))
        if g == 0:
            best = gs
        else:
            better = gs > best
            sel = jnp.where(better, g, sel)
            best = jnp.where(better, gs, best)

    v, u = [], []
    for k in range(EXPERTS_PER_GROUP):
        vk = jnp.zeros((1, tm), F32)
        uk = jnp.zeros((1, tm), F32)
        for g in range(N_GROUPS):
            vk = jnp.where(sel == g, row(biased, 4 * g + k), vk)
            uk = jnp.where(sel == g, row(scores, 4 * g + k), uk)
        v.append(vk)
        u.append(uk)

    i1 = jnp.zeros((1, tm), jnp.int32)
    b1 = v[0]
    for k in range(1, EXPERTS_PER_GROUP):
        gt = v[k] > b1
        i1 = jnp.where(gt, k, i1)
        b1 = jnp.where(gt, v[k], b1)
    i2 = jnp.zeros((1, tm), jnp.int32)
    b2 = jnp.full((1, tm), -jnp.inf, F32)
    for k in range(EXPERTS_PER_GROUP):
        cand = (i1 != k) & (v[k] > b2)
        i2 = jnp.where(cand, k, i2)
        b2 = jnp.where(cand, v[k], b2)

    lo = jnp.minimum(i1, i2)
    hi = jnp.maximum(i1, i2)
    pair = jnp.where(lo == 0, hi - 1, jnp.where(lo == 1, hi + 1, 5))
    bucket = sel * PAIRS_PER_GROUP + pair

    u_lo = jnp.zeros((1, tm), F32)
    u_hi = jnp.zeros((1, tm), F32)
    for k in range(EXPERTS_PER_GROUP):
        u_lo = jnp.where(lo == k, u[k], u_lo)
        u_hi = jnp.where(hi == k, u[k], u_hi)
    tot = u_lo + u_hi
    w_lo = u_lo / tot
    w_hi = u_hi / tot

    rows = lax.broadcasted_iota(jnp.int32, (BUCKET_ROWS, tm), 0)
    onehot = (rows == bucket).astype(F32)
    t_src = lax.broadcasted_iota(jnp.int32, (tm, tm), 0)
    t_dst = lax.broadcasted_iota(jnp.int32, (tm, tm), 1)
    before = (t_src < t_dst).astype(BF16)
    cum = jnp.dot(onehot.astype(BF16), before, preferred_element_type=F32)
    base = base_ref[...]
    rank = jnp.sum(onehot * (cum + base), axis=0, keepdims=True)
    base = base + jnp.sum(onehot, axis=1, keepdims=True)
    base_ref[...] = base
    cnt_ref[...] = jnp.broadcast_to(base, cnt_ref.shape)

    srow = lax.broadcasted_iota(jnp.int32, (8, tm), 0)
    stats_ref[...] = jnp.where(srow == 0, bucket.astype(F32), jnp.where(srow == 1, rank, 0.0))

    grow = lax.broadcasted_iota(jnp.int32, (GATE_COLS, tm), 0)
    gates_t = jnp.where(grow == 0, w_lo, jnp.where(grow == 1, w_hi, 0.0))
    hg_ref[:, :D_MODEL] = h
    hg_ref[:, D_MODEL:] = gates_t.T


def _router(x, norm_g, sc, sh, router_wt, router_b, seq_len):
    T, D = x.shape
    tm = ROUTER_TM
    per_batch = seq_len // tm
    return pl.pallas_call(
        _router_kernel,
        out_shape=(jax.ShapeDtypeStruct((T, D + GATE_COLS), F32),
                   jax.ShapeDtypeStruct((8, T), F32),
                   jax.ShapeDtypeStruct((BUCKET_ROWS, LANES), F32)),
        grid=(T // tm,),
        in_specs=[pl.BlockSpec((tm, D), lambda i: (i, 0)),
                  pl.BlockSpec((1, D), lambda i: (0, 0)),
                  pl.BlockSpec((1, 1, D), lambda i: (i // per_batch, 0, 0)),
                  pl.BlockSpec((1, 1, D), lambda i: (i // per_batch, 0, 0)),
                  pl.BlockSpec((N_EXPERTS, D), lambda i: (0, 0)),
                  pl.BlockSpec((N_EXPERTS, 1), lambda i: (0, 0))],
        out_specs=(pl.BlockSpec((tm, D + GATE_COLS), lambda i: (i, 0)),
                   pl.BlockSpec((8, tm), lambda i: (0, i)),
                   pl.BlockSpec((BUCKET_ROWS, LANES), lambda i: (0, 0))),
        scratch_shapes=[pltpu.VMEM((BUCKET_ROWS, 1), F32)],
        compiler_params=_params("arbitrary"),
        name="moe_router",
    )(x, norm_g, sc, sh, router_wt, router_b)


def _start_row_copies(n_rows, make_copy):
    group = 8

    def body(g, carry):
        base = pl.multiple_of(g * group, group)
        for k in range(group):
            make_copy(base + k).start(priority=k % 2)
        return carry

    lax.fori_loop(0, n_rows // group, body, 0)


def _dispatch_kernel(pos_ref, hg_ref, xs_init_hbm, xs_hbm, sem):
    del xs_init_hbm
    _start_row_copies(ROW_BLOCK, lambda r: pltpu.make_async_copy(
        hg_ref.at[pl.ds(r, 1), :], xs_hbm.at[pl.ds(pos_ref[r], 1), :], sem))
    pltpu.make_async_copy(hg_ref, xs_hbm.at[pl.ds(0, ROW_BLOCK), :], sem).wait()


def _dispatch(pos, hg, n_rows):
    T, W = hg.shape
    zeros = jnp.zeros((n_rows, W), F32)
    return pl.pallas_call(
        _dispatch_kernel,
        out_shape=jax.ShapeDtypeStruct((n_rows, W), F32),
        grid=(T // ROW_BLOCK,),
        in_specs=[pl.BlockSpec((ROW_BLOCK,), lambda i: (i,), memory_space=pltpu.SMEM),
                  pl.BlockSpec((ROW_BLOCK, W), lambda i: (i, 0)),
                  pl.BlockSpec(memory_space=pl.ANY)],
        out_specs=pl.BlockSpec(memory_space=pl.ANY),
        scratch_shapes=[pltpu.SemaphoreType.DMA],
        input_output_aliases={2: 0},
        compiler_params=_params("arbitrary"),
        name="moe_dispatch",
    )(pos, hg, zeros)


def _expert_kernel(e_lo_ref, e_hi_ref, n_used_ref, xs_ref, w1a, w3a, w2a, w1b, w3b, w2b, y_ref):
    del e_lo_ref, e_hi_ref
    used = pl.program_id(0) < n_used_ref[0]

    @pl.when(jnp.logical_not(used))
    def _():
        y_ref[...] = jnp.zeros_like(y_ref)

    @pl.when(used)
    def _():
        xb = xs_ref[:, :D_MODEL].astype(BF16)
        g_lo = xs_ref[:, D_MODEL:D_MODEL + 1]
        g_hi = xs_ref[:, D_MODEL + 1:D_MODEL + 2]

        def ffn(w1, w3, w2):
            a = jnp.dot(xb, w1[0, 0], preferred_element_type=F32)
            b = jnp.dot(xb, w3[0, 0], preferred_element_type=F32)
            act = (a * _sigmoid(a)) * b
            return jnp.dot(act.astype(BF16), w2[0, 0], preferred_element_type=F32)

        y_ref[...] = g_lo * ffn(w1a, w3a, w2a) + g_hi * ffn(w1b, w3b, w2b)


def _experts(tile_lo, tile_hi, n_used, xs, w1, w3, w2, layer):
    n_rows, W = xs.shape
    D, F = w1.shape[2], w1.shape[3]
    n_tiles = n_rows // EXPERT_TM

    def x_map(j, lo, hi, nu):
        return (jnp.minimum(j, nu[0] - 1), 0)

    def w_lo_map(j, lo, hi, nu):
        return (layer, lo[j], 0, 0)

    def w_hi_map(j, lo, hi, nu):
        return (layer, hi[j], 0, 0)

    grid_spec = pltpu.PrefetchScalarGridSpec(
        num_scalar_prefetch=3,
        grid=(n_tiles,),
        in_specs=[pl.BlockSpec((EXPERT_TM, W), x_map),
                  pl.BlockSpec((1, 1, D, F), w_lo_map), pl.BlockSpec((1, 1, D, F), w_lo_map),
                  pl.BlockSpec((1, 1, F, D), w_lo_map),
                  pl.BlockSpec((1, 1, D, F), w_hi_map), pl.BlockSpec((1, 1, D, F), w_hi_map),
                  pl.BlockSpec((1, 1, F, D), w_hi_map)],
        out_specs=pl.BlockSpec((EXPERT_TM, D), lambda j, lo, hi, nu: (j, 0)),
    )
    return pl.pallas_call(
        _expert_kernel,
        out_shape=jax.ShapeDtypeStruct((n_rows, D), F32),
        grid_spec=grid_spec,
        compiler_params=_params("arbitrary"),
        name="moe_experts",
    )(tile_lo, tile_hi, n_used, xs, w1, w3, w2, w1, w3, w2)


def _combine_kernel(pos_ref, x_ref, g2_ref, ng_ref, sc_ref, sh_ref, ys_hbm, *refs):
    *out_refs, buf, sem = refs
    _start_row_copies(ROW_BLOCK, lambda r: pltpu.make_async_copy(
        ys_hbm.at[pl.ds(pos_ref[r], 1), :], buf.at[pl.ds(r, 1), :], sem))
    pltpu.make_async_copy(ys_hbm.at[pl.ds(0, ROW_BLOCK), :], buf, sem).wait()
    xn = x_ref[...] + g2_ref[0] * buf[...]
    r = lax.rsqrt(jnp.mean(xn * xn, axis=-1, keepdims=True) + NORM_EPS)
    h = (xn * r) * ng_ref[...]
    if len(out_refs) == 2:
        out_refs[0][...] = xn
        h = h * (1.0 + sc_ref[0]) + sh_ref[0]
    out_refs[-1][...] = h.astype(out_refs[-1].dtype)


def _combine(pos, x, g2, ys, norm_g, sc, sh, seq_len, last):
    T, D = x.shape
    per_batch = seq_len // ROW_BLOCK
    row = pl.BlockSpec((ROW_BLOCK, D), lambda i: (i, 0))
    per_b = pl.BlockSpec((1, 1, D), lambda i: (i // per_batch, 0, 0))
    if last:
        out_shape, out_specs = jax.ShapeDtypeStruct((T, D), F32), row
    else:
        out_shape = (jax.ShapeDtypeStruct((T, D), F32), jax.ShapeDtypeStruct((T, D), BF16))
        out_specs = (row, row)
    return pl.pallas_call(
        _combine_kernel,
        out_shape=out_shape,
        grid=(T // ROW_BLOCK,),
        in_specs=[pl.BlockSpec((ROW_BLOCK,), lambda i: (i,), memory_space=pltpu.SMEM),
                  row, per_b, pl.BlockSpec((1, D), lambda i: (0, 0)), per_b, per_b,
                  pl.BlockSpec(memory_space=pl.ANY)],
        out_specs=out_specs,
        scratch_shapes=[pltpu.VMEM((ROW_BLOCK, D), F32), pltpu.SemaphoreType.DMA],
        compiler_params=_params("arbitrary"),
        name="moe_combine",
    )(pos, x, g2, norm_g, sc, sh, ys)


def _moe_layer(x, norm_g, sc2, sh2, g2, router_wt, router_b, w1, w3, w2, layer, seq_len, next_norm, last):
    T, D = x.shape
    hg, stats, counts = _router(x, norm_g, sc2, sh2, router_wt, router_b, seq_len)

    cnt = counts[:N_BUCKETS, 0].astype(jnp.int32)
    padded = ((cnt + EXPERT_TM - 1) // EXPERT_TM) * EXPERT_TM
    ends = jnp.cumsum(padded)
    starts = ends - padded
    bucket = stats[0].astype(jnp.int32)
    pos = starts[bucket] + stats[1].astype(jnp.int32)
    n_tiles = T // EXPERT_TM + N_BUCKETS
    tile_row0 = jnp.arange(n_tiles, dtype=jnp.int32) * EXPERT_TM
    tile_bucket = jnp.sum((ends[None, :] <= tile_row0[:, None]).astype(jnp.int32), axis=1)
    tile_bucket = jnp.minimum(tile_bucket, N_BUCKETS - 1)
    grp, pair = tile_bucket // PAIRS_PER_GROUP, tile_bucket % PAIRS_PER_GROUP
    pair_lo = jnp.array([0, 0, 0, 1, 1, 2], jnp.int32)[pair]
    pair_hi = jnp.array([1, 2, 3, 2, 3, 3], jnp.int32)[pair]
    tile_lo = grp * EXPERTS_PER_GROUP + pair_lo
    tile_hi = grp * EXPERTS_PER_GROUP + pair_hi
    n_used = (ends[-1] // EXPERT_TM).astype(jnp.int32).reshape(1)

    xs = _dispatch(pos, hg, n_tiles * EXPERT_TM)
    ys = _experts(tile_lo, tile_hi, n_used, xs, w1, w3, w2, layer)
    return _combine(pos, x, g2, ys, *next_norm, seq_len, last)


def kernel(x, c, norm1_g, norm2_g, w_ada, b_ada, w_in, w_sc, b_sc, hf_w1, hf_b1, hf_w2, hf_b2, hf_w3, hf_b3,
           hf_w4, hf_freq, hy_bias, w_br_h, w_br_a, w_out, router_w, router_bias, moe_w1, moe_w3, moe_w2, final_g):
    B, L, D = x.shape
    T = B * L
    C = HY_WIDTH
    xt = x.reshape(T, D)

    rope = _rope_tables(L)
    feat, deltas = _hyena_features(L)
    mats = _dft_matrices()

    c_pad = jnp.pad(c, ((0, 8 - B), (0, 0)))
    mod = _ada(c_pad, w_ada, b_ada)[:, :B]
    router_wt = router_w.T
    router_b = router_bias.reshape(N_EXPERTS, 1)
    w_in_b, w_br_h_b, w_br_a_b, w_out_b = (w.astype(BF16) for w in (w_in, w_br_h, w_br_a, w_out))
    moe_w1_b, moe_w3_b, moe_w2_b = (w.astype(BF16) for w in (moe_w1, moe_w3, moe_w2))
    w_gate_b = w_in_b[:, :, HY_COLS + QKV_COLS:]

    half = LANES // 2

    def diag2(w):
        wp = jnp.pad(w, ((0, 0), (0, half - w.shape[1]), (0, half - w.shape[2])))
        zero = jnp.zeros_like(wp)
        return jnp.concatenate([jnp.concatenate([wp, zero], axis=2), jnp.concatenate([zero, wp], axis=2)], axis=1)

    twice = lambda v: jnp.concatenate([v, v], axis=1)[:, None]
    w4_zero = jnp.zeros_like(hf_w4)
    w4_halves = jnp.stack([jnp.concatenate([hf_w4, w4_zero], axis=1),
                           jnp.concatenate([w4_zero, hf_w4], axis=1)], axis=1)
    k2 = _hyena_filter(feat, diag2(hf_w1), twice(hf_b1), diag2(hf_w2), twice(hf_b2), diag2(hf_w3), twice(hf_b3),
                       w4_halves, twice(hf_freq), deltas)
    kf = _filter_fft(k2.reshape(DEPTH, -1, 2 * C), mats)

    mods = [[mod[i, :, k * D:(k + 1) * D].reshape(B, 1, D) for k in range(6)] for i in range(DEPTH)]
    h = _norm_mod(xt, norm1_g[0].reshape(1, D), mods[0][1], mods[0][0], L)
    for i in range(DEPTH):
        sh1, sc1, g1, sh2, sc2, g2 = mods[i]

        u = _proj(h, w_in_b, i, 0, HY_COLS, tn=HY_WIDTH)
        qkv = [_proj_qkv(h, w_in_b, i, g, rope, L, BF16 if dil == 1 else F32)
               for g, (_, dil) in enumerate(ATT_GROUPS)]

        y_hy = _long_conv(u.reshape(B, L, 3 * C), w_sc[i], b_sc[i], kf, i, hy_bias[i], mats).reshape(T, C)

        attn = [(_attention_group if dil == 1 else _attention_strided)(qkv[g], dil, B, L)
                for g, (_, dil) in enumerate(ATT_GROUPS)]
        xt = _merge(xt, y_hy, attn, h, w_gate_b, w_br_h_b, w_br_a_b, w_out_b, i, g1, L)

        last = i == DEPTH - 1
        if last:
            next_norm = (final_g.reshape(1, D), sc2, sh2)
        else:
            next_norm = (norm1_g[i + 1].reshape(1, D), mods[i + 1][1], mods[i + 1][0])
        res = _moe_layer(xt, norm2_g[i].reshape(1, D), sc2, sh2, g2, router_wt, router_b,
                         moe_w1_b, moe_w3_b, moe_w2_b, i, L, next_norm, last)
        if last:
            return res.reshape(B, L, D)
        xt, h = res
```

```python
import math

import jax
import jax.numpy as jnp
import numpy as np
from jax import lax
from jax.experimental import pallas as pl
from jax.experimental.pallas import tpu as pltpu

D_MODEL = 1024
DEPTH = 2
HY_WIDTH = 768
HY_EMB = 33
HY_FAST_DECAY_PCT = 0.3
HY_SLOW_DECAY_PCT = 1.5
HY_TARGET = 1e-2
HEAD_DIM = 64
ATT_GROUPS = ((128, 1), (512, 4), (2048, 16))
HEADS_PER_GROUP = 4
N_HEADS = HEADS_PER_GROUP * len(ATT_GROUPS)
ATT_WIDTH = N_HEADS * HEAD_DIM
ATT_OUT = HEADS_PER_GROUP * HEAD_DIM
ROPE_THETA = 10000.0
N_EXPERTS = 16
N_GROUPS = 4
EXPERTS_PER_GROUP = N_EXPERTS // N_GROUPS
NORM_EPS = 1e-6
MASK_VALUE = -1e30

LANES = 128
MXU_DIM = 256
VMEM_LIMIT_BYTES = 56 * 1024 * 1024

F32 = jnp.float32
BF16 = jnp.bfloat16
HIGHEST = lax.Precision.HIGHEST

PROJ_TM = 2048
PROJ_TN = MXU_DIM
QKV_TM = 1024
PROJ_CHUNK = 512
HY_COLS = 3 * HY_WIDTH
QKV_COLS = 3 * ATT_WIDTH

FFT_R = 128
FFT_KH = FFT_R // 2 + 1
FFT_KP = 72
FFT_SLAB = 2 * FFT_KP
FFT_G_PITCH = FFT_SLAB + 8
FFT_T_PITCH = FFT_R + 8
CONV_CB = 128
FILTER_TM = 1024
FFT_UNROLL_TIME = 64
FFT_UNROLL_FREQ = 65

PAIRS_PER_GROUP = 6
N_BUCKETS = N_GROUPS * PAIRS_PER_GROUP
BUCKET_ROWS = 32
ROUTER_TM = 1024
EXPERT_TM = 256
ROW_BLOCK = 1024
GATE_COLS = LANES


def _params(*sem):
    return pltpu.CompilerParams(dimension_semantics=sem, vmem_limit_bytes=VMEM_LIMIT_BYTES)


def _sigmoid(x):
    return 1.0 / (1.0 + jnp.exp(-x))


def _store_time_padded(o_ref, val):
    groups = val.shape[0] // FFT_R
    o_ref[:, :FFT_R, :] = val.reshape(groups, FFT_R, val.shape[1])
    o_ref[:, FFT_R:, :] = jnp.zeros((groups, FFT_T_PITCH - FFT_R, val.shape[1]), o_ref.dtype)


def _time_padded_shape(rows, cols):
    return jax.ShapeDtypeStruct((rows // FFT_R, FFT_T_PITCH, cols), F32)


def _ada_kernel(c_ref, w_ref, b_ref, o_ref):
    c = c_ref[...]
    c_act = c * _sigmoid(c)
    o_ref[0] = jnp.dot(c_act, w_ref[0], precision=HIGHEST, preferred_element_type=F32) + b_ref[0]


def _ada(c_pad, w_ada, b_ada):
    depth, D, N = w_ada.shape
    rows = c_pad.shape[0]
    tn = N // 4
    return pl.pallas_call(
        _ada_kernel,
        out_shape=jax.ShapeDtypeStruct((depth, rows, N), F32),
        grid=(depth, N // tn),
        in_specs=[pl.BlockSpec((rows, D), lambda l, j: (0, 0)),
                  pl.BlockSpec((1, D, tn), lambda l, j: (l, 0, j)),
                  pl.BlockSpec((1, 1, tn), lambda l, j: (l, 0, j))],
        out_specs=pl.BlockSpec((1, rows, tn), lambda l, j: (l, 0, j)),
        compiler_params=_params("arbitrary", "arbitrary"),
        name="ada_mod",
    )(c_pad, w_ada, b_ada.reshape(depth, 1, N))


def _norm_mod_kernel(x_ref, g_ref, sc_ref, sh_ref, o_ref):
    x = x_ref[...]
    r = lax.rsqrt(jnp.mean(x * x, axis=-1, keepdims=True) + NORM_EPS)
    h = (x * r) * g_ref[...]
    o_ref[...] = (h * (1.0 + sc_ref[0]) + sh_ref[0]).astype(o_ref.dtype)


def _norm_mod(x, g, sc, sh, seq_len):
    T, D = x.shape
    tm = 1024
    per_batch = seq_len // tm
    return pl.pallas_call(
        _norm_mod_kernel,
        out_shape=jax.ShapeDtypeStruct((T, D), BF16),
        grid=(T // tm,),
        in_specs=[pl.BlockSpec((tm, D), lambda i: (i, 0)),
                  pl.BlockSpec((1, D), lambda i: (0, 0)),
                  pl.BlockSpec((1, 1, D), lambda i: (i // per_batch, 0, 0)),
                  pl.BlockSpec((1, 1, D), lambda i: (i // per_batch, 0, 0))],
        out_specs=pl.BlockSpec((tm, D), lambda i: (i, 0)),
        compiler_params=_params("arbitrary"),
        name="norm_mod",
    )(x, g, sc, sh)


def _proj_kernel(h_ref, w_ref, o_ref):
    o_ref[...] = jnp.dot(h_ref[...], w_ref[0], preferred_element_type=F32).astype(o_ref.dtype)


def _proj(h, w, layer, col0, n_cols, tn):
    T, D = h.shape
    tm = PROJ_TM
    assert col0 % tn == 0 and n_cols % tn == 0
    off = col0 // tn
    return pl.pallas_call(
        _proj_kernel,
        out_shape=jax.ShapeDtypeStruct((T, n_cols), BF16),
        grid=(T // tm, n_cols // tn),
        in_specs=[pl.BlockSpec((tm, D), lambda i, j: (i, 0)),
                  pl.BlockSpec((1, D, tn), lambda i, j: (layer, 0, off + j))],
        out_specs=pl.BlockSpec((tm, tn), lambda i, j: (i, j)),
        compiler_params=_params("arbitrary", "arbitrary"),
        name="proj",
    )(h, w)


def _proj_qkv_kernel(h_ref, wq_ref, wk_ref, wv_ref, cos_ref, sin_ref, rot_ref, o_ref):
    tm, tn = h_ref.shape[0], PROJ_TN
    rot = rot_ref[...]
    for c in range(tm // PROJ_CHUNK):
        rows = slice(c * PROJ_CHUNK, (c + 1) * PROJ_CHUNK)
        hc = h_ref[rows, :]
        cos = jnp.concatenate([cos_ref[rows, :], cos_ref[rows, :]], axis=1)
        sin = jnp.concatenate([sin_ref[rows, :], sin_ref[rows, :]], axis=1)

        def roped(w_ref):
            acc = jnp.dot(hc, w_ref[0], preferred_element_type=F32)
            swapped = jnp.dot(acc.astype(BF16), rot, preferred_element_type=F32)
            return acc * cos + swapped * sin

        o_ref[rows, 0:tn] = (roped(wq_ref) * HEAD_DIM ** -0.5).astype(o_ref.dtype)
        o_ref[rows, tn:2 * tn] = roped(wk_ref).astype(o_ref.dtype)
        o_ref[rows, 2 * tn:3 * tn] = jnp.dot(hc, wv_ref[0], preferred_element_type=F32).astype(o_ref.dtype)


def _proj_qkv(h, w, layer, group, rope, seq_len, out_dtype):
    T, D = h.shape
    tm, tn = QKV_TM, PROJ_TN
    n_groups = len(ATT_GROUPS)
    off = HY_COLS // tn + group
    cos_t, sin_t, rot = rope
    per_batch = seq_len // tm
    tab_spec = pl.BlockSpec((tm, LANES), lambda i: (i % per_batch, 0))
    w_spec = lambda kind: pl.BlockSpec((1, D, tn), lambda i: (layer, 0, off + n_groups * kind))
    return pl.pallas_call(
        _proj_qkv_kernel,
        out_shape=jax.ShapeDtypeStruct((T, 3 * tn), out_dtype),
        grid=(T // tm,),
        in_specs=[pl.BlockSpec((tm, D), lambda i: (i, 0)), w_spec(0), w_spec(1), w_spec(2),
                  tab_spec, tab_spec, pl.BlockSpec((tn, tn), lambda i: (0, 0))],
        out_specs=pl.BlockSpec((tm, 3 * tn), lambda i: (i, 0)),
        compiler_params=_params("arbitrary"),
        name="proj_qkv",
    )(h, w, w, w, cos_t, sin_t, rot)


def _rope_tables(seq_len):
    half = HEAD_DIM // 2
    inv = ROPE_THETA ** (-jnp.arange(half, dtype=F32) / half)
    ang = jnp.arange(seq_len, dtype=F32)[:, None] * inv[None, :]
    reps = LANES // half
    cos_t = jnp.tile(jnp.cos(ang), (1, reps))
    sin_t = jnp.tile(jnp.sin(ang), (1, reps))
    rot = np.zeros((PROJ_TN, PROJ_TN), np.float32)
    for j in range(PROJ_TN):
        if j % HEAD_DIM < half:
            rot[j + half, j] = -1.0
        else:
            rot[j - half, j] = 1.0
    return cos_t, sin_t, jnp.asarray(rot, BF16)


def _hyena_filter_kernel(feat_ref, w1_ref, b1_ref, w2_ref, b2_ref, w3_ref, b3_ref, w4_ref, fr_ref,
                         dl_ref, o_ref):
    i = pl.program_id(1)
    half = feat_ref.shape[0]
    z = feat_ref[...]
    fr = fr_ref[0]
    h = jnp.sin(fr * (jnp.dot(z, w1_ref[0], precision=HIGHEST, preferred_element_type=F32) + b1_ref[0]))
    h = jnp.sin(fr * (jnp.dot(h, w2_ref[0], precision=HIGHEST, preferred_element_type=F32) + b2_ref[0]))
    h = jnp.sin(fr * (jnp.dot(h, w3_ref[0], precision=HIGHEST, preferred_element_type=F32) + b3_ref[0]))
    rate = jnp.abs(dl_ref[...])
    parts = []
    for s in range(2):
        k = jnp.dot(h, w4_ref[0, s], precision=HIGHEST, preferred_element_type=F32)
        t = z[:, s * (LANES // 2):s * (LANES // 2) + 1]
        parts.append(k * jnp.exp(-t * rate))
    val = jnp.concatenate(parts, axis=0)
    pos = i * 2 * half + lax.broadcasted_iota(jnp.int32, (2 * half, 1), 0)
    is_bwd = lax.broadcasted_iota(jnp.int32, (1, val.shape[1]), 1) >= HY_WIDTH
    _store_time_padded(o_ref.at[0], jnp.where((pos == 0) & is_bwd, 0.0, val))


def _hyena_filter(feat, w1, b1, w2, b2, w3, b3, w4, freq, deltas2):
    n_rows = 2 * feat.shape[0]
    depth = w1.shape[0]
    C2 = 2 * HY_WIDTH
    tm = FILTER_TM
    per_layer = lambda r, c: pl.BlockSpec((1, r, c), lambda l, i: (l, 0, 0))
    shape = _time_padded_shape(n_rows, C2)
    return pl.pallas_call(
        _hyena_filter_kernel,
        out_shape=jax.ShapeDtypeStruct((depth,) + shape.shape, shape.dtype),
        grid=(depth, n_rows // tm),
        in_specs=[pl.BlockSpec((tm // 2, LANES), lambda l, i: (i, 0)),
                  per_layer(LANES, LANES), per_layer(1, LANES),
                  per_layer(LANES, LANES), per_layer(1, LANES),
                  per_layer(LANES, LANES), per_layer(1, LANES),
                  pl.BlockSpec((1, 2, LANES, C2), lambda l, i: (l, 0, 0, 0)),
                  per_layer(1, LANES),
                  pl.BlockSpec((1, C2), lambda l, i: (0, 0))],
        out_specs=pl.BlockSpec((1, tm // FFT_R, FFT_T_PITCH, C2), lambda l, i: (l, i, 0, 0)),
        compiler_params=_params("arbitrary", "arbitrary"),
        name="hyena_filter",
    )(feat, w1, b1, w2, b2, w3, b3, w4, freq, deltas2)


def _hyena_features(seq_len):
    L = seq_len
    half = LANES // 2
    t = jnp.linspace(0.0, 1.0, L, dtype=F32)[:, None]
    bands = (HY_EMB - 1) // 2
    w = 2.0 * math.pi * jnp.arange(L, dtype=F32)[:, None] / L
    f = jnp.linspace(1e-4, bands - 1, bands, dtype=F32)[None, :]
    z = jnp.concatenate([t, jnp.cos(f * w), -jnp.sin(f * w)], axis=-1)
    z = jnp.pad(z, ((0, 0), (0, half - HY_EMB)))
    z = z.reshape(L // FILTER_TM, 2, FILTER_TM // 2, half).transpose(0, 2, 1, 3).reshape(L // 2, LANES)
    max_decay = math.log(HY_TARGET) / HY_FAST_DECAY_PCT
    min_decay = math.log(HY_TARGET) / HY_SLOW_DECAY_PCT
    deltas = jnp.linspace(min_decay, max_decay, HY_WIDTH, dtype=F32)[None, :]
    return z, jnp.concatenate([deltas, deltas], axis=1)


def _dft_matrices():
    R, KH, KP = FFT_R, FFT_KH, FFT_KP
    N = R * R
    n1 = np.arange(R)[:, None, None]
    k2 = np.arange(KP)[None, :, None]
    n2 = np.arange(R)[None, None, :]
    phase = 2.0 * np.pi * ((n2 * k2 % R) / R + (n1 * k2) / N)
    live = (k2 < KH)
    a1 = np.concatenate([np.cos(phase) * live, -np.sin(phase) * live], axis=1)
    wgt = np.where((k2 == 0) | (k2 == R // 2), 1.0, 2.0) * live / N
    b1 = np.concatenate([np.cos(phase) * wgt, -np.sin(phase) * wgt], axis=1)
    b1 = np.transpose(b1, (0, 2, 1))[:, :R // 2, :]
    th = 2.0 * np.pi * (np.arange(R)[:, None] * np.arange(R)[None, :] % R) / R
    c, s = np.cos(th), np.sin(th)
    w2f = np.block([[c, s], [-s, c]])
    w2i = np.block([[c, -s], [s, c]])
    as_bf = lambda a: jnp.asarray(a.astype(np.float32)).astype(BF16)
    return dict(a1=as_bf(a1[:, :, :R // 2]), b1=as_bf(b1), w2f=as_bf(w2f), w2i=as_bf(w2i))


def _fft_stage1(src_ref, a1_ref, g_ref, n_rows):
    def body(n1, carry):
        xs = src_ref[pl.ds(n1, n_rows, stride=FFT_T_PITCH), :].astype(BF16)
        slab = jnp.dot(a1_ref[n1], xs, preferred_element_type=F32)
        g_ref[pl.ds(pl.multiple_of(n1 * FFT_G_PITCH, 8), FFT_SLAB), :] = slab
        return carry
    lax.fori_loop(0, FFT_R, body, 0, unroll=FFT_UNROLL_TIME)


def _load_freq_rows(g_ref, k2):
    re = g_ref[pl.ds(k2, FFT_R, stride=FFT_G_PITCH), :]
    im = g_ref[pl.ds(FFT_KP + k2, FFT_R, stride=FFT_G_PITCH), :]
    return jnp.concatenate([re, im], axis=0)


def _filter_fft_kernel(fwd_ref, bwd_ref, a1_ref, w2_ref, o_ref, gf_ref, gb_ref):
    n2_rows = fwd_ref.shape[1] // FFT_T_PITCH
    cb = fwd_ref.shape[2]
    fwd, bwd = fwd_ref.at[0], bwd_ref.at[0]

    def stage1(n1, carry):
        rows = pl.ds(n1, n2_rows, stride=FFT_T_PITCH)
        xs = jnp.concatenate([fwd[rows, :], bwd[rows, :]], axis=1).astype(BF16)
        slab = jnp.dot(a1_ref[n1], xs, preferred_element_type=F32)
        dst = pl.ds(pl.multiple_of(n1 * FFT_G_PITCH, 8), FFT_SLAB)
        gf_ref[dst, :] = slab[:, :cb]
        gb_ref[dst, :] = slab[:, cb:]
        return carry
    lax.fori_loop(0, FFT_R, stage1, 0, unroll=FFT_UNROLL_TIME)

    def body(k2, carry):
        gk = jnp.concatenate([_load_freq_rows(gf_ref, k2), _load_freq_rows(gb_ref, k2)], axis=1).astype(BF16)
        x = jnp.dot(w2_ref[...], gk, preferred_element_type=F32)
        f, b = x[:, :cb], x[:, cb:]
        spec = jnp.concatenate([f[:FFT_R] + b[:FFT_R], f[FFT_R:] - b[FFT_R:]], axis=0)
        o_ref[0, k2] = spec.astype(o_ref.dtype)
        return carry
    lax.fori_loop(0, FFT_KH, body, 0, unroll=FFT_UNROLL_FREQ)


def _filter_fft(k2, mats):
    depth, n_rows, _ = k2.shape
    C = HY_WIDTH
    cb = CONV_CB
    ncb = C // cb
    return pl.pallas_call(
        _filter_fft_kernel,
        out_shape=jax.ShapeDtypeStruct((depth, FFT_KH, 2 * FFT_R, C), BF16),
        grid=(depth, ncb),
        in_specs=[pl.BlockSpec((1, n_rows, cb), lambda l, j: (l, 0, j)),
                  pl.BlockSpec((1, n_rows, cb), lambda l, j: (l, 0, ncb + j)),
                  pl.BlockSpec((FFT_R, FFT_SLAB, FFT_R // 2), lambda l, j: (0, 0, 0)),
                  pl.BlockSpec((2 * FFT_R, 2 * FFT_R), lambda l, j: (0, 0))],
        out_specs=pl.BlockSpec((1, FFT_KH, 2 * FFT_R, cb), lambda l, j: (l, 0, 0, j)),
        scratch_shapes=[pltpu.VMEM((FFT_R * FFT_G_PITCH, cb), F32), pltpu.VMEM((FFT_R * FFT_G_PITCH, cb), F32)],
        compiler_params=_params("arbitrary", "arbitrary"),
        name="hyena_filter_fft",
    )(k2, k2, mats["a1"], mats["w2f"])


SC_HALO = 16


def _long_conv_kernel(x0_ref, x1_ref, v_ref, w0_ref, w1_ref, wv_ref, b0_ref, b1s_ref, bv_ref, kf_ref, bias_ref,
                      a1_ref, b1_ref, w2f_ref, w2i_ref, o_ref, g_ref, zv_ref, y_ref, stage_ref):
    seq_len = x0_ref.shape[1]
    n_groups = seq_len // FFT_R

    def short_conv(part, u_ref, w_ref, b_ref, g):
        row0 = pl.multiple_of(g * FFT_R, FFT_R)
        lo = pl.multiple_of(jnp.maximum(row0 - SC_HALO, 0), SC_HALO)
        hi = pl.multiple_of(jnp.minimum(row0 + FFT_R, seq_len - SC_HALO), SC_HALO)
        st = stage_ref.at[part]
        st[0:SC_HALO, :] = jnp.where(g > 0, u_ref[0, pl.ds(lo, SC_HALO), :].astype(F32), 0.0)
        st[SC_HALO:SC_HALO + FFT_R, :] = u_ref[0, pl.ds(row0, FFT_R), :].astype(F32)
        st[SC_HALO + FFT_R:, :] = jnp.where(g < n_groups - 1, u_ref[0, pl.ds(hi, SC_HALO), :].astype(F32), 0.0)
        w = w_ref[...]
        return (st[SC_HALO - 1:SC_HALO - 1 + FFT_R, :] * w[0:1] + st[SC_HALO:SC_HALO + FFT_R, :] * w[1:2]
                + st[SC_HALO + 1:SC_HALO + 1 + FFT_R, :] * w[2:3] + b_ref[...])

    def gate_body(g, carry):
        zv = short_conv(2, v_ref, wv_ref, bv_ref, g) * short_conv(1, x1_ref, w1_ref, b1s_ref, g)
        zv_ref[pl.ds(pl.multiple_of(g * FFT_T_PITCH, 8), FFT_R), :] = zv
        return carry
    lax.fori_loop(0, n_groups, gate_body, 0, unroll=2)

    _fft_stage1(zv_ref, a1_ref, g_ref, n_groups)

    def freq_body(k2, carry):
        gk = _load_freq_rows(g_ref, k2).astype(BF16)
        x = jnp.dot(w2f_ref[...], gk, preferred_element_type=F32)
        kf = kf_ref[0, k2].astype(F32)
        xr, xi = x[:FFT_R], x[FFT_R:]
        kr, ki = kf[:FFT_R], kf[FFT_R:]
        p = jnp.concatenate([xr * kr - xi * ki, xr * ki + xi * kr], axis=0).astype(BF16)
        hk = jnp.dot(w2i_ref[...], p, preferred_element_type=F32)
        g_ref[pl.ds(k2, FFT_R, stride=FFT_G_PITCH), :] = hk[:FFT_R]
        g_ref[pl.ds(FFT_KP + k2, FFT_R, stride=FFT_G_PITCH), :] = hk[FFT_R:]
        return carry
    lax.fori_loop(0, FFT_KH, freq_body, 0, unroll=FFT_UNROLL_FREQ)

    def time_body(n1, carry):
        slab = g_ref[pl.ds(pl.multiple_of(n1 * FFT_G_PITCH, 8), FFT_SLAB), :].astype(BF16)
        y_ref[pl.ds(n1, n_groups, stride=FFT_T_PITCH), :] = jnp.dot(b1_ref[n1], slab, preferred_element_type=F32)
        return carry
    lax.fori_loop(0, FFT_R, time_body, 0, unroll=FFT_UNROLL_TIME)

    bias = bias_ref[...]

    def out_body(g, carry):
        rows = pl.ds(pl.multiple_of(g * FFT_T_PITCH, 8), FFT_R)
        x0 = short_conv(0, x0_ref, w0_ref, b0_ref, g)
        y = (y_ref[rows, :] + zv_ref[rows, :] * bias) * x0
        o_ref[0, pl.ds(pl.multiple_of(g * FFT_R, FFT_R), FFT_R), :] = y.astype(o_ref.dtype)
        return carry
    lax.fori_loop(0, n_groups, out_body, 0, unroll=2)


def _long_conv(u, w_sc, b_sc, kf, layer, hy_bias, mats):
    B, L, _ = u.shape
    C = HY_WIDTH
    cb = CONV_CB
    ncb = C // cb
    lp = L // FFT_R * FFT_T_PITCH
    part = lambda p: pl.BlockSpec((1, L, cb), lambda j, b: (b, 0, p * ncb + j))
    wpart = lambda p: pl.BlockSpec((3, cb), lambda j, b: (0, p * ncb + j))
    bpart = lambda p: pl.BlockSpec((1, cb), lambda j, b: (0, p * ncb + j))
    b2 = b_sc.reshape(1, 3 * C)
    return pl.pallas_call(
        _long_conv_kernel,
        out_shape=jax.ShapeDtypeStruct((B, L, C), BF16),
        grid=(ncb, B),
        in_specs=[part(0), part(1), part(2), wpart(0), wpart(1), wpart(2), bpart(0), bpart(1), bpart(2),
                  pl.BlockSpec((1, FFT_KH, 2 * FFT_R, cb), lambda j, b: (layer, 0, 0, j)),
                  pl.BlockSpec((1, cb), lambda j, b: (0, j)),
                  pl.BlockSpec((FFT_R, FFT_SLAB, FFT_R // 2), lambda j, b: (0, 0, 0)),
                  pl.BlockSpec((FFT_R, FFT_R // 2, FFT_SLAB), lambda j, b: (0, 0, 0)),
                  pl.BlockSpec((2 * FFT_R, 2 * FFT_R), lambda j, b: (0, 0)),
                  pl.BlockSpec((2 * FFT_R, 2 * FFT_R), lambda j, b: (0, 0))],
        out_specs=pl.BlockSpec((1, L, cb), lambda j, b: (b, 0, j)),
        scratch_shapes=[pltpu.VMEM((FFT_R * FFT_G_PITCH, cb), F32), pltpu.VMEM((lp, cb), F32),
                        pltpu.VMEM((lp, cb), F32), pltpu.VMEM((3, FFT_R + 2 * SC_HALO, cb), F32)],
        compiler_params=_params("arbitrary", "arbitrary"),
        name="hyena_long_conv",
    )(u, u, u, w_sc, w_sc, w_sc, b2, b2, b2, kf, hy_bias.reshape(1, C),
      mats["a1"], mats["b1"], mats["w2f"], mats["w2i"])


ATT_TQ = 128
ATT_RADIUS = 64
ATT_WINDOW = ATT_TQ + 2 * ATT_RADIUS


ATT_SUB = 8


def _attn_kernel(q_ref, k_ref, v_ref, o_ref, lse_ref):
    i = pl.program_id(2)
    ls = k_ref.shape[1]
    nh = HEADS_PER_GROUP
    head_of_col = lax.broadcasted_iota(jnp.int32, (1, ATT_OUT), 1) // HEAD_DIM
    row_iota = lax.broadcasted_iota(jnp.int32, (nh * ATT_TQ, ATT_WINDOW), 0) % ATT_TQ
    col_iota = lax.broadcasted_iota(jnp.int32, (nh * ATT_TQ, ATT_WINDOW), 1)
    rel = row_iota - col_iota
    for sub in range(ATT_SUB):
        q0 = (i * ATT_SUB + sub) * ATT_TQ
        rows = slice(sub * ATT_TQ, (sub + 1) * ATT_TQ)
        q = q_ref[0, rows, :]
        start = jnp.clip(q0 - ATT_RADIUS, 0, ls - ATT_WINDOW)
        start = pl.multiple_of(start, ATT_RADIUS)
        kw = k_ref[0, pl.ds(start, ATT_WINDOW), :]
        vw = v_ref[0, pl.ds(start, ATT_WINDOW), :]
        out, lse = _attn_unit(q, kw, vw, rel, q0 - start, head_of_col)
        o_ref[0, rows, :] = out.astype(o_ref.dtype)
        lse_ref[0, rows, :] = lse


def _attn_unit(q, kw, vw, rel, q_minus_start, head_of_col):
    nh = HEADS_PER_GROUP
    band = jnp.abs(q_minus_start + rel) <= ATT_RADIUS
    zero = jnp.zeros_like(q)
    q4 = jnp.concatenate([jnp.where(head_of_col == h, q, zero) for h in range(nh)], axis=0)
    s = lax.dot_general(q4, kw, (((1,), (1,)), ((), ())), preferred_element_type=F32)
    s = jnp.where(band, s, MASK_VALUE)
    m = jnp.max(s, axis=-1, keepdims=True)
    p = jnp.exp(s - m)
    den = jnp.sum(p, axis=-1, keepdims=True)
    pv = jnp.dot(p.astype(BF16), vw, preferred_element_type=F32) / den
    lse4 = m + jnp.log(den)
    out = jnp.zeros((ATT_TQ, ATT_OUT), F32)
    lse = jnp.zeros((ATT_TQ, ATT_OUT), F32)
    for h in range(nh):
        mine = head_of_col == h
        hrows = slice(h * ATT_TQ, (h + 1) * ATT_TQ)
        out = jnp.where(mine, pv[hrows], out)
        lse = jnp.where(mine, lse4[hrows], lse)
    return out, lse


ATT_CHUNK = 2048


def _attn_strided_kernel(q0_ref, q1_ref, k0_ref, k1_ref, v0_ref, v1_ref, o_ref, lse_ref, o_scr, lse_scr, *, dil):
    i = pl.program_id(1)
    ls = q0_ref.shape[1] // dil

    def gather(lo_ref, hi_ref, first, n):
        rows = pl.ds(first, n, stride=dil)
        return jnp.concatenate([lo_ref[0, rows, :], hi_ref[0, rows, :]], axis=1).astype(BF16)

    units = ATT_CHUNK // ATT_TQ
    shift = dil.bit_length() - 1
    head_of_col = lax.broadcasted_iota(jnp.int32, (1, ATT_OUT), 1) // HEAD_DIM
    row_iota = lax.broadcasted_iota(jnp.int32, (HEADS_PER_GROUP * ATT_TQ, ATT_WINDOW), 0) % ATT_TQ
    rel = row_iota - lax.broadcasted_iota(jnp.int32, (HEADS_PER_GROUP * ATT_TQ, ATT_WINDOW), 1)

    def unit(u, carry):
        r = u & (dil - 1)
        sb = u >> shift
        m0 = i * (ATT_CHUNK // dil) + sb * ATT_TQ
        start = jnp.clip(m0 - ATT_RADIUS, 0, ls - ATT_WINDOW)
        q = gather(q0_ref, q1_ref, m0 * dil + r, ATT_TQ)
        kw = gather(k0_ref, k1_ref, start * dil + r, ATT_WINDOW)
        vw = gather(v0_ref, v1_ref, start * dil + r, ATT_WINDOW)
        out, lse = _attn_unit(q, kw, vw, rel, m0 - start, head_of_col)
        dst = pl.ds(sb * ATT_TQ * dil + r, ATT_TQ, stride=dil)
        for half in range(2):
            lanes = slice(half * LANES, (half + 1) * LANES)
            o_scr[half, dst, :] = out[:, lanes]
            lse_scr[half, dst, :] = lse[:, lanes]
        return carry
    lax.fori_loop(0, units, unit, 0, unroll=4 if dil <= 4 else 2)
    for half in range(2):
        lanes = slice(half * LANES, (half + 1) * LANES)
        o_ref[0, :, lanes] = o_scr[half]
        lse_ref[0, :, lanes] = lse_scr[half]


def _attention_strided(qkv_g, dil, batch, seq_len):
    view = qkv_g.reshape(batch, seq_len, 3 * ATT_OUT)
    halves = [pl.BlockSpec((1, seq_len, LANES), lambda b, i, c=c: (b, 0, c), pipeline_mode=pl.Buffered(1))
              for c in range(3 * ATT_OUT // LANES)]
    o_spec = pl.BlockSpec((1, ATT_CHUNK, ATT_OUT), lambda b, i: (b, i, 0))
    kern = lambda *refs: _attn_strided_kernel(*refs, dil=dil)
    o, lse = pl.pallas_call(
        kern,
        out_shape=(jax.ShapeDtypeStruct((batch, seq_len, ATT_OUT), F32),
                   jax.ShapeDtypeStruct((batch, seq_len, ATT_OUT), F32)),
        grid=(batch, seq_len // ATT_CHUNK),
        in_specs=halves,
        out_specs=(o_spec, o_spec),
        scratch_shapes=[pltpu.VMEM((2, ATT_CHUNK, LANES), F32), pltpu.VMEM((2, ATT_CHUNK, LANES), F32)],
        compiler_params=_params("arbitrary", "arbitrary"),
        name=f"dilated_attn_d{dil}",
    )(*([view] * len(halves)))
    T = batch * seq_len
    return o.reshape(T, ATT_OUT), lse.reshape(T, ATT_OUT)


def _attention_group(qkv_g, dil, batch, seq_len):
    ls = seq_len // dil
    tq = ATT_SUB * ATT_TQ
    view = qkv_g.reshape(batch, ls, dil * 3 * ATT_OUT)

    def col(which):
        return lambda b, r, i: (b, 0, r * 3 + which)

    q_map = lambda b, r, i: (b, i, r * 3)
    o_map = lambda b, r, i: (b, i, r)
    o, lse = pl.pallas_call(
        _attn_kernel,
        out_shape=(jax.ShapeDtypeStruct((batch, ls, dil * ATT_OUT), BF16),
                   jax.ShapeDtypeStruct((batch, ls, dil * ATT_OUT), F32)),
        grid=(batch, dil, ls // tq),
        in_specs=[pl.BlockSpec((1, tq, ATT_OUT), q_map),
                  pl.BlockSpec((1, ls, ATT_OUT), col(1)),
                  pl.BlockSpec((1, ls, ATT_OUT), col(2))],
        out_specs=(pl.BlockSpec((1, tq, ATT_OUT), o_map), pl.BlockSpec((1, tq, ATT_OUT), o_map)),
        compiler_params=_params("arbitrary", "arbitrary", "arbitrary"),
        name=f"dilated_attn_d{dil}",
    )(view, view, view)
    T = batch * seq_len
    return o.reshape(T, ATT_OUT), lse.reshape(T, ATT_OUT)


def _merge_kernel(x_ref, yhy_ref, o1_ref, o2_ref, o3_ref, l1_ref, l2_ref, l3_ref, h_ref, wg_ref,
                  wh_ref, wa_ref, wo_ref, g1_ref, out_ref):
    l1, l2, l3 = l1_ref[...], l2_ref[...], l3_ref[...]
    m = jnp.maximum(jnp.maximum(l1, l2), l3)
    e1, e2, e3 = jnp.exp(l1 - m), jnp.exp(l2 - m), jnp.exp(l3 - m)
    tot = e1 + e2 + e3
    y_at = (e1 * o1_ref[...].astype(F32) + e2 * o2_ref[...].astype(F32) + e3 * o3_ref[...].astype(F32)) / tot
    a = jnp.dot(yhy_ref[...], wh_ref[0], preferred_element_type=F32)
    b = jnp.dot(y_at.astype(BF16), wa_ref[0], preferred_element_type=F32)
    d_model = x_ref.shape[1]
    gate = jnp.dot(h_ref[...], wg_ref[0], preferred_element_type=F32)
    merged = _sigmoid(gate[:, :d_model]) * a + _sigmoid(gate[:, d_model:]) * b
    upd = jnp.dot(merged.astype(BF16), wo_ref[0], preferred_element_type=F32)
    out_ref[...] = x_ref[...] + g1_ref[0] * upd


def _merge(x, y_hy, attn, h, w_gate, w_br_h, w_br_a, w_out, layer, g1, seq_len):
    T, D = x.shape
    tm = 512
    per_batch = seq_len // tm
    (o1, l1), (o2, l2), (o3, l3) = attn
    row = lambda w: pl.BlockSpec((tm, w), lambda i: (i, 0))
    full = lambda a: pl.BlockSpec((1,) + a.shape[1:], lambda i: (layer, 0, 0))
    return pl.pallas_call(
        _merge_kernel,
        out_shape=jax.ShapeDtypeStruct((T, D), F32),
        grid=(T // tm,),
        in_specs=[row(D), row(HY_WIDTH), row(ATT_OUT), row(ATT_OUT), row(ATT_OUT),
                  row(ATT_OUT), row(ATT_OUT), row(ATT_OUT),
                  row(D), full(w_gate), full(w_br_h), full(w_br_a), full(w_out),
                  pl.BlockSpec((1, 1, D), lambda i: (i // per_batch, 0, 0))],
        out_specs=row(D),
        compiler_params=_params("arbitrary"),
        name="mixer_merge",
    )(x, y_hy, o1, o2, o3, l1, l2, l3, h, w_gate, w_br_h, w_br_a, w_out, g1)


def _router_kernel(x_ref, g_ref, sc_ref, sh_ref, rwt_ref, rb_ref, hg_ref, stats_ref, cnt_ref, base_ref):
    i = pl.program_id(0)
    tm = x_ref.shape[0]

    @pl.when(i == 0)
    def _():
        base_ref[...] = jnp.zeros_like(base_ref)

    x = x_ref[...]
    r = lax.rsqrt(jnp.mean(x * x, axis=-1, keepdims=True) + NORM_EPS)
    h = (x * r) * g_ref[...]
    h = h * (1.0 + sc_ref[0]) + sh_ref[0]

    logits = lax.dot_general(rwt_ref[...], h, (((1,), (1,)), ((), ())),
                             precision=HIGHEST, preferred_element_type=F32)
    scores = _sigmoid(logits)
    biased = scores + rb_ref[...]

    def row(a, k):
        return a[k:k + 1, :]

    sel = jnp.zeros((1, tm), jnp.int32)
    best = None
    for g in range(N_GROUPS):
        a, b, c, d = (row(biased, 4 * g + k) for k in range(4))
        m_ab, n_ab = jnp.maximum(a, b), jnp.minimum(a, b)
        m_cd, n_cd = jnp.maximum(c, d), jnp.minimum(c, d)
        gs = jnp.maximum(m_ab, m_cd) + jnp.maximum(jnp.minimum(m_ab, m_cd), jnp.maximum(n_ab, n_cd))
        if g == 0:
            best = gs
        else:
            better = gs > best
            sel = jnp.where(better, g, sel)
            best = jnp.where(better, gs, best)

    v, u = [], []
    for k in range(EXPERTS_PER_GROUP):
        vk = jnp.zeros((1, tm), F32)
        uk = jnp.zeros((1, tm), F32)
        for g in range(N_GROUPS):
            vk = jnp.where(sel == g, row(biased, 4 * g + k), vk)
            uk = jnp.where(sel == g, row(scores, 4 * g + k), uk)
        v.append(vk)
        u.append(uk)

    i1 = jnp.zeros((1, tm), jnp.int32)
    b1 = v[0]
    for k in range(1, EXPERTS_PER_GROUP):
        gt = v[k] > b1
        i1 = jnp.where(gt, k, i1)
        b1 = jnp.where(gt, v[k], b1)
    i2 = jnp.zeros((1, tm), jnp.int32)
    b2 = jnp.full((1, tm), -jnp.inf, F32)
    for k in range(EXPERTS_PER_GROUP):
        cand = (i1 != k) & (v[k] > b2)
        i2 = jnp.where(cand, k, i2)
        b2 = jnp.where(cand, v[k], b2)

    lo = jnp.minimum(i1, i2)
    hi = jnp.maximum(i1, i2)
    pair = jnp.where(lo == 0, hi - 1, jnp.where(lo == 1, hi + 1, 5))
    bucket = sel * PAIRS_PER_GROUP + pair

    u_lo = jnp.zeros((1, tm), F32)
    u_hi = jnp.zeros((1, tm), F32)
    for k in range(EXPERTS_PER_GROUP):
        u_lo = jnp.where(lo == k, u[k], u_lo)
        u_hi = jnp.where(hi == k, u[k], u_hi)
    tot = u_lo + u_hi
    w_lo = u_lo / tot
    w_hi = u_hi / tot

    rows = lax.broadcasted_iota(jnp.int32, (BUCKET_ROWS, tm), 0)
    onehot = (rows == bucket).astype(F32)
    t_src = lax.broadcasted_iota(jnp.int32, (tm, tm), 0)
    t_dst = lax.broadcasted_iota(jnp.int32, (tm, tm), 1)
    before = (t_src < t_dst).astype(BF16)
    cum = jnp.dot(onehot.astype(BF16), before, preferred_element_type=F32)
    base = base_ref[...]
    rank = jnp.sum(onehot * (cum + base), axis=0, keepdims=True)
    base = base + jnp.sum(onehot, axis=1, keepdims=True)
    base_ref[...] = base
    cnt_ref[...] = jnp.broadcast_to(base, cnt_ref.shape)

    srow = lax.broadcasted_iota(jnp.int32, (8, tm), 0)
    stats_ref[...] = jnp.where(srow == 0, bucket.astype(F32), jnp.where(srow == 1, rank, 0.0))

    grow = lax.broadcasted_iota(jnp.int32, (GATE_COLS, tm), 0)
    gates_t = jnp.where(grow == 0, w_lo, jnp.where(grow == 1, w_hi, 0.0))
    hg_ref[:, :D_MODEL] = h
    hg_ref[:, D_MODEL:] = gates_t.T


def _router(x, norm_g, sc, sh, router_wt, router_b, seq_len):
    T, D = x.shape
    tm = ROUTER_TM
    per_batch = seq_len // tm
    return pl.pallas_call(
        _router_kernel,
        out_shape=(jax.ShapeDtypeStruct((T, D + GATE_COLS), F32),
                   jax.ShapeDtypeStruct((8, T), F32),
                   jax.ShapeDtypeStruct((BUCKET_ROWS, LANES), F32)),
        grid=(T // tm,),
        in_specs=[pl.BlockSpec((tm, D), lambda i: (i, 0)),
                  pl.BlockSpec((1, D), lambda i: (0, 0)),
                  pl.BlockSpec((1, 1, D), lambda i: (i // per_batch, 0, 0)),
                  pl.BlockSpec((1, 1, D), lambda i: (i // per_batch, 0, 0)),
                  pl.BlockSpec((N_EXPERTS, D), lambda i: (0, 0)),
                  pl.BlockSpec((N_EXPERTS, 1), lambda i: (0, 0))],
        out_specs=(pl.BlockSpec((tm, D + GATE_COLS), lambda i: (i, 0)),
                   pl.BlockSpec((8, tm), lambda i: (0, i)),
                   pl.BlockSpec((BUCKET_ROWS, LANES), lambda i: (0, 0))),
        scratch_shapes=[pltpu.VMEM((BUCKET_ROWS, 1), F32)],
        compiler_params=_params("arbitrary"),
        name="moe_router",
    )(x, norm_g, sc, sh, router_wt, router_b)


def _start_row_copies(n_rows, make_copy):
    group = 8

    def body(g, carry):
        base = pl.multiple_of(g * group, group)
        for k in range(group):
            make_copy(base + k).start(priority=k % 2)
        return carry

    lax.fori_loop(0, n_rows // group, body, 0)


def _dispatch_kernel(pos_ref, hg_ref, xs_init_hbm, xs_hbm, sem):
    del xs_init_hbm
    _start_row_copies(ROW_BLOCK, lambda r: pltpu.make_async_copy(
        hg_ref.at[pl.ds(r, 1), :], xs_hbm.at[pl.ds(pos_ref[r], 1), :], sem))
    pltpu.make_async_copy(hg_ref, xs_hbm.at[pl.ds(0, ROW_BLOCK), :], sem).wait()


def _dispatch(pos, hg, n_rows):
    T, W = hg.shape
    zeros = jnp.zeros((n_rows, W), F32)
    return pl.pallas_call(
        _dispatch_kernel,
        out_shape=jax.ShapeDtypeStruct((n_rows, W), F32),
        grid=(T // ROW_BLOCK,),
        in_specs=[pl.BlockSpec((ROW_BLOCK,), lambda i: (i,), memory_space=pltpu.SMEM),
                  pl.BlockSpec((ROW_BLOCK, W), lambda i: (i, 0)),
                  pl.BlockSpec(memory_space=pl.ANY)],
        out_specs=pl.BlockSpec(memory_space=pl.ANY),
        scratch_shapes=[pltpu.SemaphoreType.DMA],
        input_output_aliases={2: 0},
        compiler_params=_params("arbitrary"),
        name="moe_dispatch",
    )(pos, hg, zeros)


def _expert_kernel(e_lo_ref, e_hi_ref, n_used_ref, xs_ref, w1a, w3a, w2a, w1b, w3b, w2b, y_ref):
    del e_lo_ref, e_hi_ref
    used = pl.program_id(0) < n_used_ref[0]

    @pl.when(jnp.logical_not(used))
    def _():
        y_ref[...] = jnp.zeros_like(y_ref)

    @pl.when(used)
    def _():
        xb = xs_ref[:, :D_MODEL].astype(BF16)
        g_lo = xs_ref[:, D_MODEL:D_MODEL + 1]
        g_hi = xs_ref[:, D_MODEL + 1:D_MODEL + 2]

        def ffn(w1, w3, w2):
            a = jnp.dot(xb, w1[0, 0], preferred_element_type=F32)
            b = jnp.dot(xb, w3[0, 0], preferred_element_type=F32)
            act = (a * _sigmoid(a)) * b
            return jnp.dot(act.astype(BF16), w2[0, 0], preferred_element_type=F32)

        y_ref[...] = g_lo * ffn(w1a, w3a, w2a) + g_hi * ffn(w1b, w3b, w2b)


def _experts(tile_lo, tile_hi, n_used, xs, w1, w3, w2, layer):
    n_rows, W = xs.shape
    D, F = w1.shape[2], w1.shape[3]
    n_tiles = n_rows // EXPERT_TM

    def x_map(j, lo, hi, nu):
        return (jnp.minimum(j, nu[0] - 1), 0)

    def w_lo_map(j, lo, hi, nu):
        return (layer, lo[j], 0, 0)

    def w_hi_map(j, lo, hi, nu):
        return (layer, hi[j], 0, 0)

    grid_spec = pltpu.PrefetchScalarGridSpec(
        num_scalar_prefetch=3,
        grid=(n_tiles,),
        in_specs=[pl.BlockSpec((EXPERT_TM, W), x_map),
                  pl.BlockSpec((1, 1, D, F), w_lo_map), pl.BlockSpec((1, 1, D, F), w_lo_map),
                  pl.BlockSpec((1, 1, F, D), w_lo_map),
                  pl.BlockSpec((1, 1, D, F), w_hi_map), pl.BlockSpec((1, 1, D, F), w_hi_map),
                  pl.BlockSpec((1, 1, F, D), w_hi_map)],
        out_specs=pl.BlockSpec((EXPERT_TM, D), lambda j, lo, hi, nu: (j, 0)),
    )
    return pl.pallas_call(
        _expert_kernel,
        out_shape=jax.ShapeDtypeStruct((n_rows, D), F32),
        grid_spec=grid_spec,
        compiler_params=_params("arbitrary"),
        name="moe_experts",
    )(tile_lo, tile_hi, n_used, xs, w1, w3, w2, w1, w3, w2)


def _combine_kernel(pos_ref, x_ref, g2_ref, ng_ref, sc_ref, sh_ref, ys_hbm, *refs):
    *out_refs, buf, sem = refs
    _start_row_copies(ROW_BLOCK, lambda r: pltpu.make_async_copy(
        ys_hbm.at[pl.ds(pos_ref[r], 1), :], buf.at[pl.ds(r, 1), :], sem))
    pltpu.make_async_copy(ys_hbm.at[pl.ds(0, ROW_BLOCK), :], buf, sem).wait()
    xn = x_ref[...] + g2_ref[0] * buf[...]
    r = lax.rsqrt(jnp.mean(xn * xn, axis=-1, keepdims=True) + NORM_EPS)
    h = (xn * r) * ng_ref[...]
    if len(out_refs) == 2:
        out_refs[0][...] = xn
        h = h * (1.0 + sc_ref[0]) + sh_ref[0]
    out_refs[-1][...] = h.astype(out_refs[-1].dtype)


def _combine(pos, x, g2, ys, norm_g, sc, sh, seq_len, last):
    T, D = x.shape
    per_batch = seq_len // ROW_BLOCK
    row = pl.BlockSpec((ROW_BLOCK, D), lambda i: (i, 0))
    per_b = pl.BlockSpec((1, 1, D), lambda i: (i // per_batch, 0, 0))
    if last:
        out_shape, out_specs = jax.ShapeDtypeStruct((T, D), F32), row
    else:
        out_shape = (jax.ShapeDtypeStruct((T, D), F32), jax.ShapeDtypeStruct((T, D), BF16))
        out_specs = (row, row)
    return pl.pallas_call(
        _combine_kernel,
        out_shape=out_shape,
        grid=(T // ROW_BLOCK,),
        in_specs=[pl.BlockSpec((ROW_BLOCK,), lambda i: (i,), memory_space=pltpu.SMEM),
                  row, per_b, pl.BlockSpec((1, D), lambda i: (0, 0)), per_b, per_b,
                  pl.BlockSpec(memory_space=pl.ANY)],
        out_specs=out_specs,
        scratch_shapes=[pltpu.VMEM((ROW_BLOCK, D), F32), pltpu.SemaphoreType.DMA],
        compiler_params=_params("arbitrary"),
        name="moe_combine",
    )(pos, x, g2, norm_g, sc, sh, ys)


def _moe_layer(x, norm_g, sc2, sh2, g2, router_wt, router_b, w1, w3, w2, layer, seq_len, next_norm, last):
    T, D = x.shape
    hg, stats, counts = _router(x, norm_g, sc2, sh2, router_wt, router_b, seq_len)

    cnt = counts[:N_BUCKETS, 0].astype(jnp.int32)
    padded = ((cnt + EXPERT_TM - 1) // EXPERT_TM) * EXPERT_TM
    ends = jnp.cumsum(padded)
    starts = ends - padded
    bucket = stats[0].astype(jnp.int32)
    pos = starts[bucket] + stats[1].astype(jnp.int32)
    n_tiles = T // EXPERT_TM + N_BUCKETS
    tile_row0 = jnp.arange(n_tiles, dtype=jnp.int32) * EXPERT_TM
    tile_bucket = jnp.sum((ends[None, :] <= tile_row0[:, None]).astype(jnp.int32), axis=1)
    tile_bucket = jnp.minimum(tile_bucket, N_BUCKETS - 1)
    grp, pair = tile_bucket // PAIRS_PER_GROUP, tile_bucket % PAIRS_PER_GROUP
    pair_lo = jnp.array([0, 0, 0, 1, 1, 2], jnp.int32)[pair]
    pair_hi = jnp.array([1, 2, 3, 2, 3, 3], jnp.int32)[pair]
    tile_lo = grp * EXPERTS_PER_GROUP + pair_lo
    tile_hi = grp * EXPERTS_PER_GROUP + pair_hi
    n_used = (ends[-1] // EXPERT_TM).astype(jnp.int32).reshape(1)

    xs = _dispatch(pos, hg, n_tiles * EXPERT_TM)
    ys = _experts(tile_lo, tile_hi, n_used, xs, w1, w3, w2, layer)
    return _combine(pos, x, g2, ys, *next_norm, seq_len, last)


def kernel(x, c, norm1_g, norm2_g, w_ada, b_ada, w_in, w_sc, b_sc, hf_w1, hf_b1, hf_w2, hf_b2, hf_w3, hf_b3,
           hf_w4, hf_freq, hy_bias, w_br_h, w_br_a, w_out, router_w, router_bias, moe_w1, moe_w3, moe_w2, final_g):
    B, L, D = x.shape
    T = B * L
    C = HY_WIDTH
    xt = x.reshape(T, D)

    rope = _rope_tables(L)
    feat, deltas = _hyena_features(L)
    mats = _dft_matrices()

    c_pad = jnp.pad(c, ((0, 8 - B), (0, 0)))
    mod = _ada(c_pad, w_ada, b_ada)[:, :B]
    router_wt = router_w.T
    router_b = router_bias.reshape(N_EXPERTS, 1)
    w_in_b, w_br_h_b, w_br_a_b, w_out_b = (w.astype(BF16) for w in (w_in, w_br_h, w_br_a, w_out))
    moe_w1_b, moe_w3_b, moe_w2_b = (w.astype(BF16) for w in (moe_w1, moe_w3, moe_w2))
    w_gate_b = w_in_b[:, :, HY_COLS + QKV_COLS:]

    half = LANES // 2

    def diag2(w):
        wp = jnp.pad(w, ((0, 0), (0, half - w.shape[1]), (0, half - w.shape[2])))
        zero = jnp.zeros_like(wp)
        return jnp.concatenate([jnp.concatenate([wp, zero], axis=2), jnp.concatenate([zero, wp], axis=2)], axis=1)

    twice = lambda v: jnp.concatenate([v, v], axis=1)[:, None]
    w4_zero = jnp.zeros_like(hf_w4)
    w4_halves = jnp.stack([jnp.concatenate([hf_w4, w4_zero], axis=1),
                           jnp.concatenate([w4_zero, hf_w4], axis=1)], axis=1)
    k2 = _hyena_filter(feat, diag2(hf_w1), twice(hf_b1), diag2(hf_w2), twice(hf_b2), diag2(hf_w3), twice(hf_b3),
                       w4_halves, twice(hf_freq), deltas)
    kf = _filter_fft(k2.reshape(DEPTH, -1, 2 * C), mats)

    mods = [[mod[i, :, k * D:(k + 1) * D].reshape(B, 1, D) for k in range(6)] for i in range(DEPTH)]
    h = _norm_mod(xt, norm1_g[0].reshape(1, D), mods[0][1], mods[0][0], L)
    for i in range(DEPTH):
        sh1, sc1, g1, sh2, sc2, g2 = mods[i]

        u = _proj(h, w_in_b, i, 0, HY_COLS, tn=HY_WIDTH)
        qkv = [_proj_qkv(h, w_in_b, i, g, rope, L, BF16 if dil == 1 else F32)
               for g, (_, dil) in enumerate(ATT_GROUPS)]

        y_hy = _long_conv(u.reshape(B, L, 3 * C), w_sc[i], b_sc[i], kf, i, hy_bias[i], mats).reshape(T, C)

        attn = [(_attention_group if dil == 1 else _attention_strided)(qkv[g], dil, B, L)
                for g, (_, dil) in enumerate(ATT_GROUPS)]
        xt = _merge(xt, y_hy, attn, h, w_gate_b, w_br_h_b, w_br_a_b, w_out_b, i, g1, L)

        last = i == DEPTH - 1
        if last:
            next_norm = (final_g.reshape(1, D), sc2, sh2)
        else:
            next_norm = (norm1_g[i + 1].reshape(1, D), mods[i + 1][1], mods[i + 1][0])
        res = _moe_layer(xt, norm2_g[i].reshape(1, D), sc2, sh2, g2, router_wt, router_b,
                         moe_w1_b, moe_w3_b, moe_w2_b, i, L, next_norm, last)
        if last:
            return res.reshape(B, L, D)
        xt, h = res
```
